```python
import jax, jax.numpy as jnp
from jax import lax
import numpy as np

D_MODEL = 1024
BATCH = 8
SEQ = 4096
DEPTH = 1
DEC_BATCH = 32
DEC_SEQ = 8
PAST_LEN = 16384
PAGE_SIZE = 128

HEAD_DIM_A = 64
N_HEADS_A = 8
WIDTH_A = N_HEADS_A * HEAD_DIM_A
DILATED_CONFIGS = ((128, 1), (512, 4), (2048, 16))
MAX_DIST = 2048
ROPE_THETA = 10000.0
Q_BLOCK = 128

N_HEADS_B = 4
WIDTH_B = D_MODEL - WIDTH_A
DV_B = WIDTH_B // N_HEADS_B
DK_B = DV_B // 2
QK_B = N_HEADS_B * DK_B
GATE_RANK = 16
GATE_LOGIT_NORM = 16.0
GLA_CHUNK = 64

N_EXPERTS = 32
TOP_K = 4
D_FF = D_MODEL
SWIGLU_ALPHA = 1.702
SWIGLU_LIMIT = 7.0
MOE_BLOCK = 128
EPS = 1e-6

IN_COLS = 3 * WIDTH_A + 2 * QK_B + 2 * WIDTH_B + GATE_RANK
SPLITS = (WIDTH_A, 2 * WIDTH_A, 3 * WIDTH_A, 3 * WIDTH_A + QK_B, 3 * WIDTH_A + 2 * QK_B,
          3 * WIDTH_A + 2 * QK_B + WIDTH_B, 3 * WIDTH_A + 2 * QK_B + 2 * WIDTH_B)

kernel_name = "hymba_dilated_swa_gla_moe_step"


def rmsnorm(x, g):
    xf = x.astype(jnp.float32)
    y = xf * lax.rsqrt(jnp.mean(xf * xf, axis=-1, keepdims=True) + EPS)
    return (y * g.astype(jnp.float32)).astype(x.dtype)


def rope(x, pos):
    half = x.shape[-1] // 2
    inv = ROPE_THETA ** (-jnp.arange(half, dtype=jnp.float32) / half)
    ang = pos.astype(jnp.float32)[:, None] * inv[None, :]
    cos = jnp.cos(ang)[None, :, None, :]
    sin = jnp.sin(ang)[None, :, None, :]
    xf = x.astype(jnp.float32)
    x1, x2 = xf[..., :half], xf[..., half:]
    return jnp.concatenate([x1 * cos - x2 * sin, x2 * cos + x1 * sin], axis=-1).astype(x.dtype)


def project_mixers(xn, pos, w_in, w_alpha, b_alpha):
    B, T, _ = xn.shape
    proj = xn @ w_in
    qa, ka, va, qb, kb, vb, zg, lr = jnp.split(proj, SPLITS, axis=-1)
    heads_a = lambda t: t.reshape(B, T, N_HEADS_A, HEAD_DIM_A)
    qa, ka, va = rope(heads_a(qa), pos), rope(heads_a(ka), pos), heads_a(va)
    heads_b = lambda t, d: t.reshape(B, T, N_HEADS_B, d).transpose(0, 2, 1, 3).astype(jnp.float32)
    qb = heads_b(qb, DK_B) * (DK_B ** -0.5)
    kb = heads_b(kb, DK_B)
    vb = heads_b(vb, DV_B)
    gb = jax.nn.log_sigmoid((lr @ w_alpha + b_alpha).astype(jnp.float32)) / GATE_LOGIT_NORM
    gb = heads_b(gb, DK_B)
    return qa, ka, va, qb, kb, vb, gb, zg


def dilated_attention(q, k_ext, v_ext, q_pos, kv_offset):
    qf = q.astype(jnp.float32) * (HEAD_DIM_A ** -0.5)
    outs, lses = [], []
    for window, dil in DILATED_CONFIGS:
        n_keys = window // dil + 1
        key_pos = q_pos[:, None] - dil * jnp.arange(n_keys, dtype=jnp.int32)[None, :]
        valid = key_pos >= 0
        rows = jnp.clip(key_pos - kv_offset, 0, k_ext.shape[1] - 1)
        kg = k_ext[:, rows]
        vg = v_ext[:, rows]
        s = jnp.einsum('bthd,btkhd->bthk', qf, kg.astype(jnp.float32))
        s = jnp.where(valid[None, :, None, :], s, -jnp.inf)
        m = jnp.max(s, axis=-1, keepdims=True)
        p = jnp.exp(s - m)
        den = jnp.sum(p, axis=-1, keepdims=True)
        outs.append(jnp.einsum('bthk,btkhd->bthd', p, vg.astype(jnp.float32)) / den)
        lses.append(m + jnp.log(den))
    w = jax.nn.softmax(jnp.concatenate(lses, axis=-1), axis=-1)
    out = w[..., 0:1] * outs[0] + w[..., 1:2] * outs[1] + w[..., 2:3] * outs[2]
    return out.astype(q.dtype)


def prompt_dilated_attention(q, k, v):
    B, S, H, Dh = q.shape
    n_blocks = S // Q_BLOCK

    def block(i):
        start = i * Q_BLOCK
        qb = lax.dynamic_slice_in_dim(q, start, Q_BLOCK, axis=1)
        pos = start + jnp.arange(Q_BLOCK, dtype=jnp.int32)
        return dilated_attention(qb, k, v, pos, 0)

    out = lax.map(block, jnp.arange(n_blocks, dtype=jnp.int32))
    return out.transpose(1, 0, 2, 3, 4).reshape(B, S, H, Dh)


def gla_chunk(state, q, k, v, g):
    state = state.astype(jnp.float32)
    C = q.shape[2]
    b = jnp.cumsum(g, axis=2)
    o_inter = jnp.einsum('bhck,bhkv->bhcv', q * jnp.exp(b), state)
    causal = jnp.tril(jnp.ones((C, C), dtype=bool))
    diff = b[:, :, :, None, :] - b[:, :, None, :, :]
    decay = jnp.exp(jnp.where(causal[None, None, :, :, None], diff, -jnp.inf))
    a = jnp.einsum('bhtk,bhsk,bhtsk->bhts', q, k, decay)
    o = o_inter + jnp.einsum('bhts,bhsv->bhtv', a, v)
    b_last = b[:, :, -1:, :]
    k_dec = k * jnp.exp(b_last - b)
    new_state = jnp.exp(b_last[:, :, 0, :])[..., None] * state + jnp.einsum('bhsk,bhsv->bhkv', k_dec, v)
    return new_state, o


def gla_prompt(q, k, v, g):
    B, H, S, _ = q.shape
    n_chunks = S // GLA_CHUNK

    def chunks(t):
        return t.reshape(B, H, n_chunks, GLA_CHUNK, t.shape[-1]).transpose(2, 0, 1, 3, 4)

    state0 = jnp.zeros((B, H, DK_B, DV_B), jnp.float32)
    final, o = lax.scan(lambda s, xs: gla_chunk(s, *xs), state0,
                        (chunks(q), chunks(k), chunks(v), chunks(g)))
    return final, o.transpose(1, 2, 0, 3, 4).reshape(B, H, S, DV_B)


def merge_mixers(oa, ob, zg, gla_norm_g, w_out):
    B, T = oa.shape[:2]
    obn = ob * lax.rsqrt(jnp.mean(ob * ob, axis=-1, keepdims=True) + EPS)
    obn = obn * gla_norm_g.astype(jnp.float32).reshape(N_HEADS_B, 1, DV_B)
    obn = obn.transpose(0, 2, 1, 3).reshape(B, T, WIDTH_B) * jax.nn.silu(zg.astype(jnp.float32))
    mixed = jnp.concatenate([oa.reshape(B, T, WIDTH_A).astype(jnp.float32), obn], axis=-1)
    return mixed.astype(zg.dtype) @ w_out


def moe_ffn(x, w_router, b_router, w_gate_up, b_gate_up, w_down, b_down):
    B, T, D = x.shape
    xt = x.reshape(-1, D)
    n = xt.shape[0]
    logits = xt.astype(jnp.float32) @ w_router.astype(jnp.float32) + b_router.astype(jnp.float32)
    top_val, top_idx = lax.top_k(logits, TOP_K)
    gates = jax.nn.softmax(top_val, axis=-1)
    n_assign = n * TOP_K
    flat_e = top_idx.reshape(-1)
    order = jnp.argsort(flat_e, stable=True)
    e_sorted = flat_e[order]
    tok_sorted = order // TOP_K
    gate_sorted = gates.reshape(-1)[order]
    counts = jnp.bincount(flat_e, length=N_EXPERTS)
    padded = (counts + MOE_BLOCK - 1) // MOE_BLOCK * MOE_BLOCK
    start_raw = jnp.cumsum(counts) - counts
    end_pad = jnp.cumsum(padded)
    start_pad = end_pad - padded
    dest = start_pad[e_sorted] + (jnp.arange(n_assign, dtype=jnp.int32) - start_raw[e_sorted])
    n_blocks = -(-(n_assign + N_EXPERTS * (MOE_BLOCK - 1)) // MOE_BLOCK)
    cap = n_blocks * MOE_BLOCK
    xs = jnp.zeros((cap, D), x.dtype).at[dest].set(xt[tok_sorted])
    block_start = jnp.arange(n_blocks, dtype=jnp.int32) * MOE_BLOCK
    block_e = jnp.clip(jnp.searchsorted(end_pad, block_start, side='right'), 0, N_EXPERTS - 1)

    def expert_block(args):
        xb, e = args
        h = xb @ w_gate_up[e] + b_gate_up[e]
        glu = jnp.minimum(h[:, :D_FF], SWIGLU_LIMIT)
        lin = jnp.clip(h[:, D_FF:], -SWIGLU_LIMIT, SWIGLU_LIMIT)
        act = glu * jax.nn.sigmoid(SWIGLU_ALPHA * glu) * (lin + 1.0)
        return act @ w_down[e] + b_down[e]

    ys = lax.map(expert_block, (xs.reshape(n_blocks, MOE_BLOCK, D), block_e)).reshape(cap, D)
    y = jax.ops.segment_sum(ys[dest] * gate_sorted[:, None].astype(ys.dtype), tok_sorted, num_segments=n)
    return y.reshape(B, T, D).astype(x.dtype)


def setup_inputs(seed: int = 0) -> dict:
    key = jax.random.key(seed)
    ks = jax.random.split(key, 20)
    rows = min(MAX_DIST, PAST_LEN)
    nrm = lambda k, shape, scale: jax.random.normal(k, shape, jnp.float32) * scale
    return {
        "x_prompt": nrm(ks[0], (BATCH, SEQ, D_MODEL), 1.0),
        "x_sample": nrm(ks[1], (DEC_BATCH, DEC_SEQ, D_MODEL), 1.0),
        "cache_swa_k": nrm(ks[2], (DEPTH, DEC_BATCH, rows, N_HEADS_A, HEAD_DIM_A), 1.0),
        "cache_swa_v": nrm(ks[3], (DEPTH, DEC_BATCH, rows, N_HEADS_A, HEAD_DIM_A), 1.0),
        "state_gla": nrm(ks[4], (DEPTH, DEC_BATCH, N_HEADS_B, DK_B, DV_B), 0.1),
        "norm_mix_g": 1.0 + nrm(ks[5], (DEPTH, D_MODEL), 0.02),
        "w_in": nrm(ks[6], (DEPTH, D_MODEL, IN_COLS), D_MODEL ** -0.5),
        "w_alpha": nrm(ks[7], (DEPTH, GATE_RANK, QK_B), GATE_RANK ** -0.5),
        "b_alpha": nrm(ks[8], (DEPTH, QK_B), 0.1),
        "gla_norm_g": 1.0 + nrm(ks[9], (DEPTH, WIDTH_B), 0.02),
        "w_out": nrm(ks[10], (DEPTH, D_MODEL, D_MODEL), D_MODEL ** -0.5),
        "norm_ffn_g": 1.0 + nrm(ks[11], (DEPTH, D_MODEL), 0.02),
        "w_router": nrm(ks[12], (DEPTH, D_MODEL, N_EXPERTS), D_MODEL ** -0.5),
        "b_router": nrm(ks[13], (DEPTH, N_EXPERTS), 0.01),
        "w_gate_up": nrm(ks[14], (DEPTH, N_EXPERTS, D_MODEL, 2 * D_FF), D_MODEL ** -0.5),
        "b_gate_up": nrm(ks[15], (DEPTH, N_EXPERTS, 2 * D_FF), 0.01),
        "w_down": nrm(ks[16], (DEPTH, N_EXPERTS, D_FF, D_MODEL), D_FF ** -0.5),
        "b_down": nrm(ks[17], (DEPTH, N_EXPERTS, D_MODEL), 0.01),
        "norm_final_g": 1.0 + nrm(ks[18], (D_MODEL,), 0.02),
    }


def reference(x_prompt, x_sample, cache_swa_k, cache_swa_v, state_gla, norm_mix_g, w_in, w_alpha,
              b_alpha, gla_norm_g, w_out, norm_ffn_g, w_router, b_router, w_gate_up, b_gate_up,
              w_down, b_down, norm_final_g):
    pos_p = jnp.arange(SEQ, dtype=jnp.int32)
    pos_s = PAST_LEN + jnp.arange(DEC_SEQ, dtype=jnp.int32)
    cache_rows = cache_swa_k.shape[2]
    rows_p = min(MAX_DIST, SEQ)
    h_p, h_s = x_prompt, x_sample
    kp, vp, sp, ksn, vsn, ssn = [], [], [], [], [], []
    for l in range(DEPTH):
        qa, ka, va, qb, kb, vb, gb, zg = project_mixers(rmsnorm(h_p, norm_mix_g[l]), pos_p,
                                                        w_in[l], w_alpha[l], b_alpha[l])
        oa = prompt_dilated_attention(qa, ka, va)
        st, ob = gla_prompt(qb, kb, vb, gb)
        h_p = h_p + merge_mixers(oa, ob, zg, gla_norm_g[l], w_out[l])
        h_p = h_p + moe_ffn(rmsnorm(h_p, norm_ffn_g[l]), w_router[l], b_router[l], w_gate_up[l],
                            b_gate_up[l], w_down[l], b_down[l])
        kp.append(ka[:, SEQ - rows_p:])
        vp.append(va[:, SEQ - rows_p:])
        sp.append(st)
        qa, ka, va, qb, kb, vb, gb, zg = project_mixers(rmsnorm(h_s, norm_mix_g[l]), pos_s,
                                                        w_in[l], w_alpha[l], b_alpha[l])
        k_ext = jnp.concatenate([cache_swa_k[l].astype(ka.dtype), ka], axis=1)
        v_ext = jnp.concatenate([cache_swa_v[l].astype(va.dtype), va], axis=1)
        oa = dilated_attention(qa, k_ext, v_ext, pos_s, PAST_LEN - cache_rows)
        st, ob = gla_chunk(state_gla[l], qb, kb, vb, gb)
        h_s = h_s + merge_mixers(oa, ob, zg, gla_norm_g[l], w_out[l])
        h_s = h_s + moe_ffn(rmsnorm(h_s, norm_ffn_g[l]), w_router[l], b_router[l], w_gate_up[l],
                            b_gate_up[l], w_down[l], b_down[l])
        ksn.append(k_ext[:, DEC_SEQ:])
        vsn.append(v_ext[:, DEC_SEQ:])
        ssn.append(st)
    y_prompt = rmsnorm(h_p, norm_final_g)
    y_sample = rmsnorm(h_s, norm_final_g)
    return (y_prompt, y_sample, jnp.stack(kp), jnp.stack(vp), jnp.stack(sp),
            jnp.stack(ksn), jnp.stack(vsn), jnp.stack(ssn))
```

```python
import functools

import jax
import jax.numpy as jnp
import numpy as np
from jax import lax
from jax.experimental import pallas as pl
from jax.experimental.pallas import tpu as pltpu

F32 = jnp.float32
BF16 = jnp.bfloat16

N_HEADS_A = 8
HEAD_DIM_A = 64
WIDTH_A = N_HEADS_A * HEAD_DIM_A
N_HEADS_B = 4
DK_B = 64
DV_B = 128
QK_B = N_HEADS_B * DK_B
WIDTH_B = N_HEADS_B * DV_B
GATE_RANK = 16
GATE_LOGIT_NORM = 16.0
DILATIONS = (1, 4, 16)
KEYS_PER_CONFIG = 128
ROPE_THETA = 10000.0
N_EXPERTS = 32
TOP_K = 4
SWIGLU_ALPHA = 1.702
SWIGLU_LIMIT = 7.0
EPS = 1e-6

LANES = 128
VMEM_LIMIT = 56 * 1024 * 1024


def _cparams(sem, vmem=VMEM_LIMIT):
    return pltpu.CompilerParams(dimension_semantics=sem, vmem_limit_bytes=vmem)


PROJ_MAIN = 3 * WIDTH_A + 2 * QK_B + 2 * WIDTH_B


def _rope_tables(pos):
    half = HEAD_DIM_A // 2
    inv = ROPE_THETA ** (-jnp.arange(half, dtype=F32) / half)
    ang = pos.astype(F32)[:, None] * inv[None, :]
    cos = jnp.cos(ang)
    sin = jnp.sin(ang)
    cos_t = jnp.concatenate([cos, cos, cos, cos], axis=-1)
    sin_t = jnp.concatenate([-sin, sin, -sin, sin], axis=-1)
    return cos_t, sin_t


def _rope_block(t, cos, sin, first_half):
    partner = jnp.where(first_half, pltpu.roll(t, LANES - 32, 1), pltpu.roll(t, 32, 1))
    return t * cos + partner * sin


def _proj_kernel(x_ref, g_ref, w_ref, wlr_ref, wa_ref, ba_ref, cos_ref, sin_ref,
                 qa_ref, ka_ref, va_ref, qb_ref, kb_ref, vb_ref, zg_ref, gb_ref):
    x = x_ref[...]
    ms = jnp.mean(x * x, axis=-1, keepdims=True)
    xn = ((x * lax.rsqrt(ms + EPS)) * g_ref[...]).astype(BF16)

    def cols(lo, hi):
        return jnp.dot(xn, w_ref[:, lo:hi], preferred_element_type=F32)

    cos = cos_ref[...]
    sin = sin_ref[...]
    lane = lax.broadcasted_iota(jnp.int32, cos.shape, 1)
    first_half = (lane % HEAD_DIM_A) < (HEAD_DIM_A // 2)
    for j in range(WIDTH_A // LANES):
        lo = j * LANES
        q = cols(lo, lo + LANES)
        qa_ref[:, lo:lo + LANES] = _rope_block(q, cos, sin, first_half) * (HEAD_DIM_A ** -0.5)
        k = cols(WIDTH_A + lo, WIDTH_A + lo + LANES)
        ka_ref[:, lo:lo + LANES] = _rope_block(k, cos, sin, first_half)
    o = 2 * WIDTH_A
    va_ref[...] = cols(o, o + WIDTH_A)
    o += WIDTH_A
    qb_ref[...] = cols(o, o + QK_B) * (DK_B ** -0.5)
    o += QK_B
    kb_ref[...] = cols(o, o + QK_B)
    o += QK_B
    vb_ref[...] = cols(o, o + WIDTH_B)
    o += WIDTH_B
    zg_ref[...] = cols(o, o + WIDTH_B)
    lr = jnp.dot(xn, wlr_ref[...], preferred_element_type=F32)
    z = jnp.dot(lr.astype(BF16), wa_ref[...], preferred_element_type=F32) + ba_ref[...]
    logsig = jnp.minimum(z, 0.0) - jnp.log(1.0 + jnp.exp(-jnp.abs(z)))
    gb_ref[...] = logsig / GATE_LOGIT_NORM


def _project(x, pos, norm_g, w_main, w_lr, w_alpha, b_alpha, tm):
    B, T, D = x.shape
    cos_t, sin_t = _rope_tables(pos)
    grid = (B, T // tm)
    row = lambda w: pl.BlockSpec((None, tm, w), lambda b, i: (b, i, 0))
    full = lambda a: pl.BlockSpec(a.shape, lambda b, i: (0,) * a.ndim)
    tab = pl.BlockSpec((tm, LANES), lambda b, i: (i, 0))
    widths = (WIDTH_A, WIDTH_A, WIDTH_A, QK_B, QK_B, WIDTH_B, WIDTH_B, QK_B)
    return pl.pallas_call(
        _proj_kernel,
        grid=grid,
        in_specs=[row(D), full(norm_g), full(w_main), full(w_lr), full(w_alpha), full(b_alpha), tab, tab],
        out_specs=[row(w) for w in widths],
        out_shape=[jax.ShapeDtypeStruct((B, T, w), F32) for w in widths],
        compiler_params=_cparams(("parallel", "parallel")),
        name="proj",
    )(x, norm_g, w_main, w_lr, w_alpha, b_alpha, cos_t, sin_t)


Q_BLOCK = 128
NEG_INF = float("-inf")


def _attn_block(q, k, v, mask, head0, state):
    kb = k.astype(BF16)
    vb = v.astype(BF16)
    m_new, sums, pvs = [], [], []
    for h in range(2):
        hm = head0 if h == 0 else jnp.logical_not(head0)
        qh = jnp.where(hm, q, 0.0).astype(BF16)
        s = lax.dot_general(qh, kb, (((1,), (1,)), ((), ())), preferred_element_type=F32)
        s = jnp.where(mask, s, NEG_INF)
        if state is None:
            mh = jnp.max(s, axis=1, keepdims=True)
        else:
            prev = jnp.where(hm, state[0], NEG_INF)
            mh = jnp.max(jnp.concatenate([s, prev], axis=1), axis=1, keepdims=True)
        p = jnp.exp(s - mh)
        m_new.append(mh)
        sums.append(jnp.sum(p, axis=1, keepdims=True))
        pvs.append(jnp.dot(p.astype(BF16), vb, preferred_element_type=F32))
    m_full = jnp.where(head0, m_new[0], m_new[1])
    l_full = jnp.where(head0, sums[0], sums[1])
    pv_full = jnp.where(head0, pvs[0], pvs[1])
    if state is None:
        return m_full, l_full, pv_full
    a = jnp.exp(state[0] - m_full)
    return m_full, a * state[1] + l_full, a * state[2] + pv_full


def _attn_kernel(q_ref, k_ref, v_ref, o_ref, m_ref, l_ref, acc_ref):
    S = q_ref.shape[0]
    QB = Q_BLOCK
    lane = lax.broadcasted_iota(jnp.int32, (QB, LANES), 1)
    head0 = lane < HEAD_DIM_A
    qq = lax.broadcasted_iota(jnp.int32, (QB, 2 * QB), 0)
    kk = lax.broadcasted_iota(jnp.int32, (QB, 2 * QB), 1)
    band = jnp.logical_and(kk >= qq, kk - qq <= KEYS_PER_CONFIG)
    causal = (lax.broadcasted_iota(jnp.int32, (QB, QB), 1)
              <= lax.broadcasted_iota(jnp.int32, (QB, QB), 0))

    for ci, dil in enumerate(DILATIONS):
        nblk = S // (dil * QB)

        def rows(start, n, dil=dil):
            return pl.ds(start, n) if dil == 1 else pl.ds(start, n, stride=dil)

        def do_block(q0, k0, nk, mask, ci=ci, rows=rows):
            qs = rows(q0, QB)
            ks = rows(k0, nk)
            state = None if ci == 0 else (m_ref[qs, :], l_ref[qs, :], acc_ref[qs, :])
            m, l, acc = _attn_block(q_ref[qs, :], k_ref[ks, :], v_ref[ks, :], mask, head0, state)
            m_ref[qs, :] = m
            l_ref[qs, :] = l
            acc_ref[qs, :] = acc

        def residue(r, carry, dil=dil, nblk=nblk, do_block=do_block):
            do_block(r, r, QB, causal)

            def blk(j, c):
                do_block(r + dil * QB * j, r + dil * QB * (j - 1), 2 * QB, band)
                return c

            return lax.fori_loop(1, nblk, blk, carry)

        lax.fori_loop(0, dil, residue, 0)

    def finish(i, c):
        rs = pl.ds(pl.multiple_of(i * QB, QB), QB)
        o_ref[rs, :] = (acc_ref[rs, :] / l_ref[rs, :]).astype(o_ref.dtype)
        return c

    lax.fori_loop(0, S // QB, finish, 0)


def _prompt_attention(qa, ka, va):
    B, S, W = qa.shape
    spec = pl.BlockSpec((None, S, LANES), lambda b, hp: (b, 0, hp))
    return pl.pallas_call(
        _attn_kernel,
        grid=(B, W // LANES),
        in_specs=[spec, spec, spec],
        out_specs=spec,
        out_shape=jax.ShapeDtypeStruct((B, S, W), BF16),
        scratch_shapes=[pltpu.VMEM((S, LANES), F32)] * 3,
        compiler_params=_cparams(("parallel", "parallel")),
        name="prompt_attn",
    )(qa, ka, va)


def _split3(x):
    x1 = x.astype(BF16)
    r1 = x - x1.astype(F32)
    x2 = r1.astype(BF16)
    x3 = (r1 - x2.astype(F32)).astype(BF16)
    return x1, x2, x3


def _gla_kernel(q_ref, k_ref, g_ref, v_ref, z_ref, s0_ref, ng_ref, o_ref, sfin_ref, st_ref, *, chunk):
    C = chunk
    TS = q_ref.shape[0]
    n_pairs = N_HEADS_B // 2
    PW = 2 * DV_B
    t_idx = pl.program_id(1)

    lane_k = lax.broadcasted_iota(jnp.int32, (C, LANES), 1)
    head0 = lane_k < DK_B
    tri_incl = (lax.broadcasted_iota(jnp.int32, (C, C), 1)
                <= lax.broadcasted_iota(jnp.int32, (C, C), 0))
    tri_bf = jnp.where(tri_incl, 1.0, 0.0).astype(BF16)
    bd_mask = ((lax.broadcasted_iota(jnp.int32, (PW, LANES), 0) // DV_B)
               == (lax.broadcasted_iota(jnp.int32, (PW, LANES), 1) // DK_B))

    @pl.when(t_idx == 0)
    def _():
        for p in range(n_pairs):
            for h in range(2):
                blk = jnp.transpose(s0_ref[2 * p + h])
                pad = jnp.zeros((DV_B, DK_B), F32)
                row = jnp.concatenate([blk, pad] if h == 0 else [pad, blk], axis=1)
                st_ref[p, h * DV_B:(h + 1) * DV_B, :] = row

    def chunk_body(c, carry):
        rs = pl.ds(pl.multiple_of(c * C, C), C)
        for p in range(n_pairs):
            kl = slice(p * LANES, (p + 1) * LANES)
            vl = slice(p * PW, (p + 1) * PW)
            q = q_ref[rs, kl]
            k = k_ref[rs, kl]
            g = g_ref[rs, kl]
            v = v_ref[rs, vl].astype(BF16)
            g1, g2, g3 = _split3(g)
            b = (jnp.dot(tri_bf, g1, preferred_element_type=F32)
                 + jnp.dot(tri_bf, g2, preferred_element_type=F32)
                 + jnp.dot(tri_bf, g3, preferred_element_type=F32))
            b_last = b[C - 1:C, :]
            b_mid = b[C // 2 - 1:C // 2, :] if C > 1 else b_last
            qe = q * jnp.exp(b - b_mid)
            ke = (k * jnp.exp(b_mid - b)).astype(BF16)
            st = st_ref[p]
            q_in = (q * jnp.exp(b)).astype(BF16)
            o = lax.dot_general(q_in, st.astype(BF16), (((1,), (1,)), ((), ())),
                                preferred_element_type=F32)
            intra = []
            for h in range(2):
                hm = head0 if h == 0 else jnp.logical_not(head0)
                a = lax.dot_general(jnp.where(hm, qe, 0.0).astype(BF16), ke,
                                    (((1,), (1,)), ((), ())), preferred_element_type=F32)
                a = jnp.where(tri_incl, a, 0.0).astype(BF16)
                intra.append(jnp.dot(a, v[:, h * DV_B:(h + 1) * DV_B], preferred_element_type=F32))
            o = o + jnp.concatenate(intra, axis=1)
            k_dec = (k * jnp.exp(b_last - b)).astype(BF16)
            upd = lax.dot_general(v, k_dec, (((0,), (0,)), ((), ())), preferred_element_type=F32)
            st_ref[p] = jnp.exp(b_last) * st + jnp.where(bd_mask, upd, 0.0)
            for h in range(2):
                oh = o[:, h * DV_B:(h + 1) * DV_B]
                hl = slice((2 * p + h) * DV_B, (2 * p + h + 1) * DV_B)
                ms = jnp.mean(oh * oh, axis=-1, keepdims=True)
                z = z_ref[rs, hl]
                gated = (oh * lax.rsqrt(ms + EPS)) * ng_ref[:, hl] * (z / (1.0 + jnp.exp(-z)))
                o_ref[rs, hl] = gated.astype(o_ref.dtype)
        return carry

    lax.fori_loop(0, TS // C, chunk_body, 0)

    @pl.when(t_idx == pl.num_programs(1) - 1)
    def _():
        for p in range(n_pairs):
            for h in range(2):
                blk = st_ref[p, h * DV_B:(h + 1) * DV_B, h * DK_B:(h + 1) * DK_B]
                sfin_ref[2 * p + h] = jnp.transpose(blk)


def _gla(qb, kb, gb, vb, zg, state0, norm_g, ts, chunk):
    B, T, _ = qb.shape
    row = lambda w: pl.BlockSpec((None, ts, w), lambda b, i: (b, i, 0))
    st_spec = pl.BlockSpec((None, N_HEADS_B, DK_B, DV_B), lambda b, i: (b, 0, 0, 0))
    return pl.pallas_call(
        functools.partial(_gla_kernel, chunk=chunk),
        grid=(B, T // ts),
        in_specs=[row(QK_B), row(QK_B), row(QK_B), row(WIDTH_B), row(WIDTH_B), st_spec,
                  pl.BlockSpec((1, WIDTH_B), lambda b, i: (0, 0))],
        out_specs=[row(WIDTH_B), st_spec],
        out_shape=[jax.ShapeDtypeStruct((B, T, WIDTH_B), BF16),
                   jax.ShapeDtypeStruct((B, N_HEADS_B, DK_B, DV_B), F32)],
        scratch_shapes=[pltpu.VMEM((N_HEADS_B // 2, 2 * DV_B, LANES), F32)],
        compiler_params=_cparams(("parallel", "arbitrary")),
        name="gla",
    )(qb, kb, gb, vb, zg, state0, norm_g)


def _sample_attn_kernel(q_ref, kn_ref, vn_ref, kc_ref, vc_ref, o_ref, ko_ref, vo_ref):
    T = q_ref.shape[0]
    R = kc_ref.shape[0]
    W = q_ref.shape[1]
    HT = N_HEADS_A * T
    q = q_ref[...]
    qx = jnp.concatenate([q] * N_HEADS_A, axis=0)
    own = ((lax.broadcasted_iota(jnp.int32, (HT, W), 0) // T)
           == (lax.broadcasted_iota(jnp.int32, (HT, W), 1) // HEAD_DIM_A))
    qx = jnp.where(own, qx, 0.0).astype(BF16)
    pad = jnp.zeros((LANES - T, W), F32)
    kn = jnp.concatenate([kn_ref[...], pad], axis=0).astype(BF16)
    vn = jnp.concatenate([vn_ref[...], pad], axis=0).astype(BF16)
    nt = (((1,), (1,)), ((), ()))
    s_c = lax.dot_general(qx, kc_ref[...].astype(BF16), nt, preferred_element_type=F32)
    s_n = lax.dot_general(qx, kn, nt, preferred_element_type=F32)

    def multiplicity(n_cols, first_row):
        t = lax.broadcasted_iota(jnp.int32, (HT, n_cols), 0) % T
        j = lax.broadcasted_iota(jnp.int32, (HT, n_cols), 1) + first_row
        delta = R + t - j
        cnt = jnp.zeros((HT, n_cols), F32)
        for dil in DILATIONS:
            hit = (delta >= 0) & (delta <= dil * KEYS_PER_CONFIG) & (delta % dil == 0)
            cnt = cnt + jnp.where(hit, 1.0, 0.0)
        return cnt

    cnt_c = multiplicity(R, 0)
    cnt_n = multiplicity(LANES, R)
    s_c = jnp.where(cnt_c > 0.0, s_c, NEG_INF)
    s_n = jnp.where(cnt_n > 0.0, s_n, NEG_INF)
    m = jnp.maximum(jnp.max(s_c, axis=1, keepdims=True), jnp.max(s_n, axis=1, keepdims=True))
    p_c = cnt_c * jnp.exp(s_c - m)
    p_n = cnt_n * jnp.exp(s_n - m)
    den = jnp.sum(p_c, axis=1, keepdims=True) + jnp.sum(p_n, axis=1, keepdims=True)
    full = (jnp.dot(p_c.astype(BF16), vc_ref[...].astype(BF16), preferred_element_type=F32)
            + jnp.dot(p_n.astype(BF16), vn, preferred_element_type=F32)) / den
    full = jnp.where(own, full, 0.0)
    out = full[0:T, :]
    for h in range(1, N_HEADS_A):
        out = out + full[h * T:(h + 1) * T, :]
    o_ref[...] = out.astype(o_ref.dtype)
    ko_ref[0:R - T, :] = kc_ref[T:R, :]
    ko_ref[R - T:R, :] = kn_ref[...]
    vo_ref[0:R - T, :] = vc_ref[T:R, :]
    vo_ref[R - T:R, :] = vn_ref[...]


def _sample_attention(qa, ka, va, cache_k, cache_v):
    B, T, W = qa.shape
    R = cache_k.shape[1]
    assert R >= DILATIONS[-1] * KEYS_PER_CONFIG and T % 8 == 0 and T <= LANES
    new = pl.BlockSpec((None, T, W), lambda b: (b, 0, 0))
    cache = pl.BlockSpec((None, R, W), lambda b: (b, 0, 0))
    return pl.pallas_call(
        _sample_attn_kernel,
        grid=(B,),
        in_specs=[new, new, new, cache, cache],
        out_specs=[new, cache, cache],
        out_shape=[jax.ShapeDtypeStruct((B, T, W), BF16),
                   jax.ShapeDtypeStruct((B, R, W), F32),
                   jax.ShapeDtypeStruct((B, R, W), F32)],
        compiler_params=_cparams(("parallel",)),
        name="sample_attn",
    )(qa, ka, va, cache_k, cache_v)


CHUNKS = 8


def _store_chunked(ref, val):
    n = val.shape[0]
    for s in range(CHUNKS):
        ref[pl.ds(s, n, stride=CHUNKS), :] = val[:, s * LANES:(s + 1) * LANES]


def _load_chunked(ref, n):
    return jnp.concatenate([ref[pl.ds(s, n, stride=CHUNKS), :] for s in range(CHUNKS)], axis=1)


def _split2(x):
    hi = x.astype(BF16)
    return hi, (x - hi.astype(F32)).astype(BF16)


def _merge_kernel(oa_ref, ob_ref, x_ref, wo_ref, g_ref, wr_ref, br_ref,
                  h_ref, xn_ref, idx_ref, gate_ref):
    TM = x_ref.shape[0]
    mixed = (jnp.dot(oa_ref[...], wo_ref[0:WIDTH_A, :], preferred_element_type=F32)
             + jnp.dot(ob_ref[...], wo_ref[WIDTH_A:, :], preferred_element_type=F32))
    h = x_ref[...] + mixed
    h_ref[...] = h
    ms = jnp.mean(h * h, axis=-1, keepdims=True)
    xn = (h * lax.rsqrt(ms + EPS)) * g_ref[...]
    _store_chunked(xn_ref, xn)
    nt = (((1,), (1,)), ((), ()))
    xh, xl = _split2(xn)
    wh, wl = _split2(wr_ref[...])
    logits = (lax.dot_general(wh, xh, nt, preferred_element_type=F32)
              + lax.dot_general(wh, xl, nt, preferred_element_type=F32)
              + lax.dot_general(wl, xh, nt, preferred_element_type=F32)) + br_ref[...]
    e_iota = lax.broadcasted_iota(jnp.int32, (N_EXPERTS, TM), 0)
    vals, idxs = [], []
    for _ in range(TOP_K):
        m = jnp.max(logits, axis=0, keepdims=True)
        sel = jnp.min(jnp.where(logits == m, e_iota, N_EXPERTS), axis=0, keepdims=True)
        vals.append(m)
        idxs.append(sel)
        logits = jnp.where(e_iota == sel, NEG_INF, logits)
    ex = [jnp.exp(v - vals[0]) for v in vals]
    den = ex[0] + ex[1] + ex[2] + ex[3]
    idx_ref[...] = jnp.concatenate(idxs, axis=0)
    gate_ref[...] = jnp.concatenate([e / den for e in ex], axis=0)


def _merge(oa, ob, x, w_out, norm_g, w_router_t, b_router, tm):
    N, D = x.shape
    full = lambda a: pl.BlockSpec(a.shape, lambda i: (0,) * a.ndim)
    row = lambda w: pl.BlockSpec((tm, w), lambda i: (i, 0))
    col = pl.BlockSpec((TOP_K, tm), lambda i: (0, i))
    return pl.pallas_call(
        _merge_kernel,
        grid=(N // tm,),
        in_specs=[row(WIDTH_A), row(WIDTH_B), row(D), full(w_out), full(norm_g), full(w_router_t),
                  full(b_router)],
        out_specs=[row(D), pl.BlockSpec((tm * CHUNKS, LANES), lambda i: (i, 0)), col, col],
        out_shape=[jax.ShapeDtypeStruct((N, D), F32),
                   jax.ShapeDtypeStruct((N * CHUNKS, LANES), F32),
                   jax.ShapeDtypeStruct((TOP_K, N), jnp.int32),
                   jax.ShapeDtypeStruct((TOP_K, N), F32)],
        compiler_params=_cparams(("parallel",)),
        name="merge_router",
    )(oa, ob, x, w_out, norm_g, w_router_t, b_router)


MOE_ROWS = 256
ROUTE_TILE = 256


def _route_kernel(idx_ref, dest_ref, be_ref, nused_ref, tot_ref, carry_ref, start_ref):
    ph = pl.program_id(0)
    i = pl.program_id(1)
    TT = idx_ref.shape[1]
    NBP = be_ref.shape[1]
    idx = idx_ref[...]
    e_iota = lax.broadcasted_iota(jnp.int32, (N_EXPERTS, TT), 0)
    onehot = [idx[k:k + 1, :] == e_iota for k in range(TOP_K)]
    cnt = jnp.zeros((N_EXPERTS, TT), F32)
    for oh in onehot:
        cnt = cnt + jnp.where(oh, 1.0, 0.0)
    tile_tot = jnp.sum(cnt, axis=1, keepdims=True)

    @pl.when(jnp.logical_and(ph == 0, i == 0))
    def _():
        tot_ref[...] = jnp.zeros_like(tot_ref)

    @pl.when(ph == 0)
    def _():
        tot_ref[...] += tile_tot

    @pl.when(jnp.logical_and(ph == 1, i == 0))
    def _():
        tot = tot_ref[...]
        padded = jnp.floor((tot + (MOE_ROWS - 1)) / MOE_ROWS) * MOE_ROWS
        r_i = lax.broadcasted_iota(jnp.int32, (N_EXPERTS, N_EXPERTS), 0)
        c_i = lax.broadcasted_iota(jnp.int32, (N_EXPERTS, N_EXPERTS), 1)
        padded_row = jnp.sum(jnp.where(r_i == c_i, padded, 0.0), axis=0, keepdims=True)
        start = jnp.sum(jnp.where(c_i < r_i, padded_row, 0.0), axis=1, keepdims=True)
        start_ref[...] = start
        carry_ref[...] = jnp.zeros_like(carry_ref)
        end = start + padded
        block_start = lax.broadcasted_iota(jnp.int32, (N_EXPERTS, NBP), 1).astype(F32) * MOE_ROWS
        be = jnp.sum(jnp.where(end <= block_start, 1.0, 0.0), axis=0, keepdims=True)
        be_ref[...] = jnp.minimum(be, N_EXPERTS - 1).astype(jnp.int32)
        total = jnp.sum(padded, axis=0, keepdims=True) / MOE_ROWS
        nused_ref[...] = jnp.broadcast_to(total, nused_ref.shape).astype(jnp.int32)

    @pl.when(ph == 1)
    def _():
        earlier = (lax.broadcasted_iota(jnp.int32, (TT, TT), 0)
                   < lax.broadcasted_iota(jnp.int32, (TT, TT), 1))
        before = jnp.dot(cnt.astype(BF16), jnp.where(earlier, 1.0, 0.0).astype(BF16),
                         preferred_element_type=F32)
        slot = start_ref[...] + carry_ref[...] + before
        rows = [jnp.sum(jnp.where(oh, slot, 0.0), axis=0, keepdims=True) for oh in onehot]
        dest_ref[...] = jnp.concatenate(rows, axis=0).astype(jnp.int32)
        carry_ref[...] += tile_tot


def _route(idx, n_blocks):
    _, N = idx.shape
    nbp = -(-n_blocks // LANES) * LANES
    tile = pl.BlockSpec((TOP_K, ROUTE_TILE), lambda ph, i: (0, i))
    return pl.pallas_call(
        _route_kernel,
        grid=(2, N // ROUTE_TILE),
        in_specs=[tile],
        out_specs=[pl.BlockSpec((TOP_K, ROUTE_TILE), lambda ph, i: (0, i * ph)),
                   pl.BlockSpec((1, nbp), lambda ph, i: (0, 0)),
                   pl.BlockSpec((1, LANES), lambda ph, i: (0, 0))],
        out_shape=[jax.ShapeDtypeStruct((TOP_K, N), jnp.int32),
                   jax.ShapeDtypeStruct((1, nbp), jnp.int32),
                   jax.ShapeDtypeStruct((1, LANES), jnp.int32)],
        scratch_shapes=[pltpu.VMEM((N_EXPERTS, 1), F32)] * 3,
        compiler_params=_cparams(("arbitrary", "arbitrary")),
        name="route",
    )(idx)


TOKEN_TILE = 256


def _token_rows(ref, t):
    return ref.at[pl.ds(pl.multiple_of(t * CHUNKS, CHUNKS), CHUNKS), :]


def _dispatch_kernel(dest_ref, x_ref, xs_in_ref, xs_ref, sem):
    del xs_in_ref
    TD = dest_ref.shape[1]

    def issue(t, c):
        for k in range(TOP_K):
            pltpu.make_async_copy(_token_rows(x_ref, t), _token_rows(xs_ref, dest_ref[k, t]), sem).start()
        return c

    lax.fori_loop(0, TD, issue, 0)
    for k in range(TOP_K):
        pltpu.make_async_copy(x_ref, xs_ref.at[pl.ds(0, TD * CHUNKS), :], sem).wait()


def _dispatch(dest_tiles, xn, xs):
    n_tiles, _, TD = dest_tiles.shape
    return pl.pallas_call(
        _dispatch_kernel,
        grid=(n_tiles,),
        in_specs=[pl.BlockSpec((None, TOP_K, TD), lambda i: (i, 0, 0), memory_space=pltpu.SMEM),
                  pl.BlockSpec((TD * CHUNKS, LANES), lambda i: (i, 0)),
                  pl.BlockSpec(memory_space=pl.ANY)],
        out_specs=pl.BlockSpec(memory_space=pl.ANY),
        out_shape=jax.ShapeDtypeStruct(xs.shape, xs.dtype),
        scratch_shapes=[pltpu.SemaphoreType.DMA(())],
        input_output_aliases={2: 0},
        compiler_params=_cparams(("arbitrary",)),
        name="dispatch",
    )(dest_tiles, xn, xs)


def _expert_kernel(be_ref, nused_ref, xs_ref, wgu_ref, bgu_ref, wd_ref, bd_ref, ys_ref,
                   wgu_bf, wd_bf):
    i = pl.program_id(0)
    D_FF = wd_ref.shape[0]
    new_expert = jnp.logical_or(i == 0, be_ref[i] != be_ref[jnp.maximum(i - 1, 0)])

    @pl.when(jnp.logical_and(i < nused_ref[0], new_expert))
    def _():
        wgu_bf[...] = wgu_ref[...].astype(BF16)
        wd_bf[...] = wd_ref[...].astype(BF16)

    @pl.when(i < nused_ref[0])
    def _():
        x = _load_chunked(xs_ref, MOE_ROWS).astype(BF16)
        hdn = jnp.dot(x, wgu_bf[...], preferred_element_type=F32) + bgu_ref[...]
        glu = jnp.minimum(hdn[:, :D_FF], SWIGLU_LIMIT)
        lin = jnp.clip(hdn[:, D_FF:], -SWIGLU_LIMIT, SWIGLU_LIMIT)
        act = glu * (1.0 / (1.0 + jnp.exp(-SWIGLU_ALPHA * glu))) * (lin + 1.0)
        y = jnp.dot(act.astype(BF16), wd_bf[...], preferred_element_type=F32) + bd_ref[...]
        _store_chunked(ys_ref, y)

    @pl.when(i >= nused_ref[0])
    def _():
        ys_ref[...] = jnp.zeros_like(ys_ref)


def _experts(block_expert, n_used, xs, w_gate_up, b_gate_up, w_down, b_down, n_blocks):
    E, D, F2 = w_gate_up.shape
    D_FF = w_down.shape[1]
    rows = pl.BlockSpec((MOE_ROWS * CHUNKS, LANES), lambda i, be, nu: (i, 0))
    grid_spec = pltpu.PrefetchScalarGridSpec(
        num_scalar_prefetch=2,
        grid=(n_blocks,),
        in_specs=[rows,
                  pl.BlockSpec((None, D, F2), lambda i, be, nu: (be[i], 0, 0)),
                  pl.BlockSpec((None, 1, F2), lambda i, be, nu: (be[i], 0, 0)),
                  pl.BlockSpec((None, D_FF, D), lambda i, be, nu: (be[i], 0, 0)),
                  pl.BlockSpec((None, 1, D), lambda i, be, nu: (be[i], 0, 0))],
        out_specs=rows,
        scratch_shapes=[pltpu.VMEM((D, F2), BF16), pltpu.VMEM((D_FF, D), BF16)],
    )
    return pl.pallas_call(
        _expert_kernel,
        grid_spec=grid_spec,
        out_shape=jax.ShapeDtypeStruct(xs.shape, F32),
        compiler_params=_cparams(("arbitrary",)),
        name="experts",
    )(block_expert, n_used, xs, w_gate_up, b_gate_up.reshape(E, 1, F2), w_down, b_down.reshape(E, 1, D))


def _combine_kernel(dest_ref, gate_ref, h_ref, g_ref, ys_ref, y_ref, buf, sem):
    TC = dest_ref.shape[1]

    def issue(t, c):
        for k in range(TOP_K):
            pltpu.make_async_copy(_token_rows(ys_ref, dest_ref[k, t]), _token_rows(buf.at[k], t), sem).start()
        return c

    lax.fori_loop(0, TC, issue, 0)
    gates = jnp.concatenate([gate_ref[...], jnp.zeros((8 - TOP_K, TC), F32)], axis=0)
    gates_t = jnp.transpose(gates)
    for k in range(TOP_K):
        pltpu.make_async_copy(ys_ref.at[pl.ds(0, TC * CHUNKS), :], buf.at[k], sem).wait()
    moe = gates_t[:, 0:1] * _load_chunked(buf.at[0], TC)
    for k in range(1, TOP_K):
        moe = moe + gates_t[:, k:k + 1] * _load_chunked(buf.at[k], TC)
    hf = h_ref[...] + moe
    ms = jnp.mean(hf * hf, axis=-1, keepdims=True)
    y_ref[...] = (hf * lax.rsqrt(ms + EPS)) * g_ref[...]


def _combine(dest_tiles, gates, h, norm_g, ys):
    n_tiles, _, TC = dest_tiles.shape
    N, D = h.shape
    return pl.pallas_call(
        _combine_kernel,
        grid=(n_tiles,),
        in_specs=[pl.BlockSpec((None, TOP_K, TC), lambda i: (i, 0, 0), memory_space=pltpu.SMEM),
                  pl.BlockSpec((TOP_K, TC), lambda i: (0, i)),
                  pl.BlockSpec((TC, D), lambda i: (i, 0)),
                  pl.BlockSpec((1, D), lambda i: (0, 0)),
                  pl.BlockSpec(memory_space=pl.ANY)],
        out_specs=pl.BlockSpec((TC, D), lambda i: (i, 0)),
        out_shape=jax.ShapeDtypeStruct((N, D), F32),
        scratch_shapes=[pltpu.VMEM((TOP_K, TC * CHUNKS, LANES), F32), pltpu.SemaphoreType.DMA(())],
        compiler_params=_cparams(("arbitrary",)),
        name="combine",
    )(dest_tiles, gates, h, norm_g, ys)


def _prep_weights(w_in, w_alpha):
    w_main = w_in[:, :PROJ_MAIN].astype(BF16)
    w_lr = jnp.pad(w_in[:, PROJ_MAIN:], ((0, 0), (0, LANES - GATE_RANK))).astype(BF16)
    w_al = jnp.pad(w_alpha, ((0, LANES - GATE_RANK), (0, 0))).astype(BF16)
    return w_main, w_lr, w_al


def kernel(x_prompt, x_sample, cache_swa_k, cache_swa_v, state_gla, norm_mix_g, w_in, w_alpha, b_alpha, gla_norm_g, w_out, norm_ffn_g, w_router, b_router, w_gate_up, b_gate_up, w_down, b_down, norm_final_g):
    B, S, D = x_prompt.shape
    Bs, Ts, _ = x_sample.shape
    assert w_in.shape[0] == 1, "single-layer trunk"
    l = 0
    R = cache_swa_k.shape[2]
    rows_p = min(DILATIONS[-1] * KEYS_PER_CONFIG, S)
    w_main, w_lr, w_al = _prep_weights(w_in[l], w_alpha[l])
    g_mix = norm_mix_g[l][None]
    b_al = b_alpha[l][None]
    g_gla = gla_norm_g[l][None]

    pos_p = jnp.arange(S, dtype=jnp.int32)
    qa, ka, va, qb, kb, vb, zg, gb = _project(x_prompt, pos_p, g_mix, w_main, w_lr, w_al, b_al, PROJ_TILE)
    oa_p = _prompt_attention(qa, ka, va)
    ob_p, st_p = _gla(qb, kb, gb, vb, zg, jnp.zeros((B, N_HEADS_B, DK_B, DV_B), F32), g_gla,
                      GLA_TILE, GLA_CHUNK)
    k_prompt = ka[:, S - rows_p:].reshape(1, B, rows_p, N_HEADS_A, HEAD_DIM_A)
    v_prompt = va[:, S - rows_p:].reshape(1, B, rows_p, N_HEADS_A, HEAD_DIM_A)

    pos_s = PAST_LEN + (jnp.arange(Bs * Ts, dtype=jnp.int32) % Ts)
    proj_s = _project(x_sample.reshape(1, Bs * Ts, D), pos_s, g_mix, w_main, w_lr, w_al, b_al, Bs * Ts)
    qa_s, ka_s, va_s, qb_s, kb_s, vb_s, zg_s, gb_s = [t.reshape(Bs, Ts, -1) for t in proj_s]
    oa_s, k_sample, v_sample = _sample_attention(qa_s, ka_s, va_s,
                                                 cache_swa_k[l].reshape(Bs, R, WIDTH_A),
                                                 cache_swa_v[l].reshape(Bs, R, WIDTH_A))
    ob_s, st_s = _gla(qb_s, kb_s, gb_s, vb_s, zg_s, state_gla[l], g_gla, Ts, Ts)

    w_out_bf = w_out[l].astype(BF16)
    g_ffn = norm_ffn_g[l][None]
    w_router_t = jnp.transpose(w_router[l])
    b_router_c = b_router[l][:, None]
    Np, Ns = B * S, Bs * Ts
    h_p, xn_p, idx_p, gate_p = _merge(oa_p.reshape(Np, WIDTH_A), ob_p.reshape(Np, WIDTH_B),
                                      x_prompt.reshape(Np, D), w_out_bf, g_ffn, w_router_t, b_router_c,
                                      MERGE_TILE)
    h_s, xn_s, idx_s, gate_s = _merge(oa_s.reshape(Ns, WIDTH_A), ob_s.reshape(Ns, WIDTH_B),
                                      x_sample.reshape(Ns, D), w_out_bf, g_ffn, w_router_t, b_router_c,
                                      Ns)

    y_p, y_s = _moe([(xn_p, idx_p, gate_p, h_p), (xn_s, idx_s, gate_s, h_s)],
                    w_gate_up[l], b_gate_up[l], w_down[l], b_down[l], norm_final_g[None])
    return (y_p.reshape(B, S, D), y_s.reshape(Bs, Ts, D), k_prompt, v_prompt, st_p[None],
            k_sample.reshape(1, Bs, R, N_HEADS_A, HEAD_DIM_A),
            v_sample.reshape(1, Bs, R, N_HEADS_A, HEAD_DIM_A), st_s[None])


PAST_LEN = 16384
PROJ_TILE = 512
MERGE_TILE = 512
GLA_TILE = 1024
GLA_CHUNK = 64


def _moe(groups, w_gate_up, b_gate_up, w_down, b_down, g_final):
    sizes = [g[3].shape[0] for g in groups]
    N = sum(sizes)
    assert all(n % TOKEN_TILE == 0 for n in sizes) and TOKEN_TILE == ROUTE_TILE
    n_blocks = -(-(N * TOP_K + N_EXPERTS * (MOE_ROWS - 1)) // MOE_ROWS)
    idx = jnp.concatenate([g[1] for g in groups], axis=1)
    dest, block_expert, n_used = _route(idx, n_blocks)
    dest_tiles = dest.reshape(TOP_K, N // TOKEN_TILE, TOKEN_TILE).transpose(1, 0, 2)
    xs = jnp.zeros((n_blocks * MOE_ROWS * CHUNKS, LANES), F32)
    tile0 = 0
    spans = []
    for (xn, _, _, _), n in zip(groups, sizes):
        spans.append((tile0, tile0 + n // TOKEN_TILE))
        xs = _dispatch(dest_tiles[spans[-1][0]:spans[-1][1]], xn, xs)
        tile0 = spans[-1][1]
    ys = _experts(block_expert[0], n_used[0, :1], xs, w_gate_up, b_gate_up, w_down, b_down, n_blocks)
    return [_combine(dest_tiles[a:b], gates, h, g_final, ys)
            for (a, b), (_, _, gates, h) in zip(spans, groups)]
```

```python
import functools

import jax
import jax.numpy as jnp
import numpy as np
from jax import lax
from jax.experimental import pallas as pl
from jax.experimental.pallas import tpu as pltpu

F32 = jnp.float32
BF16 = jnp.bfloat16

N_HEADS_A = 8
HEAD_DIM_A = 64
WIDTH_A = N_HEADS_A * HEAD_DIM_A
N_HEADS_B = 4
DK_B = 64
DV_B = 128
QK_B = N_HEADS_B * DK_B
WIDTH_B = N_HEADS_B * DV_B
GATE_RANK = 16
GATE_LOGIT_NORM = 16.0
DILATIONS = (1, 4, 16)
KEYS_PER_CONFIG = 128
ROPE_THETA = 10000.0
N_EXPERTS = 32
TOP_K = 4
SWIGLU_ALPHA = 1.702
SWIGLU_LIMIT = 7.0
EPS = 1e-6

LANES = 128
VMEM_LIMIT = 56 * 1024 * 1024


def _cparams(sem, vmem=VMEM_LIMIT):
    return pltpu.CompilerParams(dimension_semantics=sem, vmem_limit_bytes=vmem)


PROJ_MAIN = 3 * WIDTH_A + 2 * QK_B + 2 * WIDTH_B


def _rope_tables(pos):
    half = HEAD_DIM_A // 2
    inv = ROPE_THETA ** (-jnp.arange(half, dtype=F32) / half)
    ang = pos.astype(F32)[:, None] * inv[None, :]
    cos = jnp.cos(ang)
    sin = jnp.sin(ang)
    cos_t = jnp.concatenate([cos, cos, cos, cos], axis=-1)
    sin_t = jnp.concatenate([-sin, sin, -sin, sin], axis=-1)
    return cos_t, sin_t


def _rope_block(t, cos, sin, first_half):
    partner = jnp.where(first_half, pltpu.roll(t, LANES - 32, 1), pltpu.roll(t, 32, 1))
    return t * cos + partner * sin


def _proj_kernel(x_ref, g_ref, w_ref, wlr_ref, wa_ref, ba_ref, cos_ref, sin_ref,
                 qa_ref, ka_ref, va_ref, qb_ref, kb_ref, vb_ref, zg_ref, gb_ref):
    x = x_ref[...]
    ms = jnp.mean(x * x, axis=-1, keepdims=True)
    xn = ((x * lax.rsqrt(ms + EPS)) * g_ref[...]).astype(BF16)

    def cols(lo, hi):
        return jnp.dot(xn, w_ref[:, lo:hi], preferred_element_type=F32)

    cos = cos_ref[...]
    sin = sin_ref[...]
    lane = lax.broadcasted_iota(jnp.int32, cos.shape, 1)
    first_half = (lane % HEAD_DIM_A) < (HEAD_DIM_A // 2)
    for j in range(WIDTH_A // LANES):
        lo = j * LANES
        q = cols(lo, lo + LANES)
        qa_ref[:, lo:lo + LANES] = _rope_block(q, cos, sin, first_half) * (HEAD_DIM_A ** -0.5)
        k = cols(WIDTH_A + lo, WIDTH_A + lo + LANES)
        ka_ref[:, lo:lo + LANES] = _rope_block(k, cos, sin, first_half)
    o = 2 * WIDTH_A
    va_ref[...] = cols(o, o + WIDTH_A)
    o += WIDTH_A
    qb_ref[...] = cols(o, o + QK_B) * (DK_B ** -0.5)
    o += QK_B
    kb_ref[...] = cols(o, o + QK_B)
    o += QK_B
    vb_ref[...] = cols(o, o + WIDTH_B)
    o += WIDTH_B
    zg_ref[...] = cols(o, o + WIDTH_B)
    lr = jnp.dot(xn, wlr_ref[...], preferred_element_type=F32)
    z = jnp.dot(lr.astype(BF16), wa_ref[...], preferred_element_type=F32) + ba_ref[...]
    logsig = jnp.minimum(z, 0.0) - jnp.log(1.0 + jnp.exp(-jnp.abs(z)))
    gb_ref[...] = logsig / GATE_LOGIT_NORM


def _project(x, pos, norm_g, w_main, w_lr, w_alpha, b_alpha, tm):
    B, T, D = x.shape
    cos_t, sin_t = _rope_tables(pos)
    grid = (B, T // tm)
    row = lambda w: pl.BlockSpec((None, tm, w), lambda b, i: (b, i, 0))
    full = lambda a: pl.BlockSpec(a.shape, lambda b, i: (0,) * a.ndim)
    tab = pl.BlockSpec((tm, LANES), lambda b, i: (i, 0))
    widths = (WIDTH_A, WIDTH_A, WIDTH_A, QK_B, QK_B, WIDTH_B, WIDTH_B, QK_B)
    return pl.pallas_call(
        _proj_kernel,
        grid=grid,
        in_specs=[row(D), full(norm_g), full(w_main), full(w_lr), full(w_alpha), full(b_alpha), tab, tab],
        out_specs=[row(w) for w in widths],
        out_shape=[jax.ShapeDtypeStruct((B, T, w), F32) for w in widths],
        compiler_params=_cparams(("parallel", "parallel")),
        name="proj",
    )(x, norm_g, w_main, w_lr, w_alpha, b_alpha, cos_t, sin_t)


Q_BLOCK = 128
NEG_INF = float("-inf")


def _attn_block(q, k, v, mask, head0, state):
    kb = k.astype(BF16)
    vb = v.astype(BF16)
    v_head0 = lax.broadcasted_iota(jnp.int32, vb.shape, 1) < HEAD_DIM_A
    m_new, pvs = [], []
    for h in range(2):
        hm = head0 if h == 0 else jnp.logical_not(head0)
        qh = jnp.where(hm, q, 0.0).astype(BF16)
        s = lax.dot_general(qh, kb, (((1,), (1,)), ((), ())), preferred_element_type=F32)
        s = jnp.where(mask, s, NEG_INF)
        if state is None:
            mh = jnp.max(s, axis=1, keepdims=True)
        else:
            prev = jnp.where(hm, state[0], NEG_INF)
            mh = jnp.max(jnp.concatenate([s, prev], axis=1), axis=1, keepdims=True)
        p = jnp.exp(s - mh)
        m_new.append(mh)
        vh = jnp.where(v_head0 if h == 0 else jnp.logical_not(v_head0), vb, jnp.ones_like(vb))
        pvs.append(jnp.dot(p.astype(BF16), vh, preferred_element_type=F32))
    m_full = jnp.where(head0, m_new[0], m_new[1])
    l_swapped = jnp.where(head0, pvs[1], pvs[0])
    pv_full = jnp.where(head0, pvs[0], pvs[1])
    if state is None:
        return m_full, l_swapped, pv_full
    a = jnp.exp(state[0] - m_full)
    a_swapped = pltpu.roll(a, HEAD_DIM_A, 1)
    return m_full, a_swapped * state[1] + l_swapped, a * state[2] + pv_full


def _attn_kernel(q_ref, k_ref, v_ref, o_ref, m_ref, l_ref, acc_ref):
    S = q_ref.shape[0]
    QB = Q_BLOCK
    lane = lax.broadcasted_iota(jnp.int32, (QB, LANES), 1)
    head0 = lane < HEAD_DIM_A
    qq = lax.broadcasted_iota(jnp.int32, (QB, 2 * QB), 0)
    kk = lax.broadcasted_iota(jnp.int32, (QB, 2 * QB), 1)
    band = jnp.logical_and(kk >= qq, kk - qq <= KEYS_PER_CONFIG)
    causal = (lax.broadcasted_iota(jnp.int32, (QB, QB), 1)
              <= lax.broadcasted_iota(jnp.int32, (QB, QB), 0))

    GROUP = 4

    for ci, dil in enumerate(DILATIONS):
        nblk = S // (dil * QB)
        assert nblk % GROUP == 0 or GROUP % nblk == 0

        def rows(start, n, dil=dil):
            return pl.ds(start, n) if dil == 1 else pl.ds(start, n, stride=dil)

        def do_group(blocks, ci=ci, dil=dil, rows=rows):
            loaded = []
            for r, j, first in blocks:
                q0 = r + dil * QB * j
                qs = rows(q0, QB)
                ks = rows(r, QB) if first else rows(q0 - dil * QB, 2 * QB)
                state = None if ci == 0 else (m_ref[qs, :], l_ref[qs, :], acc_ref[qs, :])
                loaded.append((qs, q_ref[qs, :], k_ref[ks, :], v_ref[ks, :], causal if first else band, state))
            results = [_attn_block(q, k, v, mask, head0, state) for _, q, k, v, mask, state in loaded]
            for (qs, *_), (m, l, acc) in zip(loaded, results):
                m_ref[qs, :] = m
                l_ref[qs, :] = l
                acc_ref[qs, :] = acc

        if nblk >= GROUP:
            def residue(r, carry, nblk=nblk, do_group=do_group):
                do_group([(r, j, j == 0) for j in range(GROUP)])

                def rest(g, c):
                    do_group([(r, GROUP * g + u, False) for u in range(GROUP)])
                    return c

                return lax.fori_loop(1, nblk // GROUP, rest, carry)

            lax.fori_loop(0, dil, residue, 0)
        else:
            per = GROUP // nblk

            def residues(g, carry, nblk=nblk, per=per, do_group=do_group):
                do_group([(g * per + i, j, j == 0) for i in range(per) for j in range(nblk)])
                return carry

            lax.fori_loop(0, dil // per, residues, 0)

    def finish(i, c):
        rs = pl.ds(pl.multiple_of(i * QB, QB), QB)
        den = pltpu.roll(l_ref[rs, :], HEAD_DIM_A, 1)
        o_ref[rs, :] = (acc_ref[rs, :] / den).astype(o_ref.dtype)
        return c

    lax.fori_loop(0, S // QB, finish, 0, unroll=4)


def _prompt_attention(qa, ka, va):
    B, S, W = qa.shape
    spec = pl.BlockSpec((None, S, LANES), lambda b, hp: (b, 0, hp))
    return pl.pallas_call(
        _attn_kernel,
        grid=(B, W // LANES),
        in_specs=[spec, spec, spec],
        out_specs=spec,
        out_shape=jax.ShapeDtypeStruct((B, S, W), BF16),
        scratch_shapes=[pltpu.VMEM((S, LANES), F32)] * 3,
        compiler_params=_cparams(("parallel", "parallel")),
        name="prompt_attn",
    )(qa, ka, va)


def _split3(x):
    x1 = x.astype(BF16)
    r1 = x - x1.astype(F32)
    x2 = r1.astype(BF16)
    x3 = (r1 - x2.astype(F32)).astype(BF16)
    return x1, x2, x3


def _gla_kernel(q_ref, k_ref, g_ref, v_ref, z_ref, s0_ref, ng_ref, o_ref, sfin_ref, st_ref, *, chunk):
    C = chunk
    TS = q_ref.shape[0]
    n_pairs = N_HEADS_B // 2
    PW = 2 * DV_B
    t_idx = pl.program_id(1)

    lane_k = lax.broadcasted_iota(jnp.int32, (C, LANES), 1)
    head0 = lane_k < DK_B
    tri_incl = (lax.broadcasted_iota(jnp.int32, (C, C), 1)
                <= lax.broadcasted_iota(jnp.int32, (C, C), 0))
    tri_bf = jnp.where(tri_incl, 1.0, 0.0).astype(BF16)
    bd_mask = ((lax.broadcasted_iota(jnp.int32, (PW, LANES), 0) // DV_B)
               == (lax.broadcasted_iota(jnp.int32, (PW, LANES), 1) // DK_B))

    @pl.when(t_idx == 0)
    def _():
        for p in range(n_pairs):
            for h in range(2):
                blk = jnp.transpose(s0_ref[2 * p + h])
                pad = jnp.zeros((DV_B, DK_B), F32)
                row = jnp.concatenate([blk, pad] if h == 0 else [pad, blk], axis=1)
                st_ref[p, h * DV_B:(h + 1) * DV_B, :] = row

    def chunk_body(c, carry):
        rs = pl.ds(pl.multiple_of(c * C, C), C)
        for p in range(n_pairs):
            kl = slice(p * LANES, (p + 1) * LANES)
            vl = slice(p * PW, (p + 1) * PW)
            q = q_ref[rs, kl]
            k = k_ref[rs, kl]
            g = g_ref[rs, kl]
            v = v_ref[rs, vl].astype(BF16)
            g1, g2, g3 = _split3(g)
            b = (jnp.dot(tri_bf, g1, preferred_element_type=F32)
                 + jnp.dot(tri_bf, g2, preferred_element_type=F32)
                 + jnp.dot(tri_bf, g3, preferred_element_type=F32))
            b_last = b[C - 1:C, :]
            b_mid = b[C // 2 - 1:C // 2, :] if C > 1 else b_last
            qe = q * jnp.exp(b - b_mid)
            ke = (k * jnp.exp(b_mid - b)).astype(BF16)
            st = st_ref[p]
            q_in = (q * jnp.exp(b)).astype(BF16)
            o = lax.dot_general(q_in, st.astype(BF16), (((1,), (1,)), ((), ())),
                                preferred_element_type=F32)
            intra = []
            for h in range(2):
                hm = head0 if h == 0 else jnp.logical_not(head0)
                a = lax.dot_general(jnp.where(hm, qe, 0.0).astype(BF16), ke,
                                    (((1,), (1,)), ((), ())), preferred_element_type=F32)
                a = jnp.where(tri_incl, a, 0.0).astype(BF16)
                intra.append(jnp.dot(a, v[:, h * DV_B:(h + 1) * DV_B], preferred_element_type=F32))
            o = o + jnp.concatenate(intra, axis=1)
            k_dec = (k * jnp.exp(b_last - b)).astype(BF16)
            upd = lax.dot_general(v, k_dec, (((0,), (0,)), ((), ())), preferred_element_type=F32)
            st_ref[p] = jnp.exp(b_last) * st + jnp.where(bd_mask, upd, 0.0)
            for h in range(2):
                oh = o[:, h * DV_B:(h + 1) * DV_B]
                hl = slice((2 * p + h) * DV_B, (2 * p + h + 1) * DV_B)
                ms = jnp.mean(oh * oh, axis=-1, keepdims=True)
                z = z_ref[rs, hl]
                gated = (oh * lax.rsqrt(ms + EPS)) * ng_ref[:, hl] * (z / (1.0 + jnp.exp(-z)))
                o_ref[rs, hl] = gated.astype(o_ref.dtype)
        return carry

    lax.fori_loop(0, TS // C, chunk_body, 0)

    @pl.when(t_idx == pl.num_programs(1) - 1)
    def _():
        for p in range(n_pairs):
            for h in range(2):
                blk = st_ref[p, h * DV_B:(h + 1) * DV_B, h * DK_B:(h + 1) * DK_B]
                sfin_ref[2 * p + h] = jnp.transpose(blk)


def _gla(qb, kb, gb, vb, zg, state0, norm_g, ts, chunk):
    B, T, _ = qb.shape
    row = lambda w: pl.BlockSpec((None, ts, w), lambda b, i: (b, i, 0))
    st_spec = pl.BlockSpec((None, N_HEADS_B, DK_B, DV_B), lambda b, i: (b, 0, 0, 0))
    return pl.pallas_call(
        functools.partial(_gla_kernel, chunk=chunk),
        grid=(B, T // ts),
        in_specs=[row(QK_B), row(QK_B), row(QK_B), row(WIDTH_B), row(WIDTH_B), st_spec,
                  pl.BlockSpec((1, WIDTH_B), lambda b, i: (0, 0))],
        out_specs=[row(WIDTH_B), st_spec],
        out_shape=[jax.ShapeDtypeStruct((B, T, WIDTH_B), BF16),
                   jax.ShapeDtypeStruct((B, N_HEADS_B, DK_B, DV_B), F32)],
        scratch_shapes=[pltpu.VMEM((N_HEADS_B // 2, 2 * DV_B, LANES), F32)],
        compiler_params=_cparams(("parallel", "arbitrary")),
        name="gla",
    )(qb, kb, gb, vb, zg, state0, norm_g)


def _sample_attn_kernel(q_ref, kn_ref, vn_ref, kc_ref, vc_ref, o_ref, ko_ref, vo_ref):
    T = q_ref.shape[0]
    R = kc_ref.shape[0]
    W = q_ref.shape[1]
    HT = N_HEADS_A * T
    q = q_ref[...]
    qx = jnp.concatenate([q] * N_HEADS_A, axis=0)
    own = ((lax.broadcasted_iota(jnp.int32, (HT, W), 0) // T)
           == (lax.broadcasted_iota(jnp.int32, (HT, W), 1) // HEAD_DIM_A))
    qx = jnp.where(own, qx, 0.0).astype(BF16)
    pad = jnp.zeros((LANES - T, W), F32)
    kn = jnp.concatenate([kn_ref[...], pad], axis=0).astype(BF16)
    vn = jnp.concatenate([vn_ref[...], pad], axis=0).astype(BF16)
    nt = (((1,), (1,)), ((), ()))
    s_c = lax.dot_general(qx, kc_ref[...].astype(BF16), nt, preferred_element_type=F32)
    s_n = lax.dot_general(qx, kn, nt, preferred_element_type=F32)

    def multiplicity(n_cols, first_row):
        t = lax.broadcasted_iota(jnp.int32, (HT, n_cols), 0) % T
        j = lax.broadcasted_iota(jnp.int32, (HT, n_cols), 1) + first_row
        delta = R + t - j
        cnt = jnp.zeros((HT, n_cols), F32)
        for dil in DILATIONS:
            hit = (delta >= 0) & (delta <= dil * KEYS_PER_CONFIG) & (delta % dil == 0)
            cnt = cnt + jnp.where(hit, 1.0, 0.0)
        return cnt

    cnt_c = multiplicity(R, 0)
    cnt_n = multiplicity(LANES, R)
    s_c = jnp.where(cnt_c > 0.0, s_c, NEG_INF)
    s_n = jnp.where(cnt_n > 0.0, s_n, NEG_INF)
    m = jnp.maximum(jnp.max(s_c, axis=1, keepdims=True), jnp.max(s_n, axis=1, keepdims=True))
    p_c = cnt_c * jnp.exp(s_c - m)
    p_n = cnt_n * jnp.exp(s_n - m)
    den = jnp.sum(p_c, axis=1, keepdims=True) + jnp.sum(p_n, axis=1, keepdims=True)
    full = (jnp.dot(p_c.astype(BF16), vc_ref[...].astype(BF16), preferred_element_type=F32)
            + jnp.dot(p_n.astype(BF16), vn, preferred_element_type=F32)) / den
    full = jnp.where(own, full, 0.0)
    out = full[0:T, :]
    for h in range(1, N_HEADS_A):
        out = out + full[h * T:(h + 1) * T, :]
    o_ref[...] = out.astype(o_ref.dtype)
    ko_ref[0:R - T, :] = kc_ref[T:R, :]
    ko_ref[R - T:R, :] = kn_ref[...]
    vo_ref[0:R - T, :] = vc_ref[T:R, :]
    vo_ref[R - T:R, :] = vn_ref[...]


def _sample_attention(qa, ka, va, cache_k, cache_v):
    B, T, W = qa.shape
    R = cache_k.shape[1]
    assert R >= DILATIONS[-1] * KEYS_PER_CONFIG and T % 8 == 0 and T <= LANES
    new = pl.BlockSpec((None, T, W), lambda b: (b, 0, 0))
    cache = pl.BlockSpec((None, R, W), lambda b: (b, 0, 0))
    return pl.pallas_call(
        _sample_attn_kernel,
        grid=(B,),
        in_specs=[new, new, new, cache, cache],
        out_specs=[new, cache, cache],
        out_shape=[jax.ShapeDtypeStruct((B, T, W), BF16),
                   jax.ShapeDtypeStruct((B, R, W), F32),
                   jax.ShapeDtypeStruct((B, R, W), F32)],
        compiler_params=_cparams(("parallel",)),
        name="sample_attn",
    )(qa, ka, va, cache_k, cache_v)


CHUNKS = 8


def _store_chunked(ref, val):
    n = val.shape[0]
    for s in range(CHUNKS):
        ref[pl.ds(s, n, stride=CHUNKS), :] = val[:, s * LANES:(s + 1) * LANES]


def _load_chunked(ref, n):
    return jnp.concatenate([ref[pl.ds(s, n, stride=CHUNKS), :] for s in range(CHUNKS)], axis=1)


def _split2(x):
    hi = x.astype(BF16)
    return hi, (x - hi.astype(F32)).astype(BF16)


def _merge_kernel(oa_ref, ob_ref, x_ref, wo_ref, g_ref, wr_ref, br_ref,
                  h_ref, xn_ref, idx_ref, gate_ref):
    TM = x_ref.shape[0]
    mixed = (jnp.dot(oa_ref[...], wo_ref[0:WIDTH_A, :], preferred_element_type=F32)
             + jnp.dot(ob_ref[...], wo_ref[WIDTH_A:, :], preferred_element_type=F32))
    h = x_ref[...] + mixed
    h_ref[...] = h
    ms = jnp.mean(h * h, axis=-1, keepdims=True)
    xn = (h * lax.rsqrt(ms + EPS)) * g_ref[...]
    _store_chunked(xn_ref, xn)
    nt = (((1,), (1,)), ((), ()))
    xh, xl = _split2(xn)
    wh, wl = _split2(wr_ref[...])
    logits = (lax.dot_general(wh, xh, nt, preferred_element_type=F32)
              + lax.dot_general(wh, xl, nt, preferred_element_type=F32)
              + lax.dot_general(wl, xh, nt, preferred_element_type=F32)) + br_ref[...]
    e_iota = lax.broadcasted_iota(jnp.int32, (N_EXPERTS, TM), 0)
    vals, idxs = [], []
    for _ in range(TOP_K):
        m = jnp.max(logits, axis=0, keepdims=True)
        sel = jnp.min(jnp.where(logits == m, e_iota, N_EXPERTS), axis=0, keepdims=True)
        vals.append(m)
        idxs.append(sel)
        logits = jnp.where(e_iota == sel, NEG_INF, logits)
    ex = [jnp.exp(v - vals[0]) for v in vals]
    den = ex[0] + ex[1] + ex[2] + ex[3]
    idx_ref[...] = jnp.concatenate(idxs, axis=0)
    gate_ref[...] = jnp.concatenate([e / den for e in ex], axis=0)


def _merge(oa, ob, x, w_out, norm_g, w_router_t, b_router, tm):
    N, D = x.shape
    full = lambda a: pl.BlockSpec(a.shape, lambda i: (0,) * a.ndim)
    row = lambda w: pl.BlockSpec((tm, w), lambda i: (i, 0))
    col = pl.BlockSpec((TOP_K, tm), lambda i: (0, i))
    return pl.pallas_call(
        _merge_kernel,
        grid=(N // tm,),
        in_specs=[row(WIDTH_A), row(WIDTH_B), row(D), full(w_out), full(norm_g), full(w_router_t),
                  full(b_router)],
        out_specs=[row(D), pl.BlockSpec((tm * CHUNKS, LANES), lambda i: (i, 0)), col, col],
        out_shape=[jax.ShapeDtypeStruct((N, D), F32),
                   jax.ShapeDtypeStruct((N * CHUNKS, LANES), F32),
                   jax.ShapeDtypeStruct((TOP_K, N), jnp.int32),
                   jax.ShapeDtypeStruct((TOP_K, N), F32)],
        compiler_params=_cparams(("parallel",)),
        name="merge_router",
    )(oa, ob, x, w_out, norm_g, w_router_t, b_router)


MOE_ROWS = 256
ROUTE_TILE = 256


def _route_kernel(idx_ref, dest_ref, be_ref, nused_ref, tot_ref, carry_ref, start_ref):
    ph = pl.program_id(0)
    i = pl.program_id(1)
    TT = idx_ref.shape[1]
    NBP = be_ref.shape[1]
    idx = idx_ref[...]
    e_iota = lax.broadcasted_iota(jnp.int32, (N_EXPERTS, TT), 0)
    onehot = [idx[k:k + 1, :] == e_iota for k in range(TOP_K)]
    cnt = jnp.zeros((N_EXPERTS, TT), F32)
    for oh in onehot:
        cnt = cnt + jnp.where(oh, 1.0, 0.0)
    tile_tot = jnp.sum(cnt, axis=1, keepdims=True)

    @pl.when(jnp.logical_and(ph == 0, i == 0))
    def _():
        tot_ref[...] = jnp.zeros_like(tot_ref)

    @pl.when(ph == 0)
    def _():
        tot_ref[...] += tile_tot

    @pl.when(jnp.logical_and(ph == 1, i == 0))
    def _():
        tot = tot_ref[...]
        padded = jnp.floor((tot + (MOE_ROWS - 1)) / MOE_ROWS) * MOE_ROWS
        r_i = lax.broadcasted_iota(jnp.int32, (N_EXPERTS, N_EXPERTS), 0)
        c_i = lax.broadcasted_iota(jnp.int32, (N_EXPERTS, N_EXPERTS), 1)
        padded_row = jnp.sum(jnp.where(r_i == c_i, padded, 0.0), axis=0, keepdims=True)
        start = jnp.sum(jnp.where(c_i < r_i, padded_row, 0.0), axis=1, keepdims=True)
        start_ref[...] = start
        carry_ref[...] = jnp.zeros_like(carry_ref)
        end = start + padded
        block_start = lax.broadcasted_iota(jnp.int32, (N_EXPERTS, NBP), 1).astype(F32) * MOE_ROWS
        be = jnp.sum(jnp.where(end <= block_start, 1.0, 0.0), axis=0, keepdims=True)
        be_ref[...] = jnp.minimum(be, N_EXPERTS - 1).astype(jnp.int32)
        total = jnp.sum(padded, axis=0, keepdims=True) / MOE_ROWS
        nused_ref[...] = jnp.broadcast_to(total, nused_ref.shape).astype(jnp.int32)

    @pl.when(ph == 1)
    def _():
        earlier = (lax.broadcasted_iota(jnp.int32, (TT, TT), 0)
                   < lax.broadcasted_iota(jnp.int32, (TT, TT), 1))
        before = jnp.dot(cnt.astype(BF16), jnp.where(earlier, 1.0, 0.0).astype(BF16),
                         preferred_element_type=F32)
        slot = start_ref[...] + carry_ref[...] + before
        rows = [jnp.sum(jnp.where(oh, slot, 0.0), axis=0, keepdims=True) for oh in onehot]
        dest_ref[...] = jnp.concatenate(rows, axis=0).astype(jnp.int32)
        carry_ref[...] += tile_tot


def _route(idx, n_blocks):
    _, N = idx.shape
    nbp = -(-n_blocks // LANES) * LANES
    tile = pl.BlockSpec((TOP_K, ROUTE_TILE), lambda ph, i: (0, i))
    return pl.pallas_call(
        _route_kernel,
        grid=(2, N // ROUTE_TILE),
        in_specs=[tile],
        out_specs=[pl.BlockSpec((TOP_K, ROUTE_TILE), lambda ph, i: (0, i * ph)),
                   pl.BlockSpec((1, nbp), lambda ph, i: (0, 0)),
                   pl.BlockSpec((1, LANES), lambda ph, i: (0, 0))],
        out_shape=[jax.ShapeDtypeStruct((TOP_K, N), jnp.int32),
                   jax.ShapeDtypeStruct((1, nbp), jnp.int32),
                   jax.ShapeDtypeStruct((1, LANES), jnp.int32)],
        scratch_shapes=[pltpu.VMEM((N_EXPERTS, 1), F32)] * 3,
        compiler_params=_cparams(("arbitrary", "arbitrary")),
        name="route",
    )(idx)


TOKEN_TILE = 256


def _token_rows(ref, t):
    return ref.at[pl.ds(pl.multiple_of(t * CHUNKS, CHUNKS), CHUNKS), :]


def _dispatch_kernel(dest_ref, x_ref, xs_in_ref, xs_ref, sem):
    del xs_in_ref
    TD = dest_ref.shape[1]

    def issue(t, c):
        for k in range(TOP_K):
            pltpu.make_async_copy(_token_rows(x_ref, t), _token_rows(xs_ref, dest_ref[k, t]), sem).start()
        return c

    lax.fori_loop(0, TD, issue, 0)
    for k in range(TOP_K):
        pltpu.make_async_copy(x_ref, xs_ref.at[pl.ds(0, TD * CHUNKS), :], sem).wait()


def _dispatch(dest_tiles, xn, xs):
    n_tiles, _, TD = dest_tiles.shape
    return pl.pallas_call(
        _dispatch_kernel,
        grid=(n_tiles,),
        in_specs=[pl.BlockSpec((None, TOP_K, TD), lambda i: (i, 0, 0), memory_space=pltpu.SMEM),
                  pl.BlockSpec((TD * CHUNKS, LANES), lambda i: (i, 0)),
                  pl.BlockSpec(memory_space=pl.ANY)],
        out_specs=pl.BlockSpec(memory_space=pl.ANY),
        out_shape=jax.ShapeDtypeStruct(xs.shape, xs.dtype),
        scratch_shapes=[pltpu.SemaphoreType.DMA(())],
        input_output_aliases={2: 0},
        compiler_params=_cparams(("arbitrary",)),
        name="dispatch",
    )(dest_tiles, xn, xs)


def _expert_kernel(be_ref, nused_ref, xs_ref, wgu_ref, bgu_ref, wd_ref, bd_ref, ys_ref,
                   wgu_bf, wd_bf):
    i = pl.program_id(0)
    D_FF = wd_ref.shape[0]
    new_expert = jnp.logical_or(i == 0, be_ref[i] != be_ref[jnp.maximum(i - 1, 0)])

    @pl.when(jnp.logical_and(i < nused_ref[0], new_expert))
    def _():
        wgu_bf[...] = wgu_ref[...].astype(BF16)
        wd_bf[...] = wd_ref[...].astype(BF16)

    @pl.when(i < nused_ref[0])
    def _():
        x = _load_chunked(xs_ref, MOE_ROWS).astype(BF16)
        hdn = jnp.dot(x, wgu_bf[...], preferred_element_type=F32) + bgu_ref[...]
        glu = jnp.minimum(hdn[:, :D_FF], SWIGLU_LIMIT)
        lin = jnp.clip(hdn[:, D_FF:], -SWIGLU_LIMIT, SWIGLU_LIMIT)
        act = glu * (1.0 / (1.0 + jnp.exp(-SWIGLU_ALPHA * glu))) * (lin + 1.0)
        y = jnp.dot(act.astype(BF16), wd_bf[...], preferred_element_type=F32) + bd_ref[...]
        _store_chunked(ys_ref, y)

    @pl.when(i >= nused_ref[0])
    def _():
        ys_ref[...] = jnp.zeros_like(ys_ref)


def _experts(block_expert, n_used, xs, w_gate_up, b_gate_up, w_down, b_down, n_blocks):
    E, D, F2 = w_gate_up.shape
    D_FF = w_down.shape[1]
    rows = pl.BlockSpec((MOE_ROWS * CHUNKS, LANES), lambda i, be, nu: (i, 0))
    grid_spec = pltpu.PrefetchScalarGridSpec(
        num_scalar_prefetch=2,
        grid=(n_blocks,),
        in_specs=[rows,
                  pl.BlockSpec((None, D, F2), lambda i, be, nu: (be[i], 0, 0)),
                  pl.BlockSpec((None, 1, F2), lambda i, be, nu: (be[i], 0, 0)),
                  pl.BlockSpec((None, D_FF, D), lambda i, be, nu: (be[i], 0, 0)),
                  pl.BlockSpec((None, 1, D), lambda i, be, nu: (be[i], 0, 0))],
        out_specs=rows,
        scratch_shapes=[pltpu.VMEM((D, F2), BF16), pltpu.VMEM((D_FF, D), BF16)],
    )
    return pl.pallas_call(
        _expert_kernel,
        grid_spec=grid_spec,
        out_shape=jax.ShapeDtypeStruct(xs.shape, F32),
        compiler_params=_cparams(("arbitrary",)),
        name="experts",
    )(block_expert, n_used, xs, w_gate_up, b_gate_up.reshape(E, 1, F2), w_down, b_down.reshape(E, 1, D))


def _combine_kernel(dest_ref, gate_ref, h_ref, g_ref, ys_ref, y_ref, buf, sem):
    TC = dest_ref.shape[1]

    def issue(t, c):
        for k in range(TOP_K):
            pltpu.make_async_copy(_token_rows(ys_ref, dest_ref[k, t]), _token_rows(buf.at[k], t), sem).start()
        return c

    lax.fori_loop(0, TC, issue, 0)
    gates = jnp.concatenate([gate_ref[...], jnp.zeros((8 - TOP_K, TC), F32)], axis=0)
    gates_t = jnp.transpose(gates)
    for k in range(TOP_K):
        pltpu.make_async_copy(ys_ref.at[pl.ds(0, TC * CHUNKS), :], buf.at[k], sem).wait()
    moe = gates_t[:, 0:1] * _load_chunked(buf.at[0], TC)
    for k in range(1, TOP_K):
        moe = moe + gates_t[:, k:k + 1] * _load_chunked(buf.at[k], TC)
    hf = h_ref[...] + moe
    ms = jnp.mean(hf * hf, axis=-1, keepdims=True)
    y_ref[...] = (hf * lax.rsqrt(ms + EPS)) * g_ref[...]


def _combine(dest_tiles, gates, h, norm_g, ys):
    n_tiles, _, TC = dest_tiles.shape
    N, D = h.shape
    return pl.pallas_call(
        _combine_kernel,
        grid=(n_tiles,),
        in_specs=[pl.BlockSpec((None, TOP_K, TC), lambda i: (i, 0, 0), memory_space=pltpu.SMEM),
                  pl.BlockSpec((TOP_K, TC), lambda i: (0, i)),
                  pl.BlockSpec((TC, D), lambda i: (i, 0)),
                  pl.BlockSpec((1, D), lambda i: (0, 0)),
                  pl.BlockSpec(memory_space=pl.ANY)],
        out_specs=pl.BlockSpec((TC, D), lambda i: (i, 0)),
        out_shape=jax.ShapeDtypeStruct((N, D), F32),
        scratch_shapes=[pltpu.VMEM((TOP_K, TC * CHUNKS, LANES), F32), pltpu.SemaphoreType.DMA(())],
        compiler_params=_cparams(("arbitrary",)),
        name="combine",
    )(dest_tiles, gates, h, norm_g, ys)


def _prep_weights(w_in, w_alpha):
    w_main = w_in[:, :PROJ_MAIN].astype(BF16)
    w_lr = jnp.pad(w_in[:, PROJ_MAIN:], ((0, 0), (0, LANES - GATE_RANK))).astype(BF16)
    w_al = jnp.pad(w_alpha, ((0, LANES - GATE_RANK), (0, 0))).astype(BF16)
    return w_main, w_lr, w_al


def kernel(x_prompt, x_sample, cache_swa_k, cache_swa_v, state_gla, norm_mix_g, w_in, w_alpha, b_alpha, gla_norm_g, w_out, norm_ffn_g, w_router, b_router, w_gate_up, b_gate_up, w_down, b_down, norm_final_g):
    B, S, D = x_prompt.shape
    Bs, Ts, _ = x_sample.shape
    assert w_in.shape[0] == 1, "single-layer trunk"
    l = 0
    R = cache_swa_k.shape[2]
    rows_p = min(DILATIONS[-1] * KEYS_PER_CONFIG, S)
    w_main, w_lr, w_al = _prep_weights(w_in[l], w_alpha[l])
    g_mix = norm_mix_g[l][None]
    b_al = b_alpha[l][None]
    g_gla = gla_norm_g[l][None]

    pos_p = jnp.arange(S, dtype=jnp.int32)
    qa, ka, va, qb, kb, vb, zg, gb = _project(x_prompt, pos_p, g_mix, w_main, w_lr, w_al, b_al, PROJ_TILE)
    oa_p = _prompt_attention(qa, ka, va)
    ob_p, st_p = _gla(qb, kb, gb, vb, zg, jnp.zeros((B, N_HEADS_B, DK_B, DV_B), F32), g_gla,
                      GLA_TILE, GLA_CHUNK)
    k_prompt = ka[:, S - rows_p:].reshape(1, B, rows_p, N_HEADS_A, HEAD_DIM_A)
    v_prompt = va[:, S - rows_p:].reshape(1, B, rows_p, N_HEADS_A, HEAD_DIM_A)

    pos_s = PAST_LEN + (jnp.arange(Bs * Ts, dtype=jnp.int32) % Ts)
    proj_s = _project(x_sample.reshape(1, Bs * Ts, D), pos_s, g_mix, w_main, w_lr, w_al, b_al, Bs * Ts)
    qa_s, ka_s, va_s, qb_s, kb_s, vb_s, zg_s, gb_s = [t.reshape(Bs, Ts, -1) for t in proj_s]
    oa_s, k_sample, v_sample = _sample_attention(qa_s, ka_s, va_s,
                                                 cache_swa_k[l].reshape(Bs, R, WIDTH_A),
                                                 cache_swa_v[l].reshape(Bs, R, WIDTH_A))
    ob_s, st_s = _gla(qb_s, kb_s, gb_s, vb_s, zg_s, state_gla[l], g_gla, Ts, Ts)

    w_out_bf = w_out[l].astype(BF16)
    g_ffn = norm_ffn_g[l][None]
    w_router_t = jnp.transpose(w_router[l])
    b_router_c = b_router[l][:, None]
    Np, Ns = B * S, Bs * Ts
    h_p, xn_p, idx_p, gate_p = _merge(oa_p.reshape(Np, WIDTH_A), ob_p.reshape(Np, WIDTH_B),
                                      x_prompt.reshape(Np, D), w_out_bf, g_ffn, w_router_t, b_router_c,
                                      MERGE_TILE)
    h_s, xn_s, idx_s, gate_s = _merge(oa_s.reshape(Ns, WIDTH_A), ob_s.reshape(Ns, WIDTH_B),
                                      x_sample.reshape(Ns, D), w_out_bf, g_ffn, w_router_t, b_router_c,
                                      Ns)

    y_p, y_s = _moe([(xn_p, idx_p, gate_p, h_p), (xn_s, idx_s, gate_s, h_s)],
                    w_gate_up[l], b_gate_up[l], w_down[l], b_down[l], norm_final_g[None])
    return (y_p.reshape(B, S, D), y_s.reshape(Bs, Ts, D), k_prompt, v_prompt, st_p[None],
            k_sample.reshape(1, Bs, R, N_HEADS_A, HEAD_DIM_A),
            v_sample.reshape(1, Bs, R, N_HEADS_A, HEAD_DIM_A), st_s[None])


PAST_LEN = 16384
PROJ_TILE = 512
MERGE_TILE = 512
GLA_TILE = 1024
GLA_CHUNK = 64


def _moe(groups, w_gate_up, b_gate_up, w_down, b_down, g_final):
    sizes = [g[3].shape[0] for g in groups]
    N = sum(sizes)
    assert all(n % TOKEN_TILE == 0 for n in sizes) and TOKEN_TILE == ROUTE_TILE
    n_blocks = -(-(N * TOP_K + N_EXPERTS * (MOE_ROWS - 1)) // MOE_ROWS)
    idx = jnp.concatenate([g[1] for g in groups], axis=1)
    dest, block_expert, n_used = _route(idx, n_blocks)
    dest_tiles = dest.reshape(TOP_K, N // TOKEN_TILE, TOKEN_TILE).transpose(1, 0, 2)
    xs = jnp.zeros((n_blocks * MOE_ROWS * CHUNKS, LANES), F32)
    tile0 = 0
    spans = []
    for (xn, _, _, _), n in zip(groups, sizes):
        spans.append((tile0, tile0 + n // TOKEN_TILE))
        xs = _dispatch(dest_tiles[spans[-1][0]:spans[-1][1]], xn, xs)
        tile0 = spans[-1][1]
    ys = _experts(block_expert[0], n_used[0, :1], xs, w_gate_up, b_gate_up, w_down, b_down, n_blocks)
    return [_combine(dest_tiles[a:b], gates, h, g_final, ys)
            for (a, b), (_, _, gates, h) in zip(spans, groups)]
```

```python
import functools

import jax
import jax.numpy as jnp
import numpy as np
from jax import lax
from jax.experimental import pallas as pl
from jax.experimental.pallas import tpu as pltpu

F32 = jnp.float32
BF16 = jnp.bfloat16

N_HEADS_A = 8
HEAD_DIM_A = 64
WIDTH_A = N_HEADS_A * HEAD_DIM_A
N_HEADS_B = 4
DK_B = 64
DV_B = 128
QK_B = N_HEADS_B * DK_B
WIDTH_B = N_HEADS_B * DV_B
GATE_RANK = 16
GATE_LOGIT_NORM = 16.0
DILATIONS = (1, 4, 16)
KEYS_PER_CONFIG = 128
ROPE_THETA = 10000.0
N_EXPERTS = 32
TOP_K = 4
SWIGLU_ALPHA = 1.702
SWIGLU_LIMIT = 7.0
EPS = 1e-6

LANES = 128
VMEM_LIMIT = 56 * 1024 * 1024


def _cparams(sem, vmem=VMEM_LIMIT):
    return pltpu.CompilerParams(dimension_semantics=sem, vmem_limit_bytes=vmem)


PROJ_MAIN = 3 * WIDTH_A + 2 * QK_B + 2 * WIDTH_B


def _rope_tables(pos):
    half = HEAD_DIM_A // 2
    inv = ROPE_THETA ** (-jnp.arange(half, dtype=F32) / half)
    ang = pos.astype(F32)[:, None] * inv[None, :]
    cos = jnp.cos(ang)
    sin = jnp.sin(ang)
    cos_t = jnp.concatenate([cos, cos, cos, cos], axis=-1)
    sin_t = jnp.concatenate([-sin, sin, -sin, sin], axis=-1)
    return cos_t, sin_t


def _rope_block(t, cos, sin, first_half):
    partner = jnp.where(first_half, pltpu.roll(t, LANES - 32, 1), pltpu.roll(t, 32, 1))
    return t * cos + partner * sin


def _proj_kernel(x_ref, g_ref, w_ref, wlr_ref, wa_ref, ba_ref, cos_ref, sin_ref,
                 qa_ref, ka_ref, va_ref, qb_ref, kb_ref, vb_ref, zg_ref, gb_ref):
    x = x_ref[...]
    ms = jnp.mean(x * x, axis=-1, keepdims=True)
    xn = ((x * lax.rsqrt(ms + EPS)) * g_ref[...]).astype(BF16)

    def cols(lo, hi):
        return jnp.dot(xn, w_ref[:, lo:hi], preferred_element_type=F32)

    cos = cos_ref[...]
    sin = sin_ref[...]
    lane = lax.broadcasted_iota(jnp.int32, cos.shape, 1)
    first_half = (lane % HEAD_DIM_A) < (HEAD_DIM_A // 2)
    for j in range(WIDTH_A // LANES):
        lo = j * LANES
        q = cols(lo, lo + LANES)
        qa_ref[:, lo:lo + LANES] = _rope_block(q, cos, sin, first_half) * (HEAD_DIM_A ** -0.5)
        k = cols(WIDTH_A + lo, WIDTH_A + lo + LANES)
        ka_ref[:, lo:lo + LANES] = _rope_block(k, cos, sin, first_half)
    o = 2 * WIDTH_A
    va_ref[...] = cols(o, o + WIDTH_A)
    o += WIDTH_A
    qb_ref[...] = cols(o, o + QK_B) * (DK_B ** -0.5)
    o += QK_B
    kb_ref[...] = cols(o, o + QK_B)
    o += QK_B
    vb_ref[...] = cols(o, o + WIDTH_B)
    o += WIDTH_B
    zg_ref[...] = cols(o, o + WIDTH_B)
    lr = jnp.dot(xn, wlr_ref[...], preferred_element_type=F32)
    z = jnp.dot(lr.astype(BF16), wa_ref[...], preferred_element_type=F32) + ba_ref[...]
    logsig = jnp.minimum(z, 0.0) - jnp.log(1.0 + jnp.exp(-jnp.abs(z)))
    gb_ref[...] = logsig / GATE_LOGIT_NORM


def _project(x, pos, norm_g, w_main, w_lr, w_alpha, b_alpha, tm):
    B, T, D = x.shape
    cos_t, sin_t = _rope_tables(pos)
    grid = (B, T // tm)
    row = lambda w: pl.BlockSpec((None, tm, w), lambda b, i: (b, i, 0))
    full = lambda a: pl.BlockSpec(a.shape, lambda b, i: (0,) * a.ndim)
    tab = pl.BlockSpec((tm, LANES), lambda b, i: (i, 0))
    widths = (WIDTH_A, WIDTH_A, WIDTH_A, QK_B, QK_B, WIDTH_B, WIDTH_B, QK_B)
    return pl.pallas_call(
        _proj_kernel,
        grid=grid,
        in_specs=[row(D), full(norm_g), full(w_main), full(w_lr), full(w_alpha), full(b_alpha), tab, tab],
        out_specs=[row(w) for w in widths],
        out_shape=[jax.ShapeDtypeStruct((B, T, w), F32) for w in widths],
        compiler_params=_cparams(("parallel", "parallel")),
        name="proj",
    )(x, norm_g, w_main, w_lr, w_alpha, b_alpha, cos_t, sin_t)


Q_BLOCK = 128
NEG_INF = float("-inf")


def _attn_block(q, k, v, mask, head0, state):
    kb = k.astype(BF16)
    vb = v.astype(BF16)
    v_head0 = lax.broadcasted_iota(jnp.int32, vb.shape, 1) < HEAD_DIM_A
    m_new, pvs = [], []
    for h in range(2):
        hm = head0 if h == 0 else jnp.logical_not(head0)
        qh = jnp.where(hm, q, 0.0).astype(BF16)
        s = lax.dot_general(qh, kb, (((1,), (1,)), ((), ())), preferred_element_type=F32)
        s = jnp.where(mask, s, NEG_INF)
        if state is None:
            mh = jnp.max(s, axis=1, keepdims=True)
        else:
            prev = jnp.where(hm, state[0], NEG_INF)
            mh = jnp.max(jnp.concatenate([s, prev], axis=1), axis=1, keepdims=True)
        p = jnp.exp(s - mh)
        m_new.append(mh)
        vh = jnp.where(v_head0 if h == 0 else jnp.logical_not(v_head0), vb, jnp.ones_like(vb))
        pvs.append(jnp.dot(p.astype(BF16), vh, preferred_element_type=F32))
    m_full = jnp.where(head0, m_new[0], m_new[1])
    l_swapped = jnp.where(head0, pvs[1], pvs[0])
    pv_full = jnp.where(head0, pvs[0], pvs[1])
    if state is None:
        return m_full, l_swapped, pv_full
    a = jnp.exp(state[0] - m_full)
    a_swapped = pltpu.roll(a, HEAD_DIM_A, 1)
    return m_full, a_swapped * state[1] + l_swapped, a * state[2] + pv_full


def _attn_kernel(q_ref, k_ref, v_ref, o_ref, m_ref, l_ref, acc_ref):
    S = q_ref.shape[0]
    QB = Q_BLOCK
    lane = lax.broadcasted_iota(jnp.int32, (QB, LANES), 1)
    head0 = lane < HEAD_DIM_A
    qq = lax.broadcasted_iota(jnp.int32, (QB, 2 * QB), 0)
    kk = lax.broadcasted_iota(jnp.int32, (QB, 2 * QB), 1)
    band = jnp.logical_and(kk >= qq, kk - qq <= KEYS_PER_CONFIG)
    causal = (lax.broadcasted_iota(jnp.int32, (QB, QB), 1)
              <= lax.broadcasted_iota(jnp.int32, (QB, QB), 0))

    GROUP = 4

    for ci, dil in enumerate(DILATIONS):
        nblk = S // (dil * QB)
        assert nblk % GROUP == 0 or GROUP % nblk == 0

        def rows(start, n, dil=dil):
            return pl.ds(start, n) if dil == 1 else pl.ds(start, n, stride=dil)

        def do_group(blocks, ci=ci, dil=dil, rows=rows):
            loaded = []
            for r, j, first in blocks:
                q0 = r + dil * QB * j
                qs = rows(q0, QB)
                ks = rows(r, QB) if first else rows(q0 - dil * QB, 2 * QB)
                state = None if ci == 0 else (m_ref[qs, :], l_ref[qs, :], acc_ref[qs, :])
                loaded.append((qs, q_ref[qs, :], k_ref[ks, :], v_ref[ks, :], causal if first else band, state))
            results = [_attn_block(q, k, v, mask, head0, state) for _, q, k, v, mask, state in loaded]
            for (qs, *_), (m, l, acc) in zip(loaded, results):
                m_ref[qs, :] = m
                l_ref[qs, :] = l
                acc_ref[qs, :] = acc

        if nblk >= GROUP:
            def residue(r, carry, nblk=nblk, do_group=do_group):
                do_group([(r, j, j == 0) for j in range(GROUP)])

                def rest(g, c):
                    do_group([(r, GROUP * g + u, False) for u in range(GROUP)])
                    return c

                return lax.fori_loop(1, nblk // GROUP, rest, carry)

            lax.fori_loop(0, dil, residue, 0)
        else:
            per = GROUP // nblk

            def residues(g, carry, nblk=nblk, per=per, do_group=do_group):
                do_group([(g * per + i, j, j == 0) for i in range(per) for j in range(nblk)])
                return carry

            lax.fori_loop(0, dil // per, residues, 0)

    def finish(i, c):
        rs = pl.ds(pl.multiple_of(i * QB, QB), QB)
        den = pltpu.roll(l_ref[rs, :], HEAD_DIM_A, 1)
        o_ref[rs, :] = (acc_ref[rs, :] / den).astype(o_ref.dtype)
        return c

    lax.fori_loop(0, S // QB, finish, 0, unroll=4)


def _prompt_attention(qa, ka, va):
    B, S, W = qa.shape
    spec = pl.BlockSpec((None, S, LANES), lambda b, hp: (b, 0, hp))
    return pl.pallas_call(
        _attn_kernel,
        grid=(B, W // LANES),
        in_specs=[spec, spec, spec],
        out_specs=spec,
        out_shape=jax.ShapeDtypeStruct((B, S, W), BF16),
        scratch_shapes=[pltpu.VMEM((S, LANES), F32)] * 3,
        compiler_params=_cparams(("parallel", "parallel")),
        name="prompt_attn",
    )(qa, ka, va)


def _split3(x):
    x1 = x.astype(BF16)
    r1 = x - x1.astype(F32)
    x2 = r1.astype(BF16)
    x3 = (r1 - x2.astype(F32)).astype(BF16)
    return x1, x2, x3


def _gla_kernel(q_ref, k_ref, g_ref, v_ref, z_ref, s0_ref, ng_ref, o_ref, sfin_ref, st_ref, *, chunk):
    C = chunk
    TS = q_ref.shape[0]
    n_pairs = N_HEADS_B // 2
    PW = 2 * DV_B
    t_idx = pl.program_id(1)

    lane_k = lax.broadcasted_iota(jnp.int32, (C, LANES), 1)
    head0 = lane_k < DK_B
    tri_incl = (lax.broadcasted_iota(jnp.int32, (C, C), 1)
                <= lax.broadcasted_iota(jnp.int32, (C, C), 0))
    tri_bf = jnp.where(tri_incl, 1.0, 0.0).astype(BF16)
    bd_mask = ((lax.broadcasted_iota(jnp.int32, (PW, LANES), 0) // DV_B)
               == (lax.broadcasted_iota(jnp.int32, (PW, LANES), 1) // DK_B))

    @pl.when(t_idx == 0)
    def _():
        for p in range(n_pairs):
            for h in range(2):
                blk = jnp.transpose(s0_ref[2 * p + h])
                pad = jnp.zeros((DV_B, DK_B), F32)
                row = jnp.concatenate([blk, pad] if h == 0 else [pad, blk], axis=1)
                st_ref[p, h * DV_B:(h + 1) * DV_B, :] = row

    def chunk_body(c, carry):
        rs = pl.ds(pl.multiple_of(c * C, C), C)
        for p in range(n_pairs):
            kl = slice(p * LANES, (p + 1) * LANES)
            vl = slice(p * PW, (p + 1) * PW)
            q = q_ref[rs, kl]
            k = k_ref[rs, kl]
            g = g_ref[rs, kl]
            v = v_ref[rs, vl].astype(BF16)
            g1, g2, g3 = _split3(g)
            b = (jnp.dot(tri_bf, g1, preferred_element_type=F32)
                 + jnp.dot(tri_bf, g2, preferred_element_type=F32)
                 + jnp.dot(tri_bf, g3, preferred_element_type=F32))
            b_last = b[C - 1:C, :]
            b_mid = b[C // 2 - 1:C // 2, :] if C > 1 else b_last
            qe = q * jnp.exp(b - b_mid)
            ke = (k * jnp.exp(b_mid - b)).astype(BF16)
            st = st_ref[p]
            q_in = (q * jnp.exp(b)).astype(BF16)
            o = lax.dot_general(q_in, st.astype(BF16), (((1,), (1,)), ((), ())),
                                preferred_element_type=F32)
            intra = []
            for h in range(2):
                hm = head0 if h == 0 else jnp.logical_not(head0)
                a = lax.dot_general(jnp.where(hm, qe, 0.0).astype(BF16), ke,
                                    (((1,), (1,)), ((), ())), preferred_element_type=F32)
                a = jnp.where(tri_incl, a, 0.0).astype(BF16)
                intra.append(jnp.dot(a, v[:, h * DV_B:(h + 1) * DV_B], preferred_element_type=F32))
            o = o + jnp.concatenate(intra, axis=1)
            k_dec = (k * jnp.exp(b_last - b)).astype(BF16)
            upd = lax.dot_general(v, k_dec, (((0,), (0,)), ((), ())), preferred_element_type=F32)
            st_ref[p] = jnp.exp(b_last) * st + jnp.where(bd_mask, upd, 0.0)
            for h in range(2):
                oh = o[:, h * DV_B:(h + 1) * DV_B]
                hl = slice((2 * p + h) * DV_B, (2 * p + h + 1) * DV_B)
                ms = jnp.mean(oh * oh, axis=-1, keepdims=True)
                z = z_ref[rs, hl]
                gated = (oh * lax.rsqrt(ms + EPS)) * ng_ref[:, hl] * (z / (1.0 + jnp.exp(-z)))
                o_ref[rs, hl] = gated.astype(o_ref.dtype)
        return carry

    lax.fori_loop(0, TS // C, chunk_body, 0, unroll=2 if (TS // C) % 2 == 0 else 1)

    @pl.when(t_idx == pl.num_programs(1) - 1)
    def _():
        for p in range(n_pairs):
            for h in range(2):
                blk = st_ref[p, h * DV_B:(h + 1) * DV_B, h * DK_B:(h + 1) * DK_B]
                sfin_ref[2 * p + h] = jnp.transpose(blk)


def _gla(qb, kb, gb, vb, zg, state0, norm_g, ts, chunk):
    B, T, _ = qb.shape
    row = lambda w: pl.BlockSpec((None, ts, w), lambda b, i: (b, i, 0))
    st_spec = pl.BlockSpec((None, N_HEADS_B, DK_B, DV_B), lambda b, i: (b, 0, 0, 0))
    return pl.pallas_call(
        functools.partial(_gla_kernel, chunk=chunk),
        grid=(B, T // ts),
        in_specs=[row(QK_B), row(QK_B), row(QK_B), row(WIDTH_B), row(WIDTH_B), st_spec,
                  pl.BlockSpec((1, WIDTH_B), lambda b, i: (0, 0))],
        out_specs=[row(WIDTH_B), st_spec],
        out_shape=[jax.ShapeDtypeStruct((B, T, WIDTH_B), BF16),
                   jax.ShapeDtypeStruct((B, N_HEADS_B, DK_B, DV_B), F32)],
        scratch_shapes=[pltpu.VMEM((N_HEADS_B // 2, 2 * DV_B, LANES), F32)],
        compiler_params=_cparams(("parallel", "arbitrary")),
        name="gla",
    )(qb, kb, gb, vb, zg, state0, norm_g)


def _sample_attn_kernel(q_ref, kn_ref, vn_ref, kc_ref, vc_ref, o_ref, ko_ref, vo_ref):
    T = q_ref.shape[0]
    R = kc_ref.shape[0]
    W = q_ref.shape[1]
    HT = N_HEADS_A * T
    q = q_ref[...]
    qx = jnp.concatenate([q] * N_HEADS_A, axis=0)
    own = ((lax.broadcasted_iota(jnp.int32, (HT, W), 0) // T)
           == (lax.broadcasted_iota(jnp.int32, (HT, W), 1) // HEAD_DIM_A))
    qx = jnp.where(own, qx, 0.0).astype(BF16)
    pad = jnp.zeros((LANES - T, W), F32)
    kn = jnp.concatenate([kn_ref[...], pad], axis=0).astype(BF16)
    vn = jnp.concatenate([vn_ref[...], pad], axis=0).astype(BF16)
    nt = (((1,), (1,)), ((), ()))
    s_c = lax.dot_general(qx, kc_ref[...].astype(BF16), nt, preferred_element_type=F32)
    s_n = lax.dot_general(qx, kn, nt, preferred_element_type=F32)

    def multiplicity(n_cols, first_row):
        t = lax.broadcasted_iota(jnp.int32, (HT, n_cols), 0) % T
        j = lax.broadcasted_iota(jnp.int32, (HT, n_cols), 1) + first_row
        delta = R + t - j
        cnt = jnp.zeros((HT, n_cols), F32)
        for dil in DILATIONS:
            hit = (delta >= 0) & (delta <= dil * KEYS_PER_CONFIG) & (delta % dil == 0)
            cnt = cnt + jnp.where(hit, 1.0, 0.0)
        return cnt

    cnt_c = multiplicity(R, 0)
    cnt_n = multiplicity(LANES, R)
    s_c = jnp.where(cnt_c > 0.0, s_c, NEG_INF)
    s_n = jnp.where(cnt_n > 0.0, s_n, NEG_INF)
    m = jnp.maximum(jnp.max(s_c, axis=1, keepdims=True), jnp.max(s_n, axis=1, keepdims=True))
    p_c = cnt_c * jnp.exp(s_c - m)
    p_n = cnt_n * jnp.exp(s_n - m)
    den = jnp.sum(p_c, axis=1, keepdims=True) + jnp.sum(p_n, axis=1, keepdims=True)
    full = (jnp.dot(p_c.astype(BF16), vc_ref[...].astype(BF16), preferred_element_type=F32)
            + jnp.dot(p_n.astype(BF16), vn, preferred_element_type=F32)) / den
    full = jnp.where(own, full, 0.0)
    out = full[0:T, :]
    for h in range(1, N_HEADS_A):
        out = out + full[h * T:(h + 1) * T, :]
    o_ref[...] = out.astype(o_ref.dtype)
    ko_ref[0:R - T, :] = kc_ref[T:R, :]
    ko_ref[R - T:R, :] = kn_ref[...]
    vo_ref[0:R - T, :] = vc_ref[T:R, :]
    vo_ref[R - T:R, :] = vn_ref[...]


def _sample_attention(qa, ka, va, cache_k, cache_v):
    B, T, W = qa.shape
    R = cache_k.shape[1]
    assert R >= DILATIONS[-1] * KEYS_PER_CONFIG and T % 8 == 0 and T <= LANES
    new = pl.BlockSpec((None, T, W), lambda b: (b, 0, 0))
    cache = pl.BlockSpec((None, R, W), lambda b: (b, 0, 0))
    return pl.pallas_call(
        _sample_attn_kernel,
        grid=(B,),
        in_specs=[new, new, new, cache, cache],
        out_specs=[new, cache, cache],
        out_shape=[jax.ShapeDtypeStruct((B, T, W), BF16),
                   jax.ShapeDtypeStruct((B, R, W), F32),
                   jax.ShapeDtypeStruct((B, R, W), F32)],
        compiler_params=_cparams(("parallel",)),
        name="sample_attn",
    )(qa, ka, va, cache_k, cache_v)


CHUNKS = 8


def _store_chunked(ref, val):
    n = val.shape[0]
    for s in range(CHUNKS):
        ref[pl.ds(s, n, stride=CHUNKS), :] = val[:, s * LANES:(s + 1) * LANES]


def _load_chunked(ref, n):
    return jnp.concatenate([ref[pl.ds(s, n, stride=CHUNKS), :] for s in range(CHUNKS)], axis=1)


def _split2(x):
    hi = x.astype(BF16)
    return hi, (x - hi.astype(F32)).astype(BF16)


def _merge_kernel(oa_ref, ob_ref, x_ref, wo_ref, g_ref, wr_ref, br_ref,
                  h_ref, xn_ref, idx_ref, gate_ref):
    TM = x_ref.shape[0]
    mixed = (jnp.dot(oa_ref[...], wo_ref[0:WIDTH_A, :], preferred_element_type=F32)
             + jnp.dot(ob_ref[...], wo_ref[WIDTH_A:, :], preferred_element_type=F32))
    h = x_ref[...] + mixed
    h_ref[...] = h
    ms = jnp.mean(h * h, axis=-1, keepdims=True)
    xn = (h * lax.rsqrt(ms + EPS)) * g_ref[...]
    xn_ref[...] = xn.astype(xn_ref.dtype)
    nt = (((1,), (1,)), ((), ()))
    xh, xl = _split2(xn)
    wh, wl = _split2(wr_ref[...])
    logits = (lax.dot_general(wh, xh, nt, preferred_element_type=F32)
              + lax.dot_general(wh, xl, nt, preferred_element_type=F32)
              + lax.dot_general(wl, xh, nt, preferred_element_type=F32)) + br_ref[...]
    e_iota = lax.broadcasted_iota(jnp.int32, (N_EXPERTS, TM), 0)
    vals, idxs = [], []
    for _ in range(TOP_K):
        m = jnp.max(logits, axis=0, keepdims=True)
        sel = jnp.min(jnp.where(logits == m, e_iota, N_EXPERTS), axis=0, keepdims=True)
        vals.append(m)
        idxs.append(sel)
        logits = jnp.where(e_iota == sel, NEG_INF, logits)
    ex = [jnp.exp(v - vals[0]) for v in vals]
    den = ex[0] + ex[1] + ex[2] + ex[3]
    idx_ref[...] = jnp.concatenate(idxs, axis=0)
    gate_ref[...] = jnp.concatenate([e / den for e in ex], axis=0)


def _merge(oa, ob, x, w_out, norm_g, w_router_t, b_router, tm):
    N, D = x.shape
    full = lambda a: pl.BlockSpec(a.shape, lambda i: (0,) * a.ndim)
    row = lambda w: pl.BlockSpec((tm, w), lambda i: (i, 0))
    col = pl.BlockSpec((TOP_K, tm), lambda i: (0, i))
    return pl.pallas_call(
        _merge_kernel,
        grid=(N // tm,),
        in_specs=[row(WIDTH_A), row(WIDTH_B), row(D), full(w_out), full(norm_g), full(w_router_t),
                  full(b_router)],
        out_specs=[row(D), row(D), col, col],
        out_shape=[jax.ShapeDtypeStruct((N, D), F32),
                   jax.ShapeDtypeStruct((N, D), BF16),
                   jax.ShapeDtypeStruct((TOP_K, N), jnp.int32),
                   jax.ShapeDtypeStruct((TOP_K, N), F32)],
        compiler_params=_cparams(("parallel",)),
        name="merge_router",
    )(oa, ob, x, w_out, norm_g, w_router_t, b_router)


MOE_ROWS = 256
TOKEN_TILE = 256
DMA_ROWS = 8
STAGE_ROWS = 1280
assert STAGE_ROWS >= TOKEN_TILE * TOP_K + N_EXPERTS * (DMA_ROWS - 1) and STAGE_ROWS % 8 == 0


def _expert_row(col):
    r = lax.broadcasted_iota(jnp.int32, (N_EXPERTS, LANES), 0)
    c = lax.broadcasted_iota(jnp.int32, (N_EXPERTS, LANES), 1)
    return jnp.sum(jnp.where(r == c, col, 0.0), axis=0, keepdims=True)


def _expert_prefix(col):
    r = lax.broadcasted_iota(jnp.int32, (N_EXPERTS, LANES), 0)
    c = lax.broadcasted_iota(jnp.int32, (N_EXPERTS, LANES), 1)
    return jnp.sum(jnp.where(c < r, _expert_row(col), 0.0), axis=1, keepdims=True)


def _route_kernel(idx_ref, pos_ref, meta_ref, be_ref, misc_ref, tot_ref, carry_ref, start_ref):
    ph = pl.program_id(0)
    i = pl.program_id(1)
    TT = idx_ref.shape[1]
    NBP = be_ref.shape[1]
    idx = idx_ref[...]
    e_iota = lax.broadcasted_iota(jnp.int32, (N_EXPERTS, TT), 0)
    onehot = [idx[k:k + 1, :] == e_iota for k in range(TOP_K)]
    cnt = jnp.zeros((N_EXPERTS, TT), F32)
    for oh in onehot:
        cnt = cnt + jnp.where(oh, 1.0, 0.0)
    tile_tot = jnp.sum(cnt, axis=1, keepdims=True)

    @pl.when(jnp.logical_and(ph == 0, i == 0))
    def _():
        tot_ref[...] = jnp.zeros_like(tot_ref)

    @pl.when(ph == 0)
    def _():
        tot_ref[...] += tile_tot

    @pl.when(jnp.logical_and(ph == 1, i == 0))
    def _():
        tot = tot_ref[...]
        padded = jnp.floor((tot + (DMA_ROWS + MOE_ROWS - 1)) / MOE_ROWS) * MOE_ROWS
        start = _expert_prefix(padded)
        start_ref[...] = start
        carry_ref[...] = jnp.zeros_like(carry_ref)
        end = start + padded
        block_start = lax.broadcasted_iota(jnp.int32, (N_EXPERTS, NBP), 1).astype(F32) * MOE_ROWS
        be = jnp.sum(jnp.where(end <= block_start, 1.0, 0.0), axis=0, keepdims=True)
        be_ref[...] = jnp.minimum(be, N_EXPERTS - 1).astype(jnp.int32)
        n_used = jnp.broadcast_to(jnp.sum(padded, axis=0, keepdims=True) / MOE_ROWS, (1, LANES))
        zero = jnp.zeros((1, LANES), F32)
        misc_ref[...] = jnp.concatenate([n_used, _expert_row(start + tot)] + [zero] * 6,
                                        axis=0).astype(jnp.int32)

    @pl.when(ph == 1)
    def _():
        earlier = (lax.broadcasted_iota(jnp.int32, (TT, TT), 0)
                   < lax.broadcasted_iota(jnp.int32, (TT, TT), 1))
        before = jnp.dot(cnt.astype(BF16), jnp.where(earlier, 1.0, 0.0).astype(BF16),
                         preferred_element_type=F32)
        chunks = jnp.floor((tile_tot + (DMA_ROWS - 1)) / DMA_ROWS)
        seg = _expert_prefix(chunks * DMA_ROWS)
        where_staged = seg + before
        rows = [jnp.sum(jnp.where(oh, where_staged, 0.0), axis=0, keepdims=True) for oh in onehot]
        pos_ref[...] = jnp.concatenate(rows, axis=0).astype(jnp.int32)
        zero = jnp.zeros((1, LANES), F32)
        total = jnp.broadcast_to(jnp.sum(chunks, axis=0, keepdims=True), (1, LANES))
        meta_ref[...] = jnp.concatenate(
            [_expert_row(start_ref[...] + carry_ref[...]), _expert_row(seg), _expert_row(chunks), total]
            + [zero] * 4, axis=0).astype(jnp.int32)
        carry_ref[...] += tile_tot


def _route(idx, n_blocks):
    _, N = idx.shape
    nbp = -(-n_blocks // LANES) * LANES
    tile = pl.BlockSpec((TOP_K, TOKEN_TILE), lambda ph, i: (0, i))
    return pl.pallas_call(
        _route_kernel,
        grid=(2, N // TOKEN_TILE),
        in_specs=[tile],
        out_specs=[pl.BlockSpec((TOP_K, TOKEN_TILE), lambda ph, i: (0, i * ph)),
                   pl.BlockSpec((8, LANES), lambda ph, i: (i * ph, 0)),
                   pl.BlockSpec((1, nbp), lambda ph, i: (0, 0)),
                   pl.BlockSpec((8, LANES), lambda ph, i: (0, 0))],
        out_shape=[jax.ShapeDtypeStruct((TOP_K, N), jnp.int32),
                   jax.ShapeDtypeStruct((N // TOKEN_TILE * 8, LANES), jnp.int32),
                   jax.ShapeDtypeStruct((1, nbp), jnp.int32),
                   jax.ShapeDtypeStruct((8, LANES), jnp.int32)],
        scratch_shapes=[pltpu.VMEM((N_EXPERTS, 1), F32)] * 3,
        compiler_params=_cparams(("arbitrary", "arbitrary")),
        name="route",
    )(idx)


CHUNK_SUBROWS = DMA_ROWS * CHUNKS


def _chunk(ref, first_row, j):
    return ref.at[pl.ds(pl.multiple_of((first_row + j * DMA_ROWS) * CHUNKS, CHUNKS), CHUNK_SUBROWS), :]


def _for_each_chunk(meta_ref, fn):
    def per_expert(e, c):
        slot, off, n = meta_ref[0, e], meta_ref[1, e], meta_ref[2, e]

        def per_chunk(j, c2):
            fn(slot, off, j)
            return c2

        return lax.fori_loop(0, n, per_chunk, c)

    lax.fori_loop(0, N_EXPERTS, per_expert, 0)


def _wait_chunks(meta_ref, src_ref, dst_ref, sem):
    n = meta_ref[3, 0] * CHUNK_SUBROWS

    @pl.when(n > 0)
    def _():
        pltpu.make_async_copy(src_ref.at[pl.ds(0, n), :], dst_ref.at[pl.ds(0, n), :], sem).wait()


PAD_ROWS = MOE_ROWS + DMA_ROWS


def _zero_padding(misc_ref, xs_ref, zeros_ref, sem, n_blocks):
    zeros_ref[...] = jnp.zeros_like(zeros_ref)

    def pad_copy(e):
        first = pl.multiple_of(misc_ref[1, e] * CHUNKS, CHUNKS)
        return pltpu.make_async_copy(zeros_ref, xs_ref.at[pl.ds(first, PAD_ROWS * CHUNKS), :], sem)

    def tail_copy(b):
        first = pl.multiple_of(b * (MOE_ROWS * CHUNKS), MOE_ROWS * CHUNKS)
        return pltpu.make_async_copy(zeros_ref.at[pl.ds(0, MOE_ROWS * CHUNKS), :],
                                     xs_ref.at[pl.ds(first, MOE_ROWS * CHUNKS), :], sem)

    def pad(e, c):
        pad_copy(e).start()
        pad_copy(e).wait()
        return c

    def tail(start_not_wait):
        def body(b, c):
            tail_copy(b).start() if start_not_wait else tail_copy(b).wait()
            return c

        lax.fori_loop(misc_ref[0, 0], n_blocks + 1, body, 0)

    lax.fori_loop(0, N_EXPERTS, pad, 0)
    tail(True)
    tail(False)


def _selection(pos_ref, fill_ref=None):
    TT = pos_ref.shape[1]
    p_iota = lax.broadcasted_iota(jnp.int32, (STAGE_ROWS, TT), 0)
    sel = jnp.zeros((STAGE_ROWS, TT), F32)
    for k in range(TOP_K):
        val = 1.0 if fill_ref is None else fill_ref[k:k + 1, :]
        sel = jnp.where(pos_ref[k:k + 1, :] == p_iota, val, sel)
    return sel.astype(BF16)


def _dispatch_kernel(misc_ref, meta_ref, pos_ref, *refs, first_tiles, n_blocks):
    x_refs = refs[:len(first_tiles)]
    xs_ref, stage_ref, zeros_ref, sem = refs[len(first_tiles):]
    i = pl.program_id(0)

    @pl.when(i == 0)
    def _():
        _zero_padding(misc_ref, xs_ref, zeros_ref, sem, n_blocks)

    x = x_refs[0][...]
    for t0, ref in zip(first_tiles[1:], x_refs[1:]):
        x = jnp.where(i >= t0, ref[...], x)
    staged = jnp.dot(_selection(pos_ref), x, preferred_element_type=F32)
    _store_chunked(stage_ref, staged)
    _for_each_chunk(meta_ref, lambda slot, off, j: pltpu.make_async_copy(
        _chunk(stage_ref, off, j), _chunk(xs_ref, slot, j), sem).start())
    _wait_chunks(meta_ref, stage_ref, xs_ref, sem)


def _dispatch(misc, meta, pos, xns, n_blocks):
    D = xns[0].shape[1]
    tiles = [x.shape[0] // TOKEN_TILE for x in xns]
    first_tiles = tuple(sum(tiles[:g]) for g in range(len(tiles)))

    def x_spec(t0, nt):
        return pl.BlockSpec((TOKEN_TILE, D), lambda i: (jnp.clip(i - t0, 0, nt - 1), 0))

    return pl.pallas_call(
        functools.partial(_dispatch_kernel, first_tiles=first_tiles, n_blocks=n_blocks),
        grid=(sum(tiles),),
        in_specs=[pl.BlockSpec(memory_space=pltpu.SMEM),
                  pl.BlockSpec((8, LANES), lambda i: (i, 0), memory_space=pltpu.SMEM),
                  pl.BlockSpec((TOP_K, TOKEN_TILE), lambda i: (0, i))]
                 + [x_spec(t0, nt) for t0, nt in zip(first_tiles, tiles)],
        out_specs=pl.BlockSpec(memory_space=pl.ANY),
        out_shape=jax.ShapeDtypeStruct(((n_blocks + 1) * MOE_ROWS * CHUNKS, LANES), F32),
        scratch_shapes=[pltpu.VMEM((STAGE_ROWS * CHUNKS, LANES), F32),
                        pltpu.VMEM((PAD_ROWS * CHUNKS, LANES), F32),
                        pltpu.SemaphoreType.DMA(())],
        compiler_params=_cparams(("arbitrary",)),
        name="dispatch",
    )(misc, meta, pos, *xns)


def _expert_kernel(be_ref, nused_ref, xs_ref, wgu_ref, bgu_ref, wd_ref, bd_ref, ys_ref,
                   wgu_bf, wd_bf):
    i = pl.program_id(0)
    D_FF = wd_ref.shape[0]
    new_expert = jnp.logical_or(i == 0, be_ref[i] != be_ref[jnp.maximum(i - 1, 0)])

    @pl.when(jnp.logical_and(i < nused_ref[0], new_expert))
    def _():
        wgu_bf[...] = wgu_ref[...].astype(BF16)
        wd_bf[...] = wd_ref[...].astype(BF16)

    @pl.when(i < nused_ref[0])
    def _():
        x = _load_chunked(xs_ref, MOE_ROWS).astype(BF16)
        hdn = jnp.dot(x, wgu_bf[...], preferred_element_type=F32) + bgu_ref[...]
        glu = jnp.minimum(hdn[:, :D_FF], SWIGLU_LIMIT)
        lin = jnp.clip(hdn[:, D_FF:], -SWIGLU_LIMIT, SWIGLU_LIMIT)
        act = glu * (1.0 / (1.0 + jnp.exp(-SWIGLU_ALPHA * glu))) * (lin + 1.0)
        y = jnp.dot(act.astype(BF16), wd_bf[...], preferred_element_type=F32) + bd_ref[...]
        _store_chunked(ys_ref, y)

    @pl.when(i >= nused_ref[0])
    def _():
        ys_ref[...] = jnp.zeros_like(ys_ref)


def _experts(block_expert, n_used, xs, w_gate_up, b_gate_up, w_down, b_down, n_blocks):
    E, D, F2 = w_gate_up.shape
    D_FF = w_down.shape[1]
    rows = pl.BlockSpec((MOE_ROWS * CHUNKS, LANES), lambda i, be, nu: (i, 0))
    rows_in = pl.BlockSpec((MOE_ROWS * CHUNKS, LANES), lambda i, be, nu: (jnp.minimum(i, nu[0] - 1), 0))
    grid_spec = pltpu.PrefetchScalarGridSpec(
        num_scalar_prefetch=2,
        grid=(n_blocks,),
        in_specs=[rows_in,
                  pl.BlockSpec((None, D, F2), lambda i, be, nu: (be[i], 0, 0)),
                  pl.BlockSpec((None, 1, F2), lambda i, be, nu: (be[i], 0, 0)),
                  pl.BlockSpec((None, D_FF, D), lambda i, be, nu: (be[i], 0, 0)),
                  pl.BlockSpec((None, 1, D), lambda i, be, nu: (be[i], 0, 0))],
        out_specs=rows,
        scratch_shapes=[pltpu.VMEM((D, F2), BF16), pltpu.VMEM((D_FF, D), BF16)],
    )
    return pl.pallas_call(
        _expert_kernel,
        grid_spec=grid_spec,
        out_shape=jax.ShapeDtypeStruct((n_blocks * MOE_ROWS * CHUNKS, LANES), F32),
        compiler_params=_cparams(("arbitrary",)),
        name="experts",
    )(block_expert, n_used, xs, w_gate_up, b_gate_up.reshape(E, 1, F2), w_down, b_down.reshape(E, 1, D))


def _combine_kernel(meta_ref, pos_ref, gate_ref, h_ref, g_ref, ys_ref, y_ref, stage_ref, sem):
    @pl.when(pl.program_id(0) == 0)
    def _():
        stage_ref[...] = jnp.zeros_like(stage_ref)

    _for_each_chunk(meta_ref, lambda slot, off, j: pltpu.make_async_copy(
        _chunk(ys_ref, slot, j), _chunk(stage_ref, off, j), sem).start())
    weights = _selection(pos_ref, gate_ref)
    _wait_chunks(meta_ref, ys_ref, stage_ref, sem)
    staged = _load_chunked(stage_ref, STAGE_ROWS).astype(BF16)
    moe = lax.dot_general(weights, staged, (((0,), (0,)), ((), ())), preferred_element_type=F32)
    hf = h_ref[...] + moe
    ms = jnp.mean(hf * hf, axis=-1, keepdims=True)
    y_ref[...] = (hf * lax.rsqrt(ms + EPS)) * g_ref[...]


def _combine(meta, pos, gates, h, norm_g, ys, tile0):
    n, D = h.shape
    return pl.pallas_call(
        _combine_kernel,
        grid=(n // TOKEN_TILE,),
        in_specs=[pl.BlockSpec((8, LANES), lambda i: (tile0 + i, 0), memory_space=pltpu.SMEM),
                  pl.BlockSpec((TOP_K, TOKEN_TILE), lambda i: (0, tile0 + i)),
                  pl.BlockSpec((TOP_K, TOKEN_TILE), lambda i: (0, i)),
                  pl.BlockSpec((TOKEN_TILE, D), lambda i: (i, 0)),
                  pl.BlockSpec((1, D), lambda i: (0, 0)),
                  pl.BlockSpec(memory_space=pl.ANY)],
        out_specs=pl.BlockSpec((TOKEN_TILE, D), lambda i: (i, 0)),
        out_shape=jax.ShapeDtypeStruct((n, D), F32),
        scratch_shapes=[pltpu.VMEM((STAGE_ROWS * CHUNKS, LANES), F32), pltpu.SemaphoreType.DMA(())],
        compiler_params=_cparams(("arbitrary",)),
        name="combine",
    )(meta, pos, gates, h, norm_g, ys)


def _prep_weights(w_in, w_alpha):
    w_main = w_in[:, :PROJ_MAIN].astype(BF16)
    w_lr = jnp.pad(w_in[:, PROJ_MAIN:], ((0, 0), (0, LANES - GATE_RANK))).astype(BF16)
    w_al = jnp.pad(w_alpha, ((0, LANES - GATE_RANK), (0, 0))).astype(BF16)
    return w_main, w_lr, w_al


def kernel(x_prompt, x_sample, cache_swa_k, cache_swa_v, state_gla, norm_mix_g, w_in, w_alpha, b_alpha, gla_norm_g, w_out, norm_ffn_g, w_router, b_router, w_gate_up, b_gate_up, w_down, b_down, norm_final_g):
    B, S, D = x_prompt.shape
    Bs, Ts, _ = x_sample.shape
    assert w_in.shape[0] == 1, "single-layer trunk"
    l = 0
    R = cache_swa_k.shape[2]
    rows_p = min(DILATIONS[-1] * KEYS_PER_CONFIG, S)
    w_main, w_lr, w_al = _prep_weights(w_in[l], w_alpha[l])
    g_mix = norm_mix_g[l][None]
    b_al = b_alpha[l][None]
    g_gla = gla_norm_g[l][None]

    pos_p = jnp.arange(S, dtype=jnp.int32)
    qa, ka, va, qb, kb, vb, zg, gb = _project(x_prompt, pos_p, g_mix, w_main, w_lr, w_al, b_al, PROJ_TILE)
    oa_p = _prompt_attention(qa, ka, va)
    ob_p, st_p = _gla(qb, kb, gb, vb, zg, jnp.zeros((B, N_HEADS_B, DK_B, DV_B), F32), g_gla,
                      GLA_TILE, GLA_CHUNK)
    k_prompt = ka[:, S - rows_p:].reshape(1, B, rows_p, N_HEADS_A, HEAD_DIM_A)
    v_prompt = va[:, S - rows_p:].reshape(1, B, rows_p, N_HEADS_A, HEAD_DIM_A)

    pos_s = PAST_LEN + (jnp.arange(Bs * Ts, dtype=jnp.int32) % Ts)
    proj_s = _project(x_sample.reshape(1, Bs * Ts, D), pos_s, g_mix, w_main, w_lr, w_al, b_al, Bs * Ts)
    qa_s, ka_s, va_s, qb_s, kb_s, vb_s, zg_s, gb_s = [t.reshape(Bs, Ts, -1) for t in proj_s]
    oa_s, k_sample, v_sample = _sample_attention(qa_s, ka_s, va_s,
                                                 cache_swa_k[l].reshape(Bs, R, WIDTH_A),
                                                 cache_swa_v[l].reshape(Bs, R, WIDTH_A))
    ob_s, st_s = _gla(qb_s, kb_s, gb_s, vb_s, zg_s, state_gla[l], g_gla, Ts, Ts)

    w_out_bf = w_out[l].astype(BF16)
    g_ffn = norm_ffn_g[l][None]
    w_router_t = jnp.transpose(w_router[l])
    b_router_c = b_router[l][:, None]
    Np, Ns = B * S, Bs * Ts
    h_p, xn_p, idx_p, gate_p = _merge(oa_p.reshape(Np, WIDTH_A), ob_p.reshape(Np, WIDTH_B),
                                      x_prompt.reshape(Np, D), w_out_bf, g_ffn, w_router_t, b_router_c,
                                      MERGE_TILE)
    h_s, xn_s, idx_s, gate_s = _merge(oa_s.reshape(Ns, WIDTH_A), ob_s.reshape(Ns, WIDTH_B),
                                      x_sample.reshape(Ns, D), w_out_bf, g_ffn, w_router_t, b_router_c,
                                      Ns)

    y_p, y_s = _moe([(xn_p, idx_p, gate_p, h_p), (xn_s, idx_s, gate_s, h_s)],
                    w_gate_up[l], b_gate_up[l], w_down[l], b_down[l], norm_final_g[None])
    return (y_p.reshape(B, S, D), y_s.reshape(Bs, Ts, D), k_prompt, v_prompt, st_p[None],
            k_sample.reshape(1, Bs, R, N_HEADS_A, HEAD_DIM_A),
            v_sample.reshape(1, Bs, R, N_HEADS_A, HEAD_DIM_A), st_s[None])


PAST_LEN = 16384
PROJ_TILE = 512
MERGE_TILE = 512
GLA_TILE = 1024
GLA_CHUNK = 64


def _moe(groups, w_gate_up, b_gate_up, w_down, b_down, g_final):
    sizes = [g[3].shape[0] for g in groups]
    N = sum(sizes)
    assert all(n % TOKEN_TILE == 0 for n in sizes)
    n_blocks = -(-(N * TOP_K + N_EXPERTS * (MOE_ROWS - 1 + DMA_ROWS)) // MOE_ROWS)
    idx = jnp.concatenate([g[1] for g in groups], axis=1)
    pos, meta, block_expert, misc = _route(idx, n_blocks)
    first_tile = [sum(sizes[:i]) // TOKEN_TILE for i in range(len(sizes))]
    xs = _dispatch(misc, meta, pos, [g[0] for g in groups], n_blocks)
    ys = _experts(block_expert[0], misc[0, :1], xs, w_gate_up, b_gate_up, w_down, b_down, n_blocks)
    return [_combine(meta, pos, gates, h, g_final, ys, t0)
            for (_, _, gates, h), t0 in zip(groups, first_tile)]
```

```python
import functools

import jax
import jax.numpy as jnp
import numpy as np
from jax import lax
from jax.experimental import pallas as pl
from jax.experimental.pallas import tpu as pltpu

F32 = jnp.float32
BF16 = jnp.bfloat16

N_HEADS_A = 8
HEAD_DIM_A = 64
WIDTH_A = N_HEADS_A * HEAD_DIM_A
N_HEADS_B = 4
DK_B = 64
DV_B = 128
QK_B = N_HEADS_B * DK_B
WIDTH_B = N_HEADS_B * DV_B
GATE_RANK = 16
GATE_LOGIT_NORM = 16.0
DILATIONS = (1, 4, 16)
KEYS_PER_CONFIG = 128
ROPE_THETA = 10000.0
N_EXPERTS = 32
TOP_K = 4
SWIGLU_ALPHA = 1.702
SWIGLU_LIMIT = 7.0
EPS = 1e-6

LANES = 128
VMEM_LIMIT = 56 * 1024 * 1024


def _cparams(sem, vmem=VMEM_LIMIT):
    return pltpu.CompilerParams(dimension_semantics=sem, vmem_limit_bytes=vmem)


PROJ_MAIN = 3 * WIDTH_A + 2 * QK_B + 2 * WIDTH_B


def _rope_tables(pos):
    half = HEAD_DIM_A // 2
    inv = ROPE_THETA ** (-jnp.arange(half, dtype=F32) / half)
    ang = pos.astype(F32)[:, None] * inv[None, :]
    cos = jnp.cos(ang)
    sin = jnp.sin(ang)
    cos_t = jnp.concatenate([cos, cos, cos, cos], axis=-1)
    sin_t = jnp.concatenate([-sin, sin, -sin, sin], axis=-1)
    return cos_t, sin_t


def _rope_block(t, cos, sin, first_half):
    partner = jnp.where(first_half, pltpu.roll(t, LANES - 32, 1), pltpu.roll(t, 32, 1))
    return t * cos + partner * sin


def _proj_kernel(x_ref, g_ref, w_ref, wlr_ref, wa_ref, ba_ref, cos_ref, sin_ref,
                 qa_ref, ka_ref, va_ref, qb_ref, kb_ref, vb_ref, zg_ref, gb_ref, *tail_refs, first_tail_tile):
    x = x_ref[...]
    ms = jnp.mean(x * x, axis=-1, keepdims=True)
    xn = ((x * lax.rsqrt(ms + EPS)) * g_ref[...]).astype(BF16)

    def cols(lo, hi):
        return jnp.dot(xn, w_ref[:, lo:hi], preferred_element_type=F32)

    cos = cos_ref[...]
    sin = sin_ref[...]
    lane = lax.broadcasted_iota(jnp.int32, cos.shape, 1)
    first_half = (lane % HEAD_DIM_A) < (HEAD_DIM_A // 2)
    q = cols(0, WIDTH_A)
    k = cols(WIDTH_A, 2 * WIDTH_A)
    for j in range(WIDTH_A // LANES):
        sl = slice(j * LANES, (j + 1) * LANES)
        qa_ref[:, sl] = _rope_block(q[:, sl], cos, sin, first_half) * (HEAD_DIM_A ** -0.5)
        ka_ref[:, sl] = _rope_block(k[:, sl], cos, sin, first_half)
    o = 2 * WIDTH_A
    va_ref[...] = cols(o, o + WIDTH_A)
    o += WIDTH_A
    qb_ref[...] = cols(o, o + QK_B) * (DK_B ** -0.5)
    o += QK_B
    kb_ref[...] = cols(o, o + QK_B)
    o += QK_B
    vb_ref[...] = cols(o, o + WIDTH_B).astype(vb_ref.dtype)
    o += WIDTH_B
    zg_ref[...] = cols(o, o + WIDTH_B)
    lr = jnp.dot(xn, wlr_ref[...], preferred_element_type=F32)
    z = jnp.dot(lr.astype(BF16), wa_ref[...], preferred_element_type=F32) + ba_ref[...]
    logsig = jnp.minimum(z, 0.0) - jnp.log(1.0 + jnp.exp(-jnp.abs(z)))
    gb_ref[...] = logsig / GATE_LOGIT_NORM
    if tail_refs:
        @pl.when(pl.program_id(1) >= first_tail_tile)
        def _():
            tail_refs[0][...] = ka_ref[...]
            tail_refs[1][...] = va_ref[...]


def _project(x, pos, norm_g, w_main, w_lr, w_alpha, b_alpha, tm, tail_rows=0):
    B, T, D = x.shape
    assert tail_rows % tm == 0
    cos_t, sin_t = _rope_tables(pos)
    grid = (B, T // tm)
    first_tail_tile = (T - tail_rows) // tm
    row = lambda w: pl.BlockSpec((None, tm, w), lambda b, i: (b, i, 0))
    tail = pl.BlockSpec((None, tm, WIDTH_A), lambda b, i: (b, jnp.maximum(i - first_tail_tile, 0), 0))
    full = lambda a: pl.BlockSpec(a.shape, lambda b, i: (0,) * a.ndim)
    tab = pl.BlockSpec((tm, LANES), lambda b, i: (i, 0))
    widths = (WIDTH_A, WIDTH_A, WIDTH_A, QK_B, QK_B, WIDTH_B, WIDTH_B, QK_B)
    dtypes = (F32, F32, F32, F32, F32, BF16, F32, F32)
    n_tail = 2 if tail_rows else 0
    return pl.pallas_call(
        functools.partial(_proj_kernel, first_tail_tile=first_tail_tile),
        grid=grid,
        in_specs=[row(D), full(norm_g), full(w_main), full(w_lr), full(w_alpha), full(b_alpha), tab, tab],
        out_specs=[row(w) for w in widths] + [tail] * n_tail,
        out_shape=[jax.ShapeDtypeStruct((B, T, w), dt) for w, dt in zip(widths, dtypes)]
                  + [jax.ShapeDtypeStruct((B, tail_rows, WIDTH_A), F32)] * n_tail,
        compiler_params=_cparams(("parallel", "arbitrary")),
        name="proj",
    )(x, norm_g, w_main, w_lr, w_alpha, b_alpha, cos_t, sin_t)


Q_BLOCK = 128
NEG_INF = float("-inf")


def _attn_block(q, k, v, mask, head0, state):
    kb = k.astype(BF16)
    vb = v.astype(BF16)
    v_head0 = lax.broadcasted_iota(jnp.int32, vb.shape, 1) < HEAD_DIM_A
    m_new, pvs = [], []
    for h in range(2):
        hm = head0 if h == 0 else jnp.logical_not(head0)
        qh = jnp.where(hm, q, 0.0).astype(BF16)
        s = lax.dot_general(qh, kb, (((1,), (1,)), ((), ())), preferred_element_type=F32)
        s = jnp.where(mask, s, NEG_INF)
        if state is None:
            mh = jnp.max(s, axis=1, keepdims=True)
        else:
            prev = jnp.where(hm, state[0], NEG_INF)
            mh = jnp.max(jnp.concatenate([s, prev], axis=1), axis=1, keepdims=True)
        p = jnp.exp(s - mh)
        m_new.append(mh)
        vh = jnp.where(v_head0 if h == 0 else jnp.logical_not(v_head0), vb, jnp.ones_like(vb))
        pvs.append(jnp.dot(p.astype(BF16), vh, preferred_element_type=F32))
    m_full = jnp.where(head0, m_new[0], m_new[1])
    l_swapped = jnp.where(head0, pvs[1], pvs[0])
    pv_full = jnp.where(head0, pvs[0], pvs[1])
    if state is None:
        return m_full, l_swapped, pv_full
    a = jnp.exp(state[0] - m_full)
    a_swapped = pltpu.roll(a, HEAD_DIM_A, 1)
    return m_full, a_swapped * state[1] + l_swapped, a * state[2] + pv_full


def _attn_kernel(q_ref, k_ref, v_ref, o_ref, m_ref, l_ref, acc_ref):
    S = q_ref.shape[0]
    QB = Q_BLOCK
    lane = lax.broadcasted_iota(jnp.int32, (QB, LANES), 1)
    head0 = lane < HEAD_DIM_A
    qq = lax.broadcasted_iota(jnp.int32, (QB, 2 * QB), 0)
    kk = lax.broadcasted_iota(jnp.int32, (QB, 2 * QB), 1)
    band = jnp.logical_and(kk >= qq, kk - qq <= KEYS_PER_CONFIG)
    causal = (lax.broadcasted_iota(jnp.int32, (QB, QB), 1)
              <= lax.broadcasted_iota(jnp.int32, (QB, QB), 0))

    GROUP = 4

    for ci, dil in enumerate(DILATIONS):
        nblk = S // (dil * QB)
        assert nblk % GROUP == 0 or GROUP % nblk == 0

        def rows(start, n, dil=dil):
            return pl.ds(start, n) if dil == 1 else pl.ds(start, n, stride=dil)

        def do_group(blocks, ci=ci, dil=dil, rows=rows):
            loaded = []
            for r, j, first in blocks:
                q0 = r + dil * QB * j
                qs = rows(q0, QB)
                ks = rows(r, QB) if first else rows(q0 - dil * QB, 2 * QB)
                state = None if ci == 0 else (m_ref[qs, :], l_ref[qs, :], acc_ref[qs, :])
                loaded.append((qs, q_ref[qs, :], k_ref[ks, :], v_ref[ks, :], causal if first else band, state))
            results = [_attn_block(q, k, v, mask, head0, state) for _, q, k, v, mask, state in loaded]
            for (qs, *_), (m, l, acc) in zip(loaded, results):
                m_ref[qs, :] = m
                l_ref[qs, :] = l
                acc_ref[qs, :] = acc

        if nblk >= GROUP:
            def residue(r, carry, nblk=nblk, do_group=do_group):
                do_group([(r, j, j == 0) for j in range(GROUP)])

                def rest(g, c):
                    do_group([(r, GROUP * g + u, False) for u in range(GROUP)])
                    return c

                return lax.fori_loop(1, nblk // GROUP, rest, carry)

            lax.fori_loop(0, dil, residue, 0)
        else:
            per = GROUP // nblk

            def residues(g, carry, nblk=nblk, per=per, do_group=do_group):
                do_group([(g * per + i, j, j == 0) for i in range(per) for j in range(nblk)])
                return carry

            lax.fori_loop(0, dil // per, residues, 0)

    def finish(i, c):
        rs = pl.ds(pl.multiple_of(i * QB, QB), QB)
        den = pltpu.roll(l_ref[rs, :], HEAD_DIM_A, 1)
        o_ref[rs, :] = (acc_ref[rs, :] / den).astype(o_ref.dtype)
        return c

    lax.fori_loop(0, S // QB, finish, 0, unroll=4)


def _prompt_attention(qa, ka, va):
    B, S, W = qa.shape
    spec = pl.BlockSpec((None, S, LANES), lambda b, hp: (b, 0, hp))
    return pl.pallas_call(
        _attn_kernel,
        grid=(B, W // LANES),
        in_specs=[spec, spec, spec],
        out_specs=spec,
        out_shape=jax.ShapeDtypeStruct((B, S, W), BF16),
        scratch_shapes=[pltpu.VMEM((S, LANES), F32)] * 3,
        compiler_params=_cparams(("parallel", "parallel")),
        name="prompt_attn",
    )(qa, ka, va)


def _split3(x):
    x1 = x.astype(BF16)
    r1 = x - x1.astype(F32)
    x2 = r1.astype(BF16)
    x3 = (r1 - x2.astype(F32)).astype(BF16)
    return x1, x2, x3


def _gla_kernel(q_ref, k_ref, g_ref, v_ref, z_ref, s0_ref, ng_ref, o_ref, sfin_ref, st_ref, *, chunk):
    C = chunk
    TS = q_ref.shape[0]
    n_pairs = N_HEADS_B // 2
    PW = 2 * DV_B
    t_idx = pl.program_id(1)

    lane_k = lax.broadcasted_iota(jnp.int32, (C, LANES), 1)
    head0 = lane_k < DK_B
    tri_incl = (lax.broadcasted_iota(jnp.int32, (C, C), 1)
                <= lax.broadcasted_iota(jnp.int32, (C, C), 0))
    tri_bf = jnp.where(tri_incl, 1.0, 0.0).astype(BF16)
    bd_mask = ((lax.broadcasted_iota(jnp.int32, (PW, LANES), 0) // DV_B)
               == (lax.broadcasted_iota(jnp.int32, (PW, LANES), 1) // DK_B))

    @pl.when(t_idx == 0)
    def _():
        for p in range(n_pairs):
            for h in range(2):
                blk = jnp.transpose(s0_ref[2 * p + h])
                pad = jnp.zeros((DV_B, DK_B), F32)
                row = jnp.concatenate([blk, pad] if h == 0 else [pad, blk], axis=1)
                st_ref[p, h * DV_B:(h + 1) * DV_B, :] = row

    def chunk_body(c, carry):
        rs = pl.ds(pl.multiple_of(c * C, C), C)
        for p in range(n_pairs):
            kl = slice(p * LANES, (p + 1) * LANES)
            vl = slice(p * PW, (p + 1) * PW)
            q = q_ref[rs, kl]
            k = k_ref[rs, kl]
            g = g_ref[rs, kl]
            v = v_ref[rs, vl]
            g1, g2, g3 = _split3(g)
            b = (jnp.dot(tri_bf, g1, preferred_element_type=F32)
                 + jnp.dot(tri_bf, g2, preferred_element_type=F32)
                 + jnp.dot(tri_bf, g3, preferred_element_type=F32))
            b_last = b[C - 1:C, :]
            b_mid = b[C // 2 - 1:C // 2, :] if C > 1 else b_last
            qe = q * jnp.exp(b - b_mid)
            ke = (k * jnp.exp(b_mid - b)).astype(BF16)
            st = st_ref[p]
            q_in = (q * jnp.exp(b)).astype(BF16)
            o = lax.dot_general(q_in, st.astype(BF16), (((1,), (1,)), ((), ())),
                                preferred_element_type=F32)
            intra = []
            for h in range(2):
                hm = head0 if h == 0 else jnp.logical_not(head0)
                a = lax.dot_general(jnp.where(hm, qe, 0.0).astype(BF16), ke,
                                    (((1,), (1,)), ((), ())), preferred_element_type=F32)
                a = jnp.where(tri_incl, a, 0.0).astype(BF16)
                intra.append(jnp.dot(a, v[:, h * DV_B:(h + 1) * DV_B], preferred_element_type=F32))
            o = o + jnp.concatenate(intra, axis=1)
            k_dec = (k * jnp.exp(b_last - b)).astype(BF16)
            upd = lax.dot_general(v, k_dec, (((0,), (0,)), ((), ())), preferred_element_type=F32)
            st_ref[p] = jnp.exp(b_last) * st + jnp.where(bd_mask, upd, 0.0)
            for h in range(2):
                oh = o[:, h * DV_B:(h + 1) * DV_B]
                hl = slice((2 * p + h) * DV_B, (2 * p + h + 1) * DV_B)
                ms = jnp.mean(oh * oh, axis=-1, keepdims=True)
                z = z_ref[rs, hl]
                gated = (oh * lax.rsqrt(ms + EPS)) * ng_ref[:, hl] * (z / (1.0 + jnp.exp(-z)))
                o_ref[rs, hl] = gated.astype(o_ref.dtype)
        return carry

    lax.fori_loop(0, TS // C, chunk_body, 0, unroll=2 if (TS // C) % 2 == 0 else 1)

    @pl.when(t_idx == pl.num_programs(1) - 1)
    def _():
        for p in range(n_pairs):
            for h in range(2):
                blk = st_ref[p, h * DV_B:(h + 1) * DV_B, h * DK_B:(h + 1) * DK_B]
                sfin_ref[2 * p + h] = jnp.transpose(blk)


def _gla(qb, kb, gb, vb, zg, state0, norm_g, ts, chunk):
    B, T, _ = qb.shape
    row = lambda w: pl.BlockSpec((None, ts, w), lambda b, i: (b, i, 0))
    st_spec = pl.BlockSpec((None, N_HEADS_B, DK_B, DV_B), lambda b, i: (b, 0, 0, 0))
    return pl.pallas_call(
        functools.partial(_gla_kernel, chunk=chunk),
        grid=(B, T // ts),
        in_specs=[row(QK_B), row(QK_B), row(QK_B), row(WIDTH_B), row(WIDTH_B), st_spec,
                  pl.BlockSpec((1, WIDTH_B), lambda b, i: (0, 0))],
        out_specs=[row(WIDTH_B), st_spec],
        out_shape=[jax.ShapeDtypeStruct((B, T, WIDTH_B), BF16),
                   jax.ShapeDtypeStruct((B, N_HEADS_B, DK_B, DV_B), F32)],
        scratch_shapes=[pltpu.VMEM((N_HEADS_B // 2, 2 * DV_B, LANES), F32)],
        compiler_params=_cparams(("parallel", "arbitrary")),
        name="gla",
    )(qb, kb, gb, vb, zg, state0, norm_g)


def _sample_attn_kernel(q_ref, kn_ref, vn_ref, kc_ref, vc_ref, o_ref, ko_ref, vo_ref):
    T = q_ref.shape[0]
    R = kc_ref.shape[0]
    W = q_ref.shape[1]
    HT = N_HEADS_A * T
    q = q_ref[...]
    qx = jnp.concatenate([q] * N_HEADS_A, axis=0)
    own = ((lax.broadcasted_iota(jnp.int32, (HT, W), 0) // T)
           == (lax.broadcasted_iota(jnp.int32, (HT, W), 1) // HEAD_DIM_A))
    qx = jnp.where(own, qx, 0.0).astype(BF16)
    pad = jnp.zeros((LANES - T, W), F32)
    kn = jnp.concatenate([kn_ref[...], pad], axis=0).astype(BF16)
    vn = jnp.concatenate([vn_ref[...], pad], axis=0).astype(BF16)
    nt = (((1,), (1,)), ((), ()))
    s_c = lax.dot_general(qx, kc_ref[...].astype(BF16), nt, preferred_element_type=F32)
    s_n = lax.dot_general(qx, kn, nt, preferred_element_type=F32)

    def multiplicity(n_cols, first_row):
        t = lax.broadcasted_iota(jnp.int32, (HT, n_cols), 0) % T
        j = lax.broadcasted_iota(jnp.int32, (HT, n_cols), 1) + first_row
        delta = R + t - j
        cnt = jnp.zeros((HT, n_cols), F32)
        for dil in DILATIONS:
            hit = (delta >= 0) & (delta <= dil * KEYS_PER_CONFIG) & (delta % dil == 0)
            cnt = cnt + jnp.where(hit, 1.0, 0.0)
        return cnt

    cnt_c = multiplicity(R, 0)
    cnt_n = multiplicity(LANES, R)
    s_c = jnp.where(cnt_c > 0.0, s_c, NEG_INF)
    s_n = jnp.where(cnt_n > 0.0, s_n, NEG_INF)
    m = jnp.maximum(jnp.max(s_c, axis=1, keepdims=True), jnp.max(s_n, axis=1, keepdims=True))
    p_c = cnt_c * jnp.exp(s_c - m)
    p_n = cnt_n * jnp.exp(s_n - m)
    den = jnp.sum(p_c, axis=1, keepdims=True) + jnp.sum(p_n, axis=1, keepdims=True)
    full = (jnp.dot(p_c.astype(BF16), vc_ref[...].astype(BF16), preferred_element_type=F32)
            + jnp.dot(p_n.astype(BF16), vn, preferred_element_type=F32)) / den
    full = jnp.where(own, full, 0.0)
    out = full[0:T, :]
    for h in range(1, N_HEADS_A):
        out = out + full[h * T:(h + 1) * T, :]
    o_ref[...] = out.astype(o_ref.dtype)
    ko_ref[0:R - T, :] = kc_ref[T:R, :]
    ko_ref[R - T:R, :] = kn_ref[...]
    vo_ref[0:R - T, :] = vc_ref[T:R, :]
    vo_ref[R - T:R, :] = vn_ref[...]


def _sample_attention(qa, ka, va, cache_k, cache_v):
    B, T, W = qa.shape
    R = cache_k.shape[1]
    assert R >= DILATIONS[-1] * KEYS_PER_CONFIG and T % 8 == 0 and T <= LANES
    new = pl.BlockSpec((None, T, W), lambda b: (b, 0, 0))
    cache = pl.BlockSpec((None, R, W), lambda b: (b, 0, 0))
    return pl.pallas_call(
        _sample_attn_kernel,
        grid=(B,),
        in_specs=[new, new, new, cache, cache],
        out_specs=[new, cache, cache],
        out_shape=[jax.ShapeDtypeStruct((B, T, W), BF16),
                   jax.ShapeDtypeStruct((B, R, W), F32),
                   jax.ShapeDtypeStruct((B, R, W), F32)],
        compiler_params=_cparams(("parallel",)),
        name="sample_attn",
    )(qa, ka, va, cache_k, cache_v)


CHUNKS = 8


def _store_chunked(ref, val):
    n = val.shape[0]
    for s in range(CHUNKS):
        ref[pl.ds(s, n, stride=CHUNKS), :] = val[:, s * LANES:(s + 1) * LANES]


def _load_chunked(ref, n):
    return jnp.concatenate([ref[pl.ds(s, n, stride=CHUNKS), :] for s in range(CHUNKS)], axis=1)


def _split2(x):
    hi = x.astype(BF16)
    return hi, (x - hi.astype(F32)).astype(BF16)


def _merge_kernel(oa_ref, ob_ref, x_ref, wo_ref, g_ref, wr_ref, br_ref,
                  h_ref, xn_ref, idx_ref, gate_ref):
    TM = x_ref.shape[0]
    mixed = (jnp.dot(oa_ref[...], wo_ref[0:WIDTH_A, :], preferred_element_type=F32)
             + jnp.dot(ob_ref[...], wo_ref[WIDTH_A:, :], preferred_element_type=F32))
    h = x_ref[...] + mixed
    h_ref[...] = h
    ms = jnp.mean(h * h, axis=-1, keepdims=True)
    xn = (h * lax.rsqrt(ms + EPS)) * g_ref[...]
    xn_ref[...] = xn.astype(xn_ref.dtype)
    nt = (((1,), (1,)), ((), ()))
    xh, xl = _split2(xn)
    wh, wl = _split2(wr_ref[...])
    logits = (lax.dot_general(wh, xh, nt, preferred_element_type=F32)
              + lax.dot_general(wh, xl, nt, preferred_element_type=F32)
              + lax.dot_general(wl, xh, nt, preferred_element_type=F32)) + br_ref[...]
    e_iota = lax.broadcasted_iota(jnp.int32, (N_EXPERTS, TM), 0)
    vals, idxs = [], []
    for _ in range(TOP_K):
        m = jnp.max(logits, axis=0, keepdims=True)
        sel = jnp.min(jnp.where(logits == m, e_iota, N_EXPERTS), axis=0, keepdims=True)
        vals.append(m)
        idxs.append(sel)
        logits = jnp.where(e_iota == sel, NEG_INF, logits)
    ex = [jnp.exp(v - vals[0]) for v in vals]
    den = ex[0] + ex[1] + ex[2] + ex[3]
    idx_ref[...] = jnp.concatenate(idxs, axis=0)
    gate_ref[...] = jnp.concatenate([e / den for e in ex], axis=0)


def _merge(oa, ob, x, w_out, norm_g, w_router_t, b_router, tm):
    N, D = x.shape
    full = lambda a: pl.BlockSpec(a.shape, lambda i: (0,) * a.ndim)
    row = lambda w: pl.BlockSpec((tm, w), lambda i: (i, 0))
    col = pl.BlockSpec((TOP_K, tm), lambda i: (0, i))
    return pl.pallas_call(
        _merge_kernel,
        grid=(N // tm,),
        in_specs=[row(WIDTH_A), row(WIDTH_B), row(D), full(w_out), full(norm_g), full(w_router_t),
                  full(b_router)],
        out_specs=[row(D), row(D), col, col],
        out_shape=[jax.ShapeDtypeStruct((N, D), F32),
                   jax.ShapeDtypeStruct((N, D), BF16),
                   jax.ShapeDtypeStruct((TOP_K, N), jnp.int32),
                   jax.ShapeDtypeStruct((TOP_K, N), F32)],
        compiler_params=_cparams(("parallel",)),
        name="merge_router",
    )(oa, ob, x, w_out, norm_g, w_router_t, b_router)


MOE_ROWS = 512
TOKEN_TILE = 256
DMA_ROWS = 8
STAGE_ROWS = 1280
assert STAGE_ROWS >= TOKEN_TILE * TOP_K + N_EXPERTS * (DMA_ROWS - 1) and STAGE_ROWS % 8 == 0


def _expert_row(col):
    r = lax.broadcasted_iota(jnp.int32, (N_EXPERTS, LANES), 0)
    c = lax.broadcasted_iota(jnp.int32, (N_EXPERTS, LANES), 1)
    return jnp.sum(jnp.where(r == c, col, 0.0), axis=0, keepdims=True)


def _expert_prefix(col):
    r = lax.broadcasted_iota(jnp.int32, (N_EXPERTS, LANES), 0)
    c = lax.broadcasted_iota(jnp.int32, (N_EXPERTS, LANES), 1)
    return jnp.sum(jnp.where(c < r, _expert_row(col), 0.0), axis=1, keepdims=True)


def _route_kernel(idx_all_ref, idx_ref, pos_ref, meta_ref, be_ref, misc_ref, carry_ref, start_ref):
    i = pl.program_id(0)
    TT = idx_ref.shape[1]
    NBP = be_ref.shape[1]

    @pl.when(i == 0)
    def _():
        idx_all = idx_all_ref[...]
        e_all = lax.broadcasted_iota(jnp.int32, (N_EXPERTS, idx_all.shape[1]), 0)
        tot = jnp.zeros((N_EXPERTS, 1), F32)
        for k in range(TOP_K):
            tot = tot + jnp.sum(jnp.where(idx_all[k:k + 1, :] == e_all, 1.0, 0.0), axis=1, keepdims=True)
        padded = jnp.floor((tot + (DMA_ROWS + MOE_ROWS - 1)) / MOE_ROWS) * MOE_ROWS
        start = _expert_prefix(padded)
        start_ref[...] = start
        carry_ref[...] = jnp.zeros_like(carry_ref)
        end = start + padded
        block_start = lax.broadcasted_iota(jnp.int32, (N_EXPERTS, NBP), 1).astype(F32) * MOE_ROWS
        be = jnp.sum(jnp.where(end <= block_start, 1.0, 0.0), axis=0, keepdims=True)
        be_ref[...] = jnp.minimum(be, N_EXPERTS - 1).astype(jnp.int32)
        n_used = jnp.broadcast_to(jnp.sum(padded, axis=0, keepdims=True) / MOE_ROWS, (1, LANES))
        zero = jnp.zeros((1, LANES), F32)
        misc_ref[...] = jnp.concatenate([n_used, _expert_row(start + tot)] + [zero] * 6,
                                        axis=0).astype(jnp.int32)

    idx = idx_ref[...]
    e_iota = lax.broadcasted_iota(jnp.int32, (N_EXPERTS, TT), 0)
    onehot = [idx[k:k + 1, :] == e_iota for k in range(TOP_K)]
    cnt = jnp.zeros((N_EXPERTS, TT), F32)
    for oh in onehot:
        cnt = cnt + jnp.where(oh, 1.0, 0.0)
    tile_tot = jnp.sum(cnt, axis=1, keepdims=True)
    earlier = (lax.broadcasted_iota(jnp.int32, (TT, TT), 0)
               < lax.broadcasted_iota(jnp.int32, (TT, TT), 1))
    before = jnp.dot(cnt.astype(BF16), jnp.where(earlier, 1.0, 0.0).astype(BF16),
                     preferred_element_type=F32)
    chunks = jnp.floor((tile_tot + (DMA_ROWS - 1)) / DMA_ROWS)
    seg = _expert_prefix(chunks * DMA_ROWS)
    where_staged = seg + before
    rows = [jnp.sum(jnp.where(oh, where_staged, 0.0), axis=0, keepdims=True) for oh in onehot]
    pos_ref[...] = jnp.concatenate(rows, axis=0).astype(jnp.int32)
    zero = jnp.zeros((1, LANES), F32)
    total = jnp.broadcast_to(jnp.sum(chunks, axis=0, keepdims=True), (1, LANES))
    meta_ref[...] = jnp.concatenate(
        [_expert_row(start_ref[...] + carry_ref[...]), _expert_row(seg), _expert_row(chunks), total]
        + [zero] * 4, axis=0).astype(jnp.int32)
    carry_ref[...] += tile_tot


def _route(idx, n_blocks):
    _, N = idx.shape
    nbp = -(-n_blocks // LANES) * LANES
    tile = pl.BlockSpec((TOP_K, TOKEN_TILE), lambda i: (0, i))
    return pl.pallas_call(
        _route_kernel,
        grid=(N // TOKEN_TILE,),
        in_specs=[pl.BlockSpec((TOP_K, N), lambda i: (0, 0)), tile],
        out_specs=[tile,
                   pl.BlockSpec((8, LANES), lambda i: (i, 0)),
                   pl.BlockSpec((1, nbp), lambda i: (0, 0)),
                   pl.BlockSpec((8, LANES), lambda i: (0, 0))],
        out_shape=[jax.ShapeDtypeStruct((TOP_K, N), jnp.int32),
                   jax.ShapeDtypeStruct((N // TOKEN_TILE * 8, LANES), jnp.int32),
                   jax.ShapeDtypeStruct((1, nbp), jnp.int32),
                   jax.ShapeDtypeStruct((8, LANES), jnp.int32)],
        scratch_shapes=[pltpu.VMEM((N_EXPERTS, 1), F32)] * 2,
        compiler_params=_cparams(("arbitrary",)),
        name="route",
    )(idx, idx)


CHUNK_SUBROWS = DMA_ROWS * CHUNKS


def _chunk(ref, first_row, j):
    return ref.at[pl.ds(pl.multiple_of((first_row + j * DMA_ROWS) * CHUNKS, CHUNKS), CHUNK_SUBROWS), :]


def _for_each_chunk(meta_ref, fn):
    def per_expert(e, c):
        slot, off, n = meta_ref[0, e], meta_ref[1, e], meta_ref[2, e]

        def per_chunk(j, c2):
            fn(slot, off, j)
            return c2

        return lax.fori_loop(0, n, per_chunk, c)

    lax.fori_loop(0, N_EXPERTS, per_expert, 0)


def _wait_chunks(meta_ref, src_ref, dst_ref, sem):
    n = meta_ref[3, 0] * CHUNK_SUBROWS

    @pl.when(n > 0)
    def _():
        pltpu.make_async_copy(src_ref.at[pl.ds(0, n), :], dst_ref.at[pl.ds(0, n), :], sem).wait()


PAD_ROWS = MOE_ROWS + DMA_ROWS


def _zero_padding(misc_ref, xs_ref, zeros_ref, sem, n_blocks):
    zeros_ref[...] = jnp.zeros_like(zeros_ref)

    def pad_copy(e):
        first = pl.multiple_of(misc_ref[1, e] * CHUNKS, CHUNKS)
        return pltpu.make_async_copy(zeros_ref, xs_ref.at[pl.ds(first, PAD_ROWS * CHUNKS), :], sem)

    def tail_copy(b):
        first = pl.multiple_of(b * (MOE_ROWS * CHUNKS), MOE_ROWS * CHUNKS)
        return pltpu.make_async_copy(zeros_ref.at[pl.ds(0, MOE_ROWS * CHUNKS), :],
                                     xs_ref.at[pl.ds(first, MOE_ROWS * CHUNKS), :], sem)

    def pad(e, c):
        pad_copy(e).start()
        pad_copy(e).wait()
        return c

    def tail(start_not_wait):
        def body(b, c):
            tail_copy(b).start() if start_not_wait else tail_copy(b).wait()
            return c

        lax.fori_loop(misc_ref[0, 0], n_blocks + 1, body, 0)

    lax.fori_loop(0, N_EXPERTS, pad, 0)
    tail(True)
    tail(False)


def _selection(pos_ref, fill_ref=None):
    TT = pos_ref.shape[1]
    p_iota = lax.broadcasted_iota(jnp.int32, (STAGE_ROWS, TT), 0)
    sel = jnp.zeros((STAGE_ROWS, TT), F32)
    for k in range(TOP_K):
        val = 1.0 if fill_ref is None else fill_ref[k:k + 1, :]
        sel = jnp.where(pos_ref[k:k + 1, :] == p_iota, val, sel)
    return sel.astype(BF16)


def _dispatch_kernel(misc_ref, meta_ref, pos_ref, *refs, first_tiles, n_blocks):
    x_refs = refs[:len(first_tiles)]
    xs_ref, stage_ref, zeros_ref, sem = refs[len(first_tiles):]
    i = pl.program_id(0)

    @pl.when(i == 0)
    def _():
        _zero_padding(misc_ref, xs_ref, zeros_ref, sem, n_blocks)

    x = x_refs[0][...]
    for t0, ref in zip(first_tiles[1:], x_refs[1:]):
        x = jnp.where(i >= t0, ref[...], x)
    staged = jnp.dot(_selection(pos_ref), x, preferred_element_type=F32)
    _store_chunked(stage_ref, staged)
    _for_each_chunk(meta_ref, lambda slot, off, j: pltpu.make_async_copy(
        _chunk(stage_ref, off, j), _chunk(xs_ref, slot, j), sem).start())
    _wait_chunks(meta_ref, stage_ref, xs_ref, sem)


def _dispatch(misc, meta, pos, xns, n_blocks):
    D = xns[0].shape[1]
    tiles = [x.shape[0] // TOKEN_TILE for x in xns]
    first_tiles = tuple(sum(tiles[:g]) for g in range(len(tiles)))

    def x_spec(t0, nt):
        return pl.BlockSpec((TOKEN_TILE, D), lambda i: (jnp.clip(i - t0, 0, nt - 1), 0))

    return pl.pallas_call(
        functools.partial(_dispatch_kernel, first_tiles=first_tiles, n_blocks=n_blocks),
        grid=(sum(tiles),),
        in_specs=[pl.BlockSpec(memory_space=pltpu.SMEM),
                  pl.BlockSpec((8, LANES), lambda i: (i, 0), memory_space=pltpu.SMEM),
                  pl.BlockSpec((TOP_K, TOKEN_TILE), lambda i: (0, i))]
                 + [x_spec(t0, nt) for t0, nt in zip(first_tiles, tiles)],
        out_specs=pl.BlockSpec(memory_space=pl.ANY),
        out_shape=jax.ShapeDtypeStruct(((n_blocks + 1) * MOE_ROWS * CHUNKS, LANES), F32),
        scratch_shapes=[pltpu.VMEM((STAGE_ROWS * CHUNKS, LANES), F32),
                        pltpu.VMEM((PAD_ROWS * CHUNKS, LANES), F32),
                        pltpu.SemaphoreType.DMA(())],
        compiler_params=_cparams(("arbitrary",)),
        name="dispatch",
    )(misc, meta, pos, *xns)


def _expert_kernel(be_ref, nused_ref, xs_ref, wgu_ref, bgu_ref, wd_ref, bd_ref, ys_ref,
                   wgu_bf, wd_bf):
    i = pl.program_id(0)
    D_FF = wd_ref.shape[0]
    new_expert = jnp.logical_or(i == 0, be_ref[i] != be_ref[jnp.maximum(i - 1, 0)])

    @pl.when(jnp.logical_and(i < nused_ref[0], new_expert))
    def _():
        wgu_bf[...] = wgu_ref[...].astype(BF16)
        wd_bf[...] = wd_ref[...].astype(BF16)

    @pl.when(i < nused_ref[0])
    def _():
        x = _load_chunked(xs_ref, MOE_ROWS).astype(BF16)
        hdn = jnp.dot(x, wgu_bf[...], preferred_element_type=F32) + bgu_ref[...]
        glu = jnp.minimum(hdn[:, :D_FF], SWIGLU_LIMIT)
        lin = jnp.clip(hdn[:, D_FF:], -SWIGLU_LIMIT, SWIGLU_LIMIT)
        act = glu * (1.0 / (1.0 + jnp.exp(-SWIGLU_ALPHA * glu))) * (lin + 1.0)
        y = jnp.dot(act.astype(BF16), wd_bf[...], preferred_element_type=F32) + bd_ref[...]
        _store_chunked(ys_ref, y)

    @pl.when(i >= nused_ref[0])
    def _():
        ys_ref[...] = jnp.zeros_like(ys_ref)


def _experts(block_expert, n_used, xs, w_gate_up, b_gate_up, w_down, b_down, n_blocks):
    E, D, F2 = w_gate_up.shape
    D_FF = w_down.shape[1]
    rows = pl.BlockSpec((MOE_ROWS * CHUNKS, LANES), lambda i, be, nu: (i, 0))
    rows_in = pl.BlockSpec((MOE_ROWS * CHUNKS, LANES), lambda i, be, nu: (jnp.minimum(i, nu[0] - 1), 0))
    grid_spec = pltpu.PrefetchScalarGridSpec(
        num_scalar_prefetch=2,
        grid=(n_blocks,),
        in_specs=[rows_in,
                  pl.BlockSpec((None, D, F2), lambda i, be, nu: (be[i], 0, 0)),
                  pl.BlockSpec((None, 1, F2), lambda i, be, nu: (be[i], 0, 0)),
                  pl.BlockSpec((None, D_FF, D), lambda i, be, nu: (be[i], 0, 0)),
                  pl.BlockSpec((None, 1, D), lambda i, be, nu: (be[i], 0, 0))],
        out_specs=rows,
        scratch_shapes=[pltpu.VMEM((D, F2), BF16), pltpu.VMEM((D_FF, D), BF16)],
    )
    return pl.pallas_call(
        _expert_kernel,
        grid_spec=grid_spec,
        out_shape=jax.ShapeDtypeStruct((n_blocks * MOE_ROWS * CHUNKS, LANES), F32),
        compiler_params=_cparams(("arbitrary",)),
        name="experts",
    )(block_expert, n_used, xs, w_gate_up, b_gate_up.reshape(E, 1, F2), w_down, b_down.reshape(E, 1, D))


def _combine_kernel(meta_ref, pos_ref, gate_ref, h_ref, g_ref, ys_ref, y_ref, stage_ref, sem):
    @pl.when(pl.program_id(0) == 0)
    def _():
        stage_ref[...] = jnp.zeros_like(stage_ref)

    _for_each_chunk(meta_ref, lambda slot, off, j: pltpu.make_async_copy(
        _chunk(ys_ref, slot, j), _chunk(stage_ref, off, j), sem).start())
    weights = _selection(pos_ref, gate_ref)
    _wait_chunks(meta_ref, ys_ref, stage_ref, sem)
    staged = _load_chunked(stage_ref, STAGE_ROWS).astype(BF16)
    moe = lax.dot_general(weights, staged, (((0,), (0,)), ((), ())), preferred_element_type=F32)
    hf = h_ref[...] + moe
    ms = jnp.mean(hf * hf, axis=-1, keepdims=True)
    y_ref[...] = (hf * lax.rsqrt(ms + EPS)) * g_ref[...]


def _combine(meta, pos, gates, h, norm_g, ys, tile0):
    n, D = h.shape
    return pl.pallas_call(
        _combine_kernel,
        grid=(n // TOKEN_TILE,),
        in_specs=[pl.BlockSpec((8, LANES), lambda i: (tile0 + i, 0), memory_space=pltpu.SMEM),
                  pl.BlockSpec((TOP_K, TOKEN_TILE), lambda i: (0, tile0 + i)),
                  pl.BlockSpec((TOP_K, TOKEN_TILE), lambda i: (0, i)),
                  pl.BlockSpec((TOKEN_TILE, D), lambda i: (i, 0)),
                  pl.BlockSpec((1, D), lambda i: (0, 0)),
                  pl.BlockSpec(memory_space=pl.ANY)],
        out_specs=pl.BlockSpec((TOKEN_TILE, D), lambda i: (i, 0)),
        out_shape=jax.ShapeDtypeStruct((n, D), F32),
        scratch_shapes=[pltpu.VMEM((STAGE_ROWS * CHUNKS, LANES), F32), pltpu.SemaphoreType.DMA(())],
        compiler_params=_cparams(("arbitrary",)),
        name="combine",
    )(meta, pos, gates, h, norm_g, ys)


def _prep_weights(w_in, w_alpha):
    w_main = w_in[:, :PROJ_MAIN].astype(BF16)
    w_lr = jnp.pad(w_in[:, PROJ_MAIN:], ((0, 0), (0, LANES - GATE_RANK))).astype(BF16)
    w_al = jnp.pad(w_alpha, ((0, LANES - GATE_RANK), (0, 0))).astype(BF16)
    return w_main, w_lr, w_al


def kernel(x_prompt, x_sample, cache_swa_k, cache_swa_v, state_gla, norm_mix_g, w_in, w_alpha, b_alpha, gla_norm_g, w_out, norm_ffn_g, w_router, b_router, w_gate_up, b_gate_up, w_down, b_down, norm_final_g):
    B, S, D = x_prompt.shape
    Bs, Ts, _ = x_sample.shape
    assert w_in.shape[0] == 1, "single-layer trunk"
    l = 0
    R = cache_swa_k.shape[2]
    rows_p = min(DILATIONS[-1] * KEYS_PER_CONFIG, S)
    w_main, w_lr, w_al = _prep_weights(w_in[l], w_alpha[l])
    g_mix = norm_mix_g[l][None]
    b_al = b_alpha[l][None]
    g_gla = gla_norm_g[l][None]

    pos_p = jnp.arange(S, dtype=jnp.int32)
    qa, ka, va, qb, kb, vb, zg, gb, k_tail, v_tail = _project(x_prompt, pos_p, g_mix, w_main, w_lr, w_al, b_al,
                                                              PROJ_TILE, tail_rows=rows_p)
    oa_p = _prompt_attention(qa, ka, va)
    ob_p, st_p = _gla(qb, kb, gb, vb, zg, jnp.zeros((B, N_HEADS_B, DK_B, DV_B), F32), g_gla,
                      GLA_TILE, GLA_CHUNK)
    k_prompt = k_tail.reshape(1, B, rows_p, N_HEADS_A, HEAD_DIM_A)
    v_prompt = v_tail.reshape(1, B, rows_p, N_HEADS_A, HEAD_DIM_A)

    pos_s = PAST_LEN + (jnp.arange(Bs * Ts, dtype=jnp.int32) % Ts)
    proj_s = _project(x_sample.reshape(1, Bs * Ts, D), pos_s, g_mix, w_main, w_lr, w_al, b_al, Bs * Ts)
    qa_s, ka_s, va_s, qb_s, kb_s, vb_s, zg_s, gb_s = [t.reshape(Bs, Ts, -1) for t in proj_s]
    oa_s, k_sample, v_sample = _sample_attention(qa_s, ka_s, va_s,
                                                 cache_swa_k[l].reshape(Bs, R, WIDTH_A),
                                                 cache_swa_v[l].reshape(Bs, R, WIDTH_A))
    ob_s, st_s = _gla(qb_s, kb_s, gb_s, vb_s, zg_s, state_gla[l], g_gla, Ts, Ts)

    w_out_bf = w_out[l].astype(BF16)
    g_ffn = norm_ffn_g[l][None]
    w_router_t = jnp.transpose(w_router[l])
    b_router_c = b_router[l][:, None]
    Np, Ns = B * S, Bs * Ts
    h_p, xn_p, idx_p, gate_p = _merge(oa_p.reshape(Np, WIDTH_A), ob_p.reshape(Np, WIDTH_B),
                                      x_prompt.reshape(Np, D), w_out_bf, g_ffn, w_router_t, b_router_c,
                                      MERGE_TILE)
    h_s, xn_s, idx_s, gate_s = _merge(oa_s.reshape(Ns, WIDTH_A), ob_s.reshape(Ns, WIDTH_B),
                                      x_sample.reshape(Ns, D), w_out_bf, g_ffn, w_router_t, b_router_c,
                                      Ns)

    y_p, y_s = _moe([(xn_p, idx_p, gate_p, h_p), (xn_s, idx_s, gate_s, h_s)],
                    w_gate_up[l], b_gate_up[l], w_down[l], b_down[l], norm_final_g[None])
    return (y_p.reshape(B, S, D), y_s.reshape(Bs, Ts, D), k_prompt, v_prompt, st_p[None],
            k_sample.reshape(1, Bs, R, N_HEADS_A, HEAD_DIM_A),
            v_sample.reshape(1, Bs, R, N_HEADS_A, HEAD_DIM_A), st_s[None])


PAST_LEN = 16384
PROJ_TILE = 1024
MERGE_TILE = 1024
GLA_TILE = 1024
GLA_CHUNK = 64


def _moe(groups, w_gate_up, b_gate_up, w_down, b_down, g_final):
    sizes = [g[3].shape[0] for g in groups]
    N = sum(sizes)
    assert all(n % TOKEN_TILE == 0 for n in sizes)
    n_blocks = -(-(N * TOP_K + N_EXPERTS * (MOE_ROWS - 1 + DMA_ROWS)) // MOE_ROWS)
    idx = jnp.concatenate([g[1] for g in groups], axis=1)
    pos, meta, block_expert, misc = _route(idx, n_blocks)
    first_tile = [sum(sizes[:i]) // TOKEN_TILE for i in range(len(sizes))]
    xs = _dispatch(misc, meta, pos, [g[0] for g in groups], n_blocks)
    ys = _experts(block_expert[0], misc[0, :1], xs, w_gate_up, b_gate_up, w_down, b_down, n_blocks)
    return [_combine(meta, pos, gates, h, g_final, ys, t0)
            for (_, _, gates, h), t0 in zip(groups, first_tile)]
```

```python
import functools

import jax
import jax.numpy as jnp
import numpy as np
from jax import lax
from jax.experimental import pallas as pl
from jax.experimental.pallas import tpu as pltpu

F32 = jnp.float32
BF16 = jnp.bfloat16

N_HEADS_A = 8
HEAD_DIM_A = 64
WIDTH_A = N_HEADS_A * HEAD_DIM_A
N_HEADS_B = 4
DK_B = 64
DV_B = 128
QK_B = N_HEADS_B * DK_B
WIDTH_B = N_HEADS_B * DV_B
GATE_RANK = 16
GATE_LOGIT_NORM = 16.0
DILATIONS = (1, 4, 16)
KEYS_PER_CONFIG = 128
ROPE_THETA = 10000.0
N_EXPERTS = 32
TOP_K = 4
SWIGLU_ALPHA = 1.702
SWIGLU_LIMIT = 7.0
EPS = 1e-6

LANES = 128
VMEM_LIMIT = 56 * 1024 * 1024


def _cparams(sem, vmem=VMEM_LIMIT):
    return pltpu.CompilerParams(dimension_semantics=sem, vmem_limit_bytes=vmem)


PROJ_MAIN = 3 * WIDTH_A + 2 * QK_B + 2 * WIDTH_B


def _rope_tables(pos):
    half = HEAD_DIM_A // 2
    inv = ROPE_THETA ** (-jnp.arange(half, dtype=F32) / half)
    ang = pos.astype(F32)[:, None] * inv[None, :]
    cos = jnp.cos(ang)
    sin = jnp.sin(ang)
    cos_t = jnp.concatenate([cos, cos, cos, cos], axis=-1)
    sin_t = jnp.concatenate([-sin, sin, -sin, sin], axis=-1)
    return cos_t, sin_t


def _rope_block(t, cos, sin, first_half):
    partner = jnp.where(first_half, pltpu.roll(t, LANES - 32, 1), pltpu.roll(t, 32, 1))
    return t * cos + partner * sin


def _proj_kernel(x_ref, g_ref, w_ref, wlr_ref, wa_ref, ba_ref, cos_ref, sin_ref,
                 qa_ref, ka_ref, va_ref, qb_ref, kb_ref, vb_ref, zg_ref, gb_ref, *tail_refs, first_tail_tile):
    x = x_ref[...]
    ms = jnp.mean(x * x, axis=-1, keepdims=True)
    xn = ((x * lax.rsqrt(ms + EPS)) * g_ref[...]).astype(BF16)

    def cols(lo, hi):
        return jnp.dot(xn, w_ref[:, lo:hi], preferred_element_type=F32)

    cos = cos_ref[...]
    sin = sin_ref[...]
    lane = lax.broadcasted_iota(jnp.int32, cos.shape, 1)
    first_half = (lane % HEAD_DIM_A) < (HEAD_DIM_A // 2)
    q = cols(0, WIDTH_A)
    k = cols(WIDTH_A, 2 * WIDTH_A)
    for j in range(WIDTH_A // LANES):
        sl = slice(j * LANES, (j + 1) * LANES)
        qa_ref[:, sl] = _rope_block(q[:, sl], cos, sin, first_half) * (HEAD_DIM_A ** -0.5)
        ka_ref[:, sl] = _rope_block(k[:, sl], cos, sin, first_half)
    o = 2 * WIDTH_A
    va_ref[...] = cols(o, o + WIDTH_A)
    o += WIDTH_A
    qb_ref[...] = cols(o, o + QK_B) * (DK_B ** -0.5)
    o += QK_B
    kb_ref[...] = cols(o, o + QK_B)
    o += QK_B
    vb_ref[...] = cols(o, o + WIDTH_B).astype(vb_ref.dtype)
    o += WIDTH_B
    zg_ref[...] = cols(o, o + WIDTH_B)
    lr = jnp.dot(xn, wlr_ref[...], preferred_element_type=F32)
    z = jnp.dot(lr.astype(BF16), wa_ref[...], preferred_element_type=F32) + ba_ref[...]
    logsig = jnp.minimum(z, 0.0) - jnp.log(1.0 + jnp.exp(-jnp.abs(z)))
    gb_ref[...] = logsig / GATE_LOGIT_NORM
    if tail_refs:
        @pl.when(pl.program_id(1) >= first_tail_tile)
        def _():
            tail_refs[0][...] = ka_ref[...]
            tail_refs[1][...] = va_ref[...]


def _project(x, pos, norm_g, w_main, w_lr, w_alpha, b_alpha, tm, tail_rows=0):
    B, T, D = x.shape
    assert tail_rows % tm == 0
    cos_t, sin_t = _rope_tables(pos)
    grid = (B, T // tm)
    first_tail_tile = (T - tail_rows) // tm
    row = lambda w: pl.BlockSpec((None, tm, w), lambda b, i: (b, i, 0))
    tail = pl.BlockSpec((None, tm, WIDTH_A), lambda b, i: (b, jnp.maximum(i - first_tail_tile, 0), 0))
    full = lambda a: pl.BlockSpec(a.shape, lambda b, i: (0,) * a.ndim)
    tab = pl.BlockSpec((tm, LANES), lambda b, i: (i, 0))
    widths = (WIDTH_A, WIDTH_A, WIDTH_A, QK_B, QK_B, WIDTH_B, WIDTH_B, QK_B)
    dtypes = (F32, F32, F32, F32, F32, BF16, F32, F32)
    n_tail = 2 if tail_rows else 0
    return pl.pallas_call(
        functools.partial(_proj_kernel, first_tail_tile=first_tail_tile),
        grid=grid,
        in_specs=[row(D), full(norm_g), full(w_main), full(w_lr), full(w_alpha), full(b_alpha), tab, tab],
        out_specs=[row(w) for w in widths] + [tail] * n_tail,
        out_shape=[jax.ShapeDtypeStruct((B, T, w), dt) for w, dt in zip(widths, dtypes)]
                  + [jax.ShapeDtypeStruct((B, tail_rows, WIDTH_A), F32)] * n_tail,
        compiler_params=_cparams(("parallel", "arbitrary")),
        name="proj",
    )(x, norm_g, w_main, w_lr, w_alpha, b_alpha, cos_t, sin_t)


Q_BLOCK = 128
NEG_INF = float("-inf")


def _attn_block(q, kb, v1, mask, head0, state):
    QB = q.shape[0]
    q2 = jnp.concatenate([jnp.where(head0, q, 0.0), jnp.where(head0, 0.0, q)], axis=0).astype(BF16)
    s = lax.dot_general(q2, kb, (((1,), (1,)), ((), ())), preferred_element_type=F32)
    s = jnp.where(mask, s, NEG_INF)
    if state is None:
        m2 = jnp.max(s, axis=1, keepdims=True)
    else:
        prev = jnp.concatenate([jnp.where(head0, state[0], NEG_INF),
                                jnp.where(head0, NEG_INF, state[0])], axis=0)
        m2 = jnp.max(jnp.concatenate([s, prev], axis=1), axis=1, keepdims=True)
    p = jnp.exp(s - m2)
    pv = jnp.dot(p.astype(BF16), v1, preferred_element_type=F32)
    m_full = jnp.where(head0, m2[:QB], m2[QB:])
    l_full = jnp.where(head0, pv[:QB, LANES:], pv[QB:, LANES:])
    pv_full = jnp.where(head0, pv[:QB, :LANES], pv[QB:, :LANES])
    if state is None:
        return m_full, l_full, pv_full
    a = jnp.exp(state[0] - m_full)
    return m_full, a * state[1] + l_full, a * state[2] + pv_full


def _attn_kernel(q_ref, k_ref, v_ref, o_ref, m_ref, l_ref, acc_ref):
    S = q_ref.shape[0]
    QB = Q_BLOCK
    lane = lax.broadcasted_iota(jnp.int32, (QB, LANES), 1)
    head0 = lane < HEAD_DIM_A
    qq = lax.broadcasted_iota(jnp.int32, (2 * QB, 2 * QB), 0) % QB
    kk = lax.broadcasted_iota(jnp.int32, (2 * QB, 2 * QB), 1)
    band = jnp.logical_and(kk >= qq, kk - qq <= KEYS_PER_CONFIG)
    causal = (lax.broadcasted_iota(jnp.int32, (2 * QB, QB), 1)
              <= lax.broadcasted_iota(jnp.int32, (2 * QB, QB), 0) % QB)

    GROUP = 4

    for ci, dil in enumerate(DILATIONS):
        nblk = S // (dil * QB)
        assert nblk % GROUP == 0 or GROUP % nblk == 0

        def rows(start, n, dil=dil):
            return pl.ds(start, n) if dil == 1 else pl.ds(start, n, stride=dil)

        def do_group(blocks, ci=ci, dil=dil, rows=rows):
            chunks = {}

            def kv_chunk(r, rkey, jbase, c):
                if (rkey, c) not in chunks:
                    ks = rows(r + dil * QB * (jbase + c), QB)
                    v = v_ref[ks, :].astype(BF16)
                    chunks[(rkey, c)] = (k_ref[ks, :].astype(BF16),
                                         jnp.concatenate([v, jnp.ones(v.shape, BF16)], axis=1))
                return chunks[(rkey, c)]

            loaded = []
            for r, rkey, jbase, joff in blocks:
                first = isinstance(jbase, int) and jbase + joff == 0
                qs = rows(r + dil * QB * (jbase + joff), QB)
                parts = [kv_chunk(r, rkey, jbase, joff)]
                if not first:
                    parts.insert(0, kv_chunk(r, rkey, jbase, joff - 1))
                kb = jnp.concatenate([p[0] for p in parts], axis=0)
                v1 = jnp.concatenate([p[1] for p in parts], axis=0)
                state = None if ci == 0 else (m_ref[qs, :], l_ref[qs, :], acc_ref[qs, :])
                loaded.append((qs, q_ref[qs, :], kb, v1, causal if first else band, state))
            results = [_attn_block(q, kb, v1, mask, head0, state) for _, q, kb, v1, mask, state in loaded]
            for (qs, *_), (m, l, acc) in zip(loaded, results):
                m_ref[qs, :] = m
                l_ref[qs, :] = l
                acc_ref[qs, :] = acc

        if nblk >= GROUP:
            def residue(r, carry, nblk=nblk, do_group=do_group):
                do_group([(r, 0, 0, j) for j in range(GROUP)])

                def rest(g, c):
                    do_group([(r, 0, GROUP * g, u) for u in range(GROUP)])
                    return c

                return lax.fori_loop(1, nblk // GROUP, rest, carry)

            lax.fori_loop(0, dil, residue, 0)
        else:
            per = 2 * GROUP // nblk

            def residues(g, carry, nblk=nblk, per=per, do_group=do_group):
                do_group([(g * per + i, i, 0, j) for j in range(nblk) for i in range(per)])
                return carry

            lax.fori_loop(0, dil // per, residues, 0)

    def finish(i, c):
        rs = pl.ds(pl.multiple_of(i * QB, QB), QB)
        o_ref[rs, :] = (acc_ref[rs, :] / l_ref[rs, :]).astype(o_ref.dtype)
        return c

    lax.fori_loop(0, S // QB, finish, 0, unroll=4)


def _prompt_attention(qa, ka, va):
    B, S, W = qa.shape
    spec = pl.BlockSpec((None, S, LANES), lambda b, hp: (b, 0, hp))
    return pl.pallas_call(
        _attn_kernel,
        grid=(B, W // LANES),
        in_specs=[spec, spec, spec],
        out_specs=spec,
        out_shape=jax.ShapeDtypeStruct((B, S, W), BF16),
        scratch_shapes=[pltpu.VMEM((S, LANES), F32)] * 3,
        compiler_params=_cparams(("parallel", "parallel")),
        name="prompt_attn",
    )(qa, ka, va)


def _split3(x):
    x1 = x.astype(BF16)
    r1 = x - x1.astype(F32)
    x2 = r1.astype(BF16)
    x3 = (r1 - x2.astype(F32)).astype(BF16)
    return x1, x2, x3


def _gla_kernel(q_ref, k_ref, g_ref, v_ref, z_ref, s0_ref, ng_ref, o_ref, sfin_ref, st_ref, *, chunk):
    C = chunk
    TS = q_ref.shape[0]
    n_pairs = N_HEADS_B // 2
    PW = 2 * DV_B
    t_idx = pl.program_id(1)

    lane_k = lax.broadcasted_iota(jnp.int32, (C, LANES), 1)
    head0 = lane_k < DK_B
    tri_incl = (lax.broadcasted_iota(jnp.int32, (C, C), 1)
                <= lax.broadcasted_iota(jnp.int32, (C, C), 0))
    tri_bf = jnp.where(tri_incl, 1.0, 0.0).astype(BF16)
    bd_mask = ((lax.broadcasted_iota(jnp.int32, (PW, LANES), 0) // DV_B)
               == (lax.broadcasted_iota(jnp.int32, (PW, LANES), 1) // DK_B))

    @pl.when(t_idx == 0)
    def _():
        for p in range(n_pairs):
            for h in range(2):
                blk = jnp.transpose(s0_ref[2 * p + h])
                pad = jnp.zeros((DV_B, DK_B), F32)
                row = jnp.concatenate([blk, pad] if h == 0 else [pad, blk], axis=1)
                st_ref[p, h * DV_B:(h + 1) * DV_B, :] = row

    def chunk_body(c, carry):
        rs = pl.ds(pl.multiple_of(c * C, C), C)
        for p in range(n_pairs):
            kl = slice(p * LANES, (p + 1) * LANES)
            vl = slice(p * PW, (p + 1) * PW)
            q = q_ref[rs, kl]
            k = k_ref[rs, kl]
            g = g_ref[rs, kl]
            v = v_ref[rs, vl]
            g1, g2, g3 = _split3(g)
            b = (jnp.dot(tri_bf, g1, preferred_element_type=F32)
                 + jnp.dot(tri_bf, g2, preferred_element_type=F32)
                 + jnp.dot(tri_bf, g3, preferred_element_type=F32))
            b_last = b[C - 1:C, :]
            b_mid = b[C // 2 - 1:C // 2, :] if C > 1 else b_last
            qe = q * jnp.exp(b - b_mid)
            ke = (k * jnp.exp(b_mid - b)).astype(BF16)
            st = st_ref[p]
            q_in = (q * jnp.exp(b)).astype(BF16)
            o = lax.dot_general(q_in, st.astype(BF16), (((1,), (1,)), ((), ())),
                                preferred_element_type=F32)
            intra = []
            for h in range(2):
                hm = head0 if h == 0 else jnp.logical_not(head0)
                a = lax.dot_general(jnp.where(hm, qe, 0.0).astype(BF16), ke,
                                    (((1,), (1,)), ((), ())), preferred_element_type=F32)
                a = jnp.where(tri_incl, a, 0.0).astype(BF16)
                intra.append(jnp.dot(a, v[:, h * DV_B:(h + 1) * DV_B], preferred_element_type=F32))
            o = o + jnp.concatenate(intra, axis=1)
            k_dec = (k * jnp.exp(b_last - b)).astype(BF16)
            upd = lax.dot_general(v, k_dec, (((0,), (0,)), ((), ())), preferred_element_type=F32)
            st_ref[p] = jnp.exp(b_last) * st + jnp.where(bd_mask, upd, 0.0)
            for h in range(2):
                oh = o[:, h * DV_B:(h + 1) * DV_B]
                hl = slice((2 * p + h) * DV_B, (2 * p + h + 1) * DV_B)
                ms = jnp.mean(oh * oh, axis=-1, keepdims=True)
                z = z_ref[rs, hl]
                gated = (oh * lax.rsqrt(ms + EPS)) * ng_ref[:, hl] * (z / (1.0 + jnp.exp(-z)))
                o_ref[rs, hl] = gated.astype(o_ref.dtype)
        return carry

    lax.fori_loop(0, TS // C, chunk_body, 0, unroll=2 if (TS // C) % 2 == 0 else 1)

    @pl.when(t_idx == pl.num_programs(1) - 1)
    def _():
        for p in range(n_pairs):
            for h in range(2):
                blk = st_ref[p, h * DV_B:(h + 1) * DV_B, h * DK_B:(h + 1) * DK_B]
                sfin_ref[2 * p + h] = jnp.transpose(blk)


def _gla(qb, kb, gb, vb, zg, state0, norm_g, ts, chunk):
    B, T, _ = qb.shape
    row = lambda w: pl.BlockSpec((None, ts, w), lambda b, i: (b, i, 0))
    st_spec = pl.BlockSpec((None, N_HEADS_B, DK_B, DV_B), lambda b, i: (b, 0, 0, 0))
    return pl.pallas_call(
        functools.partial(_gla_kernel, chunk=chunk),
        grid=(B, T // ts),
        in_specs=[row(QK_B), row(QK_B), row(QK_B), row(WIDTH_B), row(WIDTH_B), st_spec,
                  pl.BlockSpec((1, WIDTH_B), lambda b, i: (0, 0))],
        out_specs=[row(WIDTH_B), st_spec],
        out_shape=[jax.ShapeDtypeStruct((B, T, WIDTH_B), BF16),
                   jax.ShapeDtypeStruct((B, N_HEADS_B, DK_B, DV_B), F32)],
        scratch_shapes=[pltpu.VMEM((N_HEADS_B // 2, 2 * DV_B, LANES), F32)],
        compiler_params=_cparams(("parallel", "arbitrary")),
        name="gla",
    )(qb, kb, gb, vb, zg, state0, norm_g)


def _sample_attn_kernel(q_ref, kn_ref, vn_ref, kc_ref, vc_ref, o_ref, ko_ref, vo_ref):
    T = q_ref.shape[0]
    R = kc_ref.shape[0]
    W = q_ref.shape[1]
    HT = N_HEADS_A * T
    q = q_ref[...]
    qx = jnp.concatenate([q] * N_HEADS_A, axis=0)
    own = ((lax.broadcasted_iota(jnp.int32, (HT, W), 0) // T)
           == (lax.broadcasted_iota(jnp.int32, (HT, W), 1) // HEAD_DIM_A))
    qx = jnp.where(own, qx, 0.0).astype(BF16)
    pad = jnp.zeros((LANES - T, W), F32)
    kn = jnp.concatenate([kn_ref[...], pad], axis=0).astype(BF16)
    vn = jnp.concatenate([vn_ref[...], pad], axis=0).astype(BF16)
    nt = (((1,), (1,)), ((), ()))
    s_c = lax.dot_general(qx, kc_ref[...].astype(BF16), nt, preferred_element_type=F32)
    s_n = lax.dot_general(qx, kn, nt, preferred_element_type=F32)

    def multiplicity(n_cols, first_row):
        t = lax.broadcasted_iota(jnp.int32, (HT, n_cols), 0) % T
        j = lax.broadcasted_iota(jnp.int32, (HT, n_cols), 1) + first_row
        delta = R + t - j
        cnt = jnp.zeros((HT, n_cols), F32)
        for dil in DILATIONS:
            hit = (delta >= 0) & (delta <= dil * KEYS_PER_CONFIG) & (delta % dil == 0)
            cnt = cnt + jnp.where(hit, 1.0, 0.0)
        return cnt

    cnt_c = multiplicity(R, 0)
    cnt_n = multiplicity(LANES, R)
    s_c = jnp.where(cnt_c > 0.0, s_c, NEG_INF)
    s_n = jnp.where(cnt_n > 0.0, s_n, NEG_INF)
    m = jnp.maximum(jnp.max(s_c, axis=1, keepdims=True), jnp.max(s_n, axis=1, keepdims=True))
    p_c = cnt_c * jnp.exp(s_c - m)
    p_n = cnt_n * jnp.exp(s_n - m)
    den = jnp.sum(p_c, axis=1, keepdims=True) + jnp.sum(p_n, axis=1, keepdims=True)
    full = (jnp.dot(p_c.astype(BF16), vc_ref[...].astype(BF16), preferred_element_type=F32)
            + jnp.dot(p_n.astype(BF16), vn, preferred_element_type=F32)) / den
    full = jnp.where(own, full, 0.0)
    out = full[0:T, :]
    for h in range(1, N_HEADS_A):
        out = out + full[h * T:(h + 1) * T, :]
    o_ref[...] = out.astype(o_ref.dtype)
    ko_ref[0:R - T, :] = kc_ref[T:R, :]
    ko_ref[R - T:R, :] = kn_ref[...]
    vo_ref[0:R - T, :] = vc_ref[T:R, :]
    vo_ref[R - T:R, :] = vn_ref[...]


def _sample_attention(qa, ka, va, cache_k, cache_v):
    B, T, W = qa.shape
    R = cache_k.shape[1]
    assert R >= DILATIONS[-1] * KEYS_PER_CONFIG and T % 8 == 0 and T <= LANES
    new = pl.BlockSpec((None, T, W), lambda b: (b, 0, 0))
    cache = pl.BlockSpec((None, R, W), lambda b: (b, 0, 0))
    return pl.pallas_call(
        _sample_attn_kernel,
        grid=(B,),
        in_specs=[new, new, new, cache, cache],
        out_specs=[new, cache, cache],
        out_shape=[jax.ShapeDtypeStruct((B, T, W), BF16),
                   jax.ShapeDtypeStruct((B, R, W), F32),
                   jax.ShapeDtypeStruct((B, R, W), F32)],
        compiler_params=_cparams(("parallel",)),
        name="sample_attn",
    )(qa, ka, va, cache_k, cache_v)


CHUNKS = 8


def _store_chunked(ref, val):
    n = val.shape[0]
    for s in range(CHUNKS):
        ref[pl.ds(s, n, stride=CHUNKS), :] = val[:, s * LANES:(s + 1) * LANES]


def _load_chunked(ref, n):
    return jnp.concatenate([ref[pl.ds(s, n, stride=CHUNKS), :] for s in range(CHUNKS)], axis=1)


def _split2(x):
    hi = x.astype(BF16)
    return hi, (x - hi.astype(F32)).astype(BF16)


def _merge_kernel(oa_ref, ob_ref, x_ref, wo_ref, g_ref, wr_ref, br_ref,
                  h_ref, xn_ref, idx_ref, gate_ref):
    TM = x_ref.shape[0]
    mixed = (jnp.dot(oa_ref[...], wo_ref[0:WIDTH_A, :], preferred_element_type=F32)
             + jnp.dot(ob_ref[...], wo_ref[WIDTH_A:, :], preferred_element_type=F32))
    h = x_ref[...] + mixed
    h_ref[...] = h
    ms = jnp.mean(h * h, axis=-1, keepdims=True)
    xn = (h * lax.rsqrt(ms + EPS)) * g_ref[...]
    xn_ref[...] = xn.astype(xn_ref.dtype)
    nt = (((1,), (1,)), ((), ()))
    xh, xl = _split2(xn)
    wh, wl = _split2(wr_ref[...])
    logits = (lax.dot_general(wh, xh, nt, preferred_element_type=F32)
              + lax.dot_general(wh, xl, nt, preferred_element_type=F32)
              + lax.dot_general(wl, xh, nt, preferred_element_type=F32)) + br_ref[...]
    e_iota = lax.broadcasted_iota(jnp.int32, (N_EXPERTS, TM), 0)
    vals, idxs = [], []
    for _ in range(TOP_K):
        m = jnp.max(logits, axis=0, keepdims=True)
        sel = jnp.min(jnp.where(logits == m, e_iota, N_EXPERTS), axis=0, keepdims=True)
        vals.append(m)
        idxs.append(sel)
        logits = jnp.where(e_iota == sel, NEG_INF, logits)
    ex = [jnp.exp(v - vals[0]) for v in vals]
    den = ex[0] + ex[1] + ex[2] + ex[3]
    idx_ref[...] = jnp.concatenate(idxs, axis=0)
    gate_ref[...] = jnp.concatenate([e / den for e in ex], axis=0)


def _merge(oa, ob, x, w_out, norm_g, w_router_t, b_router, tm):
    N, D = x.shape
    full = lambda a: pl.BlockSpec(a.shape, lambda i: (0,) * a.ndim)
    row = lambda w: pl.BlockSpec((tm, w), lambda i: (i, 0))
    col = pl.BlockSpec((TOP_K, tm), lambda i: (0, i))
    return pl.pallas_call(
        _merge_kernel,
        grid=(N // tm,),
        in_specs=[row(WIDTH_A), row(WIDTH_B), row(D), full(w_out), full(norm_g), full(w_router_t),
                  full(b_router)],
        out_specs=[row(D), row(D), col, col],
        out_shape=[jax.ShapeDtypeStruct((N, D), F32),
                   jax.ShapeDtypeStruct((N, D), BF16),
                   jax.ShapeDtypeStruct((TOP_K, N), jnp.int32),
                   jax.ShapeDtypeStruct((TOP_K, N), F32)],
        compiler_params=_cparams(("parallel",)),
        name="merge_router",
    )(oa, ob, x, w_out, norm_g, w_router_t, b_router)


MOE_ROWS = 512
TOKEN_TILE = 256
DMA_ROWS = 8
STAGE_ROWS = 1280
assert STAGE_ROWS >= TOKEN_TILE * TOP_K + N_EXPERTS * (DMA_ROWS - 1) and STAGE_ROWS % 8 == 0


def _expert_row(col):
    r = lax.broadcasted_iota(jnp.int32, (N_EXPERTS, LANES), 0)
    c = lax.broadcasted_iota(jnp.int32, (N_EXPERTS, LANES), 1)
    return jnp.sum(jnp.where(r == c, col, 0.0), axis=0, keepdims=True)


def _expert_prefix(col):
    r = lax.broadcasted_iota(jnp.int32, (N_EXPERTS, LANES), 0)
    c = lax.broadcasted_iota(jnp.int32, (N_EXPERTS, LANES), 1)
    return jnp.sum(jnp.where(c < r, _expert_row(col), 0.0), axis=1, keepdims=True)


def _route_kernel(idx_all_ref, idx_ref, pos_ref, meta_ref, be_ref, misc_ref, carry_ref, start_ref):
    i = pl.program_id(0)
    TT = idx_ref.shape[1]
    NBP = be_ref.shape[1]

    @pl.when(i == 0)
    def _():
        idx_all = idx_all_ref[...]
        e_all = lax.broadcasted_iota(jnp.int32, (N_EXPERTS, idx_all.shape[1]), 0)
        tot = jnp.zeros((N_EXPERTS, 1), F32)
        for k in range(TOP_K):
            tot = tot + jnp.sum(jnp.where(idx_all[k:k + 1, :] == e_all, 1.0, 0.0), axis=1, keepdims=True)
        padded = jnp.floor((tot + (DMA_ROWS + MOE_ROWS - 1)) / MOE_ROWS) * MOE_ROWS
        start = _expert_prefix(padded)
        start_ref[...] = start
        carry_ref[...] = jnp.zeros_like(carry_ref)
        end = start + padded
        block_start = lax.broadcasted_iota(jnp.int32, (N_EXPERTS, NBP), 1).astype(F32) * MOE_ROWS
        be = jnp.sum(jnp.where(end <= block_start, 1.0, 0.0), axis=0, keepdims=True)
        be_ref[...] = jnp.minimum(be, N_EXPERTS - 1).astype(jnp.int32)
        n_used = jnp.broadcast_to(jnp.sum(padded, axis=0, keepdims=True) / MOE_ROWS, (1, LANES))
        zero = jnp.zeros((1, LANES), F32)
        misc_ref[...] = jnp.concatenate([n_used, _expert_row(start + tot)] + [zero] * 6,
                                        axis=0).astype(jnp.int32)

    idx = idx_ref[...]
    e_iota = lax.broadcasted_iota(jnp.int32, (N_EXPERTS, TT), 0)
    onehot = [idx[k:k + 1, :] == e_iota for k in range(TOP_K)]
    cnt = jnp.zeros((N_EXPERTS, TT), F32)
    for oh in onehot:
        cnt = cnt + jnp.where(oh, 1.0, 0.0)
    tile_tot = jnp.sum(cnt, axis=1, keepdims=True)
    earlier = (lax.broadcasted_iota(jnp.int32, (TT, TT), 0)
               < lax.broadcasted_iota(jnp.int32, (TT, TT), 1))
    before = jnp.dot(cnt.astype(BF16), jnp.where(earlier, 1.0, 0.0).astype(BF16),
                     preferred_element_type=F32)
    chunks = jnp.floor((tile_tot + (DMA_ROWS - 1)) / DMA_ROWS)
    seg = _expert_prefix(chunks * DMA_ROWS)
    where_staged = seg + before
    rows = [jnp.sum(jnp.where(oh, where_staged, 0.0), axis=0, keepdims=True) for oh in onehot]
    pos_ref[...] = jnp.concatenate(rows, axis=0).astype(jnp.int32)
    zero = jnp.zeros((1, LANES), F32)
    total = jnp.broadcast_to(jnp.sum(chunks, axis=0, keepdims=True), (1, LANES))
    meta_ref[...] = jnp.concatenate(
        [_expert_row(start_ref[...] + carry_ref[...]), _expert_row(seg), _expert_row(tile_tot), total]
        + [zero] * 4, axis=0).astype(jnp.int32)
    carry_ref[...] += tile_tot


def _route(idx, n_blocks):
    _, N = idx.shape
    nbp = -(-n_blocks // LANES) * LANES
    tile = pl.BlockSpec((TOP_K, TOKEN_TILE), lambda i: (0, i))
    return pl.pallas_call(
        _route_kernel,
        grid=(N // TOKEN_TILE,),
        in_specs=[pl.BlockSpec((TOP_K, N), lambda i: (0, 0)), tile],
        out_specs=[tile,
                   pl.BlockSpec((8, LANES), lambda i: (i, 0)),
                   pl.BlockSpec((1, nbp), lambda i: (0, 0)),
                   pl.BlockSpec((8, LANES), lambda i: (0, 0))],
        out_shape=[jax.ShapeDtypeStruct((TOP_K, N), jnp.int32),
                   jax.ShapeDtypeStruct((N // TOKEN_TILE * 8, LANES), jnp.int32),
                   jax.ShapeDtypeStruct((1, nbp), jnp.int32),
                   jax.ShapeDtypeStruct((8, LANES), jnp.int32)],
        scratch_shapes=[pltpu.VMEM((N_EXPERTS, 1), F32)] * 2,
        compiler_params=_cparams(("arbitrary",)),
        name="route",
    )(idx, idx)


def _rows(ref, first_row, n_rows):
    return ref.at[pl.ds(pl.multiple_of(first_row * CHUNKS, CHUNKS), n_rows * CHUNKS), :]


def _for_each_run_piece(meta_ref, fn, exact):
    def per_expert(e, c):
        slot, off, n = meta_ref[0, e], meta_ref[1, e], meta_ref[2, e]
        whole = n // DMA_ROWS if exact else (n + DMA_ROWS - 1) // DMA_ROWS

        def per_chunk(j, c2):
            fn(slot + j * DMA_ROWS, off + j * DMA_ROWS, DMA_ROWS)
            return c2

        lax.fori_loop(0, whole, per_chunk, 0)
        if exact:
            rem = n % DMA_ROWS
            piece = DMA_ROWS // 2
            while piece:
                done = whole * DMA_ROWS + (rem & ~(2 * piece - 1))

                @pl.when((rem & piece) != 0)
                def _(done=done, piece=piece):
                    fn(slot + done, off + done, piece)

                piece //= 2
        return c

    lax.fori_loop(0, N_EXPERTS, per_expert, 0)


def _wait_rows(n_rows, src_ref, dst_ref, sem):
    @pl.when(n_rows > 0)
    def _():
        n = n_rows * CHUNKS
        pltpu.make_async_copy(src_ref.at[pl.ds(0, n), :], dst_ref.at[pl.ds(0, n), :], sem).wait()


PAD_ROWS = MOE_ROWS + DMA_ROWS


def _zero_padding(misc_ref, xs_ref, zeros_ref, sem, n_blocks):
    zeros_ref[...] = jnp.zeros_like(zeros_ref)

    def pad_copy(e):
        first = pl.multiple_of(misc_ref[1, e] * CHUNKS, CHUNKS)
        return pltpu.make_async_copy(zeros_ref, xs_ref.at[pl.ds(first, PAD_ROWS * CHUNKS), :], sem)

    def tail_copy(b):
        first = pl.multiple_of(b * (MOE_ROWS * CHUNKS), MOE_ROWS * CHUNKS)
        return pltpu.make_async_copy(zeros_ref.at[pl.ds(0, MOE_ROWS * CHUNKS), :],
                                     xs_ref.at[pl.ds(first, MOE_ROWS * CHUNKS), :], sem)

    def pad(e, c):
        pad_copy(e).start()
        pad_copy(e).wait()
        return c

    def tail(start_not_wait):
        def body(b, c):
            tail_copy(b).start() if start_not_wait else tail_copy(b).wait()
            return c

        lax.fori_loop(misc_ref[0, 0], n_blocks + 1, body, 0)

    lax.fori_loop(0, N_EXPERTS, pad, 0)
    tail(True)
    tail(False)


def _selection(pos_ref, fill_ref=None):
    TT = pos_ref.shape[1]
    p_iota = lax.broadcasted_iota(jnp.int32, (STAGE_ROWS, TT), 0)
    sel = jnp.zeros((STAGE_ROWS, TT), F32)
    for k in range(TOP_K):
        val = 1.0 if fill_ref is None else fill_ref[k:k + 1, :]
        sel = jnp.where(pos_ref[k:k + 1, :] == p_iota, val, sel)
    return sel.astype(BF16)


def _dispatch_kernel(misc_ref, meta_ref, pos_ref, *refs, first_tiles, n_tiles, n_blocks):
    x_refs = refs[:len(first_tiles)]
    xs_ref, stage_ref, zeros_ref, sems, pad_sem = refs[len(first_tiles):]
    i = pl.program_id(0)
    TT = pos_ref.shape[1]
    slot = i % 2

    @pl.when(i == 0)
    def _():
        _zero_padding(misc_ref, xs_ref, zeros_ref, pad_sem, n_blocks)

    x = x_refs[0][...]
    for t0, ref in zip(first_tiles[1:], x_refs[1:]):
        x = jnp.where(i >= t0, ref[...], x)
    staged = jnp.dot(_selection(pos_ref), x, preferred_element_type=F32)
    stage = stage_ref.at[slot]
    _store_chunked(stage, staged)
    _for_each_run_piece(meta_ref, lambda dst, src, n: pltpu.make_async_copy(
        _rows(stage, src, n), _rows(xs_ref, dst, n), sems.at[slot]).start(), exact=True)

    @pl.when(i > 0)
    def _():
        _wait_rows(TT * TOP_K, stage_ref.at[1 - slot], xs_ref, sems.at[1 - slot])

    @pl.when(i == n_tiles - 1)
    def _():
        _wait_rows(TT * TOP_K, stage, xs_ref, sems.at[slot])


def _dispatch(misc, meta, pos, xns, n_blocks):
    D = xns[0].shape[1]
    tiles = [x.shape[0] // TOKEN_TILE for x in xns]
    first_tiles = tuple(sum(tiles[:g]) for g in range(len(tiles)))

    def x_spec(t0, nt):
        return pl.BlockSpec((TOKEN_TILE, D), lambda i: (jnp.clip(i - t0, 0, nt - 1), 0))

    return pl.pallas_call(
        functools.partial(_dispatch_kernel, first_tiles=first_tiles, n_tiles=sum(tiles), n_blocks=n_blocks),
        grid=(sum(tiles),),
        in_specs=[pl.BlockSpec(memory_space=pltpu.SMEM),
                  pl.BlockSpec((8, LANES), lambda i: (i, 0), memory_space=pltpu.SMEM),
                  pl.BlockSpec((TOP_K, TOKEN_TILE), lambda i: (0, i))]
                 + [x_spec(t0, nt) for t0, nt in zip(first_tiles, tiles)],
        out_specs=pl.BlockSpec(memory_space=pl.ANY),
        out_shape=jax.ShapeDtypeStruct(((n_blocks + 1) * MOE_ROWS * CHUNKS, LANES), F32),
        scratch_shapes=[pltpu.VMEM((2, STAGE_ROWS * CHUNKS, LANES), F32),
                        pltpu.VMEM((PAD_ROWS * CHUNKS, LANES), F32),
                        pltpu.SemaphoreType.DMA((2,)),
                        pltpu.SemaphoreType.DMA(())],
        compiler_params=_cparams(("arbitrary",)),
        name="dispatch",
    )(misc, meta, pos, *xns)


def _expert_kernel(be_ref, nused_ref, xs_ref, wgu_ref, bgu_ref, wd_ref, bd_ref, ys_ref,
                   wgu_bf, wd_bf):
    i = pl.program_id(0)
    D_FF = wd_ref.shape[0]
    new_expert = jnp.logical_or(i == 0, be_ref[i] != be_ref[jnp.maximum(i - 1, 0)])

    @pl.when(jnp.logical_and(i < nused_ref[0], new_expert))
    def _():
        wgu_bf[...] = wgu_ref[...].astype(BF16)
        wd_bf[...] = wd_ref[...].astype(BF16)

    @pl.when(i < nused_ref[0])
    def _():
        x = _load_chunked(xs_ref, MOE_ROWS).astype(BF16)
        hdn = jnp.dot(x, wgu_bf[...], preferred_element_type=F32) + bgu_ref[...]
        glu = jnp.minimum(hdn[:, :D_FF], SWIGLU_LIMIT)
        lin = jnp.clip(hdn[:, D_FF:], -SWIGLU_LIMIT, SWIGLU_LIMIT)
        act = glu * (1.0 / (1.0 + jnp.exp(-SWIGLU_ALPHA * glu))) * (lin + 1.0)
        y = jnp.dot(act.astype(BF16), wd_bf[...], preferred_element_type=F32) + bd_ref[...]
        _store_chunked(ys_ref, y)

    @pl.when(i >= nused_ref[0])
    def _():
        ys_ref[...] = jnp.zeros_like(ys_ref)


def _experts(block_expert, n_used, xs, w_gate_up, b_gate_up, w_down, b_down, n_blocks):
    E, D, F2 = w_gate_up.shape
    D_FF = w_down.shape[1]
    rows = pl.BlockSpec((MOE_ROWS * CHUNKS, LANES), lambda i, be, nu: (i, 0))
    rows_in = pl.BlockSpec((MOE_ROWS * CHUNKS, LANES), lambda i, be, nu: (jnp.minimum(i, nu[0] - 1), 0))
    grid_spec = pltpu.PrefetchScalarGridSpec(
        num_scalar_prefetch=2,
        grid=(n_blocks,),
        in_specs=[rows_in,
                  pl.BlockSpec((None, D, F2), lambda i, be, nu: (be[i], 0, 0)),
                  pl.BlockSpec((None, 1, F2), lambda i, be, nu: (be[i], 0, 0)),
                  pl.BlockSpec((None, D_FF, D), lambda i, be, nu: (be[i], 0, 0)),
                  pl.BlockSpec((None, 1, D), lambda i, be, nu: (be[i], 0, 0))],
        out_specs=rows,
        scratch_shapes=[pltpu.VMEM((D, F2), BF16), pltpu.VMEM((D_FF, D), BF16)],
    )
    return pl.pallas_call(
        _expert_kernel,
        grid_spec=grid_spec,
        out_shape=jax.ShapeDtypeStruct((n_blocks * MOE_ROWS * CHUNKS, LANES), F32),
        compiler_params=_cparams(("arbitrary",)),
        name="experts",
    )(block_expert, n_used, xs, w_gate_up, b_gate_up.reshape(E, 1, F2), w_down, b_down.reshape(E, 1, D))


def _combine_kernel(meta_ref, next_meta_ref, pos_ref, gate_ref, h_ref, g_ref, ys_ref, y_ref, stage_ref, sems,
                    *, n_tiles):
    i = pl.program_id(0)
    slot = i % 2

    def fetch(meta, s):
        _for_each_run_piece(meta, lambda src, dst, n: pltpu.make_async_copy(
            _rows(ys_ref, src, n), _rows(stage_ref.at[s], dst, n), sems.at[s]).start(), exact=False)

    @pl.when(i == 0)
    def _():
        stage_ref[...] = jnp.zeros_like(stage_ref)
        fetch(meta_ref, 0)

    if n_tiles > 1:
        @pl.when(i + 1 < n_tiles)
        def _():
            fetch(next_meta_ref, 1 - slot)

    weights = _selection(pos_ref, gate_ref)
    _wait_rows(meta_ref[3, 0] * DMA_ROWS, ys_ref, stage_ref.at[slot], sems.at[slot])
    staged = _load_chunked(stage_ref.at[slot], STAGE_ROWS).astype(BF16)
    moe = lax.dot_general(weights, staged, (((0,), (0,)), ((), ())), preferred_element_type=F32)
    hf = h_ref[...] + moe
    ms = jnp.mean(hf * hf, axis=-1, keepdims=True)
    y_ref[...] = (hf * lax.rsqrt(ms + EPS)) * g_ref[...]


def _combine(meta, pos, gates, h, norm_g, ys, tile0):
    n, D = h.shape
    n_tiles = n // TOKEN_TILE
    return pl.pallas_call(
        functools.partial(_combine_kernel, n_tiles=n_tiles),
        grid=(n_tiles,),
        in_specs=[pl.BlockSpec((8, LANES), lambda i: (tile0 + i, 0), memory_space=pltpu.SMEM),
                  pl.BlockSpec((8, LANES), lambda i: (tile0 + jnp.minimum(i + 1, n_tiles - 1), 0),
                               memory_space=pltpu.SMEM),
                  pl.BlockSpec((TOP_K, TOKEN_TILE), lambda i: (0, tile0 + i)),
                  pl.BlockSpec((TOP_K, TOKEN_TILE), lambda i: (0, i)),
                  pl.BlockSpec((TOKEN_TILE, D), lambda i: (i, 0)),
                  pl.BlockSpec((1, D), lambda i: (0, 0)),
                  pl.BlockSpec(memory_space=pl.ANY)],
        out_specs=pl.BlockSpec((TOKEN_TILE, D), lambda i: (i, 0)),
        out_shape=jax.ShapeDtypeStruct((n, D), F32),
        scratch_shapes=[pltpu.VMEM((2, STAGE_ROWS * CHUNKS, LANES), F32), pltpu.SemaphoreType.DMA((2,))],
        compiler_params=_cparams(("arbitrary",)),
        name="combine",
    )(meta, meta, pos, gates, h, norm_g, ys)


def _prep_weights(w_in, w_alpha):
    w_main = w_in[:, :PROJ_MAIN].astype(BF16)
    w_lr = jnp.pad(w_in[:, PROJ_MAIN:], ((0, 0), (0, LANES - GATE_RANK))).astype(BF16)
    w_al = jnp.pad(w_alpha, ((0, LANES - GATE_RANK), (0, 0))).astype(BF16)
    return w_main, w_lr, w_al


def kernel(x_prompt, x_sample, cache_swa_k, cache_swa_v, state_gla, norm_mix_g, w_in, w_alpha, b_alpha, gla_norm_g, w_out, norm_ffn_g, w_router, b_router, w_gate_up, b_gate_up, w_down, b_down, norm_final_g):
    B, S, D = x_prompt.shape
    Bs, Ts, _ = x_sample.shape
    assert w_in.shape[0] == 1, "single-layer trunk"
    l = 0
    R = cache_swa_k.shape[2]
    rows_p = min(DILATIONS[-1] * KEYS_PER_CONFIG, S)
    w_main, w_lr, w_al = _prep_weights(w_in[l], w_alpha[l])
    g_mix = norm_mix_g[l][None]
    b_al = b_alpha[l][None]
    g_gla = gla_norm_g[l][None]

    pos_p = jnp.arange(S, dtype=jnp.int32)
    qa, ka, va, qb, kb, vb, zg, gb, k_tail, v_tail = _project(x_prompt, pos_p, g_mix, w_main, w_lr, w_al, b_al,
                                                              PROJ_TILE, tail_rows=rows_p)
    oa_p = _prompt_attention(qa, ka, va)
    ob_p, st_p = _gla(qb, kb, gb, vb, zg, jnp.zeros((B, N_HEADS_B, DK_B, DV_B), F32), g_gla,
                      GLA_TILE, GLA_CHUNK)
    k_prompt = k_tail.reshape(1, B, rows_p, N_HEADS_A, HEAD_DIM_A)
    v_prompt = v_tail.reshape(1, B, rows_p, N_HEADS_A, HEAD_DIM_A)

    pos_s = PAST_LEN + (jnp.arange(Bs * Ts, dtype=jnp.int32) % Ts)
    proj_s = _project(x_sample.reshape(1, Bs * Ts, D), pos_s, g_mix, w_main, w_lr, w_al, b_al, Bs * Ts)
    qa_s, ka_s, va_s, qb_s, kb_s, vb_s, zg_s, gb_s = [t.reshape(Bs, Ts, -1) for t in proj_s]
    oa_s, k_sample, v_sample = _sample_attention(qa_s, ka_s, va_s,
                                                 cache_swa_k[l].reshape(Bs, R, WIDTH_A),
                                                 cache_swa_v[l].reshape(Bs, R, WIDTH_A))
    ob_s, st_s = _gla(qb_s, kb_s, gb_s, vb_s, zg_s, state_gla[l], g_gla, Ts, Ts)

    w_out_bf = w_out[l].astype(BF16)
    g_ffn = norm_ffn_g[l][None]
    w_router_t = jnp.transpose(w_router[l])
    b_router_c = b_router[l][:, None]
    Np, Ns = B * S, Bs * Ts
    h_p, xn_p, idx_p, gate_p = _merge(oa_p.reshape(Np, WIDTH_A), ob_p.reshape(Np, WIDTH_B),
                                      x_prompt.reshape(Np, D), w_out_bf, g_ffn, w_router_t, b_router_c,
                                      MERGE_TILE)
    h_s, xn_s, idx_s, gate_s = _merge(oa_s.reshape(Ns, WIDTH_A), ob_s.reshape(Ns, WIDTH_B),
                                      x_sample.reshape(Ns, D), w_out_bf, g_ffn, w_router_t, b_router_c,
                                      Ns)

    y_p, y_s = _moe([(xn_p, idx_p, gate_p, h_p), (xn_s, idx_s, gate_s, h_s)],
                    w_gate_up[l], b_gate_up[l], w_down[l], b_down[l], norm_final_g[None])
    return (y_p.reshape(B, S, D), y_s.reshape(Bs, Ts, D), k_prompt, v_prompt, st_p[None],
            k_sample.reshape(1, Bs, R, N_HEADS_A, HEAD_DIM_A),
            v_sample.reshape(1, Bs, R, N_HEADS_A, HEAD_DIM_A), st_s[None])


PAST_LEN = 16384
PROJ_TILE = 1024
MERGE_TILE = 1024
GLA_TILE = 1024
GLA_CHUNK = 64


def _moe(groups, w_gate_up, b_gate_up, w_down, b_down, g_final):
    sizes = [g[3].shape[0] for g in groups]
    N = sum(sizes)
    assert all(n % TOKEN_TILE == 0 for n in sizes)
    n_blocks = -(-(N * TOP_K + N_EXPERTS * (MOE_ROWS - 1 + DMA_ROWS)) // MOE_ROWS)
    idx = jnp.concatenate([g[1] for g in groups], axis=1)
    pos, meta, block_expert, misc = _route(idx, n_blocks)
    first_tile = [sum(sizes[:i]) // TOKEN_TILE for i in range(len(sizes))]
    xs = _dispatch(misc, meta, pos, [g[0] for g in groups], n_blocks)
    ys = _experts(block_expert[0], misc[0, :1], xs, w_gate_up, b_gate_up, w_down, b_down, n_blocks)
    return [_combine(meta, pos, gates, h, g_final, ys, t0)
            for (_, _, gates, h), t0 in zip(groups, first_tile)]
```

```python
import functools

import jax
import jax.numpy as jnp
import numpy as np
from jax import lax
from jax.experimental import pallas as pl
from jax.experimental.pallas import tpu as pltpu

F32 = jnp.float32
BF16 = jnp.bfloat16

N_HEADS_A = 8
HEAD_DIM_A = 64
WIDTH_A = N_HEADS_A * HEAD_DIM_A
N_HEADS_B = 4
DK_B = 64
DV_B = 128
QK_B = N_HEADS_B * DK_B
WIDTH_B = N_HEADS_B * DV_B
GATE_RANK = 16
GATE_LOGIT_NORM = 16.0
DILATIONS = (1, 4, 16)
KEYS_PER_CONFIG = 128
ROPE_THETA = 10000.0
N_EXPERTS = 32
TOP_K = 4
SWIGLU_ALPHA = 1.702
SWIGLU_LIMIT = 7.0
EPS = 1e-6

LANES = 128
VMEM_LIMIT = 56 * 1024 * 1024


def _cparams(sem, vmem=VMEM_LIMIT):
    return pltpu.CompilerParams(dimension_semantics=sem, vmem_limit_bytes=vmem)


PROJ_MAIN = 3 * WIDTH_A + 2 * QK_B + 2 * WIDTH_B


def _rope_tables(pos):
    half = HEAD_DIM_A // 2
    inv = ROPE_THETA ** (-jnp.arange(half, dtype=F32) / half)
    ang = pos.astype(F32)[:, None] * inv[None, :]
    cos = jnp.cos(ang)
    sin = jnp.sin(ang)
    cos_t = jnp.concatenate([cos, cos, cos, cos], axis=-1)
    sin_t = jnp.concatenate([-sin, sin, -sin, sin], axis=-1)
    return cos_t, sin_t


def _rope_block(t, cos, sin, first_half):
    partner = jnp.where(first_half, pltpu.roll(t, LANES - 32, 1), pltpu.roll(t, 32, 1))
    return t * cos + partner * sin


def _proj_kernel(x_ref, g_ref, w_ref, wlr_ref, wa_ref, ba_ref, cos_ref, sin_ref,
                 qa_ref, ka_ref, va_ref, qb_ref, kb_ref, vb_ref, zg_ref, gb_ref, *tail_refs, first_tail_tile):
    x = x_ref[...]
    ms = jnp.mean(x * x, axis=-1, keepdims=True)
    xn = ((x * lax.rsqrt(ms + EPS)) * g_ref[...]).astype(BF16)

    def cols(lo, hi):
        return jnp.dot(xn, w_ref[:, lo:hi], preferred_element_type=F32)

    cos = cos_ref[...]
    sin = sin_ref[...]
    lane = lax.broadcasted_iota(jnp.int32, cos.shape, 1)
    first_half = (lane % HEAD_DIM_A) < (HEAD_DIM_A // 2)
    q = cols(0, WIDTH_A)
    k = cols(WIDTH_A, 2 * WIDTH_A)
    for j in range(WIDTH_A // LANES):
        sl = slice(j * LANES, (j + 1) * LANES)
        qa_ref[:, sl] = _rope_block(q[:, sl], cos, sin, first_half) * (HEAD_DIM_A ** -0.5)
        ka_ref[:, sl] = _rope_block(k[:, sl], cos, sin, first_half)
    o = 2 * WIDTH_A
    va_ref[...] = cols(o, o + WIDTH_A)
    o += WIDTH_A
    qb_ref[...] = cols(o, o + QK_B) * (DK_B ** -0.5)
    o += QK_B
    kb_ref[...] = cols(o, o + QK_B)
    o += QK_B
    vb_ref[...] = cols(o, o + WIDTH_B).astype(vb_ref.dtype)
    o += WIDTH_B
    zg_ref[...] = cols(o, o + WIDTH_B)
    lr = jnp.dot(xn, wlr_ref[...], preferred_element_type=F32)
    z = jnp.dot(lr.astype(BF16), wa_ref[...], preferred_element_type=F32) + ba_ref[...]
    logsig = jnp.minimum(z, 0.0) - jnp.log(1.0 + jnp.exp(-jnp.abs(z)))
    gb_ref[...] = logsig / GATE_LOGIT_NORM
    if tail_refs:
        @pl.when(pl.program_id(1) >= first_tail_tile)
        def _():
            tail_refs[0][...] = ka_ref[...]
            tail_refs[1][...] = va_ref[...]


def _project(x, pos, norm_g, w_main, w_lr, w_alpha, b_alpha, tm, tail_rows=0):
    B, T, D = x.shape
    assert tail_rows % tm == 0
    cos_t, sin_t = _rope_tables(pos)
    grid = (B, T // tm)
    first_tail_tile = (T - tail_rows) // tm
    row = lambda w: pl.BlockSpec((None, tm, w), lambda b, i: (b, i, 0))
    tail = pl.BlockSpec((None, tm, WIDTH_A), lambda b, i: (b, jnp.maximum(i - first_tail_tile, 0), 0))
    full = lambda a: pl.BlockSpec(a.shape, lambda b, i: (0,) * a.ndim)
    tab = pl.BlockSpec((tm, LANES), lambda b, i: (i, 0))
    widths = (WIDTH_A, WIDTH_A, WIDTH_A, QK_B, QK_B, WIDTH_B, WIDTH_B, QK_B)
    dtypes = (F32, F32, F32, F32, F32, BF16, F32, F32)
    n_tail = 2 if tail_rows else 0
    return pl.pallas_call(
        functools.partial(_proj_kernel, first_tail_tile=first_tail_tile),
        grid=grid,
        in_specs=[row(D), full(norm_g), full(w_main), full(w_lr), full(w_alpha), full(b_alpha), tab, tab],
        out_specs=[row(w) for w in widths] + [tail] * n_tail,
        out_shape=[jax.ShapeDtypeStruct((B, T, w), dt) for w, dt in zip(widths, dtypes)]
                  + [jax.ShapeDtypeStruct((B, tail_rows, WIDTH_A), F32)] * n_tail,
        compiler_params=_cparams(("parallel", "arbitrary")),
        name="proj",
    )(x, norm_g, w_main, w_lr, w_alpha, b_alpha, cos_t, sin_t)


Q_BLOCK = 128
NEG_INF = float("-inf")


def _attn_block(q, kb, v1, mask, head0, state):
    QB = q.shape[0]
    q2 = jnp.concatenate([jnp.where(head0, q, 0.0), jnp.where(head0, 0.0, q)], axis=0).astype(BF16)
    s = lax.dot_general(q2, kb, (((1,), (1,)), ((), ())), preferred_element_type=F32)
    s = jnp.where(mask, s, NEG_INF)
    if state is None:
        m2 = jnp.max(s, axis=1, keepdims=True)
    else:
        prev = jnp.concatenate([jnp.where(head0, state[0], NEG_INF),
                                jnp.where(head0, NEG_INF, state[0])], axis=0)
        m2 = jnp.max(jnp.concatenate([s, prev], axis=1), axis=1, keepdims=True)
    p = jnp.exp(s - m2)
    pv = jnp.dot(p.astype(BF16), v1, preferred_element_type=F32)
    m_full = jnp.where(head0, m2[:QB], m2[QB:])
    l_full = jnp.where(head0, pv[:QB, LANES:], pv[QB:, LANES:])
    pv_full = jnp.where(head0, pv[:QB, :LANES], pv[QB:, :LANES])
    if state is None:
        return m_full, l_full, pv_full
    a = jnp.exp(state[0] - m_full)
    return m_full, a * state[1] + l_full, a * state[2] + pv_full


def _attn_kernel(q_ref, k_ref, v_ref, o_ref, m_ref, l_ref, acc_ref):
    S = q_ref.shape[0]
    QB = Q_BLOCK
    lane = lax.broadcasted_iota(jnp.int32, (QB, LANES), 1)
    head0 = lane < HEAD_DIM_A
    qq = lax.broadcasted_iota(jnp.int32, (2 * QB, 2 * QB), 0) % QB
    kk = lax.broadcasted_iota(jnp.int32, (2 * QB, 2 * QB), 1)
    band = jnp.logical_and(kk >= qq, kk - qq <= KEYS_PER_CONFIG)
    causal = (lax.broadcasted_iota(jnp.int32, (2 * QB, QB), 1)
              <= lax.broadcasted_iota(jnp.int32, (2 * QB, QB), 0) % QB)

    GROUP = 4

    for ci, dil in enumerate(sorted(DILATIONS, reverse=True)):
        nblk = S // (dil * QB)
        assert nblk % GROUP == 0 or GROUP % nblk == 0

        def rows(start, n, dil=dil):
            return pl.ds(start, n) if dil == 1 else pl.ds(start, n, stride=dil)

        def do_group(blocks, ci=ci, dil=dil, rows=rows):
            chunks = {}

            def kv_chunk(r, rkey, jbase, c):
                if (rkey, c) not in chunks:
                    ks = rows(r + dil * QB * (jbase + c), QB)
                    v = v_ref[ks, :].astype(BF16)
                    chunks[(rkey, c)] = (k_ref[ks, :].astype(BF16),
                                         jnp.concatenate([v, jnp.ones(v.shape, BF16)], axis=1))
                return chunks[(rkey, c)]

            loaded = []
            for r, rkey, jbase, joff in blocks:
                first = isinstance(jbase, int) and jbase + joff == 0
                qs = rows(r + dil * QB * (jbase + joff), QB)
                parts = [kv_chunk(r, rkey, jbase, joff)]
                if not first:
                    parts.insert(0, kv_chunk(r, rkey, jbase, joff - 1))
                kb = jnp.concatenate([p[0] for p in parts], axis=0)
                v1 = jnp.concatenate([p[1] for p in parts], axis=0)
                state = None if ci == 0 else (m_ref[qs, :], l_ref[qs, :], acc_ref[qs, :])
                loaded.append((qs, q_ref[qs, :], kb, v1, causal if first else band, state))
            results = [_attn_block(q, kb, v1, mask, head0, state) for _, q, kb, v1, mask, state in loaded]
            for (qs, *_), (m, l, acc) in zip(loaded, results):
                m_ref[qs, :] = m
                l_ref[qs, :] = l
                acc_ref[qs, :] = acc

        if nblk >= GROUP:
            def residue(r, carry, nblk=nblk, do_group=do_group):
                do_group([(r, 0, 0, j) for j in range(GROUP)])

                def rest(g, c):
                    do_group([(r, 0, GROUP * g, u) for u in range(GROUP)])
                    return c

                return lax.fori_loop(1, nblk // GROUP, rest, carry)

            lax.fori_loop(0, dil, residue, 0)
        else:
            per = 2 * GROUP // nblk

            def residues(g, carry, nblk=nblk, per=per, do_group=do_group):
                do_group([(g * per + i, i, 0, j) for j in range(nblk) for i in range(per)])
                return carry

            lax.fori_loop(0, dil // per, residues, 0)

    def finish(i, c):
        rs = pl.ds(pl.multiple_of(i * QB, QB), QB)
        o_ref[rs, :] = (acc_ref[rs, :] / l_ref[rs, :]).astype(o_ref.dtype)
        return c

    lax.fori_loop(0, S // QB, finish, 0, unroll=4)


def _prompt_attention(qa, ka, va):
    B, S, W = qa.shape
    spec = pl.BlockSpec((None, S, LANES), lambda b, hp: (b, 0, hp))
    return pl.pallas_call(
        _attn_kernel,
        grid=(B, W // LANES),
        in_specs=[spec, spec, spec],
        out_specs=spec,
        out_shape=jax.ShapeDtypeStruct((B, S, W), BF16),
        scratch_shapes=[pltpu.VMEM((S, LANES), F32)] * 3,
        compiler_params=_cparams(("parallel", "parallel")),
        name="prompt_attn",
    )(qa, ka, va)


def _split3(x):
    x1 = x.astype(BF16)
    r1 = x - x1.astype(F32)
    x2 = r1.astype(BF16)
    x3 = (r1 - x2.astype(F32)).astype(BF16)
    return x1, x2, x3


def _gla_kernel(q_ref, k_ref, g_ref, v_ref, z_ref, s0_ref, ng_ref, o_ref, sfin_ref, st_ref, *, chunk):
    C = chunk
    TS = q_ref.shape[0]
    n_pairs = N_HEADS_B // 2
    PW = 2 * DV_B
    t_idx = pl.program_id(1)

    lane_k = lax.broadcasted_iota(jnp.int32, (C, LANES), 1)
    head0 = lane_k < DK_B
    row_k = lax.broadcasted_iota(jnp.int32, (C, LANES), 0)
    tri2 = (lax.broadcasted_iota(jnp.int32, (2 * C, C), 1)
            <= lax.broadcasted_iota(jnp.int32, (2 * C, C), 0) % C)
    value_head0 = lax.broadcasted_iota(jnp.int32, (C, PW), 1) < DV_B
    bd_mask = ((lax.broadcasted_iota(jnp.int32, (PW, LANES), 0) // DV_B)
               == (lax.broadcasted_iota(jnp.int32, (PW, LANES), 1) // DK_B))

    def prefix_rows(x):
        shift = 1
        while shift < C:
            x = x + jnp.where(row_k >= shift, pltpu.roll(x, shift, 0), 0.0)
            shift *= 2
        return x

    @pl.when(t_idx == 0)
    def _():
        for p in range(n_pairs):
            for h in range(2):
                blk = jnp.transpose(s0_ref[2 * p + h])
                pad = jnp.zeros((DV_B, DK_B), F32)
                row = jnp.concatenate([blk, pad] if h == 0 else [pad, blk], axis=1)
                st_ref[p, h * DV_B:(h + 1) * DV_B, :] = row

    def chunk_body(c, carry):
        rs = pl.ds(pl.multiple_of(c * C, C), C)
        for p in range(n_pairs):
            kl = slice(p * LANES, (p + 1) * LANES)
            vl = slice(p * PW, (p + 1) * PW)
            q = q_ref[rs, kl]
            k = k_ref[rs, kl]
            g = g_ref[rs, kl]
            v = v_ref[rs, vl]
            b = prefix_rows(g)
            b_last = b[C - 1:C, :]
            b_mid = b[C // 2 - 1:C // 2, :] if C > 1 else b_last
            qe = q * jnp.exp(b - b_mid)
            ke = (k * jnp.exp(b_mid - b)).astype(BF16)
            st = st_ref[p]
            q_in = (q * jnp.exp(b)).astype(BF16)
            nt = (((1,), (1,)), ((), ()))
            o = lax.dot_general(q_in, st.astype(BF16), nt, preferred_element_type=F32)
            q2 = jnp.concatenate([jnp.where(head0, qe, 0.0), jnp.where(head0, 0.0, qe)], axis=0)
            a2 = lax.dot_general(q2.astype(BF16), ke, nt, preferred_element_type=F32)
            a2 = jnp.where(tri2, a2, 0.0).astype(BF16)
            av = jnp.dot(a2, v, preferred_element_type=F32)
            o = o + jnp.where(value_head0, av[:C], av[C:])
            k_dec = (k * jnp.exp(b_last - b)).astype(BF16)
            upd = lax.dot_general(v, k_dec, (((0,), (0,)), ((), ())), preferred_element_type=F32)
            st_ref[p] = jnp.exp(b_last) * st + jnp.where(bd_mask, upd, 0.0)
            for h in range(2):
                oh = o[:, h * DV_B:(h + 1) * DV_B]
                hl = slice((2 * p + h) * DV_B, (2 * p + h + 1) * DV_B)
                ms = jnp.mean(oh * oh, axis=-1, keepdims=True)
                z = z_ref[rs, hl]
                gated = (oh * lax.rsqrt(ms + EPS)) * ng_ref[:, hl] * (z / (1.0 + jnp.exp(-z)))
                o_ref[rs, hl] = gated.astype(o_ref.dtype)
        return carry

    lax.fori_loop(0, TS // C, chunk_body, 0, unroll=2 if (TS // C) % 2 == 0 else 1)

    @pl.when(t_idx == pl.num_programs(1) - 1)
    def _():
        for p in range(n_pairs):
            for h in range(2):
                blk = st_ref[p, h * DV_B:(h + 1) * DV_B, h * DK_B:(h + 1) * DK_B]
                sfin_ref[2 * p + h] = jnp.transpose(blk)


def _gla(qb, kb, gb, vb, zg, state0, norm_g, ts, chunk):
    B, T, _ = qb.shape
    row = lambda w: pl.BlockSpec((None, ts, w), lambda b, i: (b, i, 0))
    st_spec = pl.BlockSpec((None, N_HEADS_B, DK_B, DV_B), lambda b, i: (b, 0, 0, 0))
    return pl.pallas_call(
        functools.partial(_gla_kernel, chunk=chunk),
        grid=(B, T // ts),
        in_specs=[row(QK_B), row(QK_B), row(QK_B), row(WIDTH_B), row(WIDTH_B), st_spec,
                  pl.BlockSpec((1, WIDTH_B), lambda b, i: (0, 0))],
        out_specs=[row(WIDTH_B), st_spec],
        out_shape=[jax.ShapeDtypeStruct((B, T, WIDTH_B), BF16),
                   jax.ShapeDtypeStruct((B, N_HEADS_B, DK_B, DV_B), F32)],
        scratch_shapes=[pltpu.VMEM((N_HEADS_B // 2, 2 * DV_B, LANES), F32)],
        compiler_params=_cparams(("parallel", "arbitrary")),
        name="gla",
    )(qb, kb, gb, vb, zg, state0, norm_g)


def _sample_attn_kernel(q_ref, kn_ref, vn_ref, kc_ref, vc_ref, o_ref, ko_ref, vo_ref):
    T = q_ref.shape[0]
    R = kc_ref.shape[0]
    W = q_ref.shape[1]
    HT = N_HEADS_A * T
    q = q_ref[...]
    qx = jnp.concatenate([q] * N_HEADS_A, axis=0)
    own = ((lax.broadcasted_iota(jnp.int32, (HT, W), 0) // T)
           == (lax.broadcasted_iota(jnp.int32, (HT, W), 1) // HEAD_DIM_A))
    qx = jnp.where(own, qx, 0.0).astype(BF16)
    pad = jnp.zeros((LANES - T, W), F32)
    kn = jnp.concatenate([kn_ref[...], pad], axis=0).astype(BF16)
    vn = jnp.concatenate([vn_ref[...], pad], axis=0).astype(BF16)
    nt = (((1,), (1,)), ((), ()))
    s_c = lax.dot_general(qx, kc_ref[...].astype(BF16), nt, preferred_element_type=F32)
    s_n = lax.dot_general(qx, kn, nt, preferred_element_type=F32)

    def multiplicity(n_cols, first_row):
        t = lax.broadcasted_iota(jnp.int32, (HT, n_cols), 0) % T
        j = lax.broadcasted_iota(jnp.int32, (HT, n_cols), 1) + first_row
        delta = R + t - j
        cnt = jnp.zeros((HT, n_cols), F32)
        for dil in DILATIONS:
            hit = (delta >= 0) & (delta <= dil * KEYS_PER_CONFIG) & (delta % dil == 0)
            cnt = cnt + jnp.where(hit, 1.0, 0.0)
        return cnt

    cnt_c = multiplicity(R, 0)
    cnt_n = multiplicity(LANES, R)
    s_c = jnp.where(cnt_c > 0.0, s_c, NEG_INF)
    s_n = jnp.where(cnt_n > 0.0, s_n, NEG_INF)
    m = jnp.maximum(jnp.max(s_c, axis=1, keepdims=True), jnp.max(s_n, axis=1, keepdims=True))
    p_c = cnt_c * jnp.exp(s_c - m)
    p_n = cnt_n * jnp.exp(s_n - m)
    den = jnp.sum(p_c, axis=1, keepdims=True) + jnp.sum(p_n, axis=1, keepdims=True)
    full = (jnp.dot(p_c.astype(BF16), vc_ref[...].astype(BF16), preferred_element_type=F32)
            + jnp.dot(p_n.astype(BF16), vn, preferred_element_type=F32)) / den
    full = jnp.where(own, full, 0.0)
    out = full[0:T, :]
    for h in range(1, N_HEADS_A):
        out = out + full[h * T:(h + 1) * T, :]
    o_ref[...] = out.astype(o_ref.dtype)
    ko_ref[0:R - T, :] = kc_ref[T:R, :]
    ko_ref[R - T:R, :] = kn_ref[...]
    vo_ref[0:R - T, :] = vc_ref[T:R, :]
    vo_ref[R - T:R, :] = vn_ref[...]


def _sample_attention(qa, ka, va, cache_k, cache_v):
    B, T, W = qa.shape
    R = cache_k.shape[1]
    assert R >= DILATIONS[-1] * KEYS_PER_CONFIG and T % 8 == 0 and T <= LANES
    new = pl.BlockSpec((None, T, W), lambda b: (b, 0, 0))
    cache = pl.BlockSpec((None, R, W), lambda b: (b, 0, 0))
    return pl.pallas_call(
        _sample_attn_kernel,
        grid=(B,),
        in_specs=[new, new, new, cache, cache],
        out_specs=[new, cache, cache],
        out_shape=[jax.ShapeDtypeStruct((B, T, W), BF16),
                   jax.ShapeDtypeStruct((B, R, W), F32),
                   jax.ShapeDtypeStruct((B, R, W), F32)],
        compiler_params=_cparams(("parallel",)),
        name="sample_attn",
    )(qa, ka, va, cache_k, cache_v)


CHUNKS = 8


def _store_chunked(ref, val):
    n = val.shape[0]
    for s in range(CHUNKS):
        ref[pl.ds(s, n, stride=CHUNKS), :] = val[:, s * LANES:(s + 1) * LANES]


def _load_chunked(ref, n):
    return jnp.concatenate([ref[pl.ds(s, n, stride=CHUNKS), :] for s in range(CHUNKS)], axis=1)


def _split2(x):
    hi = x.astype(BF16)
    return hi, (x - hi.astype(F32)).astype(BF16)


def _merge_kernel(oa_ref, ob_ref, x_ref, wo_ref, g_ref, wr_ref, br_ref,
                  h_ref, xn_ref, idx_ref, gate_ref):
    TM = x_ref.shape[0]
    mixed = (jnp.dot(oa_ref[...], wo_ref[0:WIDTH_A, :], preferred_element_type=F32)
             + jnp.dot(ob_ref[...], wo_ref[WIDTH_A:, :], preferred_element_type=F32))
    h = x_ref[...] + mixed
    h_ref[...] = h
    ms = jnp.mean(h * h, axis=-1, keepdims=True)
    xn = (h * lax.rsqrt(ms + EPS)) * g_ref[...]
    xn_ref[...] = xn.astype(xn_ref.dtype)
    nt = (((1,), (1,)), ((), ()))
    xh, xl = _split2(xn)
    wh, wl = _split2(wr_ref[...])
    logits = (lax.dot_general(wh, xh, nt, preferred_element_type=F32)
              + lax.dot_general(wh, xl, nt, preferred_element_type=F32)
              + lax.dot_general(wl, xh, nt, preferred_element_type=F32)) + br_ref[...]
    e_iota = lax.broadcasted_iota(jnp.int32, (N_EXPERTS, TM), 0)
    vals, idxs = [], []
    for _ in range(TOP_K):
        m = jnp.max(logits, axis=0, keepdims=True)
        sel = jnp.min(jnp.where(logits == m, e_iota, N_EXPERTS), axis=0, keepdims=True)
        vals.append(m)
        idxs.append(sel)
        logits = jnp.where(e_iota == sel, NEG_INF, logits)
    ex = [jnp.exp(v - vals[0]) for v in vals]
    den = ex[0] + ex[1] + ex[2] + ex[3]
    idx_ref[...] = jnp.concatenate(idxs, axis=0)
    gate_ref[...] = jnp.concatenate([e / den for e in ex], axis=0)


def _merge(oa, ob, x, w_out, norm_g, w_router_t, b_router, tm):
    N, D = x.shape
    full = lambda a: pl.BlockSpec(a.shape, lambda i: (0,) * a.ndim)
    row = lambda w: pl.BlockSpec((tm, w), lambda i: (i, 0))
    col = pl.BlockSpec((TOP_K, tm), lambda i: (0, i))
    return pl.pallas_call(
        _merge_kernel,
        grid=(N // tm,),
        in_specs=[row(WIDTH_A), row(WIDTH_B), row(D), full(w_out), full(norm_g), full(w_router_t),
                  full(b_router)],
        out_specs=[row(D), row(D), col, col],
        out_shape=[jax.ShapeDtypeStruct((N, D), F32),
                   jax.ShapeDtypeStruct((N, D), BF16),
                   jax.ShapeDtypeStruct((TOP_K, N), jnp.int32),
                   jax.ShapeDtypeStruct((TOP_K, N), F32)],
        compiler_params=_cparams(("parallel",)),
        name="merge_router",
    )(oa, ob, x, w_out, norm_g, w_router_t, b_router)


MOE_ROWS = 512
TOKEN_TILE = 256
DMA_ROWS = 8
STAGE_ROWS = 1280
assert STAGE_ROWS >= TOKEN_TILE * TOP_K + N_EXPERTS * (DMA_ROWS - 1) and STAGE_ROWS % 8 == 0


def _expert_row(col):
    r = lax.broadcasted_iota(jnp.int32, (N_EXPERTS, LANES), 0)
    c = lax.broadcasted_iota(jnp.int32, (N_EXPERTS, LANES), 1)
    return jnp.sum(jnp.where(r == c, col, 0.0), axis=0, keepdims=True)


def _expert_prefix(col):
    r = lax.broadcasted_iota(jnp.int32, (N_EXPERTS, LANES), 0)
    c = lax.broadcasted_iota(jnp.int32, (N_EXPERTS, LANES), 1)
    return jnp.sum(jnp.where(c < r, _expert_row(col), 0.0), axis=1, keepdims=True)


def _route_kernel(idx_all_ref, idx_ref, pos_ref, meta_ref, be_ref, misc_ref, carry_ref, start_ref):
    i = pl.program_id(0)
    TT = idx_ref.shape[1]
    NBP = be_ref.shape[1]

    @pl.when(i == 0)
    def _():
        idx_all = idx_all_ref[...]
        e_all = lax.broadcasted_iota(jnp.int32, (N_EXPERTS, idx_all.shape[1]), 0)
        tot = jnp.zeros((N_EXPERTS, 1), F32)
        for k in range(TOP_K):
            tot = tot + jnp.sum(jnp.where(idx_all[k:k + 1, :] == e_all, 1.0, 0.0), axis=1, keepdims=True)
        padded = jnp.floor((tot + (DMA_ROWS + MOE_ROWS - 1)) / MOE_ROWS) * MOE_ROWS
        start = _expert_prefix(padded)
        start_ref[...] = start
        carry_ref[...] = jnp.zeros_like(carry_ref)
        end = start + padded
        block_start = lax.broadcasted_iota(jnp.int32, (N_EXPERTS, NBP), 1).astype(F32) * MOE_ROWS
        be = jnp.sum(jnp.where(end <= block_start, 1.0, 0.0), axis=0, keepdims=True)
        be_ref[...] = jnp.minimum(be, N_EXPERTS - 1).astype(jnp.int32)
        n_used = jnp.broadcast_to(jnp.sum(padded, axis=0, keepdims=True) / MOE_ROWS, (1, LANES))
        zero = jnp.zeros((1, LANES), F32)
        misc_ref[...] = jnp.concatenate([n_used, _expert_row(start + tot)] + [zero] * 6,
                                        axis=0).astype(jnp.int32)

    idx = idx_ref[...]
    e_iota = lax.broadcasted_iota(jnp.int32, (N_EXPERTS, TT), 0)
    onehot = [idx[k:k + 1, :] == e_iota for k in range(TOP_K)]
    cnt = jnp.zeros((N_EXPERTS, TT), F32)
    for oh in onehot:
        cnt = cnt + jnp.where(oh, 1.0, 0.0)
    tile_tot = jnp.sum(cnt, axis=1, keepdims=True)
    earlier = (lax.broadcasted_iota(jnp.int32, (TT, TT), 0)
               < lax.broadcasted_iota(jnp.int32, (TT, TT), 1))
    before = jnp.dot(cnt.astype(BF16), jnp.where(earlier, 1.0, 0.0).astype(BF16),
                     preferred_element_type=F32)
    chunks = jnp.floor((tile_tot + (DMA_ROWS - 1)) / DMA_ROWS)
    seg = _expert_prefix(chunks * DMA_ROWS)
    where_staged = seg + before
    rows = [jnp.sum(jnp.where(oh, where_staged, 0.0), axis=0, keepdims=True) for oh in onehot]
    pos_ref[...] = jnp.concatenate(rows, axis=0).astype(jnp.int32)
    zero = jnp.zeros((1, LANES), F32)
    total = jnp.broadcast_to(jnp.sum(chunks, axis=0, keepdims=True), (1, LANES))
    meta_ref[...] = jnp.concatenate(
        [_expert_row(start_ref[...] + carry_ref[...]), _expert_row(seg), _expert_row(tile_tot), total]
        + [zero] * 4, axis=0).astype(jnp.int32)
    carry_ref[...] += tile_tot


def _route(idx, n_blocks):
    _, N = idx.shape
    nbp = -(-n_blocks // LANES) * LANES
    tile = pl.BlockSpec((TOP_K, TOKEN_TILE), lambda i: (0, i))
    return pl.pallas_call(
        _route_kernel,
        grid=(N // TOKEN_TILE,),
        in_specs=[pl.BlockSpec((TOP_K, N), lambda i: (0, 0)), tile],
        out_specs=[tile,
                   pl.BlockSpec((8, LANES), lambda i: (i, 0)),
                   pl.BlockSpec((1, nbp), lambda i: (0, 0)),
                   pl.BlockSpec((8, LANES), lambda i: (0, 0))],
        out_shape=[jax.ShapeDtypeStruct((TOP_K, N), jnp.int32),
                   jax.ShapeDtypeStruct((N // TOKEN_TILE * 8, LANES), jnp.int32),
                   jax.ShapeDtypeStruct((1, nbp), jnp.int32),
                   jax.ShapeDtypeStruct((8, LANES), jnp.int32)],
        scratch_shapes=[pltpu.VMEM((N_EXPERTS, 1), F32)] * 2,
        compiler_params=_cparams(("arbitrary",)),
        name="route",
    )(idx, idx)


def _rows(ref, first_row, n_rows):
    return ref.at[pl.ds(pl.multiple_of(first_row * CHUNKS, CHUNKS), n_rows * CHUNKS), :]


def _for_each_run_piece(meta_ref, fn, exact):
    def per_expert(e, c):
        slot, off, n = meta_ref[0, e], meta_ref[1, e], meta_ref[2, e]
        whole = n // DMA_ROWS if exact else (n + DMA_ROWS - 1) // DMA_ROWS

        def per_chunk(j, c2):
            fn(slot + j * DMA_ROWS, off + j * DMA_ROWS, DMA_ROWS)
            return c2

        lax.fori_loop(0, whole, per_chunk, 0)
        if exact:
            rem = n % DMA_ROWS
            piece = DMA_ROWS // 2
            while piece:
                done = whole * DMA_ROWS + (rem & ~(2 * piece - 1))

                @pl.when((rem & piece) != 0)
                def _(done=done, piece=piece):
                    fn(slot + done, off + done, piece)

                piece //= 2
        return c

    lax.fori_loop(0, N_EXPERTS, per_expert, 0)


def _wait_rows(n_rows, src_ref, dst_ref, sem):
    @pl.when(n_rows > 0)
    def _():
        n = n_rows * CHUNKS
        pltpu.make_async_copy(src_ref.at[pl.ds(0, n), :], dst_ref.at[pl.ds(0, n), :], sem).wait()


PAD_ROWS = MOE_ROWS + DMA_ROWS


def _zero_padding(misc_ref, xs_ref, zeros_ref, sem, n_blocks):
    zeros_ref[...] = jnp.zeros_like(zeros_ref)

    def pad_copy(e):
        first = pl.multiple_of(misc_ref[1, e] * CHUNKS, CHUNKS)
        return pltpu.make_async_copy(zeros_ref, xs_ref.at[pl.ds(first, PAD_ROWS * CHUNKS), :], sem)

    def tail_copy(b):
        first = pl.multiple_of(b * (MOE_ROWS * CHUNKS), MOE_ROWS * CHUNKS)
        return pltpu.make_async_copy(zeros_ref.at[pl.ds(0, MOE_ROWS * CHUNKS), :],
                                     xs_ref.at[pl.ds(first, MOE_ROWS * CHUNKS), :], sem)

    def pad(e, c):
        pad_copy(e).start()
        pad_copy(e).wait()
        return c

    def tail(start_not_wait):
        def body(b, c):
            tail_copy(b).start() if start_not_wait else tail_copy(b).wait()
            return c

        lax.fori_loop(misc_ref[0, 0], n_blocks + 1, body, 0)

    lax.fori_loop(0, N_EXPERTS, pad, 0)
    tail(True)
    tail(False)


def _selection(pos_ref, fill_ref=None):
    TT = pos_ref.shape[1]
    p_iota = lax.broadcasted_iota(jnp.int32, (STAGE_ROWS, TT), 0)
    sel = jnp.zeros((STAGE_ROWS, TT), F32)
    for k in range(TOP_K):
        val = 1.0 if fill_ref is None else fill_ref[k:k + 1, :]
        sel = jnp.where(pos_ref[k:k + 1, :] == p_iota, val, sel)
    return sel.astype(BF16)


def _dispatch_kernel(misc_ref, meta_ref, pos_ref, *refs, first_tiles, n_tiles, n_blocks):
    x_refs = refs[:len(first_tiles)]
    xs_ref, stage_ref, zeros_ref, sems, pad_sem = refs[len(first_tiles):]
    i = pl.program_id(0)
    TT = pos_ref.shape[1]
    slot = i % 2

    @pl.when(i == 0)
    def _():
        _zero_padding(misc_ref, xs_ref, zeros_ref, pad_sem, n_blocks)

    x = x_refs[0][...]
    for t0, ref in zip(first_tiles[1:], x_refs[1:]):
        x = jnp.where(i >= t0, ref[...], x)
    staged = jnp.dot(_selection(pos_ref), x, preferred_element_type=F32)
    stage = stage_ref.at[slot]
    _store_chunked(stage, staged)
    _for_each_run_piece(meta_ref, lambda dst, src, n: pltpu.make_async_copy(
        _rows(stage, src, n), _rows(xs_ref, dst, n), sems.at[slot]).start(), exact=True)

    @pl.when(i > 0)
    def _():
        _wait_rows(TT * TOP_K, stage_ref.at[1 - slot], xs_ref, sems.at[1 - slot])

    @pl.when(i == n_tiles - 1)
    def _():
        _wait_rows(TT * TOP_K, stage, xs_ref, sems.at[slot])


def _dispatch(misc, meta, pos, xns, n_blocks):
    D = xns[0].shape[1]
    tiles = [x.shape[0] // TOKEN_TILE for x in xns]
    first_tiles = tuple(sum(tiles[:g]) for g in range(len(tiles)))

    def x_spec(t0, nt):
        return pl.BlockSpec((TOKEN_TILE, D), lambda i: (jnp.clip(i - t0, 0, nt - 1), 0))

    return pl.pallas_call(
        functools.partial(_dispatch_kernel, first_tiles=first_tiles, n_tiles=sum(tiles), n_blocks=n_blocks),
        grid=(sum(tiles),),
        in_specs=[pl.BlockSpec(memory_space=pltpu.SMEM),
                  pl.BlockSpec((8, LANES), lambda i: (i, 0), memory_space=pltpu.SMEM),
                  pl.BlockSpec((TOP_K, TOKEN_TILE), lambda i: (0, i))]
                 + [x_spec(t0, nt) for t0, nt in zip(first_tiles, tiles)],
        out_specs=pl.BlockSpec(memory_space=pl.ANY),
        out_shape=jax.ShapeDtypeStruct(((n_blocks + 1) * MOE_ROWS * CHUNKS, LANES), F32),
        scratch_shapes=[pltpu.VMEM((2, STAGE_ROWS * CHUNKS, LANES), F32),
                        pltpu.VMEM((PAD_ROWS * CHUNKS, LANES), F32),
                        pltpu.SemaphoreType.DMA((2,)),
                        pltpu.SemaphoreType.DMA(())],
        compiler_params=_cparams(("arbitrary",)),
        name="dispatch",
    )(misc, meta, pos, *xns)


def _expert_kernel(be_ref, nused_ref, xs_ref, wgu_ref, bgu_ref, wd_ref, bd_ref, ys_ref,
                   wgu_bf, wd_bf):
    i = pl.program_id(0)
    D_FF = wd_ref.shape[0]
    new_expert = jnp.logical_or(i == 0, be_ref[i] != be_ref[jnp.maximum(i - 1, 0)])

    @pl.when(jnp.logical_and(i < nused_ref[0], new_expert))
    def _():
        wgu_bf[...] = wgu_ref[...].astype(BF16)
        wd_bf[...] = wd_ref[...].astype(BF16)

    @pl.when(i < nused_ref[0])
    def _():
        x = _load_chunked(xs_ref, MOE_ROWS).astype(BF16)
        hdn = jnp.dot(x, wgu_bf[...], preferred_element_type=F32) + bgu_ref[...]
        glu = jnp.minimum(hdn[:, :D_FF], SWIGLU_LIMIT)
        lin = jnp.clip(hdn[:, D_FF:], -SWIGLU_LIMIT, SWIGLU_LIMIT)
        act = glu * (1.0 / (1.0 + jnp.exp(-SWIGLU_ALPHA * glu))) * (lin + 1.0)
        y = jnp.dot(act.astype(BF16), wd_bf[...], preferred_element_type=F32) + bd_ref[...]
        _store_chunked(ys_ref, y)

    @pl.when(i >= nused_ref[0])
    def _():
        ys_ref[...] = jnp.zeros_like(ys_ref)


def _experts(block_expert, n_used, xs, w_gate_up, b_gate_up, w_down, b_down, n_blocks):
    E, D, F2 = w_gate_up.shape
    D_FF = w_down.shape[1]
    rows = pl.BlockSpec((MOE_ROWS * CHUNKS, LANES), lambda i, be, nu: (i, 0))
    rows_in = pl.BlockSpec((MOE_ROWS * CHUNKS, LANES), lambda i, be, nu: (jnp.minimum(i, nu[0] - 1), 0))
    grid_spec = pltpu.PrefetchScalarGridSpec(
        num_scalar_prefetch=2,
        grid=(n_blocks,),
        in_specs=[rows_in,
                  pl.BlockSpec((None, D, F2), lambda i, be, nu: (be[i], 0, 0)),
                  pl.BlockSpec((None, 1, F2), lambda i, be, nu: (be[i], 0, 0)),
                  pl.BlockSpec((None, D_FF, D), lambda i, be, nu: (be[i], 0, 0)),
                  pl.BlockSpec((None, 1, D), lambda i, be, nu: (be[i], 0, 0))],
        out_specs=rows,
        scratch_shapes=[pltpu.VMEM((D, F2), BF16), pltpu.VMEM((D_FF, D), BF16)],
    )
    return pl.pallas_call(
        _expert_kernel,
        grid_spec=grid_spec,
        out_shape=jax.ShapeDtypeStruct((n_blocks * MOE_ROWS * CHUNKS, LANES), F32),
        compiler_params=_cparams(("arbitrary",)),
        name="experts",
    )(block_expert, n_used, xs, w_gate_up, b_gate_up.reshape(E, 1, F2), w_down, b_down.reshape(E, 1, D))


def _combine_kernel(meta_ref, next_meta_ref, pos_ref, gate_ref, h_ref, g_ref, ys_ref, y_ref, stage_ref, sems,
                    *, n_tiles):
    i = pl.program_id(0)
    slot = i % 2

    def fetch(meta, s):
        _for_each_run_piece(meta, lambda src, dst, n: pltpu.make_async_copy(
            _rows(ys_ref, src, n), _rows(stage_ref.at[s], dst, n), sems.at[s]).start(), exact=False)

    @pl.when(i == 0)
    def _():
        stage_ref[...] = jnp.zeros_like(stage_ref)
        fetch(meta_ref, 0)

    if n_tiles > 1:
        @pl.when(i + 1 < n_tiles)
        def _():
            fetch(next_meta_ref, 1 - slot)

    weights = _selection(pos_ref, gate_ref)
    _wait_rows(meta_ref[3, 0] * DMA_ROWS, ys_ref, stage_ref.at[slot], sems.at[slot])
    staged = _load_chunked(stage_ref.at[slot], STAGE_ROWS).astype(BF16)
    moe = lax.dot_general(weights, staged, (((0,), (0,)), ((), ())), preferred_element_type=F32)
    hf = h_ref[...] + moe
    ms = jnp.mean(hf * hf, axis=-1, keepdims=True)
    y_ref[...] = (hf * lax.rsqrt(ms + EPS)) * g_ref[...]


def _combine(meta, pos, gates, h, norm_g, ys, tile0):
    n, D = h.shape
    n_tiles = n // TOKEN_TILE
    return pl.pallas_call(
        functools.partial(_combine_kernel, n_tiles=n_tiles),
        grid=(n_tiles,),
        in_specs=[pl.BlockSpec((8, LANES), lambda i: (tile0 + i, 0), memory_space=pltpu.SMEM),
                  pl.BlockSpec((8, LANES), lambda i: (tile0 + jnp.minimum(i + 1, n_tiles - 1), 0),
                               memory_space=pltpu.SMEM),
                  pl.BlockSpec((TOP_K, TOKEN_TILE), lambda i: (0, tile0 + i)),
                  pl.BlockSpec((TOP_K, TOKEN_TILE), lambda i: (0, i)),
                  pl.BlockSpec((TOKEN_TILE, D), lambda i: (i, 0)),
                  pl.BlockSpec((1, D), lambda i: (0, 0)),
                  pl.BlockSpec(memory_space=pl.ANY)],
        out_specs=pl.BlockSpec((TOKEN_TILE, D), lambda i: (i, 0)),
        out_shape=jax.ShapeDtypeStruct((n, D), F32),
        scratch_shapes=[pltpu.VMEM((2, STAGE_ROWS * CHUNKS, LANES), F32), pltpu.SemaphoreType.DMA((2,))],
        compiler_params=_cparams(("arbitrary",)),
        name="combine",
    )(meta, meta, pos, gates, h, norm_g, ys)


def _prep_weights(w_in, w_alpha):
    w_main = w_in[:, :PROJ_MAIN].astype(BF16)
    w_lr = jnp.pad(w_in[:, PROJ_MAIN:], ((0, 0), (0, LANES - GATE_RANK))).astype(BF16)
    w_al = jnp.pad(w_alpha, ((0, LANES - GATE_RANK), (0, 0))).astype(BF16)
    return w_main, w_lr, w_al


def kernel(x_prompt, x_sample, cache_swa_k, cache_swa_v, state_gla, norm_mix_g, w_in, w_alpha, b_alpha, gla_norm_g, w_out, norm_ffn_g, w_router, b_router, w_gate_up, b_gate_up, w_down, b_down, norm_final_g):
    B, S, D = x_prompt.shape
    Bs, Ts, _ = x_sample.shape
    assert w_in.shape[0] == 1, "single-layer trunk"
    l = 0
    R = cache_swa_k.shape[2]
    rows_p = min(DILATIONS[-1] * KEYS_PER_CONFIG, S)
    w_main, w_lr, w_al = _prep_weights(w_in[l], w_alpha[l])
    g_mix = norm_mix_g[l][None]
    b_al = b_alpha[l][None]
    g_gla = gla_norm_g[l][None]

    pos_p = jnp.arange(S, dtype=jnp.int32)
    qa, ka, va, qb, kb, vb, zg, gb, k_tail, v_tail = _project(x_prompt, pos_p, g_mix, w_main, w_lr, w_al, b_al,
                                                              PROJ_TILE, tail_rows=rows_p)
    oa_p = _prompt_attention(qa, ka, va)
    ob_p, st_p = _gla(qb, kb, gb, vb, zg, jnp.zeros((B, N_HEADS_B, DK_B, DV_B), F32), g_gla,
                      GLA_TILE, GLA_CHUNK)
    k_prompt = k_tail.reshape(1, B, rows_p, N_HEADS_A, HEAD_DIM_A)
    v_prompt = v_tail.reshape(1, B, rows_p, N_HEADS_A, HEAD_DIM_A)

    pos_s = PAST_LEN + (jnp.arange(Bs * Ts, dtype=jnp.int32) % Ts)
    proj_s = _project(x_sample.reshape(1, Bs * Ts, D), pos_s, g_mix, w_main, w_lr, w_al, b_al, Bs * Ts)
    qa_s, ka_s, va_s, qb_s, kb_s, vb_s, zg_s, gb_s = [t.reshape(Bs, Ts, -1) for t in proj_s]
    oa_s, k_sample, v_sample = _sample_attention(qa_s, ka_s, va_s,
                                                 cache_swa_k[l].reshape(Bs, R, WIDTH_A),
                                                 cache_swa_v[l].reshape(Bs, R, WIDTH_A))
    ob_s, st_s = _gla(qb_s, kb_s, gb_s, vb_s, zg_s, state_gla[l], g_gla, Ts, Ts)

    w_out_bf = w_out[l].astype(BF16)
    g_ffn = norm_ffn_g[l][None]
    w_router_t = jnp.transpose(w_router[l])
    b_router_c = b_router[l][:, None]
    Np, Ns = B * S, Bs * Ts
    h_p, xn_p, idx_p, gate_p = _merge(oa_p.reshape(Np, WIDTH_A), ob_p.reshape(Np, WIDTH_B),
                                      x_prompt.reshape(Np, D), w_out_bf, g_ffn, w_router_t, b_router_c,
                                      MERGE_TILE)
    h_s, xn_s, idx_s, gate_s = _merge(oa_s.reshape(Ns, WIDTH_A), ob_s.reshape(Ns, WIDTH_B),
                                      x_sample.reshape(Ns, D), w_out_bf, g_ffn, w_router_t, b_router_c,
                                      Ns)

    y_p, y_s = _moe([(xn_p, idx_p, gate_p, h_p), (xn_s, idx_s, gate_s, h_s)],
                    w_gate_up[l], b_gate_up[l], w_down[l], b_down[l], norm_final_g[None])
    return (y_p.reshape(B, S, D), y_s.reshape(Bs, Ts, D), k_prompt, v_prompt, st_p[None],
            k_sample.reshape(1, Bs, R, N_HEADS_A, HEAD_DIM_A),
            v_sample.reshape(1, Bs, R, N_HEADS_A, HEAD_DIM_A), st_s[None])


PAST_LEN = 16384
PROJ_TILE = 1024
MERGE_TILE = 1024
GLA_TILE = 1024
GLA_CHUNK = 64


def _moe(groups, w_gate_up, b_gate_up, w_down, b_down, g_final):
    sizes = [g[3].shape[0] for g in groups]
    N = sum(sizes)
    assert all(n % TOKEN_TILE == 0 for n in sizes)
    n_blocks = -(-(N * TOP_K + N_EXPERTS * (MOE_ROWS - 1 + DMA_ROWS)) // MOE_ROWS)
    idx = jnp.concatenate([g[1] for g in groups], axis=1)
    pos, meta, block_expert, misc = _route(idx, n_blocks)
    first_tile = [sum(sizes[:i]) // TOKEN_TILE for i in range(len(sizes))]
    xs = _dispatch(misc, meta, pos, [g[0] for g in groups], n_blocks)
    ys = _experts(block_expert[0], misc[0, :1], xs, w_gate_up, b_gate_up, w_down, b_down, n_blocks)
    return [_combine(meta, pos, gates, h, g_final, ys, t0)
            for (_, _, gates, h), t0 in zip(groups, first_tile)]
```

```python
import functools

import jax
import jax.numpy as jnp
import numpy as np
from jax import lax
from jax.experimental import pallas as pl
from jax.experimental.pallas import tpu as pltpu

F32 = jnp.float32
BF16 = jnp.bfloat16

N_HEADS_A = 8
HEAD_DIM_A = 64
WIDTH_A = N_HEADS_A * HEAD_DIM_A
N_HEADS_B = 4
DK_B = 64
DV_B = 128
QK_B = N_HEADS_B * DK_B
WIDTH_B = N_HEADS_B * DV_B
GATE_RANK = 16
GATE_LOGIT_NORM = 16.0
DILATIONS = (1, 4, 16)
KEYS_PER_CONFIG = 128
ROPE_THETA = 10000.0
N_EXPERTS = 32
TOP_K = 4
SWIGLU_ALPHA = 1.702
SWIGLU_LIMIT = 7.0
EPS = 1e-6

LANES = 128
VMEM_LIMIT = 56 * 1024 * 1024


def _cparams(sem, vmem=VMEM_LIMIT):
    return pltpu.CompilerParams(dimension_semantics=sem, vmem_limit_bytes=vmem)


PROJ_MAIN = 3 * WIDTH_A + 2 * QK_B + 2 * WIDTH_B


def _rope_tables(pos):
    half = HEAD_DIM_A // 2
    inv = ROPE_THETA ** (-jnp.arange(half, dtype=F32) / half)
    ang = pos.astype(F32)[:, None] * inv[None, :]
    cos = jnp.cos(ang)
    sin = jnp.sin(ang)
    cos_t = jnp.concatenate([cos, cos, cos, cos], axis=-1)
    sin_t = jnp.concatenate([-sin, sin, -sin, sin], axis=-1)
    return cos_t, sin_t


def _rope_block(t, cos, sin, first_half):
    partner = jnp.where(first_half, pltpu.roll(t, LANES - 32, 1), pltpu.roll(t, 32, 1))
    return t * cos + partner * sin


def _proj_kernel(x_ref, g_ref, w_ref, wlr_ref, wa_ref, ba_ref, cos_ref, sin_ref,
                 qa_ref, ka_ref, va_ref, qb_ref, kb_ref, vb_ref, zg_ref, gb_ref, *tail_refs, first_tail_tile):
    x = x_ref[...]
    ms = jnp.mean(x * x, axis=-1, keepdims=True)
    xn = ((x * lax.rsqrt(ms + EPS)) * g_ref[...]).astype(BF16)

    def cols(lo, hi):
        return jnp.dot(xn, w_ref[:, lo:hi], preferred_element_type=F32)

    cos = cos_ref[...]
    sin = sin_ref[...]
    lane = lax.broadcasted_iota(jnp.int32, cos.shape, 1)
    first_half = (lane % HEAD_DIM_A) < (HEAD_DIM_A // 2)
    q = cols(0, WIDTH_A)
    k = cols(WIDTH_A, 2 * WIDTH_A)
    for j in range(WIDTH_A // LANES):
        sl = slice(j * LANES, (j + 1) * LANES)
        qa_ref[:, sl] = _rope_block(q[:, sl], cos, sin, first_half) * (HEAD_DIM_A ** -0.5)
        ka_ref[:, sl] = _rope_block(k[:, sl], cos, sin, first_half)
    o = 2 * WIDTH_A
    va_ref[...] = cols(o, o + WIDTH_A)
    o += WIDTH_A
    qb_ref[...] = cols(o, o + QK_B) * (DK_B ** -0.5)
    o += QK_B
    kb_ref[...] = cols(o, o + QK_B)
    o += QK_B
    vb_ref[...] = cols(o, o + WIDTH_B).astype(vb_ref.dtype)
    o += WIDTH_B
    zg_ref[...] = cols(o, o + WIDTH_B)
    lr = jnp.dot(xn, wlr_ref[...], preferred_element_type=F32)
    z = jnp.dot(lr.astype(BF16), wa_ref[...], preferred_element_type=F32) + ba_ref[...]
    logsig = jnp.minimum(z, 0.0) - jnp.log(1.0 + jnp.exp(-jnp.abs(z)))
    gb_ref[...] = logsig / GATE_LOGIT_NORM
    if tail_refs:
        @pl.when(pl.program_id(1) >= first_tail_tile)
        def _():
            tail_refs[0][...] = ka_ref[...]
            tail_refs[1][...] = va_ref[...]


def _project(x, pos, norm_g, w_main, w_lr, w_alpha, b_alpha, tm, tail_rows=0):
    B, T, D = x.shape
    assert tail_rows % tm == 0
    cos_t, sin_t = _rope_tables(pos)
    grid = (B, T // tm)
    first_tail_tile = (T - tail_rows) // tm
    row = lambda w: pl.BlockSpec((None, tm, w), lambda b, i: (b, i, 0))
    tail = pl.BlockSpec((None, tm, WIDTH_A), lambda b, i: (b, jnp.maximum(i - first_tail_tile, 0), 0))
    full = lambda a: pl.BlockSpec(a.shape, lambda b, i: (0,) * a.ndim)
    tab = pl.BlockSpec((tm, LANES), lambda b, i: (i, 0))
    widths = (WIDTH_A, WIDTH_A, WIDTH_A, QK_B, QK_B, WIDTH_B, WIDTH_B, QK_B)
    dtypes = (F32, F32, F32, F32, F32, BF16, F32, F32)
    n_tail = 2 if tail_rows else 0
    return pl.pallas_call(
        functools.partial(_proj_kernel, first_tail_tile=first_tail_tile),
        grid=grid,
        in_specs=[row(D), full(norm_g), full(w_main), full(w_lr), full(w_alpha), full(b_alpha), tab, tab],
        out_specs=[row(w) for w in widths] + [tail] * n_tail,
        out_shape=[jax.ShapeDtypeStruct((B, T, w), dt) for w, dt in zip(widths, dtypes)]
                  + [jax.ShapeDtypeStruct((B, tail_rows, WIDTH_A), F32)] * n_tail,
        compiler_params=_cparams(("parallel", "arbitrary")),
        name="proj",
    )(x, norm_g, w_main, w_lr, w_alpha, b_alpha, cos_t, sin_t)


Q_BLOCK = 128
NEG_INF = float("-inf")


def _attn_block(q, kb, v1, mask, head0, state):
    QB = q.shape[0]
    q2 = jnp.concatenate([jnp.where(head0, q, 0.0), jnp.where(head0, 0.0, q)], axis=0).astype(BF16)
    s = lax.dot_general(q2, kb, (((1,), (1,)), ((), ())), preferred_element_type=F32)
    s = jnp.where(mask, s, NEG_INF)
    if state is None:
        m2 = jnp.max(s, axis=1, keepdims=True)
    else:
        prev = jnp.concatenate([jnp.where(head0, state[0], NEG_INF),
                                jnp.where(head0, NEG_INF, state[0])], axis=0)
        m2 = jnp.max(jnp.concatenate([s, prev], axis=1), axis=1, keepdims=True)
    p = jnp.exp(s - m2)
    pv = jnp.dot(p.astype(BF16), v1, preferred_element_type=F32)
    m_full = jnp.where(head0, m2[:QB], m2[QB:])
    l_full = jnp.where(head0, pv[:QB, LANES:], pv[QB:, LANES:])
    pv_full = jnp.where(head0, pv[:QB, :LANES], pv[QB:, :LANES])
    if state is None:
        return m_full, l_full, pv_full
    a = jnp.exp(state[0] - m_full)
    return m_full, a * state[1] + l_full, a * state[2] + pv_full


def _attn_kernel(q_ref, k_ref, v_ref, o_ref, m_ref, l_ref, acc_ref):
    S = q_ref.shape[0]
    QB = Q_BLOCK
    lane = lax.broadcasted_iota(jnp.int32, (QB, LANES), 1)
    head0 = lane < HEAD_DIM_A
    qq = lax.broadcasted_iota(jnp.int32, (2 * QB, 2 * QB), 0) % QB
    kk = lax.broadcasted_iota(jnp.int32, (2 * QB, 2 * QB), 1)
    band = jnp.logical_and(kk >= qq, kk - qq <= KEYS_PER_CONFIG)
    causal = (lax.broadcasted_iota(jnp.int32, (2 * QB, QB), 1)
              <= lax.broadcasted_iota(jnp.int32, (2 * QB, QB), 0) % QB)

    GROUP = 4

    for ci, dil in enumerate(sorted(DILATIONS, reverse=True)):
        nblk = S // (dil * QB)
        assert nblk % GROUP == 0 or GROUP % nblk == 0

        def rows(start, n, dil=dil):
            return pl.ds(start, n) if dil == 1 else pl.ds(start, n, stride=dil)

        def do_group(blocks, ci=ci, dil=dil, rows=rows):
            chunks = {}

            def kv_chunk(r, rkey, jbase, c):
                if (rkey, c) not in chunks:
                    ks = rows(r + dil * QB * (jbase + c), QB)
                    v = v_ref[ks, :].astype(BF16)
                    chunks[(rkey, c)] = (k_ref[ks, :].astype(BF16),
                                         jnp.concatenate([v, jnp.ones(v.shape, BF16)], axis=1))
                return chunks[(rkey, c)]

            loaded = []
            for r, rkey, jbase, joff in blocks:
                first = isinstance(jbase, int) and jbase + joff == 0
                qs = rows(r + dil * QB * (jbase + joff), QB)
                parts = [kv_chunk(r, rkey, jbase, joff)]
                if not first:
                    parts.insert(0, kv_chunk(r, rkey, jbase, joff - 1))
                kb = jnp.concatenate([p[0] for p in parts], axis=0)
                v1 = jnp.concatenate([p[1] for p in parts], axis=0)
                state = None if ci == 0 else (m_ref[qs, :], l_ref[qs, :], acc_ref[qs, :])
                loaded.append((qs, q_ref[qs, :], kb, v1, causal if first else band, state))
            results = [_attn_block(q, kb, v1, mask, head0, state) for _, q, kb, v1, mask, state in loaded]
            for (qs, *_), (m, l, acc) in zip(loaded, results):
                m_ref[qs, :] = m
                l_ref[qs, :] = l
                acc_ref[qs, :] = acc

        if nblk >= GROUP:
            def residue(r, carry, nblk=nblk, do_group=do_group):
                do_group([(r, 0, 0, j) for j in range(GROUP)])

                def rest(g, c):
                    do_group([(r, 0, GROUP * g, u) for u in range(GROUP)])
                    return c

                return lax.fori_loop(1, nblk // GROUP, rest, carry)

            lax.fori_loop(0, dil, residue, 0)
        else:
            per = 2 * GROUP // nblk

            def residues(g, carry, nblk=nblk, per=per, do_group=do_group):
                do_group([(g * per + i, i, 0, j) for j in range(nblk) for i in range(per)])
                return carry

            lax.fori_loop(0, dil // per, residues, 0)

    def finish(i, c):
        rs = pl.ds(pl.multiple_of(i * QB, QB), QB)
        o_ref[rs, :] = (acc_ref[rs, :] / l_ref[rs, :]).astype(o_ref.dtype)
        return c

    lax.fori_loop(0, S // QB, finish, 0, unroll=4)


def _prompt_attention(qa, ka, va):
    B, S, W = qa.shape
    spec = pl.BlockSpec((None, S, LANES), lambda b, hp: (b, 0, hp))
    return pl.pallas_call(
        _attn_kernel,
        grid=(B, W // LANES),
        in_specs=[spec, spec, spec],
        out_specs=spec,
        out_shape=jax.ShapeDtypeStruct((B, S, W), BF16),
        scratch_shapes=[pltpu.VMEM((S, LANES), F32)] * 3,
        compiler_params=_cparams(("parallel", "parallel")),
        name="prompt_attn",
    )(qa, ka, va)


def _gla_kernel(q_ref, k_ref, g_ref, v_ref, z_ref, s0_ref, ng_ref, o_ref, sfin_ref, st_ref, *, chunk):
    C = chunk
    TS = q_ref.shape[0]
    n_pairs = N_HEADS_B // 2
    PW = 2 * DV_B
    t_idx = pl.program_id(1)

    lane_k = lax.broadcasted_iota(jnp.int32, (C, LANES), 1)
    head0 = lane_k < DK_B
    row_k = lax.broadcasted_iota(jnp.int32, (C, LANES), 0)
    tri2 = (lax.broadcasted_iota(jnp.int32, (2 * C, C), 1)
            <= lax.broadcasted_iota(jnp.int32, (2 * C, C), 0) % C)
    value_head0 = lax.broadcasted_iota(jnp.int32, (C, PW), 1) < DV_B
    bd_mask = ((lax.broadcasted_iota(jnp.int32, (PW, LANES), 0) // DV_B)
               == (lax.broadcasted_iota(jnp.int32, (PW, LANES), 1) // DK_B))

    def prefix_rows(x):
        shift = 1
        while shift < C:
            x = x + jnp.where(row_k >= shift, pltpu.roll(x, shift, 0), 0.0)
            shift *= 2
        return x

    @pl.when(t_idx == 0)
    def _():
        for p in range(n_pairs):
            for h in range(2):
                blk = jnp.transpose(s0_ref[2 * p + h])
                pad = jnp.zeros((DV_B, DK_B), F32)
                row = jnp.concatenate([blk, pad] if h == 0 else [pad, blk], axis=1)
                st_ref[p, h * DV_B:(h + 1) * DV_B, :] = row

    def chunk_body(c, carry):
        rs = pl.ds(pl.multiple_of(c * C, C), C)
        for p in range(n_pairs):
            kl = slice(p * LANES, (p + 1) * LANES)
            vl = slice(p * PW, (p + 1) * PW)
            q = q_ref[rs, kl]
            k = k_ref[rs, kl]
            g = g_ref[rs, kl]
            v = v_ref[rs, vl]
            b = prefix_rows(g)
            b_last = b[C - 1:C, :]
            b_mid = b[C // 2 - 1:C // 2, :] if C > 1 else b_last
            qe = q * jnp.exp(b - b_mid)
            ke = (k * jnp.exp(b_mid - b)).astype(BF16)
            st = st_ref[p]
            q_in = (q * jnp.exp(b)).astype(BF16)
            nt = (((1,), (1,)), ((), ()))
            o = lax.dot_general(q_in, st.astype(BF16), nt, preferred_element_type=F32)
            q2 = jnp.concatenate([jnp.where(head0, qe, 0.0), jnp.where(head0, 0.0, qe)], axis=0)
            a2 = lax.dot_general(q2.astype(BF16), ke, nt, preferred_element_type=F32)
            a2 = jnp.where(tri2, a2, 0.0).astype(BF16)
            av = jnp.dot(a2, v, preferred_element_type=F32)
            o = o + jnp.where(value_head0, av[:C], av[C:])
            k_dec = (k * jnp.exp(b_last - b)).astype(BF16)
            upd = lax.dot_general(v, k_dec, (((0,), (0,)), ((), ())), preferred_element_type=F32)
            st_ref[p] = jnp.exp(b_last) * st + jnp.where(bd_mask, upd, 0.0)
            for h in range(2):
                oh = o[:, h * DV_B:(h + 1) * DV_B]
                hl = slice((2 * p + h) * DV_B, (2 * p + h + 1) * DV_B)
                ms = jnp.mean(oh * oh, axis=-1, keepdims=True)
                z = z_ref[rs, hl]
                gated = (oh * lax.rsqrt(ms + EPS)) * ng_ref[:, hl] * (z / (1.0 + jnp.exp(-z)))
                o_ref[rs, hl] = gated.astype(o_ref.dtype)
        return carry

    lax.fori_loop(0, TS // C, chunk_body, 0, unroll=2 if (TS // C) % 2 == 0 else 1)

    @pl.when(t_idx == pl.num_programs(1) - 1)
    def _():
        for p in range(n_pairs):
            for h in range(2):
                blk = st_ref[p, h * DV_B:(h + 1) * DV_B, h * DK_B:(h + 1) * DK_B]
                sfin_ref[2 * p + h] = jnp.transpose(blk)


def _gla(qb, kb, gb, vb, zg, state0, norm_g, ts, chunk):
    B, T, _ = qb.shape
    row = lambda w: pl.BlockSpec((None, ts, w), lambda b, i: (b, i, 0))
    st_spec = pl.BlockSpec((None, N_HEADS_B, DK_B, DV_B), lambda b, i: (b, 0, 0, 0))
    return pl.pallas_call(
        functools.partial(_gla_kernel, chunk=chunk),
        grid=(B, T // ts),
        in_specs=[row(QK_B), row(QK_B), row(QK_B), row(WIDTH_B), row(WIDTH_B), st_spec,
                  pl.BlockSpec((1, WIDTH_B), lambda b, i: (0, 0))],
        out_specs=[row(WIDTH_B), st_spec],
        out_shape=[jax.ShapeDtypeStruct((B, T, WIDTH_B), BF16),
                   jax.ShapeDtypeStruct((B, N_HEADS_B, DK_B, DV_B), F32)],
        scratch_shapes=[pltpu.VMEM((N_HEADS_B // 2, 2 * DV_B, LANES), F32)],
        compiler_params=_cparams(("parallel", "arbitrary")),
        name="gla",
    )(qb, kb, gb, vb, zg, state0, norm_g)


def _sample_attn_kernel(q_ref, kn_ref, vn_ref, kc_ref, vc_ref, o_ref, ko_ref, vo_ref):
    T = q_ref.shape[0]
    R = kc_ref.shape[0]
    W = q_ref.shape[1]
    HT = N_HEADS_A * T
    q = q_ref[...]
    qx = jnp.concatenate([q] * N_HEADS_A, axis=0)
    own = ((lax.broadcasted_iota(jnp.int32, (HT, W), 0) // T)
           == (lax.broadcasted_iota(jnp.int32, (HT, W), 1) // HEAD_DIM_A))
    qx = jnp.where(own, qx, 0.0).astype(BF16)
    pad = jnp.zeros((LANES - T, W), F32)
    kn = jnp.concatenate([kn_ref[...], pad], axis=0).astype(BF16)
    vn = jnp.concatenate([vn_ref[...], pad], axis=0).astype(BF16)
    nt = (((1,), (1,)), ((), ()))
    s_c = lax.dot_general(qx, kc_ref[...].astype(BF16), nt, preferred_element_type=F32)
    s_n = lax.dot_general(qx, kn, nt, preferred_element_type=F32)

    def multiplicity(n_cols, first_row):
        t = lax.broadcasted_iota(jnp.int32, (HT, n_cols), 0) % T
        j = lax.broadcasted_iota(jnp.int32, (HT, n_cols), 1) + first_row
        delta = R + t - j
        cnt = jnp.zeros((HT, n_cols), F32)
        for dil in DILATIONS:
            hit = (delta >= 0) & (delta <= dil * KEYS_PER_CONFIG) & (delta % dil == 0)
            cnt = cnt + jnp.where(hit, 1.0, 0.0)
        return cnt

    cnt_c = multiplicity(R, 0)
    cnt_n = multiplicity(LANES, R)
    s_c = jnp.where(cnt_c > 0.0, s_c, NEG_INF)
    s_n = jnp.where(cnt_n > 0.0, s_n, NEG_INF)
    m = jnp.maximum(jnp.max(s_c, axis=1, keepdims=True), jnp.max(s_n, axis=1, keepdims=True))
    p_c = cnt_c * jnp.exp(s_c - m)
    p_n = cnt_n * jnp.exp(s_n - m)
    den = jnp.sum(p_c, axis=1, keepdims=True) + jnp.sum(p_n, axis=1, keepdims=True)
    full = (jnp.dot(p_c.astype(BF16), vc_ref[...].astype(BF16), preferred_element_type=F32)
            + jnp.dot(p_n.astype(BF16), vn, preferred_element_type=F32)) / den
    full = jnp.where(own, full, 0.0)
    out = full[0:T, :]
    for h in range(1, N_HEADS_A):
        out = out + full[h * T:(h + 1) * T, :]
    o_ref[...] = out.astype(o_ref.dtype)
    ko_ref[0:R - T, :] = kc_ref[T:R, :]
    ko_ref[R - T:R, :] = kn_ref[...]
    vo_ref[0:R - T, :] = vc_ref[T:R, :]
    vo_ref[R - T:R, :] = vn_ref[...]


def _sample_attention(qa, ka, va, cache_k, cache_v):
    B, T, W = qa.shape
    R = cache_k.shape[1]
    assert R >= DILATIONS[-1] * KEYS_PER_CONFIG and T % 8 == 0 and T <= LANES
    new = pl.BlockSpec((None, T, W), lambda b: (b, 0, 0))
    cache = pl.BlockSpec((None, R, W), lambda b: (b, 0, 0))
    return pl.pallas_call(
        _sample_attn_kernel,
        grid=(B,),
        in_specs=[new, new, new, cache, cache],
        out_specs=[new, cache, cache],
        out_shape=[jax.ShapeDtypeStruct((B, T, W), BF16),
                   jax.ShapeDtypeStruct((B, R, W), F32),
                   jax.ShapeDtypeStruct((B, R, W), F32)],
        compiler_params=_cparams(("parallel",)),
        name="sample_attn",
    )(qa, ka, va, cache_k, cache_v)


CHUNKS = 8


def _store_chunked(ref, val):
    n = val.shape[0]
    for s in range(CHUNKS):
        ref[pl.ds(s, n, stride=CHUNKS), :] = val[:, s * LANES:(s + 1) * LANES]


def _load_chunked(ref, n):
    return jnp.concatenate([ref[pl.ds(s, n, stride=CHUNKS), :] for s in range(CHUNKS)], axis=1)


def _split2(x):
    hi = x.astype(BF16)
    return hi, (x - hi.astype(F32)).astype(BF16)


def _merge_kernel(oa_ref, ob_ref, x_ref, wo_ref, g_ref, wr_ref, br_ref,
                  h_ref, xn_ref, idx_ref, gate_ref):
    TM = x_ref.shape[0]
    mixed = (jnp.dot(oa_ref[...], wo_ref[0:WIDTH_A, :], preferred_element_type=F32)
             + jnp.dot(ob_ref[...], wo_ref[WIDTH_A:, :], preferred_element_type=F32))
    h = x_ref[...] + mixed
    h_ref[...] = h
    ms = jnp.mean(h * h, axis=-1, keepdims=True)
    xn = (h * lax.rsqrt(ms + EPS)) * g_ref[...]
    xn_ref[...] = xn.astype(xn_ref.dtype)
    nt = (((1,), (1,)), ((), ()))
    xh, xl = _split2(xn)
    wh, wl = _split2(wr_ref[...])
    logits = (lax.dot_general(wh, xh, nt, preferred_element_type=F32)
              + lax.dot_general(wh, xl, nt, preferred_element_type=F32)
              + lax.dot_general(wl, xh, nt, preferred_element_type=F32)) + br_ref[...]
    e_iota = lax.broadcasted_iota(jnp.int32, (N_EXPERTS, TM), 0)
    vals, idxs = [], []
    for _ in range(TOP_K):
        m = jnp.max(logits, axis=0, keepdims=True)
        sel = jnp.min(jnp.where(logits == m, e_iota, N_EXPERTS), axis=0, keepdims=True)
        vals.append(m)
        idxs.append(sel)
        logits = jnp.where(e_iota == sel, NEG_INF, logits)
    ex = [jnp.exp(v - vals[0]) for v in vals]
    den = ex[0] + ex[1] + ex[2] + ex[3]
    idx_ref[...] = jnp.concatenate(idxs, axis=0)
    gate_ref[...] = jnp.concatenate([e / den for e in ex], axis=0)


def _merge(oa, ob, x, w_out, norm_g, w_router_t, b_router, tm):
    N, D = x.shape
    full = lambda a: pl.BlockSpec(a.shape, lambda i: (0,) * a.ndim)
    row = lambda w: pl.BlockSpec((tm, w), lambda i: (i, 0))
    col = pl.BlockSpec((TOP_K, tm), lambda i: (0, i))
    return pl.pallas_call(
        _merge_kernel,
        grid=(N // tm,),
        in_specs=[row(WIDTH_A), row(WIDTH_B), row(D), full(w_out), full(norm_g), full(w_router_t),
                  full(b_router)],
        out_specs=[row(D), row(D), col, col],
        out_shape=[jax.ShapeDtypeStruct((N, D), F32),
                   jax.ShapeDtypeStruct((N, D), BF16),
                   jax.ShapeDtypeStruct((TOP_K, N), jnp.int32),
                   jax.ShapeDtypeStruct((TOP_K, N), F32)],
        compiler_params=_cparams(("parallel",)),
        name="merge_router",
    )(oa, ob, x, w_out, norm_g, w_router_t, b_router)


MOE_ROWS = 512
TOKEN_TILE = 256
DMA_ROWS = 8
STAGE_ROWS = TOKEN_TILE * TOP_K


def _expert_row(col):
    r = lax.broadcasted_iota(jnp.int32, (N_EXPERTS, LANES), 0)
    c = lax.broadcasted_iota(jnp.int32, (N_EXPERTS, LANES), 1)
    return jnp.sum(jnp.where(r == c, col, 0.0), axis=0, keepdims=True)


def _expert_prefix(col):
    r = lax.broadcasted_iota(jnp.int32, (N_EXPERTS, LANES), 0)
    c = lax.broadcasted_iota(jnp.int32, (N_EXPERTS, LANES), 1)
    return jnp.sum(jnp.where(c < r, _expert_row(col), 0.0), axis=1, keepdims=True)


def _route_kernel(idx_all_ref, idx_ref, pos_ref, meta_ref, be_ref, misc_ref, carry_ref, start_ref):
    i = pl.program_id(0)
    TT = idx_ref.shape[1]
    NBP = be_ref.shape[1]

    @pl.when(i == 0)
    def _():
        idx_all = idx_all_ref[...]
        e_all = lax.broadcasted_iota(jnp.int32, (N_EXPERTS, idx_all.shape[1]), 0)
        tot = jnp.zeros((N_EXPERTS, 1), F32)
        for k in range(TOP_K):
            tot = tot + jnp.sum(jnp.where(idx_all[k:k + 1, :] == e_all, 1.0, 0.0), axis=1, keepdims=True)
        padded = jnp.floor((tot + (MOE_ROWS - 1)) / MOE_ROWS) * MOE_ROWS
        start = _expert_prefix(padded)
        start_ref[...] = start
        carry_ref[...] = jnp.zeros_like(carry_ref)
        end = start + padded
        block_start = lax.broadcasted_iota(jnp.int32, (N_EXPERTS, NBP), 1).astype(F32) * MOE_ROWS
        be = jnp.sum(jnp.where(end <= block_start, 1.0, 0.0), axis=0, keepdims=True)
        be_ref[...] = jnp.minimum(be, N_EXPERTS - 1).astype(jnp.int32)
        n_used = jnp.broadcast_to(jnp.sum(padded, axis=0, keepdims=True) / MOE_ROWS, (1, LANES))
        zero = jnp.zeros((1, LANES), F32)
        misc_ref[...] = jnp.concatenate([n_used, _expert_row(start + tot)] + [zero] * 6,
                                        axis=0).astype(jnp.int32)

    idx = idx_ref[...]
    e_iota = lax.broadcasted_iota(jnp.int32, (N_EXPERTS, TT), 0)
    onehot = [idx[k:k + 1, :] == e_iota for k in range(TOP_K)]
    cnt = jnp.zeros((N_EXPERTS, TT), F32)
    for oh in onehot:
        cnt = cnt + jnp.where(oh, 1.0, 0.0)
    tile_tot = jnp.sum(cnt, axis=1, keepdims=True)
    earlier = (lax.broadcasted_iota(jnp.int32, (TT, TT), 0)
               < lax.broadcasted_iota(jnp.int32, (TT, TT), 1))
    before = jnp.dot(cnt.astype(BF16), jnp.where(earlier, 1.0, 0.0).astype(BF16),
                     preferred_element_type=F32)
    seg = _expert_prefix(tile_tot)
    where_staged = seg + before
    rows = [jnp.sum(jnp.where(oh, where_staged, 0.0), axis=0, keepdims=True) for oh in onehot]
    pos_ref[...] = jnp.concatenate(rows, axis=0).astype(jnp.int32)
    zero = jnp.zeros((1, LANES), F32)
    meta_ref[...] = jnp.concatenate(
        [_expert_row(start_ref[...] + carry_ref[...]), _expert_row(seg), _expert_row(tile_tot)]
        + [zero] * 5, axis=0).astype(jnp.int32)
    carry_ref[...] += tile_tot


def _route(idx, n_blocks):
    _, N = idx.shape
    nbp = -(-n_blocks // LANES) * LANES
    tile = pl.BlockSpec((TOP_K, TOKEN_TILE), lambda i: (0, i))
    return pl.pallas_call(
        _route_kernel,
        grid=(N // TOKEN_TILE,),
        in_specs=[pl.BlockSpec((TOP_K, N), lambda i: (0, 0)), tile],
        out_specs=[tile,
                   pl.BlockSpec((8, LANES), lambda i: (i, 0)),
                   pl.BlockSpec((1, nbp), lambda i: (0, 0)),
                   pl.BlockSpec((8, LANES), lambda i: (0, 0))],
        out_shape=[jax.ShapeDtypeStruct((TOP_K, N), jnp.int32),
                   jax.ShapeDtypeStruct((N // TOKEN_TILE * 8, LANES), jnp.int32),
                   jax.ShapeDtypeStruct((1, nbp), jnp.int32),
                   jax.ShapeDtypeStruct((8, LANES), jnp.int32)],
        scratch_shapes=[pltpu.VMEM((N_EXPERTS, 1), F32)] * 2,
        compiler_params=_cparams(("arbitrary",)),
        name="route",
    )(idx, idx)


def _rows(ref, first_row, n_rows):
    return ref.at[pl.ds(pl.multiple_of(first_row * CHUNKS, CHUNKS), n_rows * CHUNKS), :]


def _for_each_run_piece(meta_ref, fn):
    def per_expert(e, c):
        slot, off, n = meta_ref[0, e], meta_ref[1, e], meta_ref[2, e]
        whole = n // DMA_ROWS

        def per_chunk(j, c2):
            fn(slot + j * DMA_ROWS, off + j * DMA_ROWS, DMA_ROWS)
            return c2

        lax.fori_loop(0, whole, per_chunk, 0)
        rem = n % DMA_ROWS
        piece = DMA_ROWS // 2
        while piece:
            done = whole * DMA_ROWS + (rem & ~(2 * piece - 1))

            @pl.when((rem & piece) != 0)
            def _(done=done, piece=piece):
                fn(slot + done, off + done, piece)

            piece //= 2
        return c

    lax.fori_loop(0, N_EXPERTS, per_expert, 0)


def _wait_rows(n_rows, src_ref, dst_ref, sem):
    @pl.when(n_rows > 0)
    def _():
        n = n_rows * CHUNKS
        pltpu.make_async_copy(src_ref.at[pl.ds(0, n), :], dst_ref.at[pl.ds(0, n), :], sem).wait()


PAD_ROWS = MOE_ROWS


def _zero_padding(misc_ref, xs_ref, zeros_ref, sem, n_blocks):
    zeros_ref[...] = jnp.zeros_like(zeros_ref)

    def pad_copy(e):
        first = pl.multiple_of(misc_ref[1, e] * CHUNKS, CHUNKS)
        return pltpu.make_async_copy(zeros_ref, xs_ref.at[pl.ds(first, PAD_ROWS * CHUNKS), :], sem)

    def tail_copy(b):
        first = pl.multiple_of(b * (MOE_ROWS * CHUNKS), MOE_ROWS * CHUNKS)
        return pltpu.make_async_copy(zeros_ref.at[pl.ds(0, MOE_ROWS * CHUNKS), :],
                                     xs_ref.at[pl.ds(first, MOE_ROWS * CHUNKS), :], sem)

    def pad(e, c):
        pad_copy(e).start()
        pad_copy(e).wait()
        return c

    def tail(start_not_wait):
        def body(b, c):
            tail_copy(b).start() if start_not_wait else tail_copy(b).wait()
            return c

        lax.fori_loop(misc_ref[0, 0], n_blocks + 1, body, 0)

    lax.fori_loop(0, N_EXPERTS, pad, 0)
    tail(True)
    tail(False)


def _selection(pos_ref, fill_ref=None):
    TT = pos_ref.shape[1]
    p_iota = lax.broadcasted_iota(jnp.int32, (STAGE_ROWS, TT), 0)
    sel = jnp.zeros((STAGE_ROWS, TT), F32)
    for k in range(TOP_K):
        val = 1.0 if fill_ref is None else fill_ref[k:k + 1, :]
        sel = jnp.where(pos_ref[k:k + 1, :] == p_iota, val, sel)
    return sel.astype(BF16)


def _dispatch_kernel(misc_ref, meta_ref, pos_ref, *refs, first_tiles, n_tiles, n_blocks):
    x_refs = refs[:len(first_tiles)]
    xs_ref, stage_ref, zeros_ref, sems, pad_sem = refs[len(first_tiles):]
    i = pl.program_id(0)
    TT = pos_ref.shape[1]
    slot = i % 2

    @pl.when(i == 0)
    def _():
        _zero_padding(misc_ref, xs_ref, zeros_ref, pad_sem, n_blocks)

    x = x_refs[0][...]
    for t0, ref in zip(first_tiles[1:], x_refs[1:]):
        x = jnp.where(i >= t0, ref[...], x)
    staged = jnp.dot(_selection(pos_ref), x, preferred_element_type=F32)
    stage = stage_ref.at[slot]
    _store_chunked(stage, staged)
    _for_each_run_piece(meta_ref, lambda dst, src, n: pltpu.make_async_copy(
        _rows(stage, src, n), _rows(xs_ref, dst, n), sems.at[slot]).start())

    @pl.when(i > 0)
    def _():
        _wait_rows(TT * TOP_K, stage_ref.at[1 - slot], xs_ref, sems.at[1 - slot])

    @pl.when(i == n_tiles - 1)
    def _():
        _wait_rows(TT * TOP_K, stage, xs_ref, sems.at[slot])


def _dispatch(misc, meta, pos, xns, n_blocks):
    D = xns[0].shape[1]
    tiles = [x.shape[0] // TOKEN_TILE for x in xns]
    first_tiles = tuple(sum(tiles[:g]) for g in range(len(tiles)))

    def x_spec(t0, nt):
        return pl.BlockSpec((TOKEN_TILE, D), lambda i: (jnp.clip(i - t0, 0, nt - 1), 0))

    return pl.pallas_call(
        functools.partial(_dispatch_kernel, first_tiles=first_tiles, n_tiles=sum(tiles), n_blocks=n_blocks),
        grid=(sum(tiles),),
        in_specs=[pl.BlockSpec(memory_space=pltpu.SMEM),
                  pl.BlockSpec((8, LANES), lambda i: (i, 0), memory_space=pltpu.SMEM),
                  pl.BlockSpec((TOP_K, TOKEN_TILE), lambda i: (0, i))]
                 + [x_spec(t0, nt) for t0, nt in zip(first_tiles, tiles)],
        out_specs=pl.BlockSpec(memory_space=pl.ANY),
        out_shape=jax.ShapeDtypeStruct(((n_blocks + 1) * MOE_ROWS * CHUNKS, LANES), F32),
        scratch_shapes=[pltpu.VMEM((2, STAGE_ROWS * CHUNKS, LANES), F32),
                        pltpu.VMEM((PAD_ROWS * CHUNKS, LANES), F32),
                        pltpu.SemaphoreType.DMA((2,)),
                        pltpu.SemaphoreType.DMA(())],
        compiler_params=_cparams(("arbitrary",)),
        name="dispatch",
    )(misc, meta, pos, *xns)


def _expert_kernel(be_ref, nused_ref, xs_ref, wgu_ref, bgu_ref, wd_ref, bd_ref, ys_ref,
                   wgu_bf, wd_bf):
    i = pl.program_id(0)
    D_FF = wd_ref.shape[0]
    new_expert = jnp.logical_or(i == 0, be_ref[i] != be_ref[jnp.maximum(i - 1, 0)])

    @pl.when(jnp.logical_and(i < nused_ref[0], new_expert))
    def _():
        wgu_bf[...] = wgu_ref[...].astype(BF16)
        wd_bf[...] = wd_ref[...].astype(BF16)

    @pl.when(i < nused_ref[0])
    def _():
        x = _load_chunked(xs_ref, MOE_ROWS).astype(BF16)
        hdn = jnp.dot(x, wgu_bf[...], preferred_element_type=F32) + bgu_ref[...]
        glu = jnp.minimum(hdn[:, :D_FF], SWIGLU_LIMIT)
        lin = jnp.clip(hdn[:, D_FF:], -SWIGLU_LIMIT, SWIGLU_LIMIT)
        act = glu * (1.0 / (1.0 + jnp.exp(-SWIGLU_ALPHA * glu))) * (lin + 1.0)
        y = jnp.dot(act.astype(BF16), wd_bf[...], preferred_element_type=F32) + bd_ref[...]
        _store_chunked(ys_ref, y)

    @pl.when(i >= nused_ref[0])
    def _():
        ys_ref[...] = jnp.zeros_like(ys_ref)


def _experts(block_expert, n_used, xs, w_gate_up, b_gate_up, w_down, b_down, n_blocks):
    E, D, F2 = w_gate_up.shape
    D_FF = w_down.shape[1]
    rows = pl.BlockSpec((MOE_ROWS * CHUNKS, LANES), lambda i, be, nu: (i, 0))
    rows_in = pl.BlockSpec((MOE_ROWS * CHUNKS, LANES), lambda i, be, nu: (jnp.minimum(i, nu[0] - 1), 0))
    grid_spec = pltpu.PrefetchScalarGridSpec(
        num_scalar_prefetch=2,
        grid=(n_blocks,),
        in_specs=[rows_in,
                  pl.BlockSpec((None, D, F2), lambda i, be, nu: (be[i], 0, 0)),
                  pl.BlockSpec((None, 1, F2), lambda i, be, nu: (be[i], 0, 0)),
                  pl.BlockSpec((None, D_FF, D), lambda i, be, nu: (be[i], 0, 0)),
                  pl.BlockSpec((None, 1, D), lambda i, be, nu: (be[i], 0, 0))],
        out_specs=rows,
        scratch_shapes=[pltpu.VMEM((D, F2), BF16), pltpu.VMEM((D_FF, D), BF16)],
    )
    return pl.pallas_call(
        _expert_kernel,
        grid_spec=grid_spec,
        out_shape=jax.ShapeDtypeStruct((n_blocks * MOE_ROWS * CHUNKS, LANES), F32),
        compiler_params=_cparams(("arbitrary",)),
        name="experts",
    )(block_expert, n_used, xs, w_gate_up, b_gate_up.reshape(E, 1, F2), w_down, b_down.reshape(E, 1, D))


def _combine_kernel(meta_ref, next_meta_ref, pos_ref, gate_ref, h_ref, g_ref, ys_ref, y_ref, stage_ref, sems,
                    *, n_tiles):
    i = pl.program_id(0)
    slot = i % 2

    def fetch(meta, s):
        _for_each_run_piece(meta, lambda src, dst, n: pltpu.make_async_copy(
            _rows(ys_ref, src, n), _rows(stage_ref.at[s], dst, n), sems.at[s]).start())

    @pl.when(i == 0)
    def _():
        fetch(meta_ref, 0)

    if n_tiles > 1:
        @pl.when(i + 1 < n_tiles)
        def _():
            fetch(next_meta_ref, 1 - slot)

    weights = _selection(pos_ref, gate_ref)
    _wait_rows(STAGE_ROWS, ys_ref, stage_ref.at[slot], sems.at[slot])
    staged = _load_chunked(stage_ref.at[slot], STAGE_ROWS).astype(BF16)
    moe = lax.dot_general(weights, staged, (((0,), (0,)), ((), ())), preferred_element_type=F32)
    hf = h_ref[...] + moe
    ms = jnp.mean(hf * hf, axis=-1, keepdims=True)
    y_ref[...] = (hf * lax.rsqrt(ms + EPS)) * g_ref[...]


def _combine(meta, pos, gates, h, norm_g, ys, tile0):
    n, D = h.shape
    n_tiles = n // TOKEN_TILE
    return pl.pallas_call(
        functools.partial(_combine_kernel, n_tiles=n_tiles),
        grid=(n_tiles,),
        in_specs=[pl.BlockSpec((8, LANES), lambda i: (tile0 + i, 0), memory_space=pltpu.SMEM),
                  pl.BlockSpec((8, LANES), lambda i: (tile0 + jnp.minimum(i + 1, n_tiles - 1), 0),
                               memory_space=pltpu.SMEM),
                  pl.BlockSpec((TOP_K, TOKEN_TILE), lambda i: (0, tile0 + i)),
                  pl.BlockSpec((TOP_K, TOKEN_TILE), lambda i: (0, i)),
                  pl.BlockSpec((TOKEN_TILE, D), lambda i: (i, 0)),
                  pl.BlockSpec((1, D), lambda i: (0, 0)),
                  pl.BlockSpec(memory_space=pl.ANY)],
        out_specs=pl.BlockSpec((TOKEN_TILE, D), lambda i: (i, 0)),
        out_shape=jax.ShapeDtypeStruct((n, D), F32),
        scratch_shapes=[pltpu.VMEM((2, STAGE_ROWS * CHUNKS, LANES), F32), pltpu.SemaphoreType.DMA((2,))],
        compiler_params=_cparams(("arbitrary",)),
        name="combine",
    )(meta, meta, pos, gates, h, norm_g, ys)


def _prep_weights(w_in, w_alpha):
    w_main = w_in[:, :PROJ_MAIN].astype(BF16)
    w_lr = jnp.pad(w_in[:, PROJ_MAIN:], ((0, 0), (0, LANES - GATE_RANK))).astype(BF16)
    w_al = jnp.pad(w_alpha, ((0, LANES - GATE_RANK), (0, 0))).astype(BF16)
    return w_main, w_lr, w_al


def kernel(x_prompt, x_sample, cache_swa_k, cache_swa_v, state_gla, norm_mix_g, w_in, w_alpha, b_alpha, gla_norm_g, w_out, norm_ffn_g, w_router, b_router, w_gate_up, b_gate_up, w_down, b_down, norm_final_g):
    B, S, D = x_prompt.shape
    Bs, Ts, _ = x_sample.shape
    assert w_in.shape[0] == 1, "single-layer trunk"
    l = 0
    R = cache_swa_k.shape[2]
    rows_p = min(DILATIONS[-1] * KEYS_PER_CONFIG, S)
    w_main, w_lr, w_al = _prep_weights(w_in[l], w_alpha[l])
    g_mix = norm_mix_g[l][None]
    b_al = b_alpha[l][None]
    g_gla = gla_norm_g[l][None]

    pos_p = jnp.arange(S, dtype=jnp.int32)
    qa, ka, va, qb, kb, vb, zg, gb, k_tail, v_tail = _project(x_prompt, pos_p, g_mix, w_main, w_lr, w_al, b_al,
                                                              PROJ_TILE, tail_rows=rows_p)
    oa_p = _prompt_attention(qa, ka, va)
    ob_p, st_p = _gla(qb, kb, gb, vb, zg, jnp.zeros((B, N_HEADS_B, DK_B, DV_B), F32), g_gla,
                      GLA_TILE, GLA_CHUNK)
    k_prompt = k_tail.reshape(1, B, rows_p, N_HEADS_A, HEAD_DIM_A)
    v_prompt = v_tail.reshape(1, B, rows_p, N_HEADS_A, HEAD_DIM_A)

    pos_s = PAST_LEN + (jnp.arange(Bs * Ts, dtype=jnp.int32) % Ts)
    proj_s = _project(x_sample.reshape(1, Bs * Ts, D), pos_s, g_mix, w_main, w_lr, w_al, b_al, Bs * Ts)
    qa_s, ka_s, va_s, qb_s, kb_s, vb_s, zg_s, gb_s = [t.reshape(Bs, Ts, -1) for t in proj_s]
    oa_s, k_sample, v_sample = _sample_attention(qa_s, ka_s, va_s,
                                                 cache_swa_k[l].reshape(Bs, R, WIDTH_A),
                                                 cache_swa_v[l].reshape(Bs, R, WIDTH_A))
    ob_s, st_s = _gla(qb_s, kb_s, gb_s, vb_s, zg_s, state_gla[l], g_gla, Ts, Ts)

    w_out_bf = w_out[l].astype(BF16)
    g_ffn = norm_ffn_g[l][None]
    w_router_t = jnp.transpose(w_router[l])
    b_router_c = b_router[l][:, None]
    Np, Ns = B * S, Bs * Ts
    h_p, xn_p, idx_p, gate_p = _merge(oa_p.reshape(Np, WIDTH_A), ob_p.reshape(Np, WIDTH_B),
                                      x_prompt.reshape(Np, D), w_out_bf, g_ffn, w_router_t, b_router_c,
                                      MERGE_TILE)
    h_s, xn_s, idx_s, gate_s = _merge(oa_s.reshape(Ns, WIDTH_A), ob_s.reshape(Ns, WIDTH_B),
                                      x_sample.reshape(Ns, D), w_out_bf, g_ffn, w_router_t, b_router_c,
                                      Ns)

    y_p, y_s = _moe([(xn_p, idx_p, gate_p, h_p), (xn_s, idx_s, gate_s, h_s)],
                    w_gate_up[l], b_gate_up[l], w_down[l], b_down[l], norm_final_g[None])
    return (y_p.reshape(B, S, D), y_s.reshape(Bs, Ts, D), k_prompt, v_prompt, st_p[None],
            k_sample.reshape(1, Bs, R, N_HEADS_A, HEAD_DIM_A),
            v_sample.reshape(1, Bs, R, N_HEADS_A, HEAD_DIM_A), st_s[None])


PAST_LEN = 16384
PROJ_TILE = 1024
MERGE_TILE = 1024
GLA_TILE = 1024
GLA_CHUNK = 64


def _moe(groups, w_gate_up, b_gate_up, w_down, b_down, g_final):
    sizes = [g[3].shape[0] for g in groups]
    N = sum(sizes)
    assert all(n % TOKEN_TILE == 0 for n in sizes)
    n_blocks = -(-(N * TOP_K + N_EXPERTS * (MOE_ROWS - 1)) // MOE_ROWS)
    idx = jnp.concatenate([g[1] for g in groups], axis=1)
    pos, meta, block_expert, misc = _route(idx, n_blocks)
    first_tile = [sum(sizes[:i]) // TOKEN_TILE for i in range(len(sizes))]
    xs = _dispatch(misc, meta, pos, [g[0] for g in groups], n_blocks)
    ys = _experts(block_expert[0], misc[0, :1], xs, w_gate_up, b_gate_up, w_down, b_down, n_blocks)
    return [_combine(meta, pos, gates, h, g_final, ys, t0)
            for (_, _, gates, h), t0 in zip(groups, first_tile)]
```

```python
import functools

import jax
import jax.numpy as jnp
import numpy as np
from jax import lax
from jax.experimental import pallas as pl
from jax.experimental.pallas import tpu as pltpu

F32 = jnp.float32
BF16 = jnp.bfloat16

N_HEADS_A = 8
HEAD_DIM_A = 64
WIDTH_A = N_HEADS_A * HEAD_DIM_A
N_HEADS_B = 4
DK_B = 64
DV_B = 128
QK_B = N_HEADS_B * DK_B
WIDTH_B = N_HEADS_B * DV_B
GATE_RANK = 16
GATE_LOGIT_NORM = 16.0
DILATIONS = (1, 4, 16)
KEYS_PER_CONFIG = 128
ROPE_THETA = 10000.0
N_EXPERTS = 32
TOP_K = 4
SWIGLU_ALPHA = 1.702
SWIGLU_LIMIT = 7.0
EPS = 1e-6

LANES = 128
VMEM_LIMIT = 56 * 1024 * 1024


def _cparams(sem, vmem=VMEM_LIMIT):
    return pltpu.CompilerParams(dimension_semantics=sem, vmem_limit_bytes=vmem)


PROJ_MAIN = 3 * WIDTH_A + 2 * QK_B + 2 * WIDTH_B


def _rope_tables(pos):
    half = HEAD_DIM_A // 2
    inv = ROPE_THETA ** (-jnp.arange(half, dtype=F32) / half)
    ang = pos.astype(F32)[:, None] * inv[None, :]
    cos = jnp.cos(ang)
    sin = jnp.sin(ang)
    cos_t = jnp.concatenate([cos, cos, cos, cos], axis=-1)
    sin_t = jnp.concatenate([-sin, sin, -sin, sin], axis=-1)
    return cos_t, sin_t


def _rope_block(t, cos, sin, first_half):
    partner = jnp.where(first_half, pltpu.roll(t, LANES - 32, 1), pltpu.roll(t, 32, 1))
    return t * cos + partner * sin


def _proj_kernel(x_ref, g_ref, w_ref, wlr_ref, wa_ref, ba_ref, cos_ref, sin_ref,
                 qa_ref, ka_ref, va_ref, qb_ref, kb_ref, vb_ref, zg_ref, gb_ref, *tail_refs, first_tail_tile):
    x = x_ref[...]
    ms = jnp.mean(x * x, axis=-1, keepdims=True)
    xn = ((x * lax.rsqrt(ms + EPS)) * g_ref[...]).astype(BF16)

    def cols(lo, hi):
        return jnp.dot(xn, w_ref[:, lo:hi], preferred_element_type=F32)

    cos = cos_ref[...]
    sin = sin_ref[...]
    lane = lax.broadcasted_iota(jnp.int32, cos.shape, 1)
    first_half = (lane % HEAD_DIM_A) < (HEAD_DIM_A // 2)
    q = cols(0, WIDTH_A)
    k = cols(WIDTH_A, 2 * WIDTH_A)
    for j in range(WIDTH_A // LANES):
        sl = slice(j * LANES, (j + 1) * LANES)
        qa_ref[:, sl] = _rope_block(q[:, sl], cos, sin, first_half) * (HEAD_DIM_A ** -0.5)
        ka_ref[:, sl] = _rope_block(k[:, sl], cos, sin, first_half)
    o = 2 * WIDTH_A
    va_ref[...] = cols(o, o + WIDTH_A)
    o += WIDTH_A
    qb_ref[...] = cols(o, o + QK_B) * (DK_B ** -0.5)
    o += QK_B
    kb_ref[...] = cols(o, o + QK_B)
    o += QK_B
    vb_ref[...] = cols(o, o + WIDTH_B).astype(vb_ref.dtype)
    o += WIDTH_B
    zg_ref[...] = cols(o, o + WIDTH_B)
    lr = jnp.dot(xn, wlr_ref[...], preferred_element_type=F32)
    z = jnp.dot(lr.astype(BF16), wa_ref[...], preferred_element_type=F32) + ba_ref[...]
    logsig = jnp.minimum(z, 0.0) - jnp.log(1.0 + jnp.exp(-jnp.abs(z)))
    gb_ref[...] = logsig / GATE_LOGIT_NORM
    if tail_refs:
        @pl.when(pl.program_id(1) >= first_tail_tile)
        def _():
            tail_refs[0][...] = ka_ref[...]
            tail_refs[1][...] = va_ref[...]


def _project(x, pos, norm_g, w_main, w_lr, w_alpha, b_alpha, tm, tail_rows=0):
    B, T, D = x.shape
    assert tail_rows % tm == 0
    cos_t, sin_t = _rope_tables(pos)
    grid = (B, T // tm)
    first_tail_tile = (T - tail_rows) // tm
    row = lambda w: pl.BlockSpec((None, tm, w), lambda b, i: (b, i, 0))
    tail = pl.BlockSpec((None, tm, WIDTH_A), lambda b, i: (b, jnp.maximum(i - first_tail_tile, 0), 0))
    full = lambda a: pl.BlockSpec(a.shape, lambda b, i: (0,) * a.ndim)
    tab = pl.BlockSpec((tm, LANES), lambda b, i: (i, 0))
    widths = (WIDTH_A, WIDTH_A, WIDTH_A, QK_B, QK_B, WIDTH_B, WIDTH_B, QK_B)
    dtypes = (F32, F32, F32, F32, F32, BF16, F32, F32)
    n_tail = 2 if tail_rows else 0
    return pl.pallas_call(
        functools.partial(_proj_kernel, first_tail_tile=first_tail_tile),
        grid=grid,
        in_specs=[row(D), full(norm_g), full(w_main), full(w_lr), full(w_alpha), full(b_alpha), tab, tab],
        out_specs=[row(w) for w in widths] + [tail] * n_tail,
        out_shape=[jax.ShapeDtypeStruct((B, T, w), dt) for w, dt in zip(widths, dtypes)]
                  + [jax.ShapeDtypeStruct((B, tail_rows, WIDTH_A), F32)] * n_tail,
        compiler_params=_cparams(("parallel", "arbitrary")),
        name="proj",
    )(x, norm_g, w_main, w_lr, w_alpha, b_alpha, cos_t, sin_t)


Q_BLOCK = 128
NEG_INF = float("-inf")


def _attn_block(q, kb, v1, mask, head0, state):
    QB = q.shape[0]
    q2 = jnp.concatenate([jnp.where(head0, q, 0.0), jnp.where(head0, 0.0, q)], axis=0).astype(BF16)
    s = lax.dot_general(q2, kb, (((1,), (1,)), ((), ())), preferred_element_type=F32)
    s = jnp.where(mask, s, NEG_INF)
    if state is None:
        m2 = jnp.max(s, axis=1, keepdims=True)
    else:
        prev = jnp.concatenate([jnp.where(head0, state[0], NEG_INF),
                                jnp.where(head0, NEG_INF, state[0])], axis=0)
        m2 = jnp.max(jnp.concatenate([s, prev], axis=1), axis=1, keepdims=True)
    p = jnp.exp(s - m2)
    pv = jnp.dot(p.astype(BF16), v1, preferred_element_type=F32)
    m_full = jnp.where(head0, m2[:QB], m2[QB:])
    l_full = jnp.where(head0, pv[:QB, LANES:], pv[QB:, LANES:])
    pv_full = jnp.where(head0, pv[:QB, :LANES], pv[QB:, :LANES])
    if state is None:
        return m_full, l_full, pv_full
    a = jnp.exp(state[0] - m_full)
    return m_full, a * state[1] + l_full, a * state[2] + pv_full


def _attn_kernel(q_ref, k_ref, v_ref, o_ref, m_ref, l_ref, acc_ref):
    S = q_ref.shape[0]
    QB = Q_BLOCK
    lane = lax.broadcasted_iota(jnp.int32, (QB, LANES), 1)
    head0 = lane < HEAD_DIM_A
    qq = lax.broadcasted_iota(jnp.int32, (2 * QB, 2 * QB), 0) % QB
    kk = lax.broadcasted_iota(jnp.int32, (2 * QB, 2 * QB), 1)
    band = jnp.logical_and(kk >= qq, kk - qq <= KEYS_PER_CONFIG)
    causal = (lax.broadcasted_iota(jnp.int32, (2 * QB, QB), 1)
              <= lax.broadcasted_iota(jnp.int32, (2 * QB, QB), 0) % QB)

    GROUP = 4

    for ci, dil in enumerate(sorted(DILATIONS, reverse=True)):
        nblk = S // (dil * QB)
        assert nblk % GROUP == 0 or GROUP % nblk == 0

        def rows(start, n, dil=dil):
            return pl.ds(start, n) if dil == 1 else pl.ds(start, n, stride=dil)

        def do_group(blocks, ci=ci, dil=dil, rows=rows):
            chunks = {}

            def kv_chunk(r, rkey, jbase, c):
                if (rkey, c) not in chunks:
                    ks = rows(r + dil * QB * (jbase + c), QB)
                    v = v_ref[ks, :].astype(BF16)
                    chunks[(rkey, c)] = (k_ref[ks, :].astype(BF16),
                                         jnp.concatenate([v, jnp.ones(v.shape, BF16)], axis=1))
                return chunks[(rkey, c)]

            loaded = []
            for r, rkey, jbase, joff in blocks:
                first = isinstance(jbase, int) and jbase + joff == 0
                qs = rows(r + dil * QB * (jbase + joff), QB)
                parts = [kv_chunk(r, rkey, jbase, joff)]
                if not first:
                    parts.insert(0, kv_chunk(r, rkey, jbase, joff - 1))
                kb = jnp.concatenate([p[0] for p in parts], axis=0)
                v1 = jnp.concatenate([p[1] for p in parts], axis=0)
                state = None if ci == 0 else (m_ref[qs, :], l_ref[qs, :], acc_ref[qs, :])
                loaded.append((qs, q_ref[qs, :], kb, v1, causal if first else band, state))
            results = [_attn_block(q, kb, v1, mask, head0, state) for _, q, kb, v1, mask, state in loaded]
            for (qs, *_), (m, l, acc) in zip(loaded, results):
                m_ref[qs, :] = m
                l_ref[qs, :] = l
                acc_ref[qs, :] = acc

        if nblk >= GROUP:
            def residue(r, carry, nblk=nblk, do_group=do_group):
                do_group([(r, 0, 0, j) for j in range(GROUP)])

                def rest(g, c):
                    do_group([(r, 0, GROUP * g, u) for u in range(GROUP)])
                    return c

                return lax.fori_loop(1, nblk // GROUP, rest, carry)

            lax.fori_loop(0, dil, residue, 0)
        else:
            per = 2 * GROUP // nblk

            def residues(g, carry, nblk=nblk, per=per, do_group=do_group):
                do_group([(g * per + i, i, 0, j) for j in range(nblk) for i in range(per)])
                return carry

            lax.fori_loop(0, dil // per, residues, 0)

    def finish(i, c):
        rs = pl.ds(pl.multiple_of(i * QB, QB), QB)
        o_ref[rs, :] = (acc_ref[rs, :] / l_ref[rs, :]).astype(o_ref.dtype)
        return c

    lax.fori_loop(0, S // QB, finish, 0, unroll=4)


def _prompt_attention(qa, ka, va):
    B, S, W = qa.shape
    spec = pl.BlockSpec((None, S, LANES), lambda b, hp: (b, 0, hp))
    return pl.pallas_call(
        _attn_kernel,
        grid=(B, W // LANES),
        in_specs=[spec, spec, spec],
        out_specs=spec,
        out_shape=jax.ShapeDtypeStruct((B, S, W), BF16),
        scratch_shapes=[pltpu.VMEM((S, LANES), F32)] * 3,
        compiler_params=_cparams(("parallel", "parallel")),
        name="prompt_attn",
    )(qa, ka, va)


def _gla_kernel(q_ref, k_ref, g_ref, v_ref, z_ref, s0_ref, ng_ref, o_ref, sfin_ref, st_ref, *, chunk):
    C = chunk
    TS = q_ref.shape[0]
    n_pairs = N_HEADS_B // 2
    PW = 2 * DV_B
    t_idx = pl.program_id(1)

    lane_k = lax.broadcasted_iota(jnp.int32, (C, LANES), 1)
    head0 = lane_k < DK_B
    row_k = lax.broadcasted_iota(jnp.int32, (C, LANES), 0)
    tri2 = (lax.broadcasted_iota(jnp.int32, (2 * C, C), 1)
            <= lax.broadcasted_iota(jnp.int32, (2 * C, C), 0) % C)
    value_head0 = lax.broadcasted_iota(jnp.int32, (C, PW), 1) < DV_B
    bd_mask = ((lax.broadcasted_iota(jnp.int32, (PW, LANES), 0) // DV_B)
               == (lax.broadcasted_iota(jnp.int32, (PW, LANES), 1) // DK_B))

    def prefix_rows(x):
        shift = 1
        while shift < C:
            x = x + jnp.where(row_k >= shift, pltpu.roll(x, shift, 0), 0.0)
            shift *= 2
        return x

    @pl.when(t_idx == 0)
    def _():
        for p in range(n_pairs):
            for h in range(2):
                blk = jnp.transpose(s0_ref[2 * p + h])
                pad = jnp.zeros((DV_B, DK_B), F32)
                row = jnp.concatenate([blk, pad] if h == 0 else [pad, blk], axis=1)
                st_ref[p, h * DV_B:(h + 1) * DV_B, :] = row

    def chunk_body(c, carry):
        rs = pl.ds(pl.multiple_of(c * C, C), C)
        for p in range(n_pairs):
            kl = slice(p * LANES, (p + 1) * LANES)
            vl = slice(p * PW, (p + 1) * PW)
            q = q_ref[rs, kl]
            k = k_ref[rs, kl]
            g = g_ref[rs, kl]
            v = v_ref[rs, vl]
            b = prefix_rows(g)
            b_last = b[C - 1:C, :]
            b_mid = b[C // 2 - 1:C // 2, :] if C > 1 else b_last
            qe = q * jnp.exp(b - b_mid)
            ke = (k * jnp.exp(b_mid - b)).astype(BF16)
            st = st_ref[p]
            q_in = (q * jnp.exp(b)).astype(BF16)
            nt = (((1,), (1,)), ((), ()))
            o = lax.dot_general(q_in, st.astype(BF16), nt, preferred_element_type=F32)
            q2 = jnp.concatenate([jnp.where(head0, qe, 0.0), jnp.where(head0, 0.0, qe)], axis=0)
            a2 = lax.dot_general(q2.astype(BF16), ke, nt, preferred_element_type=F32)
            a2 = jnp.where(tri2, a2, 0.0).astype(BF16)
            av = jnp.dot(a2, v, preferred_element_type=F32)
            o = o + jnp.where(value_head0, av[:C], av[C:])
            k_dec = (k * jnp.exp(b_last - b)).astype(BF16)
            upd = lax.dot_general(v, k_dec, (((0,), (0,)), ((), ())), preferred_element_type=F32)
            st_ref[p] = jnp.exp(b_last) * st + jnp.where(bd_mask, upd, 0.0)
            for h in range(2):
                oh = o[:, h * DV_B:(h + 1) * DV_B]
                hl = slice((2 * p + h) * DV_B, (2 * p + h + 1) * DV_B)
                ms = jnp.mean(oh * oh, axis=-1, keepdims=True)
                z = z_ref[rs, hl]
                gated = (oh * lax.rsqrt(ms + EPS)) * ng_ref[:, hl] * (z / (1.0 + jnp.exp(-z)))
                o_ref[rs, hl] = gated.astype(o_ref.dtype)
        return carry

    lax.fori_loop(0, TS // C, chunk_body, 0, unroll=2 if (TS // C) % 2 == 0 else 1)

    @pl.when(t_idx == pl.num_programs(1) - 1)
    def _():
        for p in range(n_pairs):
            for h in range(2):
                blk = st_ref[p, h * DV_B:(h + 1) * DV_B, h * DK_B:(h + 1) * DK_B]
                sfin_ref[2 * p + h] = jnp.transpose(blk)


def _gla(qb, kb, gb, vb, zg, state0, norm_g, ts, chunk):
    B, T, _ = qb.shape
    row = lambda w: pl.BlockSpec((None, ts, w), lambda b, i: (b, i, 0))
    st_spec = pl.BlockSpec((None, N_HEADS_B, DK_B, DV_B), lambda b, i: (b, 0, 0, 0))
    return pl.pallas_call(
        functools.partial(_gla_kernel, chunk=chunk),
        grid=(B, T // ts),
        in_specs=[row(QK_B), row(QK_B), row(QK_B), row(WIDTH_B), row(WIDTH_B), st_spec,
                  pl.BlockSpec((1, WIDTH_B), lambda b, i: (0, 0))],
        out_specs=[row(WIDTH_B), st_spec],
        out_shape=[jax.ShapeDtypeStruct((B, T, WIDTH_B), BF16),
                   jax.ShapeDtypeStruct((B, N_HEADS_B, DK_B, DV_B), F32)],
        scratch_shapes=[pltpu.VMEM((N_HEADS_B // 2, 2 * DV_B, LANES), F32)],
        compiler_params=_cparams(("parallel", "arbitrary")),
        name="gla",
    )(qb, kb, gb, vb, zg, state0, norm_g)


def _sample_attn_kernel(q_ref, kn_ref, vn_ref, kc_ref, vc_ref, o_ref, ko_ref, vo_ref):
    T = q_ref.shape[0]
    R = kc_ref.shape[0]
    W = q_ref.shape[1]
    HT = N_HEADS_A * T
    q = q_ref[...]
    qx = jnp.concatenate([q] * N_HEADS_A, axis=0)
    own = ((lax.broadcasted_iota(jnp.int32, (HT, W), 0) // T)
           == (lax.broadcasted_iota(jnp.int32, (HT, W), 1) // HEAD_DIM_A))
    qx = jnp.where(own, qx, 0.0).astype(BF16)
    pad = jnp.zeros((LANES - T, W), F32)
    kn = jnp.concatenate([kn_ref[...], pad], axis=0).astype(BF16)
    vn = jnp.concatenate([vn_ref[...], pad], axis=0).astype(BF16)
    nt = (((1,), (1,)), ((), ()))
    s_c = lax.dot_general(qx, kc_ref[...].astype(BF16), nt, preferred_element_type=F32)
    s_n = lax.dot_general(qx, kn, nt, preferred_element_type=F32)

    def multiplicity(n_cols, first_row):
        t = lax.broadcasted_iota(jnp.int32, (HT, n_cols), 0) % T
        j = lax.broadcasted_iota(jnp.int32, (HT, n_cols), 1) + first_row
        delta = R + t - j
        cnt = jnp.zeros((HT, n_cols), F32)
        for dil in DILATIONS:
            hit = (delta >= 0) & (delta <= dil * KEYS_PER_CONFIG) & (delta % dil == 0)
            cnt = cnt + jnp.where(hit, 1.0, 0.0)
        return cnt

    cnt_c = multiplicity(R, 0)
    cnt_n = multiplicity(LANES, R)
    s_c = jnp.where(cnt_c > 0.0, s_c, NEG_INF)
    s_n = jnp.where(cnt_n > 0.0, s_n, NEG_INF)
    m = jnp.maximum(jnp.max(s_c, axis=1, keepdims=True), jnp.max(s_n, axis=1, keepdims=True))
    p_c = cnt_c * jnp.exp(s_c - m)
    p_n = cnt_n * jnp.exp(s_n - m)
    den = jnp.sum(p_c, axis=1, keepdims=True) + jnp.sum(p_n, axis=1, keepdims=True)
    full = (jnp.dot(p_c.astype(BF16), vc_ref[...].astype(BF16), preferred_element_type=F32)
            + jnp.dot(p_n.astype(BF16), vn, preferred_element_type=F32)) / den
    full = jnp.where(own, full, 0.0)
    out = full[0:T, :]
    for h in range(1, N_HEADS_A):
        out = out + full[h * T:(h + 1) * T, :]
    o_ref[...] = out.astype(o_ref.dtype)
    ko_ref[0:R - T, :] = kc_ref[T:R, :]
    ko_ref[R - T:R, :] = kn_ref[...]
    vo_ref[0:R - T, :] = vc_ref[T:R, :]
    vo_ref[R - T:R, :] = vn_ref[...]


def _sample_attention(qa, ka, va, cache_k, cache_v):
    B, T, W = qa.shape
    R = cache_k.shape[1]
    assert R >= DILATIONS[-1] * KEYS_PER_CONFIG and T % 8 == 0 and T <= LANES
    new = pl.BlockSpec((None, T, W), lambda b: (b, 0, 0))
    cache = pl.BlockSpec((None, R, W), lambda b: (b, 0, 0))
    return pl.pallas_call(
        _sample_attn_kernel,
        grid=(B,),
        in_specs=[new, new, new, cache, cache],
        out_specs=[new, cache, cache],
        out_shape=[jax.ShapeDtypeStruct((B, T, W), BF16),
                   jax.ShapeDtypeStruct((B, R, W), F32),
                   jax.ShapeDtypeStruct((B, R, W), F32)],
        compiler_params=_cparams(("parallel",)),
        name="sample_attn",
    )(qa, ka, va, cache_k, cache_v)


CHUNKS = 8


def _store_chunked(ref, val):
    n = val.shape[0]
    for s in range(CHUNKS):
        ref[pl.ds(s, n, stride=CHUNKS), :] = val[:, s * LANES:(s + 1) * LANES]


def _load_chunked(ref, n):
    return jnp.concatenate([ref[pl.ds(s, n, stride=CHUNKS), :] for s in range(CHUNKS)], axis=1)


def _split2(x):
    hi = x.astype(BF16)
    return hi, (x - hi.astype(F32)).astype(BF16)


def _merge_kernel(oa_ref, ob_ref, x_ref, wo_ref, g_ref, wr_ref, br_ref,
                  h_ref, xn_ref, idx_ref, gate_ref):
    TM = x_ref.shape[0]
    mixed = (jnp.dot(oa_ref[...], wo_ref[0:WIDTH_A, :], preferred_element_type=F32)
             + jnp.dot(ob_ref[...], wo_ref[WIDTH_A:, :], preferred_element_type=F32))
    h = x_ref[...] + mixed
    h_ref[...] = h
    ms = jnp.mean(h * h, axis=-1, keepdims=True)
    xn = (h * lax.rsqrt(ms + EPS)) * g_ref[...]
    xn_ref[...] = xn.astype(xn_ref.dtype)
    nt = (((1,), (1,)), ((), ()))
    xh, xl = _split2(xn)
    wh, wl = _split2(wr_ref[...])
    logits = (lax.dot_general(wh, xh, nt, preferred_element_type=F32)
              + lax.dot_general(wh, xl, nt, preferred_element_type=F32)
              + lax.dot_general(wl, xh, nt, preferred_element_type=F32)) + br_ref[...]
    e_iota = lax.broadcasted_iota(jnp.int32, (N_EXPERTS, TM), 0)
    vals, idxs = [], []
    for _ in range(TOP_K):
        m = jnp.max(logits, axis=0, keepdims=True)
        sel = jnp.min(jnp.where(logits == m, e_iota, N_EXPERTS), axis=0, keepdims=True)
        vals.append(m)
        idxs.append(sel)
        logits = jnp.where(e_iota == sel, NEG_INF, logits)
    ex = [jnp.exp(v - vals[0]) for v in vals]
    den = ex[0] + ex[1] + ex[2] + ex[3]
    idx_ref[...] = jnp.concatenate(idxs, axis=0)
    gate_ref[...] = jnp.concatenate([e / den for e in ex], axis=0)


def _merge(oa, ob, x, w_out, norm_g, w_router_t, b_router, tm):
    N, D = x.shape
    full = lambda a: pl.BlockSpec(a.shape, lambda i: (0,) * a.ndim)
    row = lambda w: pl.BlockSpec((tm, w), lambda i: (i, 0))
    col = pl.BlockSpec((TOP_K, tm), lambda i: (0, i))
    return pl.pallas_call(
        _merge_kernel,
        grid=(N // tm,),
        in_specs=[row(WIDTH_A), row(WIDTH_B), row(D), full(w_out), full(norm_g), full(w_router_t),
                  full(b_router)],
        out_specs=[row(D), row(D), col, col],
        out_shape=[jax.ShapeDtypeStruct((N, D), F32),
                   jax.ShapeDtypeStruct((N, D), BF16),
                   jax.ShapeDtypeStruct((TOP_K, N), jnp.int32),
                   jax.ShapeDtypeStruct((TOP_K, N), F32)],
        compiler_params=_cparams(("parallel",)),
        name="merge_router",
    )(oa, ob, x, w_out, norm_g, w_router_t, b_router)


MOE_ROWS = 512
TOKEN_TILE = 256
DMA_ROWS = 8
STAGE_ROWS = TOKEN_TILE * TOP_K


def _expert_row(col):
    r = lax.broadcasted_iota(jnp.int32, (N_EXPERTS, LANES), 0)
    c = lax.broadcasted_iota(jnp.int32, (N_EXPERTS, LANES), 1)
    return jnp.sum(jnp.where(r == c, col, 0.0), axis=0, keepdims=True)


def _expert_prefix(col):
    r = lax.broadcasted_iota(jnp.int32, (N_EXPERTS, LANES), 0)
    c = lax.broadcasted_iota(jnp.int32, (N_EXPERTS, LANES), 1)
    return jnp.sum(jnp.where(c < r, _expert_row(col), 0.0), axis=1, keepdims=True)


PIECE_SIZES = (DMA_ROWS, 4, 2, 1)
META_ROWS = 16


def _piece_list(count, src, dst, stride):
    first = _expert_prefix(count)
    f = lax.broadcasted_iota(jnp.int32, (N_EXPERTS, LANES), 1).astype(F32)
    owner = jnp.sum(jnp.where(first + count <= f, 1.0, 0.0), axis=0, keepdims=True)
    hit = lax.broadcasted_iota(jnp.int32, (N_EXPERTS, LANES), 0).astype(F32) == owner
    pick = lambda col: jnp.sum(jnp.where(hit, col, 0.0), axis=0, keepdims=True)
    j = f[0:1, :] - pick(first)
    return pick(src) + stride * j, pick(dst) + stride * j


def _route_kernel(idx_all_ref, idx_ref, pos_ref, meta_ref, be_ref, misc_ref, carry_ref, start_ref):
    i = pl.program_id(0)
    TT = idx_ref.shape[1]
    NBP = be_ref.shape[1]

    @pl.when(i == 0)
    def _():
        idx_all = idx_all_ref[...]
        e_all = lax.broadcasted_iota(jnp.int32, (N_EXPERTS, idx_all.shape[1]), 0)
        tot = jnp.zeros((N_EXPERTS, 1), F32)
        for k in range(TOP_K):
            tot = tot + jnp.sum(jnp.where(idx_all[k:k + 1, :] == e_all, 1.0, 0.0), axis=1, keepdims=True)
        padded = jnp.floor((tot + (MOE_ROWS - 1)) / MOE_ROWS) * MOE_ROWS
        start = _expert_prefix(padded)
        start_ref[...] = start
        carry_ref[...] = jnp.zeros_like(carry_ref)
        end = start + padded
        block_start = lax.broadcasted_iota(jnp.int32, (N_EXPERTS, NBP), 1).astype(F32) * MOE_ROWS
        be = jnp.sum(jnp.where(end <= block_start, 1.0, 0.0), axis=0, keepdims=True)
        be_ref[...] = jnp.minimum(be, N_EXPERTS - 1).astype(jnp.int32)
        n_used = jnp.broadcast_to(jnp.sum(padded, axis=0, keepdims=True) / MOE_ROWS, (1, LANES))
        zero = jnp.zeros((1, LANES), F32)
        misc_ref[...] = jnp.concatenate([n_used, _expert_row(start + tot)] + [zero] * 6,
                                        axis=0).astype(jnp.int32)

    idx = idx_ref[...]
    e_iota = lax.broadcasted_iota(jnp.int32, (N_EXPERTS, TT), 0)
    onehot = [idx[k:k + 1, :] == e_iota for k in range(TOP_K)]
    cnt = jnp.zeros((N_EXPERTS, TT), F32)
    for oh in onehot:
        cnt = cnt + jnp.where(oh, 1.0, 0.0)
    tile_tot = jnp.sum(cnt, axis=1, keepdims=True)
    earlier = (lax.broadcasted_iota(jnp.int32, (TT, TT), 0)
               < lax.broadcasted_iota(jnp.int32, (TT, TT), 1))
    before = jnp.dot(cnt.astype(BF16), jnp.where(earlier, 1.0, 0.0).astype(BF16),
                     preferred_element_type=F32)
    seg = _expert_prefix(tile_tot)
    where_staged = seg + before
    rows = [jnp.sum(jnp.where(oh, where_staged, 0.0), axis=0, keepdims=True) for oh in onehot]
    pos_ref[...] = jnp.concatenate(rows, axis=0).astype(jnp.int32)
    slot = start_ref[...] + carry_ref[...]
    whole = jnp.floor(tile_tot / DMA_ROWS)
    covered = whole * DMA_ROWS
    lists = list(_piece_list(whole, seg, slot, float(DMA_ROWS)))
    counts = [jnp.sum(whole, axis=0, keepdims=True)]
    for size in PIECE_SIZES[1:]:
        has = jnp.floor((tile_tot - covered) / size)
        lists += _piece_list(has, seg + covered, slot + covered, 0.0)
        counts.append(jnp.sum(has, axis=0, keepdims=True))
        covered = covered + has * size
    lane = lax.broadcasted_iota(jnp.int32, (1, LANES), 1)
    count_row = jnp.zeros((1, LANES), F32)
    for k, c in enumerate(counts):
        count_row = jnp.where(lane == k, c, count_row)
    zero = jnp.zeros((1, LANES), F32)
    meta_ref[...] = jnp.concatenate(lists + [count_row] + [zero] * (META_ROWS - len(lists) - 1),
                                    axis=0).astype(jnp.int32)
    carry_ref[...] += tile_tot


def _route(idx, n_blocks):
    _, N = idx.shape
    nbp = -(-n_blocks // LANES) * LANES
    tile = pl.BlockSpec((TOP_K, TOKEN_TILE), lambda i: (0, i))
    return pl.pallas_call(
        _route_kernel,
        grid=(N // TOKEN_TILE,),
        in_specs=[pl.BlockSpec((TOP_K, N), lambda i: (0, 0)), tile],
        out_specs=[tile,
                   pl.BlockSpec((META_ROWS, LANES), lambda i: (i, 0)),
                   pl.BlockSpec((1, nbp), lambda i: (0, 0)),
                   pl.BlockSpec((8, LANES), lambda i: (0, 0))],
        out_shape=[jax.ShapeDtypeStruct((TOP_K, N), jnp.int32),
                   jax.ShapeDtypeStruct((N // TOKEN_TILE * META_ROWS, LANES), jnp.int32),
                   jax.ShapeDtypeStruct((1, nbp), jnp.int32),
                   jax.ShapeDtypeStruct((8, LANES), jnp.int32)],
        scratch_shapes=[pltpu.VMEM((N_EXPERTS, 1), F32)] * 2,
        compiler_params=_cparams(("arbitrary",)),
        name="route",
    )(idx, idx)


def _rows(ref, first_row, n_rows):
    return ref.at[pl.ds(pl.multiple_of(first_row * CHUNKS, CHUNKS), n_rows * CHUNKS), :]


def _for_each_run_piece(meta_ref, fn):
    for k, size in enumerate(PIECE_SIZES):
        def body(j, c, k=k, size=size):
            fn(meta_ref[2 * k + 1, j], meta_ref[2 * k, j], size)
            return c

        lax.fori_loop(0, meta_ref[2 * len(PIECE_SIZES), k], body, 0)


def _wait_rows(n_rows, src_ref, dst_ref, sem):
    @pl.when(n_rows > 0)
    def _():
        n = n_rows * CHUNKS
        pltpu.make_async_copy(src_ref.at[pl.ds(0, n), :], dst_ref.at[pl.ds(0, n), :], sem).wait()


PAD_ROWS = MOE_ROWS


def _zero_padding(misc_ref, xs_ref, zeros_ref, sem, n_blocks):
    zeros_ref[...] = jnp.zeros_like(zeros_ref)

    def pad_copy(e):
        first = pl.multiple_of(misc_ref[1, e] * CHUNKS, CHUNKS)
        return pltpu.make_async_copy(zeros_ref, xs_ref.at[pl.ds(first, PAD_ROWS * CHUNKS), :], sem)

    def tail_copy(b):
        first = pl.multiple_of(b * (MOE_ROWS * CHUNKS), MOE_ROWS * CHUNKS)
        return pltpu.make_async_copy(zeros_ref.at[pl.ds(0, MOE_ROWS * CHUNKS), :],
                                     xs_ref.at[pl.ds(first, MOE_ROWS * CHUNKS), :], sem)

    def pad(e, c):
        pad_copy(e).start()
        pad_copy(e).wait()
        return c

    def tail(start_not_wait):
        def body(b, c):
            tail_copy(b).start() if start_not_wait else tail_copy(b).wait()
            return c

        lax.fori_loop(misc_ref[0, 0], n_blocks + 1, body, 0)

    lax.fori_loop(0, N_EXPERTS, pad, 0)
    tail(True)
    tail(False)


def _selection(pos_ref, fill_ref=None):
    TT = pos_ref.shape[1]
    p_iota = lax.broadcasted_iota(jnp.int32, (STAGE_ROWS, TT), 0)
    sel = jnp.zeros((STAGE_ROWS, TT), F32)
    for k in range(TOP_K):
        val = 1.0 if fill_ref is None else fill_ref[k:k + 1, :]
        sel = jnp.where(pos_ref[k:k + 1, :] == p_iota, val, sel)
    return sel.astype(BF16)


def _dispatch_kernel(misc_ref, meta_ref, pos_ref, *refs, first_tiles, n_tiles, n_blocks):
    x_refs = refs[:len(first_tiles)]
    xs_ref, stage_ref, zeros_ref, sems, pad_sem = refs[len(first_tiles):]
    i = pl.program_id(0)
    TT = pos_ref.shape[1]
    slot = i % 2

    @pl.when(i == 0)
    def _():
        _zero_padding(misc_ref, xs_ref, zeros_ref, pad_sem, n_blocks)

    x = x_refs[0][...]
    for t0, ref in zip(first_tiles[1:], x_refs[1:]):
        x = jnp.where(i >= t0, ref[...], x)
    staged = jnp.dot(_selection(pos_ref), x, preferred_element_type=F32)
    stage = stage_ref.at[slot]
    _store_chunked(stage, staged)
    _for_each_run_piece(meta_ref, lambda dst, src, n: pltpu.make_async_copy(
        _rows(stage, src, n), _rows(xs_ref, dst, n), sems.at[slot]).start())

    @pl.when(i > 0)
    def _():
        _wait_rows(TT * TOP_K, stage_ref.at[1 - slot], xs_ref, sems.at[1 - slot])

    @pl.when(i == n_tiles - 1)
    def _():
        _wait_rows(TT * TOP_K, stage, xs_ref, sems.at[slot])


def _dispatch(misc, meta, pos, xns, n_blocks):
    D = xns[0].shape[1]
    tiles = [x.shape[0] // TOKEN_TILE for x in xns]
    first_tiles = tuple(sum(tiles[:g]) for g in range(len(tiles)))

    def x_spec(t0, nt):
        return pl.BlockSpec((TOKEN_TILE, D), lambda i: (jnp.clip(i - t0, 0, nt - 1), 0))

    return pl.pallas_call(
        functools.partial(_dispatch_kernel, first_tiles=first_tiles, n_tiles=sum(tiles), n_blocks=n_blocks),
        grid=(sum(tiles),),
        in_specs=[pl.BlockSpec(memory_space=pltpu.SMEM),
                  pl.BlockSpec((META_ROWS, LANES), lambda i: (i, 0), memory_space=pltpu.SMEM),
                  pl.BlockSpec((TOP_K, TOKEN_TILE), lambda i: (0, i))]
                 + [x_spec(t0, nt) for t0, nt in zip(first_tiles, tiles)],
        out_specs=pl.BlockSpec(memory_space=pl.ANY),
        out_shape=jax.ShapeDtypeStruct(((n_blocks + 1) * MOE_ROWS * CHUNKS, LANES), F32),
        scratch_shapes=[pltpu.VMEM((2, STAGE_ROWS * CHUNKS, LANES), F32),
                        pltpu.VMEM((PAD_ROWS * CHUNKS, LANES), F32),
                        pltpu.SemaphoreType.DMA((2,)),
                        pltpu.SemaphoreType.DMA(())],
        compiler_params=_cparams(("arbitrary",)),
        name="dispatch",
    )(misc, meta, pos, *xns)


def _expert_kernel(be_ref, nused_ref, xs_ref, wgu_ref, bgu_ref, wd_ref, bd_ref, ys_ref,
                   wgu_bf, wd_bf):
    i = pl.program_id(0)
    D_FF = wd_ref.shape[0]
    new_expert = jnp.logical_or(i == 0, be_ref[i] != be_ref[jnp.maximum(i - 1, 0)])

    @pl.when(jnp.logical_and(i < nused_ref[0], new_expert))
    def _():
        wgu_bf[...] = wgu_ref[...].astype(BF16)
        wd_bf[...] = wd_ref[...].astype(BF16)

    @pl.when(i < nused_ref[0])
    def _():
        x = _load_chunked(xs_ref, MOE_ROWS).astype(BF16)
        hdn = jnp.dot(x, wgu_bf[...], preferred_element_type=F32) + bgu_ref[...]
        glu = jnp.minimum(hdn[:, :D_FF], SWIGLU_LIMIT)
        lin = jnp.clip(hdn[:, D_FF:], -SWIGLU_LIMIT, SWIGLU_LIMIT)
        act = glu * (1.0 / (1.0 + jnp.exp(-SWIGLU_ALPHA * glu))) * (lin + 1.0)
        y = jnp.dot(act.astype(BF16), wd_bf[...], preferred_element_type=F32) + bd_ref[...]
        _store_chunked(ys_ref, y)

    @pl.when(i >= nused_ref[0])
    def _():
        ys_ref[...] = jnp.zeros_like(ys_ref)


def _experts(block_expert, n_used, xs, w_gate_up, b_gate_up, w_down, b_down, n_blocks):
    E, D, F2 = w_gate_up.shape
    D_FF = w_down.shape[1]
    rows = pl.BlockSpec((MOE_ROWS * CHUNKS, LANES), lambda i, be, nu: (i, 0))
    rows_in = pl.BlockSpec((MOE_ROWS * CHUNKS, LANES), lambda i, be, nu: (jnp.minimum(i, nu[0] - 1), 0))
    grid_spec = pltpu.PrefetchScalarGridSpec(
        num_scalar_prefetch=2,
        grid=(n_blocks,),
        in_specs=[rows_in,
                  pl.BlockSpec((None, D, F2), lambda i, be, nu: (be[i], 0, 0)),
                  pl.BlockSpec((None, 1, F2), lambda i, be, nu: (be[i], 0, 0)),
                  pl.BlockSpec((None, D_FF, D), lambda i, be, nu: (be[i], 0, 0)),
                  pl.BlockSpec((None, 1, D), lambda i, be, nu: (be[i], 0, 0))],
        out_specs=rows,
        scratch_shapes=[pltpu.VMEM((D, F2), BF16), pltpu.VMEM((D_FF, D), BF16)],
    )
    return pl.pallas_call(
        _expert_kernel,
        grid_spec=grid_spec,
        out_shape=jax.ShapeDtypeStruct((n_blocks * MOE_ROWS * CHUNKS, LANES), F32),
        compiler_params=_cparams(("arbitrary",)),
        name="experts",
    )(block_expert, n_used, xs, w_gate_up, b_gate_up.reshape(E, 1, F2), w_down, b_down.reshape(E, 1, D))


def _combine_kernel(meta_ref, next_meta_ref, pos_ref, gate_ref, h_ref, g_ref, ys_ref, y_ref, stage_ref, sems,
                    *, n_tiles):
    i = pl.program_id(0)
    slot = i % 2

    def fetch(meta, s):
        _for_each_run_piece(meta, lambda src, dst, n: pltpu.make_async_copy(
            _rows(ys_ref, src, n), _rows(stage_ref.at[s], dst, n), sems.at[s]).start())

    @pl.when(i == 0)
    def _():
        fetch(meta_ref, 0)

    if n_tiles > 1:
        @pl.when(i + 1 < n_tiles)
        def _():
            fetch(next_meta_ref, 1 - slot)

    weights = _selection(pos_ref, gate_ref)
    _wait_rows(STAGE_ROWS, ys_ref, stage_ref.at[slot], sems.at[slot])
    staged = _load_chunked(stage_ref.at[slot], STAGE_ROWS).astype(BF16)
    moe = lax.dot_general(weights, staged, (((0,), (0,)), ((), ())), preferred_element_type=F32)
    hf = h_ref[...] + moe
    ms = jnp.mean(hf * hf, axis=-1, keepdims=True)
    y_ref[...] = (hf * lax.rsqrt(ms + EPS)) * g_ref[...]


def _combine(meta, pos, gates, h, norm_g, ys, tile0):
    n, D = h.shape
    n_tiles = n // TOKEN_TILE
    return pl.pallas_call(
        functools.partial(_combine_kernel, n_tiles=n_tiles),
        grid=(n_tiles,),
        in_specs=[pl.BlockSpec((META_ROWS, LANES), lambda i: (tile0 + i, 0), memory_space=pltpu.SMEM),
                  pl.BlockSpec((META_ROWS, LANES), lambda i: (tile0 + jnp.minimum(i + 1, n_tiles - 1), 0),
                               memory_space=pltpu.SMEM),
                  pl.BlockSpec((TOP_K, TOKEN_TILE), lambda i: (0, tile0 + i)),
                  pl.BlockSpec((TOP_K, TOKEN_TILE), lambda i: (0, i)),
                  pl.BlockSpec((TOKEN_TILE, D), lambda i: (i, 0)),
                  pl.BlockSpec((1, D), lambda i: (0, 0)),
                  pl.BlockSpec(memory_space=pl.ANY)],
        out_specs=pl.BlockSpec((TOKEN_TILE, D), lambda i: (i, 0)),
        out_shape=jax.ShapeDtypeStruct((n, D), F32),
        scratch_shapes=[pltpu.VMEM((2, STAGE_ROWS * CHUNKS, LANES), F32), pltpu.SemaphoreType.DMA((2,))],
        compiler_params=_cparams(("arbitrary",)),
        name="combine",
    )(meta, meta, pos, gates, h, norm_g, ys)


def _prep_weights(w_in, w_alpha):
    w_main = w_in[:, :PROJ_MAIN].astype(BF16)
    w_lr = jnp.pad(w_in[:, PROJ_MAIN:], ((0, 0), (0, LANES - GATE_RANK))).astype(BF16)
    w_al = jnp.pad(w_alpha, ((0, LANES - GATE_RANK), (0, 0))).astype(BF16)
    return w_main, w_lr, w_al


def kernel(x_prompt, x_sample, cache_swa_k, cache_swa_v, state_gla, norm_mix_g, w_in, w_alpha, b_alpha, gla_norm_g, w_out, norm_ffn_g, w_router, b_router, w_gate_up, b_gate_up, w_down, b_down, norm_final_g):
    B, S, D = x_prompt.shape
    Bs, Ts, _ = x_sample.shape
    assert w_in.shape[0] == 1, "single-layer trunk"
    l = 0
    R = cache_swa_k.shape[2]
    rows_p = min(DILATIONS[-1] * KEYS_PER_CONFIG, S)
    w_main, w_lr, w_al = _prep_weights(w_in[l], w_alpha[l])
    g_mix = norm_mix_g[l][None]
    b_al = b_alpha[l][None]
    g_gla = gla_norm_g[l][None]

    pos_p = jnp.arange(S, dtype=jnp.int32)
    qa, ka, va, qb, kb, vb, zg, gb, k_tail, v_tail = _project(x_prompt, pos_p, g_mix, w_main, w_lr, w_al, b_al,
                                                              PROJ_TILE, tail_rows=rows_p)
    oa_p = _prompt_attention(qa, ka, va)
    ob_p, st_p = _gla(qb, kb, gb, vb, zg, jnp.zeros((B, N_HEADS_B, DK_B, DV_B), F32), g_gla,
                      GLA_TILE, GLA_CHUNK)
    k_prompt = k_tail.reshape(1, B, rows_p, N_HEADS_A, HEAD_DIM_A)
    v_prompt = v_tail.reshape(1, B, rows_p, N_HEADS_A, HEAD_DIM_A)

    pos_s = PAST_LEN + (jnp.arange(Bs * Ts, dtype=jnp.int32) % Ts)
    proj_s = _project(x_sample.reshape(1, Bs * Ts, D), pos_s, g_mix, w_main, w_lr, w_al, b_al, Bs * Ts)
    qa_s, ka_s, va_s, qb_s, kb_s, vb_s, zg_s, gb_s = [t.reshape(Bs, Ts, -1) for t in proj_s]
    oa_s, k_sample, v_sample = _sample_attention(qa_s, ka_s, va_s,
                                                 cache_swa_k[l].reshape(Bs, R, WIDTH_A),
                                                 cache_swa_v[l].reshape(Bs, R, WIDTH_A))
    ob_s, st_s = _gla(qb_s, kb_s, gb_s, vb_s, zg_s, state_gla[l], g_gla, Ts, Ts)

    w_out_bf = w_out[l].astype(BF16)
    g_ffn = norm_ffn_g[l][None]
    w_router_t = jnp.transpose(w_router[l])
    b_router_c = b_router[l][:, None]
    Np, Ns = B * S, Bs * Ts
    h_p, xn_p, idx_p, gate_p = _merge(oa_p.reshape(Np, WIDTH_A), ob_p.reshape(Np, WIDTH_B),
                                      x_prompt.reshape(Np, D), w_out_bf, g_ffn, w_router_t, b_router_c,
                                      MERGE_TILE)
    h_s, xn_s, idx_s, gate_s = _merge(oa_s.reshape(Ns, WIDTH_A), ob_s.reshape(Ns, WIDTH_B),
                                      x_sample.reshape(Ns, D), w_out_bf, g_ffn, w_router_t, b_router_c,
                                      Ns)

    y_p, y_s = _moe([(xn_p, idx_p, gate_p, h_p), (xn_s, idx_s, gate_s, h_s)],
                    w_gate_up[l], b_gate_up[l], w_down[l], b_down[l], norm_final_g[None])
    return (y_p.reshape(B, S, D), y_s.reshape(Bs, Ts, D), k_prompt, v_prompt, st_p[None],
            k_sample.reshape(1, Bs, R, N_HEADS_A, HEAD_DIM_A),
            v_sample.reshape(1, Bs, R, N_HEADS_A, HEAD_DIM_A), st_s[None])


PAST_LEN = 16384
PROJ_TILE = 1024
MERGE_TILE = 1024
GLA_TILE = 1024
GLA_CHUNK = 64


def _moe(groups, w_gate_up, b_gate_up, w_down, b_down, g_final):
    sizes = [g[3].shape[0] for g in groups]
    N = sum(sizes)
    assert all(n % TOKEN_TILE == 0 for n in sizes)
    n_blocks = -(-(N * TOP_K + N_EXPERTS * (MOE_ROWS - 1)) // MOE_ROWS)
    idx = jnp.concatenate([g[1] for g in groups], axis=1)
    pos, meta, block_expert, misc = _route(idx, n_blocks)
    first_tile = [sum(sizes[:i]) // TOKEN_TILE for i in range(len(sizes))]
    xs = _dispatch(misc, meta, pos, [g[0] for g in groups], n_blocks)
    ys = _experts(block_expert[0], misc[0, :1], xs, w_gate_up, b_gate_up, w_down, b_down, n_blocks)
    return [_combine(meta, pos, gates, h, g_final, ys, t0)
            for (_, _, gates, h), t0 in zip(groups, first_tile)]
```

```python
import functools

import jax
import jax.numpy as jnp
import numpy as np
from jax import lax
from jax.experimental import pallas as pl
from jax.experimental.pallas import tpu as pltpu

F32 = jnp.float32
BF16 = jnp.bfloat16

N_HEADS_A = 8
HEAD_DIM_A = 64
WIDTH_A = N_HEADS_A * HEAD_DIM_A
N_HEADS_B = 4
DK_B = 64
DV_B = 128
QK_B = N_HEADS_B * DK_B
WIDTH_B = N_HEADS_B * DV_B
GATE_RANK = 16
GATE_LOGIT_NORM = 16.0
DILATIONS = (1, 4, 16)
KEYS_PER_CONFIG = 128
ROPE_THETA = 10000.0
N_EXPERTS = 32
TOP_K = 4
SWIGLU_ALPHA = 1.702
SWIGLU_LIMIT = 7.0
EPS = 1e-6

LANES = 128
VMEM_LIMIT = 56 * 1024 * 1024


def _cparams(sem, vmem=VMEM_LIMIT):
    return pltpu.CompilerParams(dimension_semantics=sem, vmem_limit_bytes=vmem)


PROJ_MAIN = 3 * WIDTH_A + 2 * QK_B + 2 * WIDTH_B


def _rope_tables(pos):
    half = HEAD_DIM_A // 2
    inv = ROPE_THETA ** (-jnp.arange(half, dtype=F32) / half)
    ang = pos.astype(F32)[:, None] * inv[None, :]
    cos = jnp.cos(ang)
    sin = jnp.sin(ang)
    cos_t = jnp.concatenate([cos, cos, cos, cos], axis=-1)
    sin_t = jnp.concatenate([-sin, sin, -sin, sin], axis=-1)
    return cos_t, sin_t


def _rope_block(t, cos, sin, first_half):
    partner = jnp.where(first_half, pltpu.roll(t, LANES - 32, 1), pltpu.roll(t, 32, 1))
    return t * cos + partner * sin


def _proj_kernel(x_ref, g_ref, w_ref, wlr_ref, wa_ref, ba_ref, cos_ref, sin_ref,
                 qa_ref, ka_ref, va_ref, qb_ref, kb_ref, vb_ref, zg_ref, gb_ref, *tail_refs, first_tail_tile):
    x = x_ref[...]
    ms = jnp.mean(x * x, axis=-1, keepdims=True)
    xn = ((x * lax.rsqrt(ms + EPS)) * g_ref[...]).astype(BF16)

    def cols(lo, hi):
        return jnp.dot(xn, w_ref[:, lo:hi], preferred_element_type=F32)

    cos = cos_ref[...]
    sin = sin_ref[...]
    lane = lax.broadcasted_iota(jnp.int32, cos.shape, 1)
    first_half = (lane % HEAD_DIM_A) < (HEAD_DIM_A // 2)
    q = cols(0, WIDTH_A)
    k = cols(WIDTH_A, 2 * WIDTH_A)
    for j in range(WIDTH_A // LANES):
        sl = slice(j * LANES, (j + 1) * LANES)
        qa_ref[:, sl] = _rope_block(q[:, sl], cos, sin, first_half) * (HEAD_DIM_A ** -0.5)
        ka_ref[:, sl] = _rope_block(k[:, sl], cos, sin, first_half)
    o = 2 * WIDTH_A
    va_ref[...] = cols(o, o + WIDTH_A)
    o += WIDTH_A
    qb_ref[...] = cols(o, o + QK_B) * (DK_B ** -0.5)
    o += QK_B
    kb_ref[...] = cols(o, o + QK_B)
    o += QK_B
    vb_ref[...] = cols(o, o + WIDTH_B).astype(vb_ref.dtype)
    o += WIDTH_B
    zg_ref[...] = cols(o, o + WIDTH_B)
    lr = jnp.dot(xn, wlr_ref[...], preferred_element_type=F32)
    z = jnp.dot(lr.astype(BF16), wa_ref[...], preferred_element_type=F32) + ba_ref[...]
    logsig = jnp.minimum(z, 0.0) - jnp.log(1.0 + jnp.exp(-jnp.abs(z)))
    gb_ref[...] = logsig / GATE_LOGIT_NORM
    if tail_refs:
        @pl.when(pl.program_id(1) >= first_tail_tile)
        def _():
            tail_refs[0][...] = ka_ref[...]
            tail_refs[1][...] = va_ref[...]


def _project(x, pos, norm_g, w_main, w_lr, w_alpha, b_alpha, tm, tail_rows=0):
    B, T, D = x.shape
    assert tail_rows % tm == 0
    cos_t, sin_t = _rope_tables(pos)
    grid = (B, T // tm)
    first_tail_tile = (T - tail_rows) // tm
    row = lambda w: pl.BlockSpec((None, tm, w), lambda b, i: (b, i, 0))
    tail = pl.BlockSpec((None, tm, WIDTH_A), lambda b, i: (b, jnp.maximum(i - first_tail_tile, 0), 0))
    full = lambda a: pl.BlockSpec(a.shape, lambda b, i: (0,) * a.ndim)
    tab = pl.BlockSpec((tm, LANES), lambda b, i: (i, 0))
    widths = (WIDTH_A, WIDTH_A, WIDTH_A, QK_B, QK_B, WIDTH_B, WIDTH_B, QK_B)
    dtypes = (F32, F32, F32, F32, F32, BF16, F32, F32)
    n_tail = 2 if tail_rows else 0
    return pl.pallas_call(
        functools.partial(_proj_kernel, first_tail_tile=first_tail_tile),
        grid=grid,
        in_specs=[row(D), full(norm_g), full(w_main), full(w_lr), full(w_alpha), full(b_alpha), tab, tab],
        out_specs=[row(w) for w in widths] + [tail] * n_tail,
        out_shape=[jax.ShapeDtypeStruct((B, T, w), dt) for w, dt in zip(widths, dtypes)]
                  + [jax.ShapeDtypeStruct((B, tail_rows, WIDTH_A), F32)] * n_tail,
        compiler_params=_cparams(("parallel", "arbitrary")),
        name="proj",
    )(x, norm_g, w_main, w_lr, w_alpha, b_alpha, cos_t, sin_t)


Q_BLOCK = 128
NEG_INF = float("-inf")


def _attn_block(q, kb, v1, mask, head0, state):
    QB = q.shape[0]
    q2 = jnp.concatenate([jnp.where(head0, q, 0.0), jnp.where(head0, 0.0, q)], axis=0).astype(BF16)
    s = lax.dot_general(q2, kb, (((1,), (1,)), ((), ())), preferred_element_type=F32)
    s = jnp.where(mask, s, NEG_INF)
    if state is None:
        m2 = jnp.max(s, axis=1, keepdims=True)
    else:
        prev = jnp.concatenate([jnp.where(head0, state[0], NEG_INF),
                                jnp.where(head0, NEG_INF, state[0])], axis=0)
        m2 = jnp.max(jnp.concatenate([s, prev], axis=1), axis=1, keepdims=True)
    p = jnp.exp(s - m2)
    pv = jnp.dot(p.astype(BF16), v1, preferred_element_type=F32)
    m_full = jnp.where(head0, m2[:QB], m2[QB:])
    l_full = jnp.where(head0, pv[:QB, LANES:], pv[QB:, LANES:])
    pv_full = jnp.where(head0, pv[:QB, :LANES], pv[QB:, :LANES])
    if state is None:
        return m_full, l_full, pv_full
    a = jnp.exp(state[0] - m_full)
    return m_full, a * state[1] + l_full, a * state[2] + pv_full


def _attn_kernel(q_ref, k_ref, v_ref, o_ref, m_ref, l_ref, acc_ref):
    S = q_ref.shape[0]
    QB = Q_BLOCK
    lane = lax.broadcasted_iota(jnp.int32, (QB, LANES), 1)
    head0 = lane < HEAD_DIM_A
    qq = lax.broadcasted_iota(jnp.int32, (2 * QB, 2 * QB), 0) % QB
    kk = lax.broadcasted_iota(jnp.int32, (2 * QB, 2 * QB), 1)
    band = jnp.logical_and(kk >= qq, kk - qq <= KEYS_PER_CONFIG)
    causal = (lax.broadcasted_iota(jnp.int32, (2 * QB, QB), 1)
              <= lax.broadcasted_iota(jnp.int32, (2 * QB, QB), 0) % QB)

    GROUP = 8

    for ci, dil in enumerate(sorted(DILATIONS, reverse=True)):
        nblk = S // (dil * QB)
        assert nblk % GROUP == 0 or GROUP % nblk == 0

        def rows(start, n, dil=dil):
            return pl.ds(start, n) if dil == 1 else pl.ds(start, n, stride=dil)

        def do_group(blocks, ci=ci, dil=dil, rows=rows):
            chunks = {}

            def kv_chunk(r, rkey, jbase, c):
                if (rkey, c) not in chunks:
                    ks = rows(r + dil * QB * (jbase + c), QB)
                    v = v_ref[ks, :].astype(BF16)
                    chunks[(rkey, c)] = (k_ref[ks, :].astype(BF16),
                                         jnp.concatenate([v, jnp.ones(v.shape, BF16)], axis=1))
                return chunks[(rkey, c)]

            loaded = []
            for r, rkey, jbase, joff in blocks:
                first = isinstance(jbase, int) and jbase + joff == 0
                qs = rows(r + dil * QB * (jbase + joff), QB)
                parts = [kv_chunk(r, rkey, jbase, joff)]
                if not first:
                    parts.insert(0, kv_chunk(r, rkey, jbase, joff - 1))
                kb = jnp.concatenate([p[0] for p in parts], axis=0)
                v1 = jnp.concatenate([p[1] for p in parts], axis=0)
                state = None if ci == 0 else (m_ref[qs, :], l_ref[qs, :], acc_ref[qs, :])
                loaded.append((qs, q_ref[qs, :], kb, v1, causal if first else band, state))
            results = [_attn_block(q, kb, v1, mask, head0, state) for _, q, kb, v1, mask, state in loaded]
            for (qs, *_), (m, l, acc) in zip(loaded, results):
                m_ref[qs, :] = m
                l_ref[qs, :] = l
                acc_ref[qs, :] = acc

        if nblk >= GROUP:
            def residue(r, carry, nblk=nblk, do_group=do_group):
                do_group([(r, 0, 0, j) for j in range(GROUP)])

                def rest(g, c):
                    do_group([(r, 0, GROUP * g, u) for u in range(GROUP)])
                    return c

                return lax.fori_loop(1, nblk // GROUP, rest, carry)

            lax.fori_loop(0, dil, residue, 0)
        else:
            per = 2 * GROUP // nblk

            def residues(g, carry, nblk=nblk, per=per, do_group=do_group):
                do_group([(g * per + i, i, 0, j) for j in range(nblk) for i in range(per)])
                return carry

            lax.fori_loop(0, dil // per, residues, 0)

    def finish(i, c):
        rs = pl.ds(pl.multiple_of(i * QB, QB), QB)
        o_ref[rs, :] = (acc_ref[rs, :] / l_ref[rs, :]).astype(o_ref.dtype)
        return c

    lax.fori_loop(0, S // QB, finish, 0, unroll=4)


def _prompt_attention(qa, ka, va):
    B, S, W = qa.shape
    spec = pl.BlockSpec((None, S, LANES), lambda b, hp: (b, 0, hp))
    return pl.pallas_call(
        _attn_kernel,
        grid=(B, W // LANES),
        in_specs=[spec, spec, spec],
        out_specs=spec,
        out_shape=jax.ShapeDtypeStruct((B, S, W), BF16),
        scratch_shapes=[pltpu.VMEM((S, LANES), F32)] * 3,
        compiler_params=_cparams(("parallel", "parallel")),
        name="prompt_attn",
    )(qa, ka, va)


def _gla_kernel(q_ref, k_ref, g_ref, v_ref, z_ref, s0_ref, ng_ref, o_ref, sfin_ref, st_ref, *, chunk):
    C = chunk
    TS = q_ref.shape[0]
    n_pairs = N_HEADS_B // 2
    PW = 2 * DV_B
    t_idx = pl.program_id(1)

    lane_k = lax.broadcasted_iota(jnp.int32, (C, LANES), 1)
    head0 = lane_k < DK_B
    row_k = lax.broadcasted_iota(jnp.int32, (C, LANES), 0)
    tri2 = (lax.broadcasted_iota(jnp.int32, (2 * C, C), 1)
            <= lax.broadcasted_iota(jnp.int32, (2 * C, C), 0) % C)
    value_head0 = lax.broadcasted_iota(jnp.int32, (C, PW), 1) < DV_B
    bd_mask = ((lax.broadcasted_iota(jnp.int32, (PW, LANES), 0) // DV_B)
               == (lax.broadcasted_iota(jnp.int32, (PW, LANES), 1) // DK_B))

    def prefix_rows(x):
        shift = 1
        while shift < C:
            x = x + jnp.where(row_k >= shift, pltpu.roll(x, shift, 0), 0.0)
            shift *= 2
        return x

    @pl.when(t_idx == 0)
    def _():
        for p in range(n_pairs):
            for h in range(2):
                blk = jnp.transpose(s0_ref[2 * p + h])
                pad = jnp.zeros((DV_B, DK_B), F32)
                row = jnp.concatenate([blk, pad] if h == 0 else [pad, blk], axis=1)
                st_ref[p, h * DV_B:(h + 1) * DV_B, :] = row

    def chunk_body(c, carry):
        rs = pl.ds(pl.multiple_of(c * C, C), C)
        for p in range(n_pairs):
            kl = slice(p * LANES, (p + 1) * LANES)
            vl = slice(p * PW, (p + 1) * PW)
            q = q_ref[rs, kl]
            k = k_ref[rs, kl]
            g = g_ref[rs, kl]
            v = v_ref[rs, vl]
            b = prefix_rows(g)
            b_last = b[C - 1:C, :]
            b_mid = b[C // 2 - 1:C // 2, :] if C > 1 else b_last
            qe = q * jnp.exp(b - b_mid)
            ke = (k * jnp.exp(b_mid - b)).astype(BF16)
            st = st_ref[p]
            q_in = (q * jnp.exp(b)).astype(BF16)
            nt = (((1,), (1,)), ((), ()))
            o = lax.dot_general(q_in, st.astype(BF16), nt, preferred_element_type=F32)
            q2 = jnp.concatenate([jnp.where(head0, qe, 0.0), jnp.where(head0, 0.0, qe)], axis=0)
            a2 = lax.dot_general(q2.astype(BF16), ke, nt, preferred_element_type=F32)
            a2 = jnp.where(tri2, a2, 0.0).astype(BF16)
            av = jnp.dot(a2, v, preferred_element_type=F32)
            o = o + jnp.where(value_head0, av[:C], av[C:])
            k_dec = (k * jnp.exp(b_last - b)).astype(BF16)
            upd = lax.dot_general(v, k_dec, (((0,), (0,)), ((), ())), preferred_element_type=F32)
            st_ref[p] = jnp.exp(b_last) * st + jnp.where(bd_mask, upd, 0.0)
            for h in range(2):
                oh = o[:, h * DV_B:(h + 1) * DV_B]
                hl = slice((2 * p + h) * DV_B, (2 * p + h + 1) * DV_B)
                ms = jnp.mean(oh * oh, axis=-1, keepdims=True)
                z = z_ref[rs, hl]
                gated = (oh * lax.rsqrt(ms + EPS)) * ng_ref[:, hl] * (z / (1.0 + jnp.exp(-z)))
                o_ref[rs, hl] = gated.astype(o_ref.dtype)
        return carry

    lax.fori_loop(0, TS // C, chunk_body, 0, unroll=2 if (TS // C) % 2 == 0 else 1)

    @pl.when(t_idx == pl.num_programs(1) - 1)
    def _():
        for p in range(n_pairs):
            for h in range(2):
                blk = st_ref[p, h * DV_B:(h + 1) * DV_B, h * DK_B:(h + 1) * DK_B]
                sfin_ref[2 * p + h] = jnp.transpose(blk)


def _gla(qb, kb, gb, vb, zg, state0, norm_g, ts, chunk):
    B, T, _ = qb.shape
    row = lambda w: pl.BlockSpec((None, ts, w), lambda b, i: (b, i, 0))
    st_spec = pl.BlockSpec((None, N_HEADS_B, DK_B, DV_B), lambda b, i: (b, 0, 0, 0))
    return pl.pallas_call(
        functools.partial(_gla_kernel, chunk=chunk),
        grid=(B, T // ts),
        in_specs=[row(QK_B), row(QK_B), row(QK_B), row(WIDTH_B), row(WIDTH_B), st_spec,
                  pl.BlockSpec((1, WIDTH_B), lambda b, i: (0, 0))],
        out_specs=[row(WIDTH_B), st_spec],
        out_shape=[jax.ShapeDtypeStruct((B, T, WIDTH_B), BF16),
                   jax.ShapeDtypeStruct((B, N_HEADS_B, DK_B, DV_B), F32)],
        scratch_shapes=[pltpu.VMEM((N_HEADS_B // 2, 2 * DV_B, LANES), F32)],
        compiler_params=_cparams(("parallel", "arbitrary")),
        name="gla",
    )(qb, kb, gb, vb, zg, state0, norm_g)


def _sample_attn_kernel(q_ref, kn_ref, vn_ref, kc_ref, vc_ref, o_ref, ko_ref, vo_ref):
    T = q_ref.shape[0]
    R = kc_ref.shape[0]
    W = q_ref.shape[1]
    HT = N_HEADS_A * T
    q = q_ref[...]
    qx = jnp.concatenate([q] * N_HEADS_A, axis=0)
    own = ((lax.broadcasted_iota(jnp.int32, (HT, W), 0) // T)
           == (lax.broadcasted_iota(jnp.int32, (HT, W), 1) // HEAD_DIM_A))
    qx = jnp.where(own, qx, 0.0).astype(BF16)
    pad = jnp.zeros((LANES - T, W), F32)
    kn = jnp.concatenate([kn_ref[...], pad], axis=0).astype(BF16)
    vn = jnp.concatenate([vn_ref[...], pad], axis=0).astype(BF16)
    nt = (((1,), (1,)), ((), ()))
    s_c = lax.dot_general(qx, kc_ref[...].astype(BF16), nt, preferred_element_type=F32)
    s_n = lax.dot_general(qx, kn, nt, preferred_element_type=F32)

    def multiplicity(n_cols, first_row):
        t = lax.broadcasted_iota(jnp.int32, (HT, n_cols), 0) % T
        j = lax.broadcasted_iota(jnp.int32, (HT, n_cols), 1) + first_row
        delta = R + t - j
        cnt = jnp.zeros((HT, n_cols), F32)
        for dil in DILATIONS:
            hit = (delta >= 0) & (delta <= dil * KEYS_PER_CONFIG) & (delta % dil == 0)
            cnt = cnt + jnp.where(hit, 1.0, 0.0)
        return cnt

    cnt_c = multiplicity(R, 0)
    cnt_n = multiplicity(LANES, R)
    s_c = jnp.where(cnt_c > 0.0, s_c, NEG_INF)
    s_n = jnp.where(cnt_n > 0.0, s_n, NEG_INF)
    m = jnp.maximum(jnp.max(s_c, axis=1, keepdims=True), jnp.max(s_n, axis=1, keepdims=True))
    p_c = cnt_c * jnp.exp(s_c - m)
    p_n = cnt_n * jnp.exp(s_n - m)
    den = jnp.sum(p_c, axis=1, keepdims=True) + jnp.sum(p_n, axis=1, keepdims=True)
    full = (jnp.dot(p_c.astype(BF16), vc_ref[...].astype(BF16), preferred_element_type=F32)
            + jnp.dot(p_n.astype(BF16), vn, preferred_element_type=F32)) / den
    full = jnp.where(own, full, 0.0)
    out = full[0:T, :]
    for h in range(1, N_HEADS_A):
        out = out + full[h * T:(h + 1) * T, :]
    o_ref[...] = out.astype(o_ref.dtype)
    ko_ref[0:R - T, :] = kc_ref[T:R, :]
    ko_ref[R - T:R, :] = kn_ref[...]
    vo_ref[0:R - T, :] = vc_ref[T:R, :]
    vo_ref[R - T:R, :] = vn_ref[...]


def _sample_attention(qa, ka, va, cache_k, cache_v):
    B, T, W = qa.shape
    R = cache_k.shape[1]
    assert R >= DILATIONS[-1] * KEYS_PER_CONFIG and T % 8 == 0 and T <= LANES
    new = pl.BlockSpec((None, T, W), lambda b: (b, 0, 0))
    cache = pl.BlockSpec((None, R, W), lambda b: (b, 0, 0))
    return pl.pallas_call(
        _sample_attn_kernel,
        grid=(B,),
        in_specs=[new, new, new, cache, cache],
        out_specs=[new, cache, cache],
        out_shape=[jax.ShapeDtypeStruct((B, T, W), BF16),
                   jax.ShapeDtypeStruct((B, R, W), F32),
                   jax.ShapeDtypeStruct((B, R, W), F32)],
        compiler_params=_cparams(("parallel",)),
        name="sample_attn",
    )(qa, ka, va, cache_k, cache_v)


CHUNKS = 8


def _store_chunked(ref, val):
    n = val.shape[0]
    for s in range(CHUNKS):
        ref[pl.ds(s, n, stride=CHUNKS), :] = val[:, s * LANES:(s + 1) * LANES]


def _load_chunked(ref, n):
    return jnp.concatenate([ref[pl.ds(s, n, stride=CHUNKS), :] for s in range(CHUNKS)], axis=1)


def _split2(x):
    hi = x.astype(BF16)
    return hi, (x - hi.astype(F32)).astype(BF16)


def _merge_kernel(oa_ref, ob_ref, x_ref, wo_ref, g_ref, wr_ref, br_ref,
                  h_ref, xn_ref, idx_ref, gate_ref):
    TM = x_ref.shape[0]
    mixed = (jnp.dot(oa_ref[...], wo_ref[0:WIDTH_A, :], preferred_element_type=F32)
             + jnp.dot(ob_ref[...], wo_ref[WIDTH_A:, :], preferred_element_type=F32))
    h = x_ref[...] + mixed
    h_ref[...] = h
    ms = jnp.mean(h * h, axis=-1, keepdims=True)
    xn = (h * lax.rsqrt(ms + EPS)) * g_ref[...]
    xn_ref[...] = xn.astype(xn_ref.dtype)
    xh, xl = _split2(xn)
    wh, wl = _split2(wr_ref[...])
    both = jnp.dot(jnp.concatenate([xh, xl], axis=0), wh, preferred_element_type=F32)
    tok_major = both[:TM] + both[TM:] + jnp.dot(xh, wl, preferred_element_type=F32)
    logits = jnp.transpose(tok_major)[:N_EXPERTS] + br_ref[...]
    e_iota = lax.broadcasted_iota(jnp.int32, (N_EXPERTS, TM), 0)
    vals, idxs = [], []
    for _ in range(TOP_K):
        m = jnp.max(logits, axis=0, keepdims=True)
        sel = jnp.min(jnp.where(logits == m, e_iota, N_EXPERTS), axis=0, keepdims=True)
        vals.append(m)
        idxs.append(sel)
        logits = jnp.where(e_iota == sel, NEG_INF, logits)
    ex = [jnp.exp(v - vals[0]) for v in vals]
    den = ex[0] + ex[1] + ex[2] + ex[3]
    idx_ref[...] = jnp.concatenate(idxs, axis=0)
    gate_ref[...] = jnp.concatenate([e / den for e in ex], axis=0)


def _merge(oa, ob, x, w_out, norm_g, w_router_pad, b_router, tm):
    N, D = x.shape
    full = lambda a: pl.BlockSpec(a.shape, lambda i: (0,) * a.ndim)
    row = lambda w: pl.BlockSpec((tm, w), lambda i: (i, 0))
    col = pl.BlockSpec((TOP_K, tm), lambda i: (0, i))
    return pl.pallas_call(
        _merge_kernel,
        grid=(N // tm,),
        in_specs=[row(WIDTH_A), row(WIDTH_B), row(D), full(w_out), full(norm_g), full(w_router_pad),
                  full(b_router)],
        out_specs=[row(D), row(D), col, col],
        out_shape=[jax.ShapeDtypeStruct((N, D), F32),
                   jax.ShapeDtypeStruct((N, D), BF16),
                   jax.ShapeDtypeStruct((TOP_K, N), jnp.int32),
                   jax.ShapeDtypeStruct((TOP_K, N), F32)],
        compiler_params=_cparams(("parallel",)),
        name="merge_router",
    )(oa, ob, x, w_out, norm_g, w_router_pad, b_router)


MOE_ROWS = 512
TOKEN_TILE = 256
DMA_ROWS = 8
STAGE_ROWS = TOKEN_TILE * TOP_K


def _expert_row(col):
    r = lax.broadcasted_iota(jnp.int32, (N_EXPERTS, LANES), 0)
    c = lax.broadcasted_iota(jnp.int32, (N_EXPERTS, LANES), 1)
    return jnp.sum(jnp.where(r == c, col, 0.0), axis=0, keepdims=True)


def _expert_prefix(col):
    r = lax.broadcasted_iota(jnp.int32, (N_EXPERTS, LANES), 0)
    c = lax.broadcasted_iota(jnp.int32, (N_EXPERTS, LANES), 1)
    return jnp.sum(jnp.where(c < r, _expert_row(col), 0.0), axis=1, keepdims=True)


PIECE_SIZES = (DMA_ROWS, 4, 2, 1)
META_ROWS = 16


def _piece_list(count, src, dst, stride):
    first = _expert_prefix(count)
    f = lax.broadcasted_iota(jnp.int32, (N_EXPERTS, LANES), 1).astype(F32)
    owner = jnp.sum(jnp.where(first + count <= f, 1.0, 0.0), axis=0, keepdims=True)
    hit = lax.broadcasted_iota(jnp.int32, (N_EXPERTS, LANES), 0).astype(F32) == owner
    pick = lambda col: jnp.sum(jnp.where(hit, col, 0.0), axis=0, keepdims=True)
    j = f[0:1, :] - pick(first)
    return pick(src) + stride * j, pick(dst) + stride * j


def _route_kernel(idx_all_ref, idx_ref, pos_ref, meta_ref, be_ref, misc_ref, carry_ref, start_ref):
    i = pl.program_id(0)
    TT = idx_ref.shape[1]
    NBP = be_ref.shape[1]

    @pl.when(i == 0)
    def _():
        idx_all = idx_all_ref[...]
        e_all = lax.broadcasted_iota(jnp.int32, (N_EXPERTS, idx_all.shape[1]), 0)
        tot = jnp.zeros((N_EXPERTS, 1), F32)
        for k in range(TOP_K):
            tot = tot + jnp.sum(jnp.where(idx_all[k:k + 1, :] == e_all, 1.0, 0.0), axis=1, keepdims=True)
        padded = jnp.floor((tot + (MOE_ROWS - 1)) / MOE_ROWS) * MOE_ROWS
        start = _expert_prefix(padded)
        start_ref[...] = start
        carry_ref[...] = jnp.zeros_like(carry_ref)
        end = start + padded
        block_start = lax.broadcasted_iota(jnp.int32, (N_EXPERTS, NBP), 1).astype(F32) * MOE_ROWS
        be = jnp.sum(jnp.where(end <= block_start, 1.0, 0.0), axis=0, keepdims=True)
        be_ref[...] = jnp.minimum(be, N_EXPERTS - 1).astype(jnp.int32)
        n_used = jnp.broadcast_to(jnp.sum(padded, axis=0, keepdims=True) / MOE_ROWS, (1, LANES))
        zero = jnp.zeros((1, LANES), F32)
        misc_ref[...] = jnp.concatenate([n_used, _expert_row(start + tot)] + [zero] * 6,
                                        axis=0).astype(jnp.int32)

    idx = idx_ref[...]
    e_iota = lax.broadcasted_iota(jnp.int32, (N_EXPERTS, TT), 0)
    onehot = [idx[k:k + 1, :] == e_iota for k in range(TOP_K)]
    cnt = jnp.zeros((N_EXPERTS, TT), F32)
    for oh in onehot:
        cnt = cnt + jnp.where(oh, 1.0, 0.0)
    tile_tot = jnp.sum(cnt, axis=1, keepdims=True)
    earlier = (lax.broadcasted_iota(jnp.int32, (TT, TT), 0)
               < lax.broadcasted_iota(jnp.int32, (TT, TT), 1))
    before = jnp.dot(cnt.astype(BF16), jnp.where(earlier, 1.0, 0.0).astype(BF16),
                     preferred_element_type=F32)
    seg = _expert_prefix(tile_tot)
    where_staged = seg + before
    rows = [jnp.sum(jnp.where(oh, where_staged, 0.0), axis=0, keepdims=True) for oh in onehot]
    pos_ref[...] = jnp.concatenate(rows, axis=0).astype(jnp.int32)
    slot = start_ref[...] + carry_ref[...]
    whole = jnp.floor(tile_tot / DMA_ROWS)
    covered = whole * DMA_ROWS
    lists = list(_piece_list(whole, seg, slot, float(DMA_ROWS)))
    counts = [jnp.sum(whole, axis=0, keepdims=True)]
    for size in PIECE_SIZES[1:]:
        has = jnp.floor((tile_tot - covered) / size)
        lists += _piece_list(has, seg + covered, slot + covered, 0.0)
        counts.append(jnp.sum(has, axis=0, keepdims=True))
        covered = covered + has * size
    lane = lax.broadcasted_iota(jnp.int32, (1, LANES), 1)
    count_row = jnp.zeros((1, LANES), F32)
    for k, c in enumerate(counts):
        count_row = jnp.where(lane == k, c, count_row)
    zero = jnp.zeros((1, LANES), F32)
    meta_ref[...] = jnp.concatenate(lists + [count_row] + [zero] * (META_ROWS - len(lists) - 1),
                                    axis=0).astype(jnp.int32)
    carry_ref[...] += tile_tot


def _route(idx, n_blocks):
    _, N = idx.shape
    nbp = -(-n_blocks // LANES) * LANES
    tile = pl.BlockSpec((TOP_K, TOKEN_TILE), lambda i: (0, i))
    return pl.pallas_call(
        _route_kernel,
        grid=(N // TOKEN_TILE,),
        in_specs=[pl.BlockSpec((TOP_K, N), lambda i: (0, 0)), tile],
        out_specs=[tile,
                   pl.BlockSpec((META_ROWS, LANES), lambda i: (i, 0)),
                   pl.BlockSpec((1, nbp), lambda i: (0, 0)),
                   pl.BlockSpec((8, LANES), lambda i: (0, 0))],
        out_shape=[jax.ShapeDtypeStruct((TOP_K, N), jnp.int32),
                   jax.ShapeDtypeStruct((N // TOKEN_TILE * META_ROWS, LANES), jnp.int32),
                   jax.ShapeDtypeStruct((1, nbp), jnp.int32),
                   jax.ShapeDtypeStruct((8, LANES), jnp.int32)],
        scratch_shapes=[pltpu.VMEM((N_EXPERTS, 1), F32)] * 2,
        compiler_params=_cparams(("arbitrary",)),
        name="route",
    )(idx, idx)


def _rows(ref, first_row, n_rows):
    return ref.at[pl.ds(pl.multiple_of(first_row * CHUNKS, CHUNKS), n_rows * CHUNKS), :]


def _for_each_run_piece(meta_ref, fn):
    for k, size in enumerate(PIECE_SIZES):
        def body(j, c, k=k, size=size):
            fn(meta_ref[2 * k + 1, j], meta_ref[2 * k, j], size)
            return c

        lax.fori_loop(0, meta_ref[2 * len(PIECE_SIZES), k], body, 0)


def _wait_rows(n_rows, src_ref, dst_ref, sem):
    @pl.when(n_rows > 0)
    def _():
        n = n_rows * CHUNKS
        pltpu.make_async_copy(src_ref.at[pl.ds(0, n), :], dst_ref.at[pl.ds(0, n), :], sem).wait()


PAD_ROWS = MOE_ROWS


def _zero_padding(misc_ref, xs_ref, zeros_ref, sem, n_blocks):
    zeros_ref[...] = jnp.zeros_like(zeros_ref)

    def pad_copy(e):
        first = pl.multiple_of(misc_ref[1, e] * CHUNKS, CHUNKS)
        return pltpu.make_async_copy(zeros_ref, xs_ref.at[pl.ds(first, PAD_ROWS * CHUNKS), :], sem)

    def tail_copy(b):
        first = pl.multiple_of(b * (MOE_ROWS * CHUNKS), MOE_ROWS * CHUNKS)
        return pltpu.make_async_copy(zeros_ref.at[pl.ds(0, MOE_ROWS * CHUNKS), :],
                                     xs_ref.at[pl.ds(first, MOE_ROWS * CHUNKS), :], sem)

    def pad(e, c):
        pad_copy(e).start()
        pad_copy(e).wait()
        return c

    def tail(start_not_wait):
        def body(b, c):
            tail_copy(b).start() if start_not_wait else tail_copy(b).wait()
            return c

        lax.fori_loop(misc_ref[0, 0], n_blocks + 1, body, 0)

    lax.fori_loop(0, N_EXPERTS, pad, 0)
    tail(True)
    tail(False)


def _selection(pos_ref, fill_ref=None):
    TT = pos_ref.shape[1]
    p_iota = lax.broadcasted_iota(jnp.int32, (STAGE_ROWS, TT), 0)
    sel = jnp.zeros((STAGE_ROWS, TT), F32)
    for k in range(TOP_K):
        val = 1.0 if fill_ref is None else fill_ref[k:k + 1, :]
        sel = jnp.where(pos_ref[k:k + 1, :] == p_iota, val, sel)
    return sel.astype(BF16)


def _dispatch_kernel(misc_ref, meta_ref, pos_ref, *refs, first_tiles, n_tiles, n_blocks):
    x_refs = refs[:len(first_tiles)]
    xs_ref, stage_ref, zeros_ref, sems, pad_sem = refs[len(first_tiles):]
    i = pl.program_id(0)
    TT = pos_ref.shape[1]
    slot = i % 2

    @pl.when(i == 0)
    def _():
        _zero_padding(misc_ref, xs_ref, zeros_ref, pad_sem, n_blocks)

    x = x_refs[0][...]
    for t0, ref in zip(first_tiles[1:], x_refs[1:]):
        x = jnp.where(i >= t0, ref[...], x)
    staged = jnp.dot(_selection(pos_ref), x, preferred_element_type=F32)
    stage = stage_ref.at[slot]
    _store_chunked(stage, staged)
    _for_each_run_piece(meta_ref, lambda dst, src, n: pltpu.make_async_copy(
        _rows(stage, src, n), _rows(xs_ref, dst, n), sems.at[slot]).start())

    @pl.when(i > 0)
    def _():
        _wait_rows(TT * TOP_K, stage_ref.at[1 - slot], xs_ref, sems.at[1 - slot])

    @pl.when(i == n_tiles - 1)
    def _():
        _wait_rows(TT * TOP_K, stage, xs_ref, sems.at[slot])


def _dispatch(misc, meta, pos, xns, n_blocks):
    D = xns[0].shape[1]
    tiles = [x.shape[0] // TOKEN_TILE for x in xns]
    first_tiles = tuple(sum(tiles[:g]) for g in range(len(tiles)))

    def x_spec(t0, nt):
        return pl.BlockSpec((TOKEN_TILE, D), lambda i: (jnp.clip(i - t0, 0, nt - 1), 0))

    return pl.pallas_call(
        functools.partial(_dispatch_kernel, first_tiles=first_tiles, n_tiles=sum(tiles), n_blocks=n_blocks),
        grid=(sum(tiles),),
        in_specs=[pl.BlockSpec(memory_space=pltpu.SMEM),
                  pl.BlockSpec((META_ROWS, LANES), lambda i: (i, 0), memory_space=pltpu.SMEM),
                  pl.BlockSpec((TOP_K, TOKEN_TILE), lambda i: (0, i))]
                 + [x_spec(t0, nt) for t0, nt in zip(first_tiles, tiles)],
        out_specs=pl.BlockSpec(memory_space=pl.ANY),
        out_shape=jax.ShapeDtypeStruct(((n_blocks + 1) * MOE_ROWS * CHUNKS, LANES), F32),
        scratch_shapes=[pltpu.VMEM((2, STAGE_ROWS * CHUNKS, LANES), F32),
                        pltpu.VMEM((PAD_ROWS * CHUNKS, LANES), F32),
                        pltpu.SemaphoreType.DMA((2,)),
                        pltpu.SemaphoreType.DMA(())],
        compiler_params=_cparams(("arbitrary",)),
        name="dispatch",
    )(misc, meta, pos, *xns)


def _expert_kernel(be_ref, nused_ref, xs_ref, wgu_ref, bgu_ref, wd_ref, bd_ref, ys_ref,
                   wgu_bf, wd_bf):
    i = pl.program_id(0)
    D_FF = wd_ref.shape[0]
    new_expert = jnp.logical_or(i == 0, be_ref[i] != be_ref[jnp.maximum(i - 1, 0)])

    @pl.when(jnp.logical_and(i < nused_ref[0], new_expert))
    def _():
        wgu_bf[...] = wgu_ref[...].astype(BF16)
        wd_bf[...] = wd_ref[...].astype(BF16)

    def ffn(n_rows):
        x = _load_chunked(xs_ref, n_rows).astype(BF16)
        hdn = jnp.dot(x, wgu_bf[...], preferred_element_type=F32) + bgu_ref[...]
        glu = jnp.minimum(hdn[:, :D_FF], SWIGLU_LIMIT)
        lin = jnp.clip(hdn[:, D_FF:], -SWIGLU_LIMIT, SWIGLU_LIMIT)
        act = glu * (1.0 / (1.0 + jnp.exp(-SWIGLU_ALPHA * glu))) * (lin + 1.0)
        y = jnp.dot(act.astype(BF16), wd_bf[...], preferred_element_type=F32) + bd_ref[...]
        _store_chunked(ys_ref, y)

    HALF = MOE_ROWS // 2
    used = i < nused_ref[0]
    real_rows = nused_ref[1 + be_ref[i]] - i * MOE_ROWS

    @pl.when(jnp.logical_and(used, real_rows > HALF))
    def _():
        ffn(MOE_ROWS)

    @pl.when(jnp.logical_and(used, real_rows <= HALF))
    def _():
        ffn(HALF)
        ys_ref[pl.ds(HALF * CHUNKS, HALF * CHUNKS), :] = jnp.zeros((HALF * CHUNKS, LANES), ys_ref.dtype)

    @pl.when(i >= nused_ref[0])
    def _():
        ys_ref[...] = jnp.zeros_like(ys_ref)


def _experts(block_expert, n_used, xs, w_gate_up, b_gate_up, w_down, b_down, n_blocks):
    E, D, F2 = w_gate_up.shape
    D_FF = w_down.shape[1]
    rows = pl.BlockSpec((MOE_ROWS * CHUNKS, LANES), lambda i, be, nu: (i, 0))
    rows_in = pl.BlockSpec((MOE_ROWS * CHUNKS, LANES), lambda i, be, nu: (jnp.minimum(i, nu[0] - 1), 0))
    grid_spec = pltpu.PrefetchScalarGridSpec(
        num_scalar_prefetch=2,
        grid=(n_blocks,),
        in_specs=[rows_in,
                  pl.BlockSpec((None, D, F2), lambda i, be, nu: (be[i], 0, 0)),
                  pl.BlockSpec((None, 1, F2), lambda i, be, nu: (be[i], 0, 0)),
                  pl.BlockSpec((None, D_FF, D), lambda i, be, nu: (be[i], 0, 0)),
                  pl.BlockSpec((None, 1, D), lambda i, be, nu: (be[i], 0, 0))],
        out_specs=rows,
        scratch_shapes=[pltpu.VMEM((D, F2), BF16), pltpu.VMEM((D_FF, D), BF16)],
    )
    return pl.pallas_call(
        _expert_kernel,
        grid_spec=grid_spec,
        out_shape=jax.ShapeDtypeStruct((n_blocks * MOE_ROWS * CHUNKS, LANES), F32),
        compiler_params=_cparams(("arbitrary",)),
        name="experts",
    )(block_expert, n_used, xs, w_gate_up, b_gate_up.reshape(E, 1, F2), w_down, b_down.reshape(E, 1, D))


def _combine_kernel(meta_ref, next_meta_ref, pos_ref, gate_ref, h_ref, g_ref, ys_ref, y_ref, stage_ref, sems,
                    *, n_tiles):
    i = pl.program_id(0)
    slot = i % 2

    def fetch(meta, s):
        _for_each_run_piece(meta, lambda src, dst, n: pltpu.make_async_copy(
            _rows(ys_ref, src, n), _rows(stage_ref.at[s], dst, n), sems.at[s]).start())

    @pl.when(i == 0)
    def _():
        fetch(meta_ref, 0)

    if n_tiles > 1:
        @pl.when(i + 1 < n_tiles)
        def _():
            fetch(next_meta_ref, 1 - slot)

    weights = _selection(pos_ref, gate_ref)
    _wait_rows(STAGE_ROWS, ys_ref, stage_ref.at[slot], sems.at[slot])
    staged = _load_chunked(stage_ref.at[slot], STAGE_ROWS).astype(BF16)
    moe = lax.dot_general(weights, staged, (((0,), (0,)), ((), ())), preferred_element_type=F32)
    hf = h_ref[...] + moe
    ms = jnp.mean(hf * hf, axis=-1, keepdims=True)
    y_ref[...] = (hf * lax.rsqrt(ms + EPS)) * g_ref[...]


def _combine(meta, pos, gates, h, norm_g, ys, tile0):
    n, D = h.shape
    n_tiles = n // TOKEN_TILE
    return pl.pallas_call(
        functools.partial(_combine_kernel, n_tiles=n_tiles),
        grid=(n_tiles,),
        in_specs=[pl.BlockSpec((META_ROWS, LANES), lambda i: (tile0 + i, 0), memory_space=pltpu.SMEM),
                  pl.BlockSpec((META_ROWS, LANES), lambda i: (tile0 + jnp.minimum(i + 1, n_tiles - 1), 0),
                               memory_space=pltpu.SMEM),
                  pl.BlockSpec((TOP_K, TOKEN_TILE), lambda i: (0, tile0 + i)),
                  pl.BlockSpec((TOP_K, TOKEN_TILE), lambda i: (0, i)),
                  pl.BlockSpec((TOKEN_TILE, D), lambda i: (i, 0)),
                  pl.BlockSpec((1, D), lambda i: (0, 0)),
                  pl.BlockSpec(memory_space=pl.ANY)],
        out_specs=pl.BlockSpec((TOKEN_TILE, D), lambda i: (i, 0)),
        out_shape=jax.ShapeDtypeStruct((n, D), F32),
        scratch_shapes=[pltpu.VMEM((2, STAGE_ROWS * CHUNKS, LANES), F32), pltpu.SemaphoreType.DMA((2,))],
        compiler_params=_cparams(("arbitrary",)),
        name="combine",
    )(meta, meta, pos, gates, h, norm_g, ys)


def _prep_weights(w_in, w_alpha):
    w_main = w_in[:, :PROJ_MAIN].astype(BF16)
    w_lr = jnp.pad(w_in[:, PROJ_MAIN:], ((0, 0), (0, LANES - GATE_RANK))).astype(BF16)
    w_al = jnp.pad(w_alpha, ((0, LANES - GATE_RANK), (0, 0))).astype(BF16)
    return w_main, w_lr, w_al


def kernel(x_prompt, x_sample, cache_swa_k, cache_swa_v, state_gla, norm_mix_g, w_in, w_alpha, b_alpha, gla_norm_g, w_out, norm_ffn_g, w_router, b_router, w_gate_up, b_gate_up, w_down, b_down, norm_final_g):
    B, S, D = x_prompt.shape
    Bs, Ts, _ = x_sample.shape
    assert w_in.shape[0] == 1, "single-layer trunk"
    l = 0
    R = cache_swa_k.shape[2]
    rows_p = min(DILATIONS[-1] * KEYS_PER_CONFIG, S)
    w_main, w_lr, w_al = _prep_weights(w_in[l], w_alpha[l])
    g_mix = norm_mix_g[l][None]
    b_al = b_alpha[l][None]
    g_gla = gla_norm_g[l][None]

    pos_p = jnp.arange(S, dtype=jnp.int32)
    qa, ka, va, qb, kb, vb, zg, gb, k_tail, v_tail = _project(x_prompt, pos_p, g_mix, w_main, w_lr, w_al, b_al,
                                                              PROJ_TILE, tail_rows=rows_p)
    oa_p = _prompt_attention(qa, ka, va)
    ob_p, st_p = _gla(qb, kb, gb, vb, zg, jnp.zeros((B, N_HEADS_B, DK_B, DV_B), F32), g_gla,
                      GLA_TILE, GLA_CHUNK)
    k_prompt = k_tail.reshape(1, B, rows_p, N_HEADS_A, HEAD_DIM_A)
    v_prompt = v_tail.reshape(1, B, rows_p, N_HEADS_A, HEAD_DIM_A)

    pos_s = PAST_LEN + (jnp.arange(Bs * Ts, dtype=jnp.int32) % Ts)
    proj_s = _project(x_sample.reshape(1, Bs * Ts, D), pos_s, g_mix, w_main, w_lr, w_al, b_al, Bs * Ts)
    qa_s, ka_s, va_s, qb_s, kb_s, vb_s, zg_s, gb_s = [t.reshape(Bs, Ts, -1) for t in proj_s]
    oa_s, k_sample, v_sample = _sample_attention(qa_s, ka_s, va_s,
                                                 cache_swa_k[l].reshape(Bs, R, WIDTH_A),
                                                 cache_swa_v[l].reshape(Bs, R, WIDTH_A))
    ob_s, st_s = _gla(qb_s, kb_s, gb_s, vb_s, zg_s, state_gla[l], g_gla, Ts, Ts)

    w_out_bf = w_out[l].astype(BF16)
    g_ffn = norm_ffn_g[l][None]
    w_router_pad = jnp.pad(w_router[l], ((0, 0), (0, LANES - N_EXPERTS)))
    b_router_c = b_router[l][:, None]
    Np, Ns = B * S, Bs * Ts
    h_p, xn_p, idx_p, gate_p = _merge(oa_p.reshape(Np, WIDTH_A), ob_p.reshape(Np, WIDTH_B),
                                      x_prompt.reshape(Np, D), w_out_bf, g_ffn, w_router_pad, b_router_c,
                                      MERGE_TILE)
    h_s, xn_s, idx_s, gate_s = _merge(oa_s.reshape(Ns, WIDTH_A), ob_s.reshape(Ns, WIDTH_B),
                                      x_sample.reshape(Ns, D), w_out_bf, g_ffn, w_router_pad, b_router_c,
                                      Ns)

    y_p, y_s = _moe([(xn_p, idx_p, gate_p, h_p), (xn_s, idx_s, gate_s, h_s)],
                    w_gate_up[l], b_gate_up[l], w_down[l], b_down[l], norm_final_g[None])
    return (y_p.reshape(B, S, D), y_s.reshape(Bs, Ts, D), k_prompt, v_prompt, st_p[None],
            k_sample.reshape(1, Bs, R, N_HEADS_A, HEAD_DIM_A),
            v_sample.reshape(1, Bs, R, N_HEADS_A, HEAD_DIM_A), st_s[None])


PAST_LEN = 16384
PROJ_TILE = 1024
MERGE_TILE = 1024
GLA_TILE = 1024
GLA_CHUNK = 64


def _moe(groups, w_gate_up, b_gate_up, w_down, b_down, g_final):
    sizes = [g[3].shape[0] for g in groups]
    N = sum(sizes)
    assert all(n % TOKEN_TILE == 0 for n in sizes)
    n_blocks = -(-(N * TOP_K + N_EXPERTS * (MOE_ROWS - 1)) // MOE_ROWS)
    idx = jnp.concatenate([g[1] for g in groups], axis=1)
    pos, meta, block_expert, misc = _route(idx, n_blocks)
    first_tile = [sum(sizes[:i]) // TOKEN_TILE for i in range(len(sizes))]
    xs = _dispatch(misc, meta, pos, [g[0] for g in groups], n_blocks)
    used_and_fill = jnp.concatenate([misc[0, :1], misc[1, :N_EXPERTS]])
    ys = _experts(block_expert[0], used_and_fill, xs, w_gate_up, b_gate_up, w_down, b_down, n_blocks)
    return [_combine(meta, pos, gates, h, g_final, ys, t0)
            for (_, _, gates, h), t0 in zip(groups, first_tile)]
```

```python
import functools

import jax
import jax.numpy as jnp
import numpy as np
from jax import lax
from jax.experimental import pallas as pl
from jax.experimental.pallas import tpu as pltpu

F32 = jnp.float32
BF16 = jnp.bfloat16

N_HEADS_A = 8
HEAD_DIM_A = 64
WIDTH_A = N_HEADS_A * HEAD_DIM_A
N_HEADS_B = 4
DK_B = 64
DV_B = 128
QK_B = N_HEADS_B * DK_B
WIDTH_B = N_HEADS_B * DV_B
GATE_RANK = 16
GATE_LOGIT_NORM = 16.0
DILATIONS = (1, 4, 16)
KEYS_PER_CONFIG = 128
ROPE_THETA = 10000.0
N_EXPERTS = 32
TOP_K = 4
SWIGLU_ALPHA = 1.702
SWIGLU_LIMIT = 7.0
EPS = 1e-6

LANES = 128
VMEM_LIMIT = 56 * 1024 * 1024


def _cparams(sem, vmem=VMEM_LIMIT):
    return pltpu.CompilerParams(dimension_semantics=sem, vmem_limit_bytes=vmem)


PROJ_MAIN = 3 * WIDTH_A + 2 * QK_B + 2 * WIDTH_B


def _rope_tables(pos):
    half = HEAD_DIM_A // 2
    inv = ROPE_THETA ** (-jnp.arange(half, dtype=F32) / half)
    ang = pos.astype(F32)[:, None] * inv[None, :]
    cos = jnp.cos(ang)
    sin = jnp.sin(ang)
    cos_t = jnp.concatenate([cos, cos, cos, cos], axis=-1)
    sin_t = jnp.concatenate([-sin, sin, -sin, sin], axis=-1)
    return cos_t, sin_t


def _rope_block(t, cos, sin, first_half):
    partner = jnp.where(first_half, pltpu.roll(t, LANES - 32, 1), pltpu.roll(t, 32, 1))
    return t * cos + partner * sin


def _proj_kernel(x_ref, g_ref, w_ref, wlr_ref, wa_ref, ba_ref, cos_ref, sin_ref,
                 qa_ref, ka_ref, va_ref, qb_ref, kb_ref, vb_ref, zg_ref, gb_ref, *tail_refs, first_tail_tile):
    x = x_ref[...]
    ms = jnp.mean(x * x, axis=-1, keepdims=True)
    xn = ((x * lax.rsqrt(ms + EPS)) * g_ref[...]).astype(BF16)

    def cols(lo, hi):
        return jnp.dot(xn, w_ref[:, lo:hi], preferred_element_type=F32)

    cos = cos_ref[...]
    sin = sin_ref[...]
    lane = lax.broadcasted_iota(jnp.int32, cos.shape, 1)
    first_half = (lane % HEAD_DIM_A) < (HEAD_DIM_A // 2)
    q = cols(0, WIDTH_A)
    k = cols(WIDTH_A, 2 * WIDTH_A)
    for j in range(WIDTH_A // LANES):
        sl = slice(j * LANES, (j + 1) * LANES)
        qa_ref[:, sl] = _rope_block(q[:, sl], cos, sin, first_half) * (HEAD_DIM_A ** -0.5)
        ka_ref[:, sl] = _rope_block(k[:, sl], cos, sin, first_half)
    o = 2 * WIDTH_A
    va_ref[...] = cols(o, o + WIDTH_A)
    o += WIDTH_A
    qb_ref[...] = cols(o, o + QK_B) * (DK_B ** -0.5)
    o += QK_B
    kb_ref[...] = cols(o, o + QK_B)
    o += QK_B
    vb_ref[...] = cols(o, o + WIDTH_B).astype(vb_ref.dtype)
    o += WIDTH_B
    zg_ref[...] = cols(o, o + WIDTH_B)
    lr = jnp.dot(xn, wlr_ref[...], preferred_element_type=F32)
    z = jnp.dot(lr.astype(BF16), wa_ref[...], preferred_element_type=F32) + ba_ref[...]
    logsig = jnp.minimum(z, 0.0) - jnp.log(1.0 + jnp.exp(-jnp.abs(z)))
    gb_ref[...] = logsig / GATE_LOGIT_NORM
    if tail_refs:
        @pl.when(pl.program_id(1) >= first_tail_tile)
        def _():
            tail_refs[0][...] = ka_ref[...]
            tail_refs[1][...] = va_ref[...]


def _project(x, pos, norm_g, w_main, w_lr, w_alpha, b_alpha, tm, tail_rows=0):
    B, T, D = x.shape
    assert tail_rows % tm == 0
    cos_t, sin_t = _rope_tables(pos)
    grid = (B, T // tm)
    first_tail_tile = (T - tail_rows) // tm
    row = lambda w: pl.BlockSpec((None, tm, w), lambda b, i: (b, i, 0))
    tail = pl.BlockSpec((None, tm, WIDTH_A), lambda b, i: (b, jnp.maximum(i - first_tail_tile, 0), 0))
    full = lambda a: pl.BlockSpec(a.shape, lambda b, i: (0,) * a.ndim)
    tab = pl.BlockSpec((tm, LANES), lambda b, i: (i, 0))
    widths = (WIDTH_A, WIDTH_A, WIDTH_A, QK_B, QK_B, WIDTH_B, WIDTH_B, QK_B)
    dtypes = (F32, F32, F32, F32, F32, BF16, F32, F32)
    n_tail = 2 if tail_rows else 0
    return pl.pallas_call(
        functools.partial(_proj_kernel, first_tail_tile=first_tail_tile),
        grid=grid,
        in_specs=[row(D), full(norm_g), full(w_main), full(w_lr), full(w_alpha), full(b_alpha), tab, tab],
        out_specs=[row(w) for w in widths] + [tail] * n_tail,
        out_shape=[jax.ShapeDtypeStruct((B, T, w), dt) for w, dt in zip(widths, dtypes)]
                  + [jax.ShapeDtypeStruct((B, tail_rows, WIDTH_A), F32)] * n_tail,
        compiler_params=_cparams(("parallel", "arbitrary")),
        name="proj",
    )(x, norm_g, w_main, w_lr, w_alpha, b_alpha, cos_t, sin_t)


Q_BLOCK = 128
NEG_INF = float("-inf")


def _attn_block(q, kb, v1, mask, head0, state):
    QB = q.shape[0]
    q2 = jnp.concatenate([jnp.where(head0, q, 0.0), jnp.where(head0, 0.0, q)], axis=0).astype(BF16)
    s = lax.dot_general(q2, kb, (((1,), (1,)), ((), ())), preferred_element_type=F32)
    s = jnp.where(mask, s, NEG_INF)
    if state is None:
        m2 = jnp.max(s, axis=1, keepdims=True)
    else:
        prev = jnp.concatenate([jnp.where(head0, state[0], NEG_INF),
                                jnp.where(head0, NEG_INF, state[0])], axis=0)
        m2 = jnp.max(jnp.concatenate([s, prev], axis=1), axis=1, keepdims=True)
    p = jnp.exp(s - m2)
    pv = jnp.dot(p.astype(BF16), v1, preferred_element_type=F32)
    m_full = jnp.where(head0, m2[:QB], m2[QB:])
    l_full = jnp.where(head0, pv[:QB, LANES:], pv[QB:, LANES:])
    pv_full = jnp.where(head0, pv[:QB, :LANES], pv[QB:, :LANES])
    if state is None:
        return m_full, l_full, pv_full
    a = jnp.exp(state[0] - m_full)
    return m_full, a * state[1] + l_full, a * state[2] + pv_full


def _attn_kernel(q_ref, k_ref, v_ref, o_ref, m_ref, l_ref, acc_ref):
    S = q_ref.shape[0]
    QB = Q_BLOCK
    lane = lax.broadcasted_iota(jnp.int32, (QB, LANES), 1)
    head0 = lane < HEAD_DIM_A
    qq = lax.broadcasted_iota(jnp.int32, (2 * QB, 2 * QB), 0) % QB
    kk = lax.broadcasted_iota(jnp.int32, (2 * QB, 2 * QB), 1)
    band = jnp.logical_and(kk >= qq, kk - qq <= KEYS_PER_CONFIG)
    causal = (lax.broadcasted_iota(jnp.int32, (2 * QB, QB), 1)
              <= lax.broadcasted_iota(jnp.int32, (2 * QB, QB), 0) % QB)

    GROUP = 8

    for ci, dil in enumerate(sorted(DILATIONS, reverse=True)):
        nblk = S // (dil * QB)
        assert nblk % GROUP == 0 or GROUP % nblk == 0

        def rows(start, n, dil=dil):
            return pl.ds(start, n) if dil == 1 else pl.ds(start, n, stride=dil)

        def do_group(blocks, ci=ci, dil=dil, rows=rows):
            chunks = {}

            def kv_chunk(r, rkey, jbase, c):
                if (rkey, c) not in chunks:
                    ks = rows(r + dil * QB * (jbase + c), QB)
                    v = v_ref[ks, :].astype(BF16)
                    chunks[(rkey, c)] = (k_ref[ks, :].astype(BF16),
                                         jnp.concatenate([v, jnp.ones(v.shape, BF16)], axis=1))
                return chunks[(rkey, c)]

            loaded = []
            for r, rkey, jbase, joff in blocks:
                first = isinstance(jbase, int) and jbase + joff == 0
                qs = rows(r + dil * QB * (jbase + joff), QB)
                parts = [kv_chunk(r, rkey, jbase, joff)]
                if not first:
                    parts.insert(0, kv_chunk(r, rkey, jbase, joff - 1))
                kb = jnp.concatenate([p[0] for p in parts], axis=0)
                v1 = jnp.concatenate([p[1] for p in parts], axis=0)
                state = None if ci == 0 else (m_ref[qs, :], l_ref[qs, :], acc_ref[qs, :])
                loaded.append((qs, q_ref[qs, :], kb, v1, causal if first else band, state))
            results = [_attn_block(q, kb, v1, mask, head0, state) for _, q, kb, v1, mask, state in loaded]
            for (qs, *_), (m, l, acc) in zip(loaded, results):
                m_ref[qs, :] = m
                l_ref[qs, :] = l
                acc_ref[qs, :] = acc

        if nblk >= GROUP:
            def residue(r, carry, nblk=nblk, do_group=do_group):
                do_group([(r, 0, 0, j) for j in range(GROUP)])

                def rest(g, c):
                    do_group([(r, 0, GROUP * g, u) for u in range(GROUP)])
                    return c

                return lax.fori_loop(1, nblk // GROUP, rest, carry)

            lax.fori_loop(0, dil, residue, 0)
        else:
            per = 2 * GROUP // nblk

            def residues(g, carry, nblk=nblk, per=per, do_group=do_group):
                do_group([(g * per + i, i, 0, j) for j in range(nblk) for i in range(per)])
                return carry

            lax.fori_loop(0, dil // per, residues, 0)

    def finish(i, c):
        rs = pl.ds(pl.multiple_of(i * QB, QB), QB)
        o_ref[rs, :] = (acc_ref[rs, :] / l_ref[rs, :]).astype(o_ref.dtype)
        return c

    lax.fori_loop(0, S // QB, finish, 0, unroll=4)


def _prompt_attention(qa, ka, va):
    B, S, W = qa.shape
    spec = pl.BlockSpec((None, S, LANES), lambda b, hp: (b, 0, hp))
    return pl.pallas_call(
        _attn_kernel,
        grid=(B, W // LANES),
        in_specs=[spec, spec, spec],
        out_specs=spec,
        out_shape=jax.ShapeDtypeStruct((B, S, W), BF16),
        scratch_shapes=[pltpu.VMEM((S, LANES), F32)] * 3,
        compiler_params=_cparams(("parallel", "parallel")),
        name="prompt_attn",
    )(qa, ka, va)


def _gla_kernel(q_ref, k_ref, g_ref, v_ref, z_ref, s0_ref, ng_ref, o_ref, sfin_ref, st_ref, *, chunk):
    C = chunk
    TS = q_ref.shape[0]
    n_pairs = N_HEADS_B // 2
    PW = 2 * DV_B
    t_idx = pl.program_id(1)

    lane_k = lax.broadcasted_iota(jnp.int32, (C, LANES), 1)
    head0 = lane_k < DK_B
    row_k = lax.broadcasted_iota(jnp.int32, (C, LANES), 0)
    tri2 = (lax.broadcasted_iota(jnp.int32, (2 * C, C), 1)
            <= lax.broadcasted_iota(jnp.int32, (2 * C, C), 0) % C)
    value_head0 = lax.broadcasted_iota(jnp.int32, (C, PW), 1) < DV_B
    bd_mask = ((lax.broadcasted_iota(jnp.int32, (PW, LANES), 0) // DV_B)
               == (lax.broadcasted_iota(jnp.int32, (PW, LANES), 1) // DK_B))

    def prefix_rows(x):
        shift = 1
        while shift < C:
            x = x + jnp.where(row_k >= shift, pltpu.roll(x, shift, 0), 0.0)
            shift *= 2
        return x

    @pl.when(t_idx == 0)
    def _():
        for p in range(n_pairs):
            for h in range(2):
                blk = jnp.transpose(s0_ref[2 * p + h])
                pad = jnp.zeros((DV_B, DK_B), F32)
                row = jnp.concatenate([blk, pad] if h == 0 else [pad, blk], axis=1)
                st_ref[p, h * DV_B:(h + 1) * DV_B, :] = row

    def chunk_body(c, carry):
        rs = pl.ds(pl.multiple_of(c * C, C), C)
        for p in range(n_pairs):
            kl = slice(p * LANES, (p + 1) * LANES)
            vl = slice(p * PW, (p + 1) * PW)
            q = q_ref[rs, kl]
            k = k_ref[rs, kl]
            g = g_ref[rs, kl]
            v = v_ref[rs, vl]
            b = prefix_rows(g)
            b_last = b[C - 1:C, :]
            b_mid = b[C // 2 - 1:C // 2, :] if C > 1 else b_last
            qe = q * jnp.exp(b - b_mid)
            ke = (k * jnp.exp(b_mid - b)).astype(BF16)
            st = st_ref[p]
            q_in = (q * jnp.exp(b)).astype(BF16)
            nt = (((1,), (1,)), ((), ()))
            o = lax.dot_general(q_in, st.astype(BF16), nt, preferred_element_type=F32)
            q2 = jnp.concatenate([jnp.where(head0, qe, 0.0), jnp.where(head0, 0.0, qe)], axis=0)
            a2 = lax.dot_general(q2.astype(BF16), ke, nt, preferred_element_type=F32)
            a2 = jnp.where(tri2, a2, 0.0).astype(BF16)
            av = jnp.dot(a2, v, preferred_element_type=F32)
            o = o + jnp.where(value_head0, av[:C], av[C:])
            k_dec = (k * jnp.exp(b_last - b)).astype(BF16)
            upd = lax.dot_general(v, k_dec, (((0,), (0,)), ((), ())), preferred_element_type=F32)
            st_ref[p] = jnp.exp(b_last) * st + jnp.where(bd_mask, upd, 0.0)
            for h in range(2):
                oh = o[:, h * DV_B:(h + 1) * DV_B]
                hl = slice((2 * p + h) * DV_B, (2 * p + h + 1) * DV_B)
                ms = jnp.mean(oh * oh, axis=-1, keepdims=True)
                z = z_ref[rs, hl]
                gated = (oh * lax.rsqrt(ms + EPS)) * ng_ref[:, hl] * (z / (1.0 + jnp.exp(-z)))
                o_ref[rs, hl] = gated.astype(o_ref.dtype)
        return carry

    lax.fori_loop(0, TS // C, chunk_body, 0, unroll=4 if (TS // C) % 4 == 0 else 1)

    @pl.when(t_idx == pl.num_programs(1) - 1)
    def _():
        for p in range(n_pairs):
            for h in range(2):
                blk = st_ref[p, h * DV_B:(h + 1) * DV_B, h * DK_B:(h + 1) * DK_B]
                sfin_ref[2 * p + h] = jnp.transpose(blk)


def _gla(qb, kb, gb, vb, zg, state0, norm_g, ts, chunk):
    B, T, _ = qb.shape
    row = lambda w: pl.BlockSpec((None, ts, w), lambda b, i: (b, i, 0))
    st_spec = pl.BlockSpec((None, N_HEADS_B, DK_B, DV_B), lambda b, i: (b, 0, 0, 0))
    return pl.pallas_call(
        functools.partial(_gla_kernel, chunk=chunk),
        grid=(B, T // ts),
        in_specs=[row(QK_B), row(QK_B), row(QK_B), row(WIDTH_B), row(WIDTH_B), st_spec,
                  pl.BlockSpec((1, WIDTH_B), lambda b, i: (0, 0))],
        out_specs=[row(WIDTH_B), st_spec],
        out_shape=[jax.ShapeDtypeStruct((B, T, WIDTH_B), BF16),
                   jax.ShapeDtypeStruct((B, N_HEADS_B, DK_B, DV_B), F32)],
        scratch_shapes=[pltpu.VMEM((N_HEADS_B // 2, 2 * DV_B, LANES), F32)],
        compiler_params=_cparams(("parallel", "arbitrary")),
        name="gla",
    )(qb, kb, gb, vb, zg, state0, norm_g)


def _sample_attn_kernel(q_ref, kn_ref, vn_ref, kc_ref, vc_ref, o_ref, ko_ref, vo_ref):
    T = q_ref.shape[0]
    R = kc_ref.shape[0]
    W = q_ref.shape[1]
    HT = N_HEADS_A * T
    q = q_ref[...]
    qx = jnp.concatenate([q] * N_HEADS_A, axis=0)
    own = ((lax.broadcasted_iota(jnp.int32, (HT, W), 0) // T)
           == (lax.broadcasted_iota(jnp.int32, (HT, W), 1) // HEAD_DIM_A))
    qx = jnp.where(own, qx, 0.0).astype(BF16)
    pad = jnp.zeros((LANES - T, W), F32)
    kn = jnp.concatenate([kn_ref[...], pad], axis=0).astype(BF16)
    vn = jnp.concatenate([vn_ref[...], pad], axis=0).astype(BF16)
    nt = (((1,), (1,)), ((), ()))
    s_c = lax.dot_general(qx, kc_ref[...].astype(BF16), nt, preferred_element_type=F32)
    s_n = lax.dot_general(qx, kn, nt, preferred_element_type=F32)

    def multiplicity(n_cols, first_row):
        t = lax.broadcasted_iota(jnp.int32, (HT, n_cols), 0) % T
        j = lax.broadcasted_iota(jnp.int32, (HT, n_cols), 1) + first_row
        delta = R + t - j
        cnt = jnp.zeros((HT, n_cols), F32)
        for dil in DILATIONS:
            hit = (delta >= 0) & (delta <= dil * KEYS_PER_CONFIG) & (delta % dil == 0)
            cnt = cnt + jnp.where(hit, 1.0, 0.0)
        return cnt

    cnt_c = multiplicity(R, 0)
    cnt_n = multiplicity(LANES, R)
    s_c = jnp.where(cnt_c > 0.0, s_c, NEG_INF)
    s_n = jnp.where(cnt_n > 0.0, s_n, NEG_INF)
    m = jnp.maximum(jnp.max(s_c, axis=1, keepdims=True), jnp.max(s_n, axis=1, keepdims=True))
    p_c = cnt_c * jnp.exp(s_c - m)
    p_n = cnt_n * jnp.exp(s_n - m)
    den = jnp.sum(p_c, axis=1, keepdims=True) + jnp.sum(p_n, axis=1, keepdims=True)
    full = (jnp.dot(p_c.astype(BF16), vc_ref[...].astype(BF16), preferred_element_type=F32)
            + jnp.dot(p_n.astype(BF16), vn, preferred_element_type=F32)) / den
    full = jnp.where(own, full, 0.0)
    out = full[0:T, :]
    for h in range(1, N_HEADS_A):
        out = out + full[h * T:(h + 1) * T, :]
    o_ref[...] = out.astype(o_ref.dtype)
    ko_ref[0:R - T, :] = kc_ref[T:R, :]
    ko_ref[R - T:R, :] = kn_ref[...]
    vo_ref[0:R - T, :] = vc_ref[T:R, :]
    vo_ref[R - T:R, :] = vn_ref[...]


def _sample_attention(qa, ka, va, cache_k, cache_v):
    B, T, W = qa.shape
    R = cache_k.shape[1]
    assert R >= DILATIONS[-1] * KEYS_PER_CONFIG and T % 8 == 0 and T <= LANES
    new = pl.BlockSpec((None, T, W), lambda b: (b, 0, 0))
    cache = pl.BlockSpec((None, R, W), lambda b: (b, 0, 0))
    return pl.pallas_call(
        _sample_attn_kernel,
        grid=(B,),
        in_specs=[new, new, new, cache, cache],
        out_specs=[new, cache, cache],
        out_shape=[jax.ShapeDtypeStruct((B, T, W), BF16),
                   jax.ShapeDtypeStruct((B, R, W), F32),
                   jax.ShapeDtypeStruct((B, R, W), F32)],
        compiler_params=_cparams(("parallel",)),
        name="sample_attn",
    )(qa, ka, va, cache_k, cache_v)


CHUNKS = 8


def _store_chunked(ref, val):
    n = val.shape[0]
    for s in range(CHUNKS):
        ref[pl.ds(s, n, stride=CHUNKS), :] = val[:, s * LANES:(s + 1) * LANES]


def _load_chunked(ref, n):
    return jnp.concatenate([ref[pl.ds(s, n, stride=CHUNKS), :] for s in range(CHUNKS)], axis=1)


def _split2(x):
    hi = x.astype(BF16)
    return hi, (x - hi.astype(F32)).astype(BF16)


def _merge_kernel(oa_ref, ob_ref, x_ref, wo_ref, g_ref, wr_ref, br_ref,
                  h_ref, xn_ref, idx_ref, gate_ref):
    TM = x_ref.shape[0]
    mixed = (jnp.dot(oa_ref[...], wo_ref[0:WIDTH_A, :], preferred_element_type=F32)
             + jnp.dot(ob_ref[...], wo_ref[WIDTH_A:, :], preferred_element_type=F32))
    h = x_ref[...] + mixed
    h_ref[...] = h
    ms = jnp.mean(h * h, axis=-1, keepdims=True)
    xn = (h * lax.rsqrt(ms + EPS)) * g_ref[...]
    xn_ref[...] = xn.astype(xn_ref.dtype)
    xh, xl = _split2(xn)
    wh, wl = _split2(wr_ref[...])
    both = jnp.dot(jnp.concatenate([xh, xl], axis=0), wh, preferred_element_type=F32)
    tok_major = both[:TM] + both[TM:] + jnp.dot(xh, wl, preferred_element_type=F32)
    logits = jnp.transpose(tok_major)[:N_EXPERTS] + br_ref[...]
    e_iota = lax.broadcasted_iota(jnp.int32, (N_EXPERTS, TM), 0)
    vals, idxs = [], []
    for _ in range(TOP_K):
        m = jnp.max(logits, axis=0, keepdims=True)
        sel = jnp.min(jnp.where(logits == m, e_iota, N_EXPERTS), axis=0, keepdims=True)
        vals.append(m)
        idxs.append(sel)
        logits = jnp.where(e_iota == sel, NEG_INF, logits)
    ex = [jnp.exp(v - vals[0]) for v in vals]
    den = ex[0] + ex[1] + ex[2] + ex[3]
    idx_ref[...] = jnp.concatenate(idxs, axis=0)
    gate_ref[...] = jnp.concatenate([e / den for e in ex], axis=0)


def _merge(oa, ob, x, w_out, norm_g, w_router_pad, b_router, tm):
    N, D = x.shape
    full = lambda a: pl.BlockSpec(a.shape, lambda i: (0,) * a.ndim)
    row = lambda w: pl.BlockSpec((tm, w), lambda i: (i, 0))
    col = pl.BlockSpec((TOP_K, tm), lambda i: (0, i))
    return pl.pallas_call(
        _merge_kernel,
        grid=(N // tm,),
        in_specs=[row(WIDTH_A), row(WIDTH_B), row(D), full(w_out), full(norm_g), full(w_router_pad),
                  full(b_router)],
        out_specs=[row(D), row(D), col, col],
        out_shape=[jax.ShapeDtypeStruct((N, D), F32),
                   jax.ShapeDtypeStruct((N, D), BF16),
                   jax.ShapeDtypeStruct((TOP_K, N), jnp.int32),
                   jax.ShapeDtypeStruct((TOP_K, N), F32)],
        compiler_params=_cparams(("parallel",)),
        name="merge_router",
    )(oa, ob, x, w_out, norm_g, w_router_pad, b_router)


MOE_ROWS = 512
TOKEN_TILE = 256
DMA_ROWS = 8
STAGE_ROWS = TOKEN_TILE * TOP_K


def _expert_row(col):
    r = lax.broadcasted_iota(jnp.int32, (N_EXPERTS, LANES), 0)
    c = lax.broadcasted_iota(jnp.int32, (N_EXPERTS, LANES), 1)
    return jnp.sum(jnp.where(r == c, col, 0.0), axis=0, keepdims=True)


def _expert_prefix(col):
    r = lax.broadcasted_iota(jnp.int32, (N_EXPERTS, LANES), 0)
    c = lax.broadcasted_iota(jnp.int32, (N_EXPERTS, LANES), 1)
    return jnp.sum(jnp.where(c < r, _expert_row(col), 0.0), axis=1, keepdims=True)


PIECE_SIZES = (DMA_ROWS, 4, 2, 1)
META_ROWS = 16


def _piece_list(count, src, dst, stride):
    first = _expert_prefix(count)
    f = lax.broadcasted_iota(jnp.int32, (N_EXPERTS, LANES), 1).astype(F32)
    owner = jnp.sum(jnp.where(first + count <= f, 1.0, 0.0), axis=0, keepdims=True)
    hit = lax.broadcasted_iota(jnp.int32, (N_EXPERTS, LANES), 0).astype(F32) == owner
    pick = lambda col: jnp.sum(jnp.where(hit, col, 0.0), axis=0, keepdims=True)
    j = f[0:1, :] - pick(first)
    return pick(src) + stride * j, pick(dst) + stride * j


def _route_kernel(idx_all_ref, idx_ref, pos_ref, meta_ref, be_ref, misc_ref, carry_ref, start_ref):
    i = pl.program_id(0)
    TT = idx_ref.shape[1]
    NBP = be_ref.shape[1]

    @pl.when(i == 0)
    def _():
        idx_all = idx_all_ref[...]
        e_all = lax.broadcasted_iota(jnp.int32, (N_EXPERTS, idx_all.shape[1]), 0)
        tot = jnp.zeros((N_EXPERTS, 1), F32)
        for k in range(TOP_K):
            tot = tot + jnp.sum(jnp.where(idx_all[k:k + 1, :] == e_all, 1.0, 0.0), axis=1, keepdims=True)
        padded = jnp.floor((tot + (MOE_ROWS - 1)) / MOE_ROWS) * MOE_ROWS
        start = _expert_prefix(padded)
        start_ref[...] = start
        carry_ref[...] = jnp.zeros_like(carry_ref)
        end = start + padded
        block_start = lax.broadcasted_iota(jnp.int32, (N_EXPERTS, NBP), 1).astype(F32) * MOE_ROWS
        be = jnp.sum(jnp.where(end <= block_start, 1.0, 0.0), axis=0, keepdims=True)
        be_ref[...] = jnp.minimum(be, N_EXPERTS - 1).astype(jnp.int32)
        n_used = jnp.broadcast_to(jnp.sum(padded, axis=0, keepdims=True) / MOE_ROWS, (1, LANES))
        zero = jnp.zeros((1, LANES), F32)
        misc_ref[...] = jnp.concatenate([n_used, _expert_row(start + tot)] + [zero] * 6,
                                        axis=0).astype(jnp.int32)

    idx = idx_ref[...]
    e_iota = lax.broadcasted_iota(jnp.int32, (N_EXPERTS, TT), 0)
    onehot = [idx[k:k + 1, :] == e_iota for k in range(TOP_K)]
    cnt = jnp.zeros((N_EXPERTS, TT), F32)
    for oh in onehot:
        cnt = cnt + jnp.where(oh, 1.0, 0.0)
    tile_tot = jnp.sum(cnt, axis=1, keepdims=True)
    earlier = (lax.broadcasted_iota(jnp.int32, (TT, TT), 0)
               < lax.broadcasted_iota(jnp.int32, (TT, TT), 1))
    before = jnp.dot(cnt.astype(BF16), jnp.where(earlier, 1.0, 0.0).astype(BF16),
                     preferred_element_type=F32)
    seg = _expert_prefix(tile_tot)
    where_staged = seg + before
    rows = [jnp.sum(jnp.where(oh, where_staged, 0.0), axis=0, keepdims=True) for oh in onehot]
    pos_ref[...] = jnp.concatenate(rows, axis=0).astype(jnp.int32)
    slot = start_ref[...] + carry_ref[...]
    whole = jnp.floor(tile_tot / DMA_ROWS)
    covered = whole * DMA_ROWS
    lists = list(_piece_list(whole, seg, slot, float(DMA_ROWS)))
    counts = [jnp.sum(whole, axis=0, keepdims=True)]
    for size in PIECE_SIZES[1:]:
        has = jnp.floor((tile_tot - covered) / size)
        lists += _piece_list(has, seg + covered, slot + covered, 0.0)
        counts.append(jnp.sum(has, axis=0, keepdims=True))
        covered = covered + has * size
    lane = lax.broadcasted_iota(jnp.int32, (1, LANES), 1)
    count_row = jnp.zeros((1, LANES), F32)
    for k, c in enumerate(counts):
        count_row = jnp.where(lane == k, c, count_row)
    zero = jnp.zeros((1, LANES), F32)
    meta_ref[...] = jnp.concatenate(lists + [count_row] + [zero] * (META_ROWS - len(lists) - 1),
                                    axis=0).astype(jnp.int32)
    carry_ref[...] += tile_tot


def _route(idx, n_blocks):
    _, N = idx.shape
    nbp = -(-n_blocks // LANES) * LANES
    tile = pl.BlockSpec((TOP_K, TOKEN_TILE), lambda i: (0, i))
    return pl.pallas_call(
        _route_kernel,
        grid=(N // TOKEN_TILE,),
        in_specs=[pl.BlockSpec((TOP_K, N), lambda i: (0, 0)), tile],
        out_specs=[tile,
                   pl.BlockSpec((META_ROWS, LANES), lambda i: (i, 0)),
                   pl.BlockSpec((1, nbp), lambda i: (0, 0)),
                   pl.BlockSpec((8, LANES), lambda i: (0, 0))],
        out_shape=[jax.ShapeDtypeStruct((TOP_K, N), jnp.int32),
                   jax.ShapeDtypeStruct((N // TOKEN_TILE * META_ROWS, LANES), jnp.int32),
                   jax.ShapeDtypeStruct((1, nbp), jnp.int32),
                   jax.ShapeDtypeStruct((8, LANES), jnp.int32)],
        scratch_shapes=[pltpu.VMEM((N_EXPERTS, 1), F32)] * 2,
        compiler_params=_cparams(("arbitrary",)),
        name="route",
    )(idx, idx)


def _rows(ref, first_row, n_rows):
    return ref.at[pl.ds(pl.multiple_of(first_row * CHUNKS, CHUNKS), n_rows * CHUNKS), :]


def _for_each_run_piece(meta_ref, fn):
    for k, size in enumerate(PIECE_SIZES):
        def body(j, c, k=k, size=size):
            fn(meta_ref[2 * k + 1, j], meta_ref[2 * k, j], size)
            return c

        lax.fori_loop(0, meta_ref[2 * len(PIECE_SIZES), k], body, 0)


def _wait_rows(n_rows, src_ref, dst_ref, sem):
    @pl.when(n_rows > 0)
    def _():
        n = n_rows * CHUNKS
        pltpu.make_async_copy(src_ref.at[pl.ds(0, n), :], dst_ref.at[pl.ds(0, n), :], sem).wait()


PAD_ROWS = MOE_ROWS


def _zero_padding(misc_ref, xs_ref, zeros_ref, sem, n_blocks):
    zeros_ref[...] = jnp.zeros_like(zeros_ref)

    def pad_copy(e):
        first = pl.multiple_of(misc_ref[1, e] * CHUNKS, CHUNKS)
        return pltpu.make_async_copy(zeros_ref, xs_ref.at[pl.ds(first, PAD_ROWS * CHUNKS), :], sem)

    def tail_copy(b):
        first = pl.multiple_of(b * (MOE_ROWS * CHUNKS), MOE_ROWS * CHUNKS)
        return pltpu.make_async_copy(zeros_ref.at[pl.ds(0, MOE_ROWS * CHUNKS), :],
                                     xs_ref.at[pl.ds(first, MOE_ROWS * CHUNKS), :], sem)

    def pad(e, c):
        pad_copy(e).start()
        pad_copy(e).wait()
        return c

    def tail(start_not_wait):
        def body(b, c):
            tail_copy(b).start() if start_not_wait else tail_copy(b).wait()
            return c

        lax.fori_loop(misc_ref[0, 0], n_blocks + 1, body, 0)

    lax.fori_loop(0, N_EXPERTS, pad, 0)
    tail(True)
    tail(False)


def _selection(pos_ref, fill_ref=None):
    TT = pos_ref.shape[1]
    p_iota = lax.broadcasted_iota(jnp.int32, (STAGE_ROWS, TT), 0)
    sel = jnp.zeros((STAGE_ROWS, TT), F32)
    for k in range(TOP_K):
        val = 1.0 if fill_ref is None else fill_ref[k:k + 1, :]
        sel = jnp.where(pos_ref[k:k + 1, :] == p_iota, val, sel)
    return sel.astype(BF16)


def _dispatch_kernel(misc_ref, meta_ref, pos_ref, *refs, first_tiles, n_tiles, n_blocks):
    x_refs = refs[:len(first_tiles)]
    xs_ref, stage_ref, zeros_ref, sems, pad_sem = refs[len(first_tiles):]
    i = pl.program_id(0)
    TT = pos_ref.shape[1]
    slot = i % 2

    @pl.when(i == 0)
    def _():
        _zero_padding(misc_ref, xs_ref, zeros_ref, pad_sem, n_blocks)

    x = x_refs[0][...]
    for t0, ref in zip(first_tiles[1:], x_refs[1:]):
        x = jnp.where(i >= t0, ref[...], x)
    staged = jnp.dot(_selection(pos_ref), x, preferred_element_type=F32)
    stage = stage_ref.at[slot]
    _store_chunked(stage, staged)
    _for_each_run_piece(meta_ref, lambda dst, src, n: pltpu.make_async_copy(
        _rows(stage, src, n), _rows(xs_ref, dst, n), sems.at[slot]).start())

    @pl.when(i > 0)
    def _():
        _wait_rows(TT * TOP_K, stage_ref.at[1 - slot], xs_ref, sems.at[1 - slot])

    @pl.when(i == n_tiles - 1)
    def _():
        _wait_rows(TT * TOP_K, stage, xs_ref, sems.at[slot])


def _dispatch(misc, meta, pos, xns, n_blocks):
    D = xns[0].shape[1]
    tiles = [x.shape[0] // TOKEN_TILE for x in xns]
    first_tiles = tuple(sum(tiles[:g]) for g in range(len(tiles)))

    def x_spec(t0, nt):
        return pl.BlockSpec((TOKEN_TILE, D), lambda i: (jnp.clip(i - t0, 0, nt - 1), 0))

    return pl.pallas_call(
        functools.partial(_dispatch_kernel, first_tiles=first_tiles, n_tiles=sum(tiles), n_blocks=n_blocks),
        grid=(sum(tiles),),
        in_specs=[pl.BlockSpec(memory_space=pltpu.SMEM),
                  pl.BlockSpec((META_ROWS, LANES), lambda i: (i, 0), memory_space=pltpu.SMEM),
                  pl.BlockSpec((TOP_K, TOKEN_TILE), lambda i: (0, i))]
                 + [x_spec(t0, nt) for t0, nt in zip(first_tiles, tiles)],
        out_specs=pl.BlockSpec(memory_space=pl.ANY),
        out_shape=jax.ShapeDtypeStruct(((n_blocks + 1) * MOE_ROWS * CHUNKS, LANES), F32),
        scratch_shapes=[pltpu.VMEM((2, STAGE_ROWS * CHUNKS, LANES), F32),
                        pltpu.VMEM((PAD_ROWS * CHUNKS, LANES), F32),
                        pltpu.SemaphoreType.DMA((2,)),
                        pltpu.SemaphoreType.DMA(())],
        compiler_params=_cparams(("arbitrary",)),
        name="dispatch",
    )(misc, meta, pos, *xns)


def _expert_kernel(be_ref, nused_ref, xs_ref, wgu_ref, bgu_ref, wd_ref, bd_ref, ys_ref,
                   wgu_bf, wd_bf):
    i = pl.program_id(0)
    D_FF = wd_ref.shape[0]
    new_expert = jnp.logical_or(i == 0, be_ref[i] != be_ref[jnp.maximum(i - 1, 0)])

    @pl.when(jnp.logical_and(i < nused_ref[0], new_expert))
    def _():
        wgu_bf[...] = wgu_ref[...].astype(BF16)
        wd_bf[...] = wd_ref[...].astype(BF16)

    @pl.when(i < nused_ref[0])
    def _():
        x = _load_chunked(xs_ref, MOE_ROWS).astype(BF16)
        hdn = jnp.dot(x, wgu_bf[...], preferred_element_type=F32) + bgu_ref[...]
        glu = jnp.minimum(hdn[:, :D_FF], SWIGLU_LIMIT)
        lin = jnp.clip(hdn[:, D_FF:], -SWIGLU_LIMIT, SWIGLU_LIMIT)
        act = glu * (1.0 / (1.0 + jnp.exp(-SWIGLU_ALPHA * glu))) * (lin + 1.0)
        y = jnp.dot(act.astype(BF16), wd_bf[...], preferred_element_type=F32) + bd_ref[...]
        _store_chunked(ys_ref, y)

    @pl.when(i >= nused_ref[0])
    def _():
        ys_ref[...] = jnp.zeros_like(ys_ref)


def _experts(block_expert, n_used, xs, w_gate_up, b_gate_up, w_down, b_down, n_blocks):
    E, D, F2 = w_gate_up.shape
    D_FF = w_down.shape[1]
    rows = pl.BlockSpec((MOE_ROWS * CHUNKS, LANES), lambda i, be, nu: (i, 0))
    rows_in = pl.BlockSpec((MOE_ROWS * CHUNKS, LANES), lambda i, be, nu: (jnp.minimum(i, nu[0] - 1), 0))
    grid_spec = pltpu.PrefetchScalarGridSpec(
        num_scalar_prefetch=2,
        grid=(n_blocks,),
        in_specs=[rows_in,
                  pl.BlockSpec((None, D, F2), lambda i, be, nu: (be[i], 0, 0)),
                  pl.BlockSpec((None, 1, F2), lambda i, be, nu: (be[i], 0, 0)),
                  pl.BlockSpec((None, D_FF, D), lambda i, be, nu: (be[i], 0, 0)),
                  pl.BlockSpec((None, 1, D), lambda i, be, nu: (be[i], 0, 0))],
        out_specs=rows,
        scratch_shapes=[pltpu.VMEM((D, F2), BF16), pltpu.VMEM((D_FF, D), BF16)],
    )
    return pl.pallas_call(
        _expert_kernel,
        grid_spec=grid_spec,
        out_shape=jax.ShapeDtypeStruct((n_blocks * MOE_ROWS * CHUNKS, LANES), F32),
        compiler_params=_cparams(("arbitrary",)),
        name="experts",
    )(block_expert, n_used, xs, w_gate_up, b_gate_up.reshape(E, 1, F2), w_down, b_down.reshape(E, 1, D))


def _combine_kernel(meta_ref, next_meta_ref, pos_ref, gate_ref, h_ref, g_ref, ys_ref, y_ref, stage_ref, sems,
                    *, n_tiles):
    i = pl.program_id(0)
    slot = i % 2

    def fetch(meta, s):
        _for_each_run_piece(meta, lambda src, dst, n: pltpu.make_async_copy(
            _rows(ys_ref, src, n), _rows(stage_ref.at[s], dst, n), sems.at[s]).start())

    @pl.when(i == 0)
    def _():
        fetch(meta_ref, 0)

    if n_tiles > 1:
        @pl.when(i + 1 < n_tiles)
        def _():
            fetch(next_meta_ref, 1 - slot)

    weights = _selection(pos_ref, gate_ref)
    _wait_rows(STAGE_ROWS, ys_ref, stage_ref.at[slot], sems.at[slot])
    staged = _load_chunked(stage_ref.at[slot], STAGE_ROWS).astype(BF16)
    moe = lax.dot_general(weights, staged, (((0,), (0,)), ((), ())), preferred_element_type=F32)
    hf = h_ref[...] + moe
    ms = jnp.mean(hf * hf, axis=-1, keepdims=True)
    y_ref[...] = (hf * lax.rsqrt(ms + EPS)) * g_ref[...]


def _combine(meta, pos, gates, h, norm_g, ys, tile0):
    n, D = h.shape
    n_tiles = n // TOKEN_TILE
    return pl.pallas_call(
        functools.partial(_combine_kernel, n_tiles=n_tiles),
        grid=(n_tiles,),
        in_specs=[pl.BlockSpec((META_ROWS, LANES), lambda i: (tile0 + i, 0), memory_space=pltpu.SMEM),
                  pl.BlockSpec((META_ROWS, LANES), lambda i: (tile0 + jnp.minimum(i + 1, n_tiles - 1), 0),
                               memory_space=pltpu.SMEM),
                  pl.BlockSpec((TOP_K, TOKEN_TILE), lambda i: (0, tile0 + i)),
                  pl.BlockSpec((TOP_K, TOKEN_TILE), lambda i: (0, i)),
                  pl.BlockSpec((TOKEN_TILE, D), lambda i: (i, 0)),
                  pl.BlockSpec((1, D), lambda i: (0, 0)),
                  pl.BlockSpec(memory_space=pl.ANY)],
        out_specs=pl.BlockSpec((TOKEN_TILE, D), lambda i: (i, 0)),
        out_shape=jax.ShapeDtypeStruct((n, D), F32),
        scratch_shapes=[pltpu.VMEM((2, STAGE_ROWS * CHUNKS, LANES), F32), pltpu.SemaphoreType.DMA((2,))],
        compiler_params=_cparams(("arbitrary",)),
        name="combine",
    )(meta, meta, pos, gates, h, norm_g, ys)


def _prep_weights(w_in, w_alpha):
    w_main = w_in[:, :PROJ_MAIN].astype(BF16)
    w_lr = jnp.pad(w_in[:, PROJ_MAIN:], ((0, 0), (0, LANES - GATE_RANK))).astype(BF16)
    w_al = jnp.pad(w_alpha, ((0, LANES - GATE_RANK), (0, 0))).astype(BF16)
    return w_main, w_lr, w_al


def kernel(x_prompt, x_sample, cache_swa_k, cache_swa_v, state_gla, norm_mix_g, w_in, w_alpha, b_alpha, gla_norm_g, w_out, norm_ffn_g, w_router, b_router, w_gate_up, b_gate_up, w_down, b_down, norm_final_g):
    B, S, D = x_prompt.shape
    Bs, Ts, _ = x_sample.shape
    assert w_in.shape[0] == 1, "single-layer trunk"
    l = 0
    R = cache_swa_k.shape[2]
    rows_p = min(DILATIONS[-1] * KEYS_PER_CONFIG, S)
    w_main, w_lr, w_al = _prep_weights(w_in[l], w_alpha[l])
    g_mix = norm_mix_g[l][None]
    b_al = b_alpha[l][None]
    g_gla = gla_norm_g[l][None]

    pos_p = jnp.arange(S, dtype=jnp.int32)
    qa, ka, va, qb, kb, vb, zg, gb, k_tail, v_tail = _project(x_prompt, pos_p, g_mix, w_main, w_lr, w_al, b_al,
                                                              PROJ_TILE, tail_rows=rows_p)
    oa_p = _prompt_attention(qa, ka, va)
    ob_p, st_p = _gla(qb, kb, gb, vb, zg, jnp.zeros((B, N_HEADS_B, DK_B, DV_B), F32), g_gla,
                      GLA_TILE, GLA_CHUNK)
    k_prompt = k_tail.reshape(1, B, rows_p, N_HEADS_A, HEAD_DIM_A)
    v_prompt = v_tail.reshape(1, B, rows_p, N_HEADS_A, HEAD_DIM_A)

    pos_s = PAST_LEN + (jnp.arange(Bs * Ts, dtype=jnp.int32) % Ts)
    proj_s = _project(x_sample.reshape(1, Bs * Ts, D), pos_s, g_mix, w_main, w_lr, w_al, b_al, Bs * Ts)
    qa_s, ka_s, va_s, qb_s, kb_s, vb_s, zg_s, gb_s = [t.reshape(Bs, Ts, -1) for t in proj_s]
    oa_s, k_sample, v_sample = _sample_attention(qa_s, ka_s, va_s,
                                                 cache_swa_k[l].reshape(Bs, R, WIDTH_A),
                                                 cache_swa_v[l].reshape(Bs, R, WIDTH_A))
    ob_s, st_s = _gla(qb_s, kb_s, gb_s, vb_s, zg_s, state_gla[l], g_gla, Ts, Ts)

    w_out_bf = w_out[l].astype(BF16)
    g_ffn = norm_ffn_g[l][None]
    w_router_pad = jnp.pad(w_router[l], ((0, 0), (0, LANES - N_EXPERTS)))
    b_router_c = b_router[l][:, None]
    Np, Ns = B * S, Bs * Ts
    h_p, xn_p, idx_p, gate_p = _merge(oa_p.reshape(Np, WIDTH_A), ob_p.reshape(Np, WIDTH_B),
                                      x_prompt.reshape(Np, D), w_out_bf, g_ffn, w_router_pad, b_router_c,
                                      MERGE_TILE)
    h_s, xn_s, idx_s, gate_s = _merge(oa_s.reshape(Ns, WIDTH_A), ob_s.reshape(Ns, WIDTH_B),
                                      x_sample.reshape(Ns, D), w_out_bf, g_ffn, w_router_pad, b_router_c,
                                      Ns)

    y_p, y_s = _moe([(xn_p, idx_p, gate_p, h_p), (xn_s, idx_s, gate_s, h_s)],
                    w_gate_up[l], b_gate_up[l], w_down[l], b_down[l], norm_final_g[None])
    return (y_p.reshape(B, S, D), y_s.reshape(Bs, Ts, D), k_prompt, v_prompt, st_p[None],
            k_sample.reshape(1, Bs, R, N_HEADS_A, HEAD_DIM_A),
            v_sample.reshape(1, Bs, R, N_HEADS_A, HEAD_DIM_A), st_s[None])


PAST_LEN = 16384
PROJ_TILE = 512
MERGE_TILE = 1024
GLA_TILE = 1024
GLA_CHUNK = 64


def _moe(groups, w_gate_up, b_gate_up, w_down, b_down, g_final):
    sizes = [g[3].shape[0] for g in groups]
    N = sum(sizes)
    assert all(n % TOKEN_TILE == 0 for n in sizes)
    n_blocks = -(-(N * TOP_K + N_EXPERTS * (MOE_ROWS - 1)) // MOE_ROWS)
    idx = jnp.concatenate([g[1] for g in groups], axis=1)
    pos, meta, block_expert, misc = _route(idx, n_blocks)
    first_tile = [sum(sizes[:i]) // TOKEN_TILE for i in range(len(sizes))]
    xs = _dispatch(misc, meta, pos, [g[0] for g in groups], n_blocks)
    ys = _experts(block_expert[0], misc[0, :1], xs, w_gate_up, b_gate_up, w_down, b_down, n_blocks)
    return [_combine(meta, pos, gates, h, g_final, ys, t0)
            for (_, _, gates, h), t0 in zip(groups, first_tile)]
```

```python
import functools

import jax
import jax.numpy as jnp
from jax import lax
from jax.experimental import pallas as pl
from jax.experimental.pallas import tpu as pltpu

F32 = jnp.float32
BF16 = jnp.bfloat16

N_HEADS_A = 8
HEAD_DIM_A = 64
WIDTH_A = N_HEADS_A * HEAD_DIM_A
N_HEADS_B = 4
DK_B = 64
DV_B = 128
QK_B = N_HEADS_B * DK_B
WIDTH_B = N_HEADS_B * DV_B
GATE_RANK = 16
GATE_LOGIT_NORM = 16.0
DILATIONS = (1, 4, 16)
KEYS_PER_CONFIG = 128
ROPE_THETA = 10000.0
N_EXPERTS = 32
TOP_K = 4
SWIGLU_ALPHA = 1.702
SWIGLU_LIMIT = 7.0
EPS = 1e-6

LANES = 128
VMEM_LIMIT = 56 * 1024 * 1024


def _cparams(sem, vmem=VMEM_LIMIT):
    return pltpu.CompilerParams(dimension_semantics=sem, vmem_limit_bytes=vmem)


PROJ_MAIN = 3 * WIDTH_A + 2 * QK_B + 2 * WIDTH_B


def _rope_tables(pos):
    half = HEAD_DIM_A // 2
    inv = ROPE_THETA ** (-jnp.arange(half, dtype=F32) / half)
    ang = pos.astype(F32)[:, None] * inv[None, :]
    cos = jnp.cos(ang)
    sin = jnp.sin(ang)
    cos_t = jnp.concatenate([cos, cos, cos, cos], axis=-1)
    sin_t = jnp.concatenate([-sin, sin, -sin, sin], axis=-1)
    return cos_t, sin_t


def _rope_block(t, cos, sin, first_half):
    half = HEAD_DIM_A // 2
    partner = jnp.where(first_half, pltpu.roll(t, LANES - half, 1), pltpu.roll(t, half, 1))
    return t * cos + partner * sin


def _proj_kernel(x_ref, g_ref, w_ref, wlr_ref, wa_ref, ba_ref, cos_ref, sin_ref,
                 qa_ref, ka_ref, va_ref, qb_ref, kb_ref, vb_ref, zg_ref, gb_ref, *tail_refs, first_tail_tile):
    x = x_ref[...]
    ms = jnp.mean(x * x, axis=-1, keepdims=True)
    xn = ((x * lax.rsqrt(ms + EPS)) * g_ref[...]).astype(BF16)

    def cols(lo, hi):
        return jnp.dot(xn, w_ref[:, lo:hi], preferred_element_type=F32)

    cos = cos_ref[...]
    sin = sin_ref[...]
    lane = lax.broadcasted_iota(jnp.int32, cos.shape, 1)
    first_half = (lane % HEAD_DIM_A) < (HEAD_DIM_A // 2)
    q = cols(0, WIDTH_A)
    k = cols(WIDTH_A, 2 * WIDTH_A)
    for j in range(WIDTH_A // LANES):
        sl = slice(j * LANES, (j + 1) * LANES)
        qa_ref[:, sl] = _rope_block(q[:, sl], cos, sin, first_half) * (HEAD_DIM_A ** -0.5)
        ka_ref[:, sl] = _rope_block(k[:, sl], cos, sin, first_half)
    o = 2 * WIDTH_A
    va_ref[...] = cols(o, o + WIDTH_A)
    o += WIDTH_A
    qb_ref[...] = cols(o, o + QK_B) * (DK_B ** -0.5)
    o += QK_B
    kb_ref[...] = cols(o, o + QK_B)
    o += QK_B
    vb_ref[...] = cols(o, o + WIDTH_B).astype(vb_ref.dtype)
    o += WIDTH_B
    zg_ref[...] = cols(o, o + WIDTH_B)
    lr = jnp.dot(xn, wlr_ref[...], preferred_element_type=F32)
    z = jnp.dot(lr.astype(BF16), wa_ref[...], preferred_element_type=F32) + ba_ref[...]
    logsig = jnp.minimum(z, 0.0) - jnp.log(1.0 + jnp.exp(-jnp.abs(z)))
    gb_ref[...] = logsig / GATE_LOGIT_NORM
    if tail_refs:
        @pl.when(pl.program_id(1) >= first_tail_tile)
        def _():
            tail_refs[0][...] = ka_ref[...]
            tail_refs[1][...] = va_ref[...]


def _project(x, pos, norm_g, w_main, w_lr, w_alpha, b_alpha, tm, tail_rows=0):
    B, T, D = x.shape
    assert tail_rows % tm == 0
    cos_t, sin_t = _rope_tables(pos)
    grid = (B, T // tm)
    first_tail_tile = (T - tail_rows) // tm
    row = lambda w: pl.BlockSpec((None, tm, w), lambda b, i: (b, i, 0))
    tail = pl.BlockSpec((None, tm, WIDTH_A), lambda b, i: (b, jnp.maximum(i - first_tail_tile, 0), 0))
    full = lambda a: pl.BlockSpec(a.shape, lambda b, i: (0,) * a.ndim)
    tab = pl.BlockSpec((tm, LANES), lambda b, i: (i, 0))
    widths = (WIDTH_A, WIDTH_A, WIDTH_A, QK_B, QK_B, WIDTH_B, WIDTH_B, QK_B)
    dtypes = (F32, F32, F32, F32, F32, BF16, F32, F32)
    n_tail = 2 if tail_rows else 0
    return pl.pallas_call(
        functools.partial(_proj_kernel, first_tail_tile=first_tail_tile),
        grid=grid,
        in_specs=[row(D), full(norm_g), full(w_main), full(w_lr), full(w_alpha), full(b_alpha), tab, tab],
        out_specs=[row(w) for w in widths] + [tail] * n_tail,
        out_shape=[jax.ShapeDtypeStruct((B, T, w), dt) for w, dt in zip(widths, dtypes)]
                  + [jax.ShapeDtypeStruct((B, tail_rows, WIDTH_A), F32)] * n_tail,
        compiler_params=_cparams(("parallel", "arbitrary")),
        name="proj",
    )(x, norm_g, w_main, w_lr, w_alpha, b_alpha, cos_t, sin_t)


Q_BLOCK = 128
NEG_INF = float("-inf")


def _attn_block(q, kb, v1, mask, head0, state):
    QB = q.shape[0]
    q2 = jnp.concatenate([jnp.where(head0, q, 0.0), jnp.where(head0, 0.0, q)], axis=0).astype(BF16)
    s = lax.dot_general(q2, kb, (((1,), (1,)), ((), ())), preferred_element_type=F32)
    s = jnp.where(mask, s, NEG_INF)
    if state is None:
        m2 = jnp.max(s, axis=1, keepdims=True)
    else:
        prev = jnp.concatenate([jnp.where(head0, state[0], NEG_INF),
                                jnp.where(head0, NEG_INF, state[0])], axis=0)
        m2 = jnp.max(jnp.concatenate([s, prev], axis=1), axis=1, keepdims=True)
    p = jnp.exp(s - m2)
    pv = jnp.dot(p.astype(BF16), v1, preferred_element_type=F32)
    m_full = jnp.where(head0, m2[:QB], m2[QB:])
    l_full = jnp.where(head0, pv[:QB, LANES:], pv[QB:, LANES:])
    pv_full = jnp.where(head0, pv[:QB, :LANES], pv[QB:, :LANES])
    if state is None:
        return m_full, l_full, pv_full
    a = jnp.exp(state[0] - m_full)
    return m_full, a * state[1] + l_full, a * state[2] + pv_full


def _attn_kernel(q_ref, k_ref, v_ref, o_ref, m_ref, l_ref, acc_ref):
    S = q_ref.shape[0]
    QB = Q_BLOCK
    lane = lax.broadcasted_iota(jnp.int32, (QB, LANES), 1)
    head0 = lane < HEAD_DIM_A
    qq = lax.broadcasted_iota(jnp.int32, (2 * QB, 2 * QB), 0) % QB
    kk = lax.broadcasted_iota(jnp.int32, (2 * QB, 2 * QB), 1)
    band = jnp.logical_and(kk >= qq, kk - qq <= KEYS_PER_CONFIG)
    causal = (lax.broadcasted_iota(jnp.int32, (2 * QB, QB), 1)
              <= lax.broadcasted_iota(jnp.int32, (2 * QB, QB), 0) % QB)

    GROUP = 8

    for ci, dil in enumerate(sorted(DILATIONS, reverse=True)):
        nblk = S // (dil * QB)
        assert nblk % GROUP == 0 or GROUP % nblk == 0

        def rows(start, n, dil=dil):
            return pl.ds(start, n) if dil == 1 else pl.ds(start, n, stride=dil)

        def do_group(blocks, ci=ci, dil=dil, rows=rows):
            chunks = {}

            def kv_chunk(r, rkey, jbase, c):
                if (rkey, c) not in chunks:
                    ks = rows(r + dil * QB * (jbase + c), QB)
                    v = v_ref[ks, :].astype(BF16)
                    chunks[(rkey, c)] = (k_ref[ks, :].astype(BF16),
                                         jnp.concatenate([v, jnp.ones(v.shape, BF16)], axis=1))
                return chunks[(rkey, c)]

            loaded = []
            for r, rkey, jbase, joff in blocks:
                first = isinstance(jbase, int) and jbase + joff == 0
                qs = rows(r + dil * QB * (jbase + joff), QB)
                parts = [kv_chunk(r, rkey, jbase, joff)]
                if not first:
                    parts.insert(0, kv_chunk(r, rkey, jbase, joff - 1))
                kb = jnp.concatenate([p[0] for p in parts], axis=0)
                v1 = jnp.concatenate([p[1] for p in parts], axis=0)
                state = None if ci == 0 else (m_ref[qs, :], l_ref[qs, :], acc_ref[qs, :])
                loaded.append((qs, q_ref[qs, :], kb, v1, causal if first else band, state))
            results = [_attn_block(q, kb, v1, mask, head0, state) for _, q, kb, v1, mask, state in loaded]
            for (qs, *_), (m, l, acc) in zip(loaded, results):
                m_ref[qs, :] = m
                l_ref[qs, :] = l
                acc_ref[qs, :] = acc

        if nblk >= GROUP:
            def residue(r, carry, nblk=nblk, do_group=do_group):
                do_group([(r, 0, 0, j) for j in range(GROUP)])

                def rest(g, c):
                    do_group([(r, 0, GROUP * g, u) for u in range(GROUP)])
                    return c

                return lax.fori_loop(1, nblk // GROUP, rest, carry)

            lax.fori_loop(0, dil, residue, 0)
        else:
            per = 2 * GROUP // nblk

            def residues(g, carry, nblk=nblk, per=per, do_group=do_group):
                do_group([(g * per + i, i, 0, j) for j in range(nblk) for i in range(per)])
                return carry

            lax.fori_loop(0, dil // per, residues, 0)

    def finish(i, c):
        rs = pl.ds(pl.multiple_of(i * QB, QB), QB)
        o_ref[rs, :] = (acc_ref[rs, :] / l_ref[rs, :]).astype(o_ref.dtype)
        return c

    lax.fori_loop(0, S // QB, finish, 0, unroll=4)


def _prompt_attention(qa, ka, va):
    B, S, W = qa.shape
    spec = pl.BlockSpec((None, S, LANES), lambda b, hp: (b, 0, hp))
    return pl.pallas_call(
        _attn_kernel,
        grid=(B, W // LANES),
        in_specs=[spec, spec, spec],
        out_specs=spec,
        out_shape=jax.ShapeDtypeStruct((B, S, W), BF16),
        scratch_shapes=[pltpu.VMEM((S, LANES), F32)] * 3,
        compiler_params=_cparams(("parallel", "parallel")),
        name="prompt_attn",
    )(qa, ka, va)


def _gla_kernel(q_ref, k_ref, g_ref, v_ref, z_ref, s0_ref, ng_ref, o_ref, sfin_ref, st_ref, *, chunk):
    C = chunk
    TS = q_ref.shape[0]
    n_pairs = N_HEADS_B // 2
    PW = 2 * DV_B
    t_idx = pl.program_id(1)

    lane_k = lax.broadcasted_iota(jnp.int32, (C, LANES), 1)
    head0 = lane_k < DK_B
    row_k = lax.broadcasted_iota(jnp.int32, (C, LANES), 0)
    tri2 = (lax.broadcasted_iota(jnp.int32, (2 * C, C), 1)
            <= lax.broadcasted_iota(jnp.int32, (2 * C, C), 0) % C)
    value_head0 = lax.broadcasted_iota(jnp.int32, (C, PW), 1) < DV_B
    bd_mask = ((lax.broadcasted_iota(jnp.int32, (PW, LANES), 0) // DV_B)
               == (lax.broadcasted_iota(jnp.int32, (PW, LANES), 1) // DK_B))

    def prefix_rows(x):
        shift = 1
        while shift < C:
            x = x + jnp.where(row_k >= shift, pltpu.roll(x, shift, 0), 0.0)
            shift *= 2
        return x

    @pl.when(t_idx == 0)
    def _():
        for p in range(n_pairs):
            for h in range(2):
                blk = jnp.transpose(s0_ref[2 * p + h])
                pad = jnp.zeros((DV_B, DK_B), F32)
                row = jnp.concatenate([blk, pad] if h == 0 else [pad, blk], axis=1)
                st_ref[p, h * DV_B:(h + 1) * DV_B, :] = row

    def chunk_body(c, carry):
        rs = pl.ds(pl.multiple_of(c * C, C), C)
        for p in range(n_pairs):
            kl = slice(p * LANES, (p + 1) * LANES)
            vl = slice(p * PW, (p + 1) * PW)
            q = q_ref[rs, kl]
            k = k_ref[rs, kl]
            g = g_ref[rs, kl]
            v = v_ref[rs, vl]
            b = prefix_rows(g)
            b_last = b[C - 1:C, :]
            b_mid = b[C // 2 - 1:C // 2, :] if C > 1 else b_last
            qe = q * jnp.exp(b - b_mid)
            ke = (k * jnp.exp(b_mid - b)).astype(BF16)
            st = st_ref[p]
            q_in = (q * jnp.exp(b)).astype(BF16)
            nt = (((1,), (1,)), ((), ()))
            o = lax.dot_general(q_in, st.astype(BF16), nt, preferred_element_type=F32)
            q2 = jnp.concatenate([jnp.where(head0, qe, 0.0), jnp.where(head0, 0.0, qe)], axis=0)
            a2 = lax.dot_general(q2.astype(BF16), ke, nt, preferred_element_type=F32)
            a2 = jnp.where(tri2, a2, 0.0).astype(BF16)
            av = jnp.dot(a2, v, preferred_element_type=F32)
            o = o + jnp.where(value_head0, av[:C], av[C:])
            k_dec = (k * jnp.exp(b_last - b)).astype(BF16)
            upd = lax.dot_general(v, k_dec, (((0,), (0,)), ((), ())), preferred_element_type=F32)
            st_ref[p] = jnp.exp(b_last) * st + jnp.where(bd_mask, upd, 0.0)
            for h in range(2):
                oh = o[:, h * DV_B:(h + 1) * DV_B]
                hl = slice((2 * p + h) * DV_B, (2 * p + h + 1) * DV_B)
                ms = jnp.mean(oh * oh, axis=-1, keepdims=True)
                z = z_ref[rs, hl]
                gated = (oh * lax.rsqrt(ms + EPS)) * ng_ref[:, hl] * (z / (1.0 + jnp.exp(-z)))
                o_ref[rs, hl] = gated.astype(o_ref.dtype)
        return carry

    lax.fori_loop(0, TS // C, chunk_body, 0, unroll=4 if (TS // C) % 4 == 0 else 1)

    @pl.when(t_idx == pl.num_programs(1) - 1)
    def _():
        for p in range(n_pairs):
            for h in range(2):
                blk = st_ref[p, h * DV_B:(h + 1) * DV_B, h * DK_B:(h + 1) * DK_B]
                sfin_ref[2 * p + h] = jnp.transpose(blk)


def _gla(qb, kb, gb, vb, zg, state0, norm_g, ts, chunk):
    B, T, _ = qb.shape
    row = lambda w: pl.BlockSpec((None, ts, w), lambda b, i: (b, i, 0))
    st_spec = pl.BlockSpec((None, N_HEADS_B, DK_B, DV_B), lambda b, i: (b, 0, 0, 0))
    return pl.pallas_call(
        functools.partial(_gla_kernel, chunk=chunk),
        grid=(B, T // ts),
        in_specs=[row(QK_B), row(QK_B), row(QK_B), row(WIDTH_B), row(WIDTH_B), st_spec,
                  pl.BlockSpec((1, WIDTH_B), lambda b, i: (0, 0))],
        out_specs=[row(WIDTH_B), st_spec],
        out_shape=[jax.ShapeDtypeStruct((B, T, WIDTH_B), BF16),
                   jax.ShapeDtypeStruct((B, N_HEADS_B, DK_B, DV_B), F32)],
        scratch_shapes=[pltpu.VMEM((N_HEADS_B // 2, 2 * DV_B, LANES), F32)],
        compiler_params=_cparams(("parallel", "arbitrary")),
        name="gla",
    )(qb, kb, gb, vb, zg, state0, norm_g)


def _sample_attn_kernel(q_ref, kn_ref, vn_ref, kc_ref, vc_ref, o_ref, ko_ref, vo_ref):
    T = q_ref.shape[0]
    R = kc_ref.shape[0]
    W = q_ref.shape[1]
    HT = N_HEADS_A * T
    q = q_ref[...]
    qx = jnp.concatenate([q] * N_HEADS_A, axis=0)
    own = ((lax.broadcasted_iota(jnp.int32, (HT, W), 0) // T)
           == (lax.broadcasted_iota(jnp.int32, (HT, W), 1) // HEAD_DIM_A))
    qx = jnp.where(own, qx, 0.0).astype(BF16)
    pad = jnp.zeros((LANES - T, W), F32)
    kn = jnp.concatenate([kn_ref[...], pad], axis=0).astype(BF16)
    vn = jnp.concatenate([vn_ref[...], pad], axis=0).astype(BF16)
    nt = (((1,), (1,)), ((), ()))
    s_c = lax.dot_general(qx, kc_ref[...].astype(BF16), nt, preferred_element_type=F32)
    s_n = lax.dot_general(qx, kn, nt, preferred_element_type=F32)

    def multiplicity(n_cols, first_row):
        t = lax.broadcasted_iota(jnp.int32, (HT, n_cols), 0) % T
        j = lax.broadcasted_iota(jnp.int32, (HT, n_cols), 1) + first_row
        delta = R + t - j
        cnt = jnp.zeros((HT, n_cols), F32)
        for dil in DILATIONS:
            hit = (delta >= 0) & (delta <= dil * KEYS_PER_CONFIG) & (delta % dil == 0)
            cnt = cnt + jnp.where(hit, 1.0, 0.0)
        return cnt

    cnt_c = multiplicity(R, 0)
    cnt_n = multiplicity(LANES, R)
    s_c = jnp.where(cnt_c > 0.0, s_c, NEG_INF)
    s_n = jnp.where(cnt_n > 0.0, s_n, NEG_INF)
    m = jnp.maximum(jnp.max(s_c, axis=1, keepdims=True), jnp.max(s_n, axis=1, keepdims=True))
    p_c = cnt_c * jnp.exp(s_c - m)
    p_n = cnt_n * jnp.exp(s_n - m)
    den = jnp.sum(p_c, axis=1, keepdims=True) + jnp.sum(p_n, axis=1, keepdims=True)
    full = (jnp.dot(p_c.astype(BF16), vc_ref[...].astype(BF16), preferred_element_type=F32)
            + jnp.dot(p_n.astype(BF16), vn, preferred_element_type=F32)) / den
    full = jnp.where(own, full, 0.0)
    out = full[0:T, :]
    for h in range(1, N_HEADS_A):
        out = out + full[h * T:(h + 1) * T, :]
    o_ref[...] = out.astype(o_ref.dtype)
    ko_ref[0:R - T, :] = kc_ref[T:R, :]
    ko_ref[R - T:R, :] = kn_ref[...]
    vo_ref[0:R - T, :] = vc_ref[T:R, :]
    vo_ref[R - T:R, :] = vn_ref[...]


def _sample_attention(qa, ka, va, cache_k, cache_v):
    B, T, W = qa.shape
    R = cache_k.shape[1]
    assert R >= DILATIONS[-1] * KEYS_PER_CONFIG and T % 8 == 0 and T <= LANES
    new = pl.BlockSpec((None, T, W), lambda b: (b, 0, 0))
    cache = pl.BlockSpec((None, R, W), lambda b: (b, 0, 0))
    return pl.pallas_call(
        _sample_attn_kernel,
        grid=(B,),
        in_specs=[new, new, new, cache, cache],
        out_specs=[new, cache, cache],
        out_shape=[jax.ShapeDtypeStruct((B, T, W), BF16),
                   jax.ShapeDtypeStruct((B, R, W), F32),
                   jax.ShapeDtypeStruct((B, R, W), F32)],
        compiler_params=_cparams(("parallel",)),
        name="sample_attn",
    )(qa, ka, va, cache_k, cache_v)


CHUNKS = 8


def _store_chunked(ref, val):
    n = val.shape[0]
    for s in range(CHUNKS):
        ref[pl.ds(s, n, stride=CHUNKS), :] = val[:, s * LANES:(s + 1) * LANES]


def _load_chunked(ref, n):
    return jnp.concatenate([ref[pl.ds(s, n, stride=CHUNKS), :] for s in range(CHUNKS)], axis=1)


def _split2(x):
    hi = x.astype(BF16)
    return hi, (x - hi.astype(F32)).astype(BF16)


def _merge_kernel(oa_ref, ob_ref, x_ref, wo_ref, g_ref, wr_ref, br_ref,
                  h_ref, xn_ref, idx_ref, gate_ref):
    TM = x_ref.shape[0]
    mixed = (jnp.dot(oa_ref[...], wo_ref[0:WIDTH_A, :], preferred_element_type=F32)
             + jnp.dot(ob_ref[...], wo_ref[WIDTH_A:, :], preferred_element_type=F32))
    h = x_ref[...] + mixed
    h_ref[...] = h
    ms = jnp.mean(h * h, axis=-1, keepdims=True)
    xn = (h * lax.rsqrt(ms + EPS)) * g_ref[...]
    xn_ref[...] = xn.astype(xn_ref.dtype)
    xh, xl = _split2(xn)
    wh, wl = _split2(wr_ref[...])
    both = jnp.dot(jnp.concatenate([xh, xl], axis=0), wh, preferred_element_type=F32)
    tok_major = both[:TM] + both[TM:] + jnp.dot(xh, wl, preferred_element_type=F32)
    logits = jnp.transpose(tok_major)[:N_EXPERTS] + br_ref[...]
    e_iota = lax.broadcasted_iota(jnp.int32, (N_EXPERTS, TM), 0)
    vals, idxs = [], []
    for _ in range(TOP_K):
        m = jnp.max(logits, axis=0, keepdims=True)
        sel = jnp.min(jnp.where(logits == m, e_iota, N_EXPERTS), axis=0, keepdims=True)
        vals.append(m)
        idxs.append(sel)
        logits = jnp.where(e_iota == sel, NEG_INF, logits)
    ex = [jnp.exp(v - vals[0]) for v in vals]
    den = ex[0] + ex[1] + ex[2] + ex[3]
    idx_ref[...] = jnp.concatenate(idxs, axis=0)
    gate_ref[...] = jnp.concatenate([e / den for e in ex], axis=0)


def _merge(oa, ob, x, w_out, norm_g, w_router_pad, b_router, tm):
    N, D = x.shape
    full = lambda a: pl.BlockSpec(a.shape, lambda i: (0,) * a.ndim)
    row = lambda w: pl.BlockSpec((tm, w), lambda i: (i, 0))
    col = pl.BlockSpec((TOP_K, tm), lambda i: (0, i))
    return pl.pallas_call(
        _merge_kernel,
        grid=(N // tm,),
        in_specs=[row(WIDTH_A), row(WIDTH_B), row(D), full(w_out), full(norm_g), full(w_router_pad),
                  full(b_router)],
        out_specs=[row(D), row(D), col, col],
        out_shape=[jax.ShapeDtypeStruct((N, D), F32),
                   jax.ShapeDtypeStruct((N, D), BF16),
                   jax.ShapeDtypeStruct((TOP_K, N), jnp.int32),
                   jax.ShapeDtypeStruct((TOP_K, N), F32)],
        compiler_params=_cparams(("parallel",)),
        name="merge_router",
    )(oa, ob, x, w_out, norm_g, w_router_pad, b_router)


MOE_ROWS = 512
TOKEN_TILE = 256
DMA_ROWS = 16
STAGE_ROWS = TOKEN_TILE * TOP_K


def _expert_row(col):
    r = lax.broadcasted_iota(jnp.int32, (N_EXPERTS, LANES), 0)
    c = lax.broadcasted_iota(jnp.int32, (N_EXPERTS, LANES), 1)
    return jnp.sum(jnp.where(r == c, col, 0.0), axis=0, keepdims=True)


def _expert_prefix(col):
    r = lax.broadcasted_iota(jnp.int32, (N_EXPERTS, LANES), 0)
    c = lax.broadcasted_iota(jnp.int32, (N_EXPERTS, LANES), 1)
    return jnp.sum(jnp.where(c < r, _expert_row(col), 0.0), axis=1, keepdims=True)


PIECE_SIZES = (DMA_ROWS, 8, 4, 2, 1)
META_ROWS = 16


def _piece_list(count, src, dst, stride):
    first = _expert_prefix(count)
    f = lax.broadcasted_iota(jnp.int32, (N_EXPERTS, LANES), 1).astype(F32)
    owner = jnp.sum(jnp.where(first + count <= f, 1.0, 0.0), axis=0, keepdims=True)
    hit = lax.broadcasted_iota(jnp.int32, (N_EXPERTS, LANES), 0).astype(F32) == owner
    pick = lambda col: jnp.sum(jnp.where(hit, col, 0.0), axis=0, keepdims=True)
    j = f[0:1, :] - pick(first)
    return pick(src) + stride * j, pick(dst) + stride * j


def _route_kernel(idx_all_ref, idx_ref, pos_ref, meta_ref, be_ref, misc_ref, carry_ref, start_ref):
    i = pl.program_id(0)
    TT = idx_ref.shape[1]
    NBP = be_ref.shape[1]

    @pl.when(i == 0)
    def _():
        idx_all = idx_all_ref[...]
        e_all = lax.broadcasted_iota(jnp.int32, (N_EXPERTS, idx_all.shape[1]), 0)
        tot = jnp.zeros((N_EXPERTS, 1), F32)
        for k in range(TOP_K):
            tot = tot + jnp.sum(jnp.where(idx_all[k:k + 1, :] == e_all, 1.0, 0.0), axis=1, keepdims=True)
        padded = jnp.floor((tot + (MOE_ROWS - 1)) / MOE_ROWS) * MOE_ROWS
        start = _expert_prefix(padded)
        start_ref[...] = start
        carry_ref[...] = jnp.zeros_like(carry_ref)
        end = start + padded
        block_start = lax.broadcasted_iota(jnp.int32, (N_EXPERTS, NBP), 1).astype(F32) * MOE_ROWS
        be = jnp.sum(jnp.where(end <= block_start, 1.0, 0.0), axis=0, keepdims=True)
        be_ref[...] = jnp.minimum(be, N_EXPERTS - 1).astype(jnp.int32)
        n_used = jnp.broadcast_to(jnp.sum(padded, axis=0, keepdims=True) / MOE_ROWS, (1, LANES))
        zero = jnp.zeros((1, LANES), F32)
        misc_ref[...] = jnp.concatenate([n_used, _expert_row(start + tot)] + [zero] * 6,
                                        axis=0).astype(jnp.int32)

    idx = idx_ref[...]
    e_iota = lax.broadcasted_iota(jnp.int32, (N_EXPERTS, TT), 0)
    onehot = [idx[k:k + 1, :] == e_iota for k in range(TOP_K)]
    cnt = jnp.zeros((N_EXPERTS, TT), F32)
    for oh in onehot:
        cnt = cnt + jnp.where(oh, 1.0, 0.0)
    tile_tot = jnp.sum(cnt, axis=1, keepdims=True)
    earlier = (lax.broadcasted_iota(jnp.int32, (TT, TT), 0)
               < lax.broadcasted_iota(jnp.int32, (TT, TT), 1))
    before = jnp.dot(cnt.astype(BF16), jnp.where(earlier, 1.0, 0.0).astype(BF16),
                     preferred_element_type=F32)
    seg = _expert_prefix(tile_tot)
    where_staged = seg + before
    rows = [jnp.sum(jnp.where(oh, where_staged, 0.0), axis=0, keepdims=True) for oh in onehot]
    pos_ref[...] = jnp.concatenate(rows, axis=0).astype(jnp.int32)
    slot = start_ref[...] + carry_ref[...]
    whole = jnp.floor(tile_tot / DMA_ROWS)
    covered = whole * DMA_ROWS
    lists = list(_piece_list(whole, seg, slot, float(DMA_ROWS)))
    counts = [jnp.sum(whole, axis=0, keepdims=True)]
    for size in PIECE_SIZES[1:]:
        has = jnp.floor((tile_tot - covered) / size)
        lists += _piece_list(has, seg + covered, slot + covered, 0.0)
        counts.append(jnp.sum(has, axis=0, keepdims=True))
        covered = covered + has * size
    lane = lax.broadcasted_iota(jnp.int32, (1, LANES), 1)
    count_row = jnp.zeros((1, LANES), F32)
    for k, c in enumerate(counts):
        count_row = jnp.where(lane == k, c, count_row)
    zero = jnp.zeros((1, LANES), F32)
    meta_ref[...] = jnp.concatenate(lists + [count_row] + [zero] * (META_ROWS - len(lists) - 1),
                                    axis=0).astype(jnp.int32)
    carry_ref[...] += tile_tot


def _route(idx, n_blocks):
    _, N = idx.shape
    nbp = -(-n_blocks // LANES) * LANES
    tile = pl.BlockSpec((TOP_K, TOKEN_TILE), lambda i: (0, i))
    return pl.pallas_call(
        _route_kernel,
        grid=(N // TOKEN_TILE,),
        in_specs=[pl.BlockSpec((TOP_K, N), lambda i: (0, 0)), tile],
        out_specs=[tile,
                   pl.BlockSpec((META_ROWS, LANES), lambda i: (i, 0)),
                   pl.BlockSpec((1, nbp), lambda i: (0, 0)),
                   pl.BlockSpec((8, LANES), lambda i: (0, 0))],
        out_shape=[jax.ShapeDtypeStruct((TOP_K, N), jnp.int32),
                   jax.ShapeDtypeStruct((N // TOKEN_TILE * META_ROWS, LANES), jnp.int32),
                   jax.ShapeDtypeStruct((1, nbp), jnp.int32),
                   jax.ShapeDtypeStruct((8, LANES), jnp.int32)],
        scratch_shapes=[pltpu.VMEM((N_EXPERTS, 1), F32)] * 2,
        compiler_params=_cparams(("arbitrary",)),
        name="route",
    )(idx, idx)


def _rows(ref, first_row, n_rows):
    return ref.at[pl.ds(pl.multiple_of(first_row * CHUNKS, CHUNKS), n_rows * CHUNKS), :]


def _for_each_run_piece(meta_ref, fn):
    for k, size in enumerate(PIECE_SIZES):
        def body(j, c, k=k, size=size):
            fn(meta_ref[2 * k + 1, j], meta_ref[2 * k, j], size)
            return c

        lax.fori_loop(0, meta_ref[2 * len(PIECE_SIZES), k], body, 0)


def _wait_rows(n_rows, src_ref, dst_ref, sem):
    @pl.when(n_rows > 0)
    def _():
        n = n_rows * CHUNKS
        pltpu.make_async_copy(src_ref.at[pl.ds(0, n), :], dst_ref.at[pl.ds(0, n), :], sem).wait()


PAD_ROWS = MOE_ROWS


def _zero_padding(misc_ref, xs_ref, zeros_ref, sem, n_blocks):
    zeros_ref[...] = jnp.zeros_like(zeros_ref)

    def pad_copy(e):
        first = pl.multiple_of(misc_ref[1, e] * CHUNKS, CHUNKS)
        return pltpu.make_async_copy(zeros_ref, xs_ref.at[pl.ds(first, PAD_ROWS * CHUNKS), :], sem)

    def tail_copy(b):
        first = pl.multiple_of(b * (MOE_ROWS * CHUNKS), MOE_ROWS * CHUNKS)
        return pltpu.make_async_copy(zeros_ref.at[pl.ds(0, MOE_ROWS * CHUNKS), :],
                                     xs_ref.at[pl.ds(first, MOE_ROWS * CHUNKS), :], sem)

    def pad(e, c):
        pad_copy(e).start()
        pad_copy(e).wait()
        return c

    def tail(start_not_wait):
        def body(b, c):
            tail_copy(b).start() if start_not_wait else tail_copy(b).wait()
            return c

        lax.fori_loop(misc_ref[0, 0], n_blocks + 1, body, 0)

    lax.fori_loop(0, N_EXPERTS, pad, 0)
    tail(True)
    tail(False)


def _selection(pos_ref, fill_ref=None):
    TT = pos_ref.shape[1]
    p_iota = lax.broadcasted_iota(jnp.int32, (STAGE_ROWS, TT), 0)
    sel = jnp.zeros((STAGE_ROWS, TT), F32)
    for k in range(TOP_K):
        val = 1.0 if fill_ref is None else fill_ref[k:k + 1, :]
        sel = jnp.where(pos_ref[k:k + 1, :] == p_iota, val, sel)
    return sel.astype(BF16)


def _dispatch_kernel(misc_ref, meta_ref, pos_ref, *refs, first_tiles, n_tiles, n_blocks):
    x_refs = refs[:len(first_tiles)]
    xs_ref, stage_ref, zeros_ref, sems, pad_sem = refs[len(first_tiles):]
    i = pl.program_id(0)
    TT = pos_ref.shape[1]
    slot = i % 2

    @pl.when(i == 0)
    def _():
        _zero_padding(misc_ref, xs_ref, zeros_ref, pad_sem, n_blocks)

    x = x_refs[0][...]
    for t0, ref in zip(first_tiles[1:], x_refs[1:]):
        x = jnp.where(i >= t0, ref[...], x)
    staged = jnp.dot(_selection(pos_ref), x, preferred_element_type=F32)
    stage = stage_ref.at[slot]
    _store_chunked(stage, staged)
    _for_each_run_piece(meta_ref, lambda dst, src, n: pltpu.make_async_copy(
        _rows(stage, src, n), _rows(xs_ref, dst, n), sems.at[slot]).start())

    @pl.when(i > 0)
    def _():
        _wait_rows(TT * TOP_K, stage_ref.at[1 - slot], xs_ref, sems.at[1 - slot])

    @pl.when(i == n_tiles - 1)
    def _():
        _wait_rows(TT * TOP_K, stage, xs_ref, sems.at[slot])


def _dispatch(misc, meta, pos, xns, n_blocks):
    D = xns[0].shape[1]
    tiles = [x.shape[0] // TOKEN_TILE for x in xns]
    first_tiles = tuple(sum(tiles[:g]) for g in range(len(tiles)))

    def x_spec(t0, nt):
        return pl.BlockSpec((TOKEN_TILE, D), lambda i: (jnp.clip(i - t0, 0, nt - 1), 0))

    return pl.pallas_call(
        functools.partial(_dispatch_kernel, first_tiles=first_tiles, n_tiles=sum(tiles), n_blocks=n_blocks),
        grid=(sum(tiles),),
        in_specs=[pl.BlockSpec(memory_space=pltpu.SMEM),
                  pl.BlockSpec((META_ROWS, LANES), lambda i: (i, 0), memory_space=pltpu.SMEM),
                  pl.BlockSpec((TOP_K, TOKEN_TILE), lambda i: (0, i))]
                 + [x_spec(t0, nt) for t0, nt in zip(first_tiles, tiles)],
        out_specs=pl.BlockSpec(memory_space=pl.ANY),
        out_shape=jax.ShapeDtypeStruct(((n_blocks + 1) * MOE_ROWS * CHUNKS, LANES), F32),
        scratch_shapes=[pltpu.VMEM((2, STAGE_ROWS * CHUNKS, LANES), F32),
                        pltpu.VMEM((PAD_ROWS * CHUNKS, LANES), F32),
                        pltpu.SemaphoreType.DMA((2,)),
                        pltpu.SemaphoreType.DMA(())],
        compiler_params=_cparams(("arbitrary",)),
        name="dispatch",
    )(misc, meta, pos, *xns)


def _expert_kernel(be_ref, nused_ref, xs_ref, wgu_ref, bgu_ref, wd_ref, bd_ref, ys_ref,
                   wgu_bf, wd_bf):
    i = pl.program_id(0)
    D_FF = wd_ref.shape[0]
    new_expert = jnp.logical_or(i == 0, be_ref[i] != be_ref[jnp.maximum(i - 1, 0)])

    @pl.when(jnp.logical_and(i < nused_ref[0], new_expert))
    def _():
        wgu_bf[...] = wgu_ref[...].astype(BF16)
        wd_bf[...] = wd_ref[...].astype(BF16)

    @pl.when(i < nused_ref[0])
    def _():
        x = _load_chunked(xs_ref, MOE_ROWS).astype(BF16)
        hdn = jnp.dot(x, wgu_bf[...], preferred_element_type=F32) + bgu_ref[...]
        glu = jnp.minimum(hdn[:, :D_FF], SWIGLU_LIMIT)
        lin = jnp.clip(hdn[:, D_FF:], -SWIGLU_LIMIT, SWIGLU_LIMIT)
        act = glu * (1.0 / (1.0 + jnp.exp(-SWIGLU_ALPHA * glu))) * (lin + 1.0)
        y = jnp.dot(act.astype(BF16), wd_bf[...], preferred_element_type=F32) + bd_ref[...]
        _store_chunked(ys_ref, y)

    @pl.when(i >= nused_ref[0])
    def _():
        ys_ref[...] = jnp.zeros_like(ys_ref)


def _experts(block_expert, n_used, xs, w_gate_up, b_gate_up, w_down, b_down, n_blocks):
    E, D, F2 = w_gate_up.shape
    D_FF = w_down.shape[1]
    rows = pl.BlockSpec((MOE_ROWS * CHUNKS, LANES), lambda i, be, nu: (i, 0))
    rows_in = pl.BlockSpec((MOE_ROWS * CHUNKS, LANES), lambda i, be, nu: (jnp.minimum(i, nu[0] - 1), 0))
    grid_spec = pltpu.PrefetchScalarGridSpec(
        num_scalar_prefetch=2,
        grid=(n_blocks,),
        in_specs=[rows_in,
                  pl.BlockSpec((None, D, F2), lambda i, be, nu: (be[i], 0, 0)),
                  pl.BlockSpec((None, 1, F2), lambda i, be, nu: (be[i], 0, 0)),
                  pl.BlockSpec((None, D_FF, D), lambda i, be, nu: (be[i], 0, 0)),
                  pl.BlockSpec((None, 1, D), lambda i, be, nu: (be[i], 0, 0))],
        out_specs=rows,
        scratch_shapes=[pltpu.VMEM((D, F2), BF16), pltpu.VMEM((D_FF, D), BF16)],
    )
    return pl.pallas_call(
        _expert_kernel,
        grid_spec=grid_spec,
        out_shape=jax.ShapeDtypeStruct((n_blocks * MOE_ROWS * CHUNKS, LANES), F32),
        compiler_params=_cparams(("arbitrary",)),
        name="experts",
    )(block_expert, n_used, xs, w_gate_up, b_gate_up.reshape(E, 1, F2), w_down, b_down.reshape(E, 1, D))


def _combine_kernel(meta_ref, next_meta_ref, pos_ref, gate_ref, h_ref, g_ref, ys_ref, y_ref, stage_ref, sems,
                    *, n_tiles):
    i = pl.program_id(0)
    slot = i % 2

    def fetch(meta, s):
        _for_each_run_piece(meta, lambda src, dst, n: pltpu.make_async_copy(
            _rows(ys_ref, src, n), _rows(stage_ref.at[s], dst, n), sems.at[s]).start())

    @pl.when(i == 0)
    def _():
        fetch(meta_ref, 0)

    if n_tiles > 1:
        @pl.when(i + 1 < n_tiles)
        def _():
            fetch(next_meta_ref, 1 - slot)

    weights = _selection(pos_ref, gate_ref)
    _wait_rows(STAGE_ROWS, ys_ref, stage_ref.at[slot], sems.at[slot])
    staged = _load_chunked(stage_ref.at[slot], STAGE_ROWS).astype(BF16)
    moe = lax.dot_general(weights, staged, (((0,), (0,)), ((), ())), preferred_element_type=F32)
    hf = h_ref[...] + moe
    ms = jnp.mean(hf * hf, axis=-1, keepdims=True)
    y_ref[...] = (hf * lax.rsqrt(ms + EPS)) * g_ref[...]


def _combine(meta, pos, gates, h, norm_g, ys, tile0):
    n, D = h.shape
    n_tiles = n // TOKEN_TILE
    return pl.pallas_call(
        functools.partial(_combine_kernel, n_tiles=n_tiles),
        grid=(n_tiles,),
        in_specs=[pl.BlockSpec((META_ROWS, LANES), lambda i: (tile0 + i, 0), memory_space=pltpu.SMEM),
                  pl.BlockSpec((META_ROWS, LANES), lambda i: (tile0 + jnp.minimum(i + 1, n_tiles - 1), 0),
                               memory_space=pltpu.SMEM),
                  pl.BlockSpec((TOP_K, TOKEN_TILE), lambda i: (0, tile0 + i)),
                  pl.BlockSpec((TOP_K, TOKEN_TILE), lambda i: (0, i)),
                  pl.BlockSpec((TOKEN_TILE, D), lambda i: (i, 0)),
                  pl.BlockSpec((1, D), lambda i: (0, 0)),
                  pl.BlockSpec(memory_space=pl.ANY)],
        out_specs=pl.BlockSpec((TOKEN_TILE, D), lambda i: (i, 0)),
        out_shape=jax.ShapeDtypeStruct((n, D), F32),
        scratch_shapes=[pltpu.VMEM((2, STAGE_ROWS * CHUNKS, LANES), F32), pltpu.SemaphoreType.DMA((2,))],
        compiler_params=_cparams(("arbitrary",)),
        name="combine",
    )(meta, meta, pos, gates, h, norm_g, ys)


def _prep_weights(w_in, w_alpha):
    w_main = w_in[:, :PROJ_MAIN].astype(BF16)
    w_lr = jnp.pad(w_in[:, PROJ_MAIN:], ((0, 0), (0, LANES - GATE_RANK))).astype(BF16)
    w_al = jnp.pad(w_alpha, ((0, LANES - GATE_RANK), (0, 0))).astype(BF16)
    return w_main, w_lr, w_al


def kernel(x_prompt, x_sample, cache_swa_k, cache_swa_v, state_gla, norm_mix_g, w_in, w_alpha, b_alpha, gla_norm_g, w_out, norm_ffn_g, w_router, b_router, w_gate_up, b_gate_up, w_down, b_down, norm_final_g):
    B, S, D = x_prompt.shape
    Bs, Ts, _ = x_sample.shape
    assert w_in.shape[0] == 1, "single-layer trunk"
    l = 0
    R = cache_swa_k.shape[2]
    rows_p = min(DILATIONS[-1] * KEYS_PER_CONFIG, S)
    w_main, w_lr, w_al = _prep_weights(w_in[l], w_alpha[l])
    g_mix = norm_mix_g[l][None]
    b_al = b_alpha[l][None]
    g_gla = gla_norm_g[l][None]

    pos_p = jnp.arange(S, dtype=jnp.int32)
    qa, ka, va, qb, kb, vb, zg, gb, k_tail, v_tail = _project(x_prompt, pos_p, g_mix, w_main, w_lr, w_al, b_al,
                                                              PROJ_TILE, tail_rows=rows_p)
    oa_p = _prompt_attention(qa, ka, va)
    ob_p, st_p = _gla(qb, kb, gb, vb, zg, jnp.zeros((B, N_HEADS_B, DK_B, DV_B), F32), g_gla,
                      GLA_TILE, GLA_CHUNK)
    k_prompt = k_tail.reshape(1, B, rows_p, N_HEADS_A, HEAD_DIM_A)
    v_prompt = v_tail.reshape(1, B, rows_p, N_HEADS_A, HEAD_DIM_A)

    pos_s = PAST_LEN + (jnp.arange(Bs * Ts, dtype=jnp.int32) % Ts)
    proj_s = _project(x_sample.reshape(1, Bs * Ts, D), pos_s, g_mix, w_main, w_lr, w_al, b_al, Bs * Ts)
    qa_s, ka_s, va_s, qb_s, kb_s, vb_s, zg_s, gb_s = [t.reshape(Bs, Ts, -1) for t in proj_s]
    oa_s, k_sample, v_sample = _sample_attention(qa_s, ka_s, va_s,
                                                 cache_swa_k[l].reshape(Bs, R, WIDTH_A),
                                                 cache_swa_v[l].reshape(Bs, R, WIDTH_A))
    ob_s, st_s = _gla(qb_s, kb_s, gb_s, vb_s, zg_s, state_gla[l], g_gla, Ts, Ts)

    w_out_bf = w_out[l].astype(BF16)
    g_ffn = norm_ffn_g[l][None]
    w_router_pad = jnp.pad(w_router[l], ((0, 0), (0, LANES - N_EXPERTS)))
    b_router_c = b_router[l][:, None]
    Np, Ns = B * S, Bs * Ts
    h_p, xn_p, idx_p, gate_p = _merge(oa_p.reshape(Np, WIDTH_A), ob_p.reshape(Np, WIDTH_B),
                                      x_prompt.reshape(Np, D), w_out_bf, g_ffn, w_router_pad, b_router_c,
                                      MERGE_TILE)
    h_s, xn_s, idx_s, gate_s = _merge(oa_s.reshape(Ns, WIDTH_A), ob_s.reshape(Ns, WIDTH_B),
                                      x_sample.reshape(Ns, D), w_out_bf, g_ffn, w_router_pad, b_router_c,
                                      Ns)

    y_p, y_s = _moe([(xn_p, idx_p, gate_p, h_p), (xn_s, idx_s, gate_s, h_s)],
                    w_gate_up[l], b_gate_up[l], w_down[l], b_down[l], norm_final_g[None])
    return (y_p.reshape(B, S, D), y_s.reshape(Bs, Ts, D), k_prompt, v_prompt, st_p[None],
            k_sample.reshape(1, Bs, R, N_HEADS_A, HEAD_DIM_A),
            v_sample.reshape(1, Bs, R, N_HEADS_A, HEAD_DIM_A), st_s[None])


PAST_LEN = 16384
PROJ_TILE = 1024
MERGE_TILE = 1024
GLA_TILE = 1024
GLA_CHUNK = 64


def _moe(groups, w_gate_up, b_gate_up, w_down, b_down, g_final):
    sizes = [g[3].shape[0] for g in groups]
    N = sum(sizes)
    assert all(n % TOKEN_TILE == 0 for n in sizes)
    n_blocks = -(-(N * TOP_K + N_EXPERTS * (MOE_ROWS - 1)) // MOE_ROWS)
    idx = jnp.concatenate([g[1] for g in groups], axis=1)
    pos, meta, block_expert, misc = _route(idx, n_blocks)
    first_tile = [sum(sizes[:i]) // TOKEN_TILE for i in range(len(sizes))]
    xs = _dispatch(misc, meta, pos, [g[0] for g in groups], n_blocks)
    ys = _experts(block_expert[0], misc[0, :1], xs, w_gate_up, b_gate_up, w_down, b_down, n_blocks)
    return [_combine(meta, pos, gates, h, g_final, ys, t0)
            for (_, _, gates, h), t0 in zip(groups, first_tile)]
```

```python
import functools

import jax
import jax.numpy as jnp
from jax import lax
from jax.experimental import pallas as pl
from jax.experimental.pallas import tpu as pltpu

F32 = jnp.float32
BF16 = jnp.bfloat16

N_HEADS_A = 8
HEAD_DIM_A = 64
WIDTH_A = N_HEADS_A * HEAD_DIM_A
N_HEADS_B = 4
DK_B = 64
DV_B = 128
QK_B = N_HEADS_B * DK_B
WIDTH_B = N_HEADS_B * DV_B
GATE_RANK = 16
GATE_LOGIT_NORM = 16.0
DILATIONS = (1, 4, 16)
KEYS_PER_CONFIG = 128
ROPE_THETA = 10000.0
N_EXPERTS = 32
TOP_K = 4
SWIGLU_ALPHA = 1.702
SWIGLU_LIMIT = 7.0
EPS = 1e-6

LANES = 128
VMEM_LIMIT = 56 * 1024 * 1024


def _cparams(sem, vmem=VMEM_LIMIT):
    return pltpu.CompilerParams(dimension_semantics=sem, vmem_limit_bytes=vmem)


PROJ_MAIN = 3 * WIDTH_A + 2 * QK_B + 2 * WIDTH_B


def _rope_tables(pos):
    half = HEAD_DIM_A // 2
    inv = ROPE_THETA ** (-jnp.arange(half, dtype=F32) / half)
    ang = pos.astype(F32)[:, None] * inv[None, :]
    cos = jnp.cos(ang)
    sin = jnp.sin(ang)
    cos_t = jnp.concatenate([cos, cos, cos, cos], axis=-1)
    sin_t = jnp.concatenate([-sin, sin, -sin, sin], axis=-1)
    return cos_t, sin_t


def _rope_block(t, cos, sin, first_half):
    half = HEAD_DIM_A // 2
    partner = jnp.where(first_half, pltpu.roll(t, LANES - half, 1), pltpu.roll(t, half, 1))
    return t * cos + partner * sin


def _proj_kernel(x_ref, g_ref, w_ref, wlr_ref, wa_ref, ba_ref, cos_ref, sin_ref,
                 qa_ref, ka_ref, va_ref, qb_ref, kb_ref, vb_ref, zg_ref, gb_ref, *tail_refs, first_tail_tile):
    x = x_ref[...]
    ms = jnp.mean(x * x, axis=-1, keepdims=True)
    xn = ((x * lax.rsqrt(ms + EPS)) * g_ref[...]).astype(BF16)

    def cols(lo, hi):
        return jnp.dot(xn, w_ref[:, lo:hi], preferred_element_type=F32)

    cos = cos_ref[...]
    sin = sin_ref[...]
    lane = lax.broadcasted_iota(jnp.int32, cos.shape, 1)
    first_half = (lane % HEAD_DIM_A) < (HEAD_DIM_A // 2)
    q = cols(0, WIDTH_A)
    k = cols(WIDTH_A, 2 * WIDTH_A)
    for j in range(WIDTH_A // LANES):
        sl = slice(j * LANES, (j + 1) * LANES)
        qa_ref[:, sl] = _rope_block(q[:, sl], cos, sin, first_half) * (HEAD_DIM_A ** -0.5)
        ka_ref[:, sl] = _rope_block(k[:, sl], cos, sin, first_half)
    o = 2 * WIDTH_A
    va_ref[...] = cols(o, o + WIDTH_A)
    o += WIDTH_A
    qb_ref[...] = cols(o, o + QK_B) * (DK_B ** -0.5)
    o += QK_B
    kb_ref[...] = cols(o, o + QK_B)
    o += QK_B
    vb_ref[...] = cols(o, o + WIDTH_B).astype(vb_ref.dtype)
    o += WIDTH_B
    zg_ref[...] = cols(o, o + WIDTH_B)
    lr = jnp.dot(xn, wlr_ref[...], preferred_element_type=F32)
    z = jnp.dot(lr.astype(BF16), wa_ref[...], preferred_element_type=F32) + ba_ref[...]
    logsig = jnp.minimum(z, 0.0) - jnp.log(1.0 + jnp.exp(-jnp.abs(z)))
    gb_ref[...] = logsig / GATE_LOGIT_NORM
    if tail_refs:
        @pl.when(pl.program_id(1) >= first_tail_tile)
        def _():
            tail_refs[0][...] = ka_ref[...]
            tail_refs[1][...] = va_ref[...]


def _project(x, pos, norm_g, w_main, w_lr, w_alpha, b_alpha, tm, tail_rows=0):
    B, T, D = x.shape
    assert tail_rows % tm == 0
    cos_t, sin_t = _rope_tables(pos)
    grid = (B, T // tm)
    first_tail_tile = (T - tail_rows) // tm
    row = lambda w: pl.BlockSpec((None, tm, w), lambda b, i: (b, i, 0))
    tail = pl.BlockSpec((None, tm, WIDTH_A), lambda b, i: (b, jnp.maximum(i - first_tail_tile, 0), 0))
    full = lambda a: pl.BlockSpec(a.shape, lambda b, i: (0,) * a.ndim)
    tab = pl.BlockSpec((tm, LANES), lambda b, i: (i, 0))
    widths = (WIDTH_A, WIDTH_A, WIDTH_A, QK_B, QK_B, WIDTH_B, WIDTH_B, QK_B)
    dtypes = (F32, F32, F32, F32, F32, BF16, F32, F32)
    n_tail = 2 if tail_rows else 0
    return pl.pallas_call(
        functools.partial(_proj_kernel, first_tail_tile=first_tail_tile),
        grid=grid,
        in_specs=[row(D), full(norm_g), full(w_main), full(w_lr), full(w_alpha), full(b_alpha), tab, tab],
        out_specs=[row(w) for w in widths] + [tail] * n_tail,
        out_shape=[jax.ShapeDtypeStruct((B, T, w), dt) for w, dt in zip(widths, dtypes)]
                  + [jax.ShapeDtypeStruct((B, tail_rows, WIDTH_A), F32)] * n_tail,
        compiler_params=_cparams(("parallel", "arbitrary")),
        name="proj",
    )(x, norm_g, w_main, w_lr, w_alpha, b_alpha, cos_t, sin_t)


Q_BLOCK = 128
NEG_INF = float("-inf")


def _attn_block(q, kb, v1, mask, head0, state):
    QB = q.shape[0]
    q2 = jnp.concatenate([jnp.where(head0, q, 0.0), jnp.where(head0, 0.0, q)], axis=0).astype(BF16)
    s = lax.dot_general(q2, kb, (((1,), (1,)), ((), ())), preferred_element_type=F32)
    s = jnp.where(mask, s, NEG_INF)
    if state is None:
        m2 = jnp.max(s, axis=1, keepdims=True)
    else:
        prev = jnp.concatenate([jnp.where(head0, state[0], NEG_INF),
                                jnp.where(head0, NEG_INF, state[0])], axis=0)
        m2 = jnp.max(jnp.concatenate([s, prev], axis=1), axis=1, keepdims=True)
    p = jnp.exp(s - m2)
    pv = jnp.dot(p.astype(BF16), v1, preferred_element_type=F32)
    m_full = jnp.where(head0, m2[:QB], m2[QB:])
    l_full = jnp.where(head0, pv[:QB, LANES:], pv[QB:, LANES:])
    pv_full = jnp.where(head0, pv[:QB, :LANES], pv[QB:, :LANES])
    if state is None:
        return m_full, l_full, pv_full
    a = jnp.exp(state[0] - m_full)
    return m_full, a * state[1] + l_full, a * state[2] + pv_full


def _attn_kernel(q_ref, k_ref, v_ref, o_ref, m_ref, l_ref, acc_ref):
    S = q_ref.shape[0]
    QB = Q_BLOCK
    lane = lax.broadcasted_iota(jnp.int32, (QB, LANES), 1)
    head0 = lane < HEAD_DIM_A
    qq = lax.broadcasted_iota(jnp.int32, (2 * QB, 2 * QB), 0) % QB
    kk = lax.broadcasted_iota(jnp.int32, (2 * QB, 2 * QB), 1)
    band = jnp.logical_and(kk >= qq, kk - qq <= KEYS_PER_CONFIG)
    causal = (lax.broadcasted_iota(jnp.int32, (2 * QB, QB), 1)
              <= lax.broadcasted_iota(jnp.int32, (2 * QB, QB), 0) % QB)

    GROUP = 8

    for ci, dil in enumerate(sorted(DILATIONS, reverse=True)):
        nblk = S // (dil * QB)
        assert nblk % GROUP == 0 or GROUP % nblk == 0

        def rows(start, n, dil=dil):
            return pl.ds(start, n) if dil == 1 else pl.ds(start, n, stride=dil)

        def do_group(blocks, ci=ci, dil=dil, rows=rows):
            chunks = {}

            def kv_chunk(r, rkey, jbase, c):
                if (rkey, c) not in chunks:
                    ks = rows(r + dil * QB * (jbase + c), QB)
                    v = v_ref[ks, :].astype(BF16)
                    chunks[(rkey, c)] = (k_ref[ks, :].astype(BF16),
                                         jnp.concatenate([v, jnp.ones(v.shape, BF16)], axis=1))
                return chunks[(rkey, c)]

            loaded = []
            for r, rkey, jbase, joff in blocks:
                first = isinstance(jbase, int) and jbase + joff == 0
                qs = rows(r + dil * QB * (jbase + joff), QB)
                parts = [kv_chunk(r, rkey, jbase, joff)]
                if not first:
                    parts.insert(0, kv_chunk(r, rkey, jbase, joff - 1))
                kb = jnp.concatenate([p[0] for p in parts], axis=0)
                v1 = jnp.concatenate([p[1] for p in parts], axis=0)
                state = None if ci == 0 else (m_ref[qs, :], l_ref[qs, :], acc_ref[qs, :])
                loaded.append((qs, q_ref[qs, :], kb, v1, causal if first else band, state))
            results = [_attn_block(q, kb, v1, mask, head0, state) for _, q, kb, v1, mask, state in loaded]
            for (qs, *_), (m, l, acc) in zip(loaded, results):
                m_ref[qs, :] = m
                l_ref[qs, :] = l
                acc_ref[qs, :] = acc

        if nblk >= GROUP:
            def residue(r, carry, nblk=nblk, do_group=do_group):
                do_group([(r, 0, 0, j) for j in range(GROUP)])

                def rest(g, c):
                    do_group([(r, 0, GROUP * g, u) for u in range(GROUP)])
                    return c

                return lax.fori_loop(1, nblk // GROUP, rest, carry)

            lax.fori_loop(0, dil, residue, 0)
        else:
            per = 2 * GROUP // nblk

            def residues(g, carry, nblk=nblk, per=per, do_group=do_group):
                do_group([(g * per + i, i, 0, j) for j in range(nblk) for i in range(per)])
                return carry

            lax.fori_loop(0, dil // per, residues, 0)

    def finish(i, c):
        rs = pl.ds(pl.multiple_of(i * QB, QB), QB)
        o_ref[rs, :] = (acc_ref[rs, :] / l_ref[rs, :]).astype(o_ref.dtype)
        return c

    lax.fori_loop(0, S // QB, finish, 0, unroll=4)


def _prompt_attention(qa, ka, va):
    B, S, W = qa.shape
    spec = pl.BlockSpec((None, S, LANES), lambda b, hp: (b, 0, hp))
    return pl.pallas_call(
        _attn_kernel,
        grid=(B, W // LANES),
        in_specs=[spec, spec, spec],
        out_specs=spec,
        out_shape=jax.ShapeDtypeStruct((B, S, W), BF16),
        scratch_shapes=[pltpu.VMEM((S, LANES), F32)] * 3,
        compiler_params=_cparams(("parallel", "parallel")),
        name="prompt_attn",
    )(qa, ka, va)


def _gla_kernel(q_ref, k_ref, g_ref, v_ref, z_ref, s0_ref, ng_ref, o_ref, sfin_ref, st_ref, *, chunk):
    C = chunk
    TS = q_ref.shape[0]
    n_pairs = N_HEADS_B // 2
    PW = 2 * DV_B
    t_idx = pl.program_id(1)

    lane_k = lax.broadcasted_iota(jnp.int32, (C, LANES), 1)
    head0 = lane_k < DK_B
    row_k = lax.broadcasted_iota(jnp.int32, (C, LANES), 0)
    tri2 = (lax.broadcasted_iota(jnp.int32, (2 * C, C), 1)
            <= lax.broadcasted_iota(jnp.int32, (2 * C, C), 0) % C)
    value_head0 = lax.broadcasted_iota(jnp.int32, (C, PW), 1) < DV_B
    bd_mask = ((lax.broadcasted_iota(jnp.int32, (PW, LANES), 0) // DV_B)
               == (lax.broadcasted_iota(jnp.int32, (PW, LANES), 1) // DK_B))

    def prefix_rows(x):
        shift = 1
        while shift < C:
            x = x + jnp.where(row_k >= shift, pltpu.roll(x, shift, 0), 0.0)
            shift *= 2
        return x

    @pl.when(t_idx == 0)
    def _():
        for p in range(n_pairs):
            for h in range(2):
                blk = jnp.transpose(s0_ref[2 * p + h])
                pad = jnp.zeros((DV_B, DK_B), F32)
                row = jnp.concatenate([blk, pad] if h == 0 else [pad, blk], axis=1)
                st_ref[p, h * DV_B:(h + 1) * DV_B, :] = row

    def chunk_body(c, carry):
        rs = pl.ds(pl.multiple_of(c * C, C), C)
        for p in range(n_pairs):
            kl = slice(p * LANES, (p + 1) * LANES)
            vl = slice(p * PW, (p + 1) * PW)
            q = q_ref[rs, kl]
            k = k_ref[rs, kl]
            g = g_ref[rs, kl]
            v = v_ref[rs, vl]
            b = prefix_rows(g)
            b_last = b[C - 1:C, :]
            b_mid = b[C // 2 - 1:C // 2, :] if C > 1 else b_last
            qe = q * jnp.exp(b - b_mid)
            ke = (k * jnp.exp(b_mid - b)).astype(BF16)
            st = st_ref[p]
            q_in = (q * jnp.exp(b)).astype(BF16)
            nt = (((1,), (1,)), ((), ()))
            o = lax.dot_general(q_in, st.astype(BF16), nt, preferred_element_type=F32)
            q2 = jnp.concatenate([jnp.where(head0, qe, 0.0), jnp.where(head0, 0.0, qe)], axis=0)
            a2 = lax.dot_general(q2.astype(BF16), ke, nt, preferred_element_type=F32)
            a2 = jnp.where(tri2, a2, 0.0).astype(BF16)
            av = jnp.dot(a2, v, preferred_element_type=F32)
            o = o + jnp.where(value_head0, av[:C], av[C:])
            k_dec = (k * jnp.exp(b_last - b)).astype(BF16)
            upd = lax.dot_general(v, k_dec, (((0,), (0,)), ((), ())), preferred_element_type=F32)
            st_ref[p] = jnp.exp(b_last) * st + jnp.where(bd_mask, upd, 0.0)
            for h in range(2):
                oh = o[:, h * DV_B:(h + 1) * DV_B]
                hl = slice((2 * p + h) * DV_B, (2 * p + h + 1) * DV_B)
                ms = jnp.mean(oh * oh, axis=-1, keepdims=True)
                z = z_ref[rs, hl]
                gated = (oh * lax.rsqrt(ms + EPS)) * ng_ref[:, hl] * (z / (1.0 + jnp.exp(-z)))
                o_ref[rs, hl] = gated.astype(o_ref.dtype)
        return carry

    lax.fori_loop(0, TS // C, chunk_body, 0, unroll=4 if (TS // C) % 4 == 0 else 1)

    @pl.when(t_idx == pl.num_programs(1) - 1)
    def _():
        for p in range(n_pairs):
            for h in range(2):
                blk = st_ref[p, h * DV_B:(h + 1) * DV_B, h * DK_B:(h + 1) * DK_B]
                sfin_ref[2 * p + h] = jnp.transpose(blk)


def _gla(qb, kb, gb, vb, zg, state0, norm_g, ts, chunk):
    B, T, _ = qb.shape
    row = lambda w: pl.BlockSpec((None, ts, w), lambda b, i: (b, i, 0))
    st_spec = pl.BlockSpec((None, N_HEADS_B, DK_B, DV_B), lambda b, i: (b, 0, 0, 0))
    return pl.pallas_call(
        functools.partial(_gla_kernel, chunk=chunk),
        grid=(B, T // ts),
        in_specs=[row(QK_B), row(QK_B), row(QK_B), row(WIDTH_B), row(WIDTH_B), st_spec,
                  pl.BlockSpec((1, WIDTH_B), lambda b, i: (0, 0))],
        out_specs=[row(WIDTH_B), st_spec],
        out_shape=[jax.ShapeDtypeStruct((B, T, WIDTH_B), BF16),
                   jax.ShapeDtypeStruct((B, N_HEADS_B, DK_B, DV_B), F32)],
        scratch_shapes=[pltpu.VMEM((N_HEADS_B // 2, 2 * DV_B, LANES), F32)],
        compiler_params=_cparams(("parallel", "arbitrary")),
        name="gla",
    )(qb, kb, gb, vb, zg, state0, norm_g)


def _sample_attn_kernel(q_ref, kn_ref, vn_ref, kc_ref, vc_ref, o_ref, ko_ref, vo_ref):
    T = q_ref.shape[0]
    R = kc_ref.shape[0]
    W = q_ref.shape[1]
    HT = N_HEADS_A * T
    q = q_ref[...]
    qx = jnp.concatenate([q] * N_HEADS_A, axis=0)
    own = ((lax.broadcasted_iota(jnp.int32, (HT, W), 0) // T)
           == (lax.broadcasted_iota(jnp.int32, (HT, W), 1) // HEAD_DIM_A))
    qx = jnp.where(own, qx, 0.0).astype(BF16)
    pad = jnp.zeros((LANES - T, W), F32)
    kn = jnp.concatenate([kn_ref[...], pad], axis=0).astype(BF16)
    vn = jnp.concatenate([vn_ref[...], pad], axis=0).astype(BF16)
    nt = (((1,), (1,)), ((), ()))
    s_c = lax.dot_general(qx, kc_ref[...].astype(BF16), nt, preferred_element_type=F32)
    s_n = lax.dot_general(qx, kn, nt, preferred_element_type=F32)

    def multiplicity(n_cols, first_row):
        t = lax.broadcasted_iota(jnp.int32, (HT, n_cols), 0) % T
        j = lax.broadcasted_iota(jnp.int32, (HT, n_cols), 1) + first_row
        delta = R + t - j
        cnt = jnp.zeros((HT, n_cols), F32)
        for dil in DILATIONS:
            hit = (delta >= 0) & (delta <= dil * KEYS_PER_CONFIG) & (delta % dil == 0)
            cnt = cnt + jnp.where(hit, 1.0, 0.0)
        return cnt

    cnt_c = multiplicity(R, 0)
    cnt_n = multiplicity(LANES, R)
    s_c = jnp.where(cnt_c > 0.0, s_c, NEG_INF)
    s_n = jnp.where(cnt_n > 0.0, s_n, NEG_INF)
    m = jnp.maximum(jnp.max(s_c, axis=1, keepdims=True), jnp.max(s_n, axis=1, keepdims=True))
    p_c = cnt_c * jnp.exp(s_c - m)
    p_n = cnt_n * jnp.exp(s_n - m)
    den = jnp.sum(p_c, axis=1, keepdims=True) + jnp.sum(p_n, axis=1, keepdims=True)
    full = (jnp.dot(p_c.astype(BF16), vc_ref[...].astype(BF16), preferred_element_type=F32)
            + jnp.dot(p_n.astype(BF16), vn, preferred_element_type=F32)) / den
    full = jnp.where(own, full, 0.0)
    out = full[0:T, :]
    for h in range(1, N_HEADS_A):
        out = out + full[h * T:(h + 1) * T, :]
    o_ref[...] = out.astype(o_ref.dtype)
    ko_ref[0:R - T, :] = kc_ref[T:R, :]
    ko_ref[R - T:R, :] = kn_ref[...]
    vo_ref[0:R - T, :] = vc_ref[T:R, :]
    vo_ref[R - T:R, :] = vn_ref[...]


def _sample_attention(qa, ka, va, cache_k, cache_v):
    B, T, W = qa.shape
    R = cache_k.shape[1]
    assert R >= DILATIONS[-1] * KEYS_PER_CONFIG and T % 8 == 0 and T <= LANES
    new = pl.BlockSpec((None, T, W), lambda b: (b, 0, 0))
    cache = pl.BlockSpec((None, R, W), lambda b: (b, 0, 0))
    return pl.pallas_call(
        _sample_attn_kernel,
        grid=(B,),
        in_specs=[new, new, new, cache, cache],
        out_specs=[new, cache, cache],
        out_shape=[jax.ShapeDtypeStruct((B, T, W), BF16),
                   jax.ShapeDtypeStruct((B, R, W), F32),
                   jax.ShapeDtypeStruct((B, R, W), F32)],
        compiler_params=_cparams(("parallel",)),
        name="sample_attn",
    )(qa, ka, va, cache_k, cache_v)


CHUNKS = 8


def _store_chunked(ref, val):
    n = val.shape[0]
    for s in range(CHUNKS):
        ref[pl.ds(s, n, stride=CHUNKS), :] = val[:, s * LANES:(s + 1) * LANES]


def _load_chunked(ref, n):
    return jnp.concatenate([ref[pl.ds(s, n, stride=CHUNKS), :] for s in range(CHUNKS)], axis=1)


def _split2(x):
    hi = x.astype(BF16)
    return hi, (x - hi.astype(F32)).astype(BF16)


def _merge_kernel(oa_ref, ob_ref, x_ref, wo_ref, g_ref, wr_ref, br_ref,
                  h_ref, xn_ref, idx_ref, gate_ref):
    TM = x_ref.shape[0]
    mixed = (jnp.dot(oa_ref[...], wo_ref[0:WIDTH_A, :], preferred_element_type=F32)
             + jnp.dot(ob_ref[...], wo_ref[WIDTH_A:, :], preferred_element_type=F32))
    h = x_ref[...] + mixed
    h_ref[...] = h
    ms = jnp.mean(h * h, axis=-1, keepdims=True)
    xn = (h * lax.rsqrt(ms + EPS)) * g_ref[...]
    xn_ref[...] = xn.astype(xn_ref.dtype)
    xh, xl = _split2(xn)
    wh, wl = _split2(wr_ref[...])
    both = jnp.dot(jnp.concatenate([xh, xl], axis=0), wh, preferred_element_type=F32)
    tok_major = both[:TM] + both[TM:] + jnp.dot(xh, wl, preferred_element_type=F32)
    logits = jnp.transpose(tok_major)[:N_EXPERTS] + br_ref[...]
    e_iota = lax.broadcasted_iota(jnp.int32, (N_EXPERTS, TM), 0)
    vals, idxs = [], []
    for _ in range(TOP_K):
        m = jnp.max(logits, axis=0, keepdims=True)
        sel = jnp.min(jnp.where(logits == m, e_iota, N_EXPERTS), axis=0, keepdims=True)
        vals.append(m)
        idxs.append(sel)
        logits = jnp.where(e_iota == sel, NEG_INF, logits)
    ex = [jnp.exp(v - vals[0]) for v in vals]
    den = ex[0] + ex[1] + ex[2] + ex[3]
    idx_ref[...] = jnp.concatenate(idxs, axis=0)
    gate_ref[...] = jnp.concatenate([e / den for e in ex], axis=0)


def _merge(oa, ob, x, w_out, norm_g, w_router_pad, b_router, tm):
    N, D = x.shape
    full = lambda a: pl.BlockSpec(a.shape, lambda i: (0,) * a.ndim)
    row = lambda w: pl.BlockSpec((tm, w), lambda i: (i, 0))
    col = pl.BlockSpec((TOP_K, tm), lambda i: (0, i))
    return pl.pallas_call(
        _merge_kernel,
        grid=(N // tm,),
        in_specs=[row(WIDTH_A), row(WIDTH_B), row(D), full(w_out), full(norm_g), full(w_router_pad),
                  full(b_router)],
        out_specs=[row(D), row(D), col, col],
        out_shape=[jax.ShapeDtypeStruct((N, D), F32),
                   jax.ShapeDtypeStruct((N, D), BF16),
                   jax.ShapeDtypeStruct((TOP_K, N), jnp.int32),
                   jax.ShapeDtypeStruct((TOP_K, N), F32)],
        compiler_params=_cparams(("parallel",)),
        name="merge_router",
    )(oa, ob, x, w_out, norm_g, w_router_pad, b_router)


MOE_ROWS = 512
TOKEN_TILE = 256
DMA_ROWS = 32
STAGE_ROWS = TOKEN_TILE * TOP_K


def _expert_row(col):
    r = lax.broadcasted_iota(jnp.int32, (N_EXPERTS, LANES), 0)
    c = lax.broadcasted_iota(jnp.int32, (N_EXPERTS, LANES), 1)
    return jnp.sum(jnp.where(r == c, col, 0.0), axis=0, keepdims=True)


def _expert_prefix(col):
    r = lax.broadcasted_iota(jnp.int32, (N_EXPERTS, LANES), 0)
    c = lax.broadcasted_iota(jnp.int32, (N_EXPERTS, LANES), 1)
    return jnp.sum(jnp.where(c < r, _expert_row(col), 0.0), axis=1, keepdims=True)


PIECE_SIZES = (DMA_ROWS, 16, 8, 4, 2, 1)
assert all(a == 2 * b for a, b in zip(PIECE_SIZES, PIECE_SIZES[1:])) and PIECE_SIZES[-1] == 1
META_ROWS = 16
assert 2 * len(PIECE_SIZES) + 1 <= META_ROWS and TOKEN_TILE * TOP_K // DMA_ROWS <= LANES


def _piece_list(count, src, dst, stride):
    first = _expert_prefix(count)
    f = lax.broadcasted_iota(jnp.int32, (N_EXPERTS, LANES), 1).astype(F32)
    owner = jnp.sum(jnp.where(first + count <= f, 1.0, 0.0), axis=0, keepdims=True)
    hit = lax.broadcasted_iota(jnp.int32, (N_EXPERTS, LANES), 0).astype(F32) == owner
    pick = lambda col: jnp.sum(jnp.where(hit, col, 0.0), axis=0, keepdims=True)
    j = f[0:1, :] - pick(first)
    return pick(src) + stride * j, pick(dst) + stride * j


def _route_kernel(idx_all_ref, idx_ref, pos_ref, meta_ref, be_ref, misc_ref, carry_ref, start_ref):
    i = pl.program_id(0)
    TT = idx_ref.shape[1]
    NBP = be_ref.shape[1]

    @pl.when(i == 0)
    def _():
        idx_all = idx_all_ref[...]
        e_all = lax.broadcasted_iota(jnp.int32, (N_EXPERTS, idx_all.shape[1]), 0)
        tot = jnp.zeros((N_EXPERTS, 1), F32)
        for k in range(TOP_K):
            tot = tot + jnp.sum(jnp.where(idx_all[k:k + 1, :] == e_all, 1.0, 0.0), axis=1, keepdims=True)
        padded = jnp.floor((tot + (MOE_ROWS - 1)) / MOE_ROWS) * MOE_ROWS
        start = _expert_prefix(padded)
        start_ref[...] = start
        carry_ref[...] = jnp.zeros_like(carry_ref)
        end = start + padded
        block_start = lax.broadcasted_iota(jnp.int32, (N_EXPERTS, NBP), 1).astype(F32) * MOE_ROWS
        be = jnp.sum(jnp.where(end <= block_start, 1.0, 0.0), axis=0, keepdims=True)
        be_ref[...] = jnp.minimum(be, N_EXPERTS - 1).astype(jnp.int32)
        n_used = jnp.broadcast_to(jnp.sum(padded, axis=0, keepdims=True) / MOE_ROWS, (1, LANES))
        zero = jnp.zeros((1, LANES), F32)
        misc_ref[...] = jnp.concatenate([n_used, _expert_row(start + tot)] + [zero] * 6,
                                        axis=0).astype(jnp.int32)

    idx = idx_ref[...]
    e_iota = lax.broadcasted_iota(jnp.int32, (N_EXPERTS, TT), 0)
    onehot = [idx[k:k + 1, :] == e_iota for k in range(TOP_K)]
    cnt = jnp.zeros((N_EXPERTS, TT), F32)
    for oh in onehot:
        cnt = cnt + jnp.where(oh, 1.0, 0.0)
    tile_tot = jnp.sum(cnt, axis=1, keepdims=True)
    earlier = (lax.broadcasted_iota(jnp.int32, (TT, TT), 0)
               < lax.broadcasted_iota(jnp.int32, (TT, TT), 1))
    before = jnp.dot(cnt.astype(BF16), jnp.where(earlier, 1.0, 0.0).astype(BF16),
                     preferred_element_type=F32)
    seg = _expert_prefix(tile_tot)
    where_staged = seg + before
    rows = [jnp.sum(jnp.where(oh, where_staged, 0.0), axis=0, keepdims=True) for oh in onehot]
    pos_ref[...] = jnp.concatenate(rows, axis=0).astype(jnp.int32)
    slot = start_ref[...] + carry_ref[...]
    whole = jnp.floor(tile_tot / DMA_ROWS)
    covered = whole * DMA_ROWS
    lists = list(_piece_list(whole, seg, slot, float(DMA_ROWS)))
    counts = [jnp.sum(whole, axis=0, keepdims=True)]
    for size in PIECE_SIZES[1:]:
        has = jnp.floor((tile_tot - covered) / size)
        lists += _piece_list(has, seg + covered, slot + covered, 0.0)
        counts.append(jnp.sum(has, axis=0, keepdims=True))
        covered = covered + has * size
    lane = lax.broadcasted_iota(jnp.int32, (1, LANES), 1)
    count_row = jnp.zeros((1, LANES), F32)
    for k, c in enumerate(counts):
        count_row = jnp.where(lane == k, c, count_row)
    zero = jnp.zeros((1, LANES), F32)
    meta_ref[...] = jnp.concatenate(lists + [count_row] + [zero] * (META_ROWS - len(lists) - 1),
                                    axis=0).astype(jnp.int32)
    carry_ref[...] += tile_tot


def _route(idx, n_blocks):
    _, N = idx.shape
    nbp = -(-n_blocks // LANES) * LANES
    tile = pl.BlockSpec((TOP_K, TOKEN_TILE), lambda i: (0, i))
    return pl.pallas_call(
        _route_kernel,
        grid=(N // TOKEN_TILE,),
        in_specs=[pl.BlockSpec((TOP_K, N), lambda i: (0, 0)), tile],
        out_specs=[tile,
                   pl.BlockSpec((META_ROWS, LANES), lambda i: (i, 0)),
                   pl.BlockSpec((1, nbp), lambda i: (0, 0)),
                   pl.BlockSpec((8, LANES), lambda i: (0, 0))],
        out_shape=[jax.ShapeDtypeStruct((TOP_K, N), jnp.int32),
                   jax.ShapeDtypeStruct((N // TOKEN_TILE * META_ROWS, LANES), jnp.int32),
                   jax.ShapeDtypeStruct((1, nbp), jnp.int32),
                   jax.ShapeDtypeStruct((8, LANES), jnp.int32)],
        scratch_shapes=[pltpu.VMEM((N_EXPERTS, 1), F32)] * 2,
        compiler_params=_cparams(("arbitrary",)),
        name="route",
    )(idx, idx)


def _rows(ref, first_row, n_rows):
    return ref.at[pl.ds(pl.multiple_of(first_row * CHUNKS, CHUNKS), n_rows * CHUNKS), :]


def _for_each_run_piece(meta_ref, fn):
    for k, size in enumerate(PIECE_SIZES):
        def body(j, c, k=k, size=size):
            fn(meta_ref[2 * k + 1, j], meta_ref[2 * k, j], size)
            return c

        lax.fori_loop(0, meta_ref[2 * len(PIECE_SIZES), k], body, 0)


def _wait_rows(n_rows, src_ref, dst_ref, sem):
    @pl.when(n_rows > 0)
    def _():
        n = n_rows * CHUNKS
        pltpu.make_async_copy(src_ref.at[pl.ds(0, n), :], dst_ref.at[pl.ds(0, n), :], sem).wait()


PAD_ROWS = MOE_ROWS


def _zero_padding(misc_ref, xs_ref, zeros_ref, sem, n_blocks):
    zeros_ref[...] = jnp.zeros_like(zeros_ref)

    def pad_copy(e):
        first = pl.multiple_of(misc_ref[1, e] * CHUNKS, CHUNKS)
        return pltpu.make_async_copy(zeros_ref, xs_ref.at[pl.ds(first, PAD_ROWS * CHUNKS), :], sem)

    def tail_copy(b):
        first = pl.multiple_of(b * (MOE_ROWS * CHUNKS), MOE_ROWS * CHUNKS)
        return pltpu.make_async_copy(zeros_ref.at[pl.ds(0, MOE_ROWS * CHUNKS), :],
                                     xs_ref.at[pl.ds(first, MOE_ROWS * CHUNKS), :], sem)

    def pad(e, c):
        pad_copy(e).start()
        pad_copy(e).wait()
        return c

    def tail(start_not_wait):
        def body(b, c):
            tail_copy(b).start() if start_not_wait else tail_copy(b).wait()
            return c

        lax.fori_loop(misc_ref[0, 0], n_blocks + 1, body, 0)

    lax.fori_loop(0, N_EXPERTS, pad, 0)
    tail(True)
    tail(False)


def _selection(pos_ref, fill_ref=None):
    TT = pos_ref.shape[1]
    p_iota = lax.broadcasted_iota(jnp.int32, (STAGE_ROWS, TT), 0)
    sel = jnp.zeros((STAGE_ROWS, TT), F32)
    for k in range(TOP_K):
        val = 1.0 if fill_ref is None else fill_ref[k:k + 1, :]
        sel = jnp.where(pos_ref[k:k + 1, :] == p_iota, val, sel)
    return sel.astype(BF16)


def _dispatch_kernel(misc_ref, meta_ref, pos_ref, *refs, first_tiles, n_tiles, n_blocks):
    x_refs = refs[:len(first_tiles)]
    xs_ref, stage_ref, zeros_ref, sems, pad_sem = refs[len(first_tiles):]
    i = pl.program_id(0)
    TT = pos_ref.shape[1]
    slot = i % 2

    @pl.when(i == 0)
    def _():
        _zero_padding(misc_ref, xs_ref, zeros_ref, pad_sem, n_blocks)

    x = x_refs[0][...]
    for t0, ref in zip(first_tiles[1:], x_refs[1:]):
        x = jnp.where(i >= t0, ref[...], x)
    staged = jnp.dot(_selection(pos_ref), x, preferred_element_type=F32)
    stage = stage_ref.at[slot]
    _store_chunked(stage, staged)
    _for_each_run_piece(meta_ref, lambda dst, src, n: pltpu.make_async_copy(
        _rows(stage, src, n), _rows(xs_ref, dst, n), sems.at[slot]).start())

    @pl.when(i > 0)
    def _():
        _wait_rows(TT * TOP_K, stage_ref.at[1 - slot], xs_ref, sems.at[1 - slot])

    @pl.when(i == n_tiles - 1)
    def _():
        _wait_rows(TT * TOP_K, stage, xs_ref, sems.at[slot])


def _dispatch(misc, meta, pos, xns, n_blocks):
    D = xns[0].shape[1]
    tiles = [x.shape[0] // TOKEN_TILE for x in xns]
    first_tiles = tuple(sum(tiles[:g]) for g in range(len(tiles)))

    def x_spec(t0, nt):
        return pl.BlockSpec((TOKEN_TILE, D), lambda i: (jnp.clip(i - t0, 0, nt - 1), 0))

    return pl.pallas_call(
        functools.partial(_dispatch_kernel, first_tiles=first_tiles, n_tiles=sum(tiles), n_blocks=n_blocks),
        grid=(sum(tiles),),
        in_specs=[pl.BlockSpec(memory_space=pltpu.SMEM),
                  pl.BlockSpec((META_ROWS, LANES), lambda i: (i, 0), memory_space=pltpu.SMEM),
                  pl.BlockSpec((TOP_K, TOKEN_TILE), lambda i: (0, i))]
                 + [x_spec(t0, nt) for t0, nt in zip(first_tiles, tiles)],
        out_specs=pl.BlockSpec(memory_space=pl.ANY),
        out_shape=jax.ShapeDtypeStruct(((n_blocks + 1) * MOE_ROWS * CHUNKS, LANES), F32),
        scratch_shapes=[pltpu.VMEM((2, STAGE_ROWS * CHUNKS, LANES), F32),
                        pltpu.VMEM((PAD_ROWS * CHUNKS, LANES), F32),
                        pltpu.SemaphoreType.DMA((2,)),
                        pltpu.SemaphoreType.DMA(())],
        compiler_params=_cparams(("arbitrary",)),
        name="dispatch",
    )(misc, meta, pos, *xns)


def _expert_kernel(be_ref, nused_ref, xs_ref, wgu_ref, bgu_ref, wd_ref, bd_ref, ys_ref,
                   wgu_bf, wd_bf):
    i = pl.program_id(0)
    D_FF = wd_ref.shape[0]
    new_expert = jnp.logical_or(i == 0, be_ref[i] != be_ref[jnp.maximum(i - 1, 0)])

    @pl.when(jnp.logical_and(i < nused_ref[0], new_expert))
    def _():
        wgu_bf[...] = wgu_ref[...].astype(BF16)
        wd_bf[...] = wd_ref[...].astype(BF16)

    @pl.when(i < nused_ref[0])
    def _():
        x = _load_chunked(xs_ref, MOE_ROWS).astype(BF16)
        hdn = jnp.dot(x, wgu_bf[...], preferred_element_type=F32) + bgu_ref[...]
        glu = jnp.minimum(hdn[:, :D_FF], SWIGLU_LIMIT)
        lin = jnp.clip(hdn[:, D_FF:], -SWIGLU_LIMIT, SWIGLU_LIMIT)
        act = glu * (1.0 / (1.0 + jnp.exp(-SWIGLU_ALPHA * glu))) * (lin + 1.0)
        y = jnp.dot(act.astype(BF16), wd_bf[...], preferred_element_type=F32) + bd_ref[...]
        _store_chunked(ys_ref, y)

    @pl.when(i >= nused_ref[0])
    def _():
        ys_ref[...] = jnp.zeros_like(ys_ref)


def _experts(block_expert, n_used, xs, w_gate_up, b_gate_up, w_down, b_down, n_blocks):
    E, D, F2 = w_gate_up.shape
    D_FF = w_down.shape[1]
    rows = pl.BlockSpec((MOE_ROWS * CHUNKS, LANES), lambda i, be, nu: (i, 0))
    rows_in = pl.BlockSpec((MOE_ROWS * CHUNKS, LANES), lambda i, be, nu: (jnp.minimum(i, nu[0] - 1), 0))
    grid_spec = pltpu.PrefetchScalarGridSpec(
        num_scalar_prefetch=2,
        grid=(n_blocks,),
        in_specs=[rows_in,
                  pl.BlockSpec((None, D, F2), lambda i, be, nu: (be[i], 0, 0)),
                  pl.BlockSpec((None, 1, F2), lambda i, be, nu: (be[i], 0, 0)),
                  pl.BlockSpec((None, D_FF, D), lambda i, be, nu: (be[i], 0, 0)),
                  pl.BlockSpec((None, 1, D), lambda i, be, nu: (be[i], 0, 0))],
        out_specs=rows,
        scratch_shapes=[pltpu.VMEM((D, F2), BF16), pltpu.VMEM((D_FF, D), BF16)],
    )
    return pl.pallas_call(
        _expert_kernel,
        grid_spec=grid_spec,
        out_shape=jax.ShapeDtypeStruct((n_blocks * MOE_ROWS * CHUNKS, LANES), F32),
        compiler_params=_cparams(("arbitrary",)),
        name="experts",
    )(block_expert, n_used, xs, w_gate_up, b_gate_up.reshape(E, 1, F2), w_down, b_down.reshape(E, 1, D))


def _combine_kernel(meta_ref, next_meta_ref, pos_ref, gate_ref, h_ref, g_ref, ys_ref, y_ref, stage_ref, sems,
                    *, n_tiles):
    i = pl.program_id(0)
    slot = i % 2

    def fetch(meta, s):
        _for_each_run_piece(meta, lambda src, dst, n: pltpu.make_async_copy(
            _rows(ys_ref, src, n), _rows(stage_ref.at[s], dst, n), sems.at[s]).start())

    @pl.when(i == 0)
    def _():
        fetch(meta_ref, 0)

    if n_tiles > 1:
        @pl.when(i + 1 < n_tiles)
        def _():
            fetch(next_meta_ref, 1 - slot)

    weights = _selection(pos_ref, gate_ref)
    _wait_rows(STAGE_ROWS, ys_ref, stage_ref.at[slot], sems.at[slot])
    staged = _load_chunked(stage_ref.at[slot], STAGE_ROWS).astype(BF16)
    moe = lax.dot_general(weights, staged, (((0,), (0,)), ((), ())), preferred_element_type=F32)
    hf = h_ref[...] + moe
    ms = jnp.mean(hf * hf, axis=-1, keepdims=True)
    y_ref[...] = (hf * lax.rsqrt(ms + EPS)) * g_ref[...]


def _combine(meta, pos, gates, h, norm_g, ys, tile0):
    n, D = h.shape
    n_tiles = n // TOKEN_TILE
    return pl.pallas_call(
        functools.partial(_combine_kernel, n_tiles=n_tiles),
        grid=(n_tiles,),
        in_specs=[pl.BlockSpec((META_ROWS, LANES), lambda i: (tile0 + i, 0), memory_space=pltpu.SMEM),
                  pl.BlockSpec((META_ROWS, LANES), lambda i: (tile0 + jnp.minimum(i + 1, n_tiles - 1), 0),
                               memory_space=pltpu.SMEM),
                  pl.BlockSpec((TOP_K, TOKEN_TILE), lambda i: (0, tile0 + i)),
                  pl.BlockSpec((TOP_K, TOKEN_TILE), lambda i: (0, i)),
                  pl.BlockSpec((TOKEN_TILE, D), lambda i: (i, 0)),
                  pl.BlockSpec((1, D), lambda i: (0, 0)),
                  pl.BlockSpec(memory_space=pl.ANY)],
        out_specs=pl.BlockSpec((TOKEN_TILE, D), lambda i: (i, 0)),
        out_shape=jax.ShapeDtypeStruct((n, D), F32),
        scratch_shapes=[pltpu.VMEM((2, STAGE_ROWS * CHUNKS, LANES), F32), pltpu.SemaphoreType.DMA((2,))],
        compiler_params=_cparams(("arbitrary",)),
        name="combine",
    )(meta, meta, pos, gates, h, norm_g, ys)


def _prep_weights(w_in, w_alpha):
    w_main = w_in[:, :PROJ_MAIN].astype(BF16)
    w_lr = jnp.pad(w_in[:, PROJ_MAIN:], ((0, 0), (0, LANES - GATE_RANK))).astype(BF16)
    w_al = jnp.pad(w_alpha, ((0, LANES - GATE_RANK), (0, 0))).astype(BF16)
    return w_main, w_lr, w_al


def kernel(x_prompt, x_sample, cache_swa_k, cache_swa_v, state_gla, norm_mix_g, w_in, w_alpha, b_alpha, gla_norm_g, w_out, norm_ffn_g, w_router, b_router, w_gate_up, b_gate_up, w_down, b_down, norm_final_g):
    B, S, D = x_prompt.shape
    Bs, Ts, _ = x_sample.shape
    assert w_in.shape[0] == 1, "single-layer trunk"
    l = 0
    R = cache_swa_k.shape[2]
    rows_p = min(DILATIONS[-1] * KEYS_PER_CONFIG, S)
    w_main, w_lr, w_al = _prep_weights(w_in[l], w_alpha[l])
    g_mix = norm_mix_g[l][None]
    b_al = b_alpha[l][None]
    g_gla = gla_norm_g[l][None]

    pos_p = jnp.arange(S, dtype=jnp.int32)
    qa, ka, va, qb, kb, vb, zg, gb, k_tail, v_tail = _project(x_prompt, pos_p, g_mix, w_main, w_lr, w_al, b_al,
                                                              PROJ_TILE, tail_rows=rows_p)
    oa_p = _prompt_attention(qa, ka, va)
    ob_p, st_p = _gla(qb, kb, gb, vb, zg, jnp.zeros((B, N_HEADS_B, DK_B, DV_B), F32), g_gla,
                      GLA_TILE, GLA_CHUNK)
    k_prompt = k_tail.reshape(1, B, rows_p, N_HEADS_A, HEAD_DIM_A)
    v_prompt = v_tail.reshape(1, B, rows_p, N_HEADS_A, HEAD_DIM_A)

    pos_s = PAST_LEN + (jnp.arange(Bs * Ts, dtype=jnp.int32) % Ts)
    proj_s = _project(x_sample.reshape(1, Bs * Ts, D), pos_s, g_mix, w_main, w_lr, w_al, b_al, Bs * Ts)
    qa_s, ka_s, va_s, qb_s, kb_s, vb_s, zg_s, gb_s = [t.reshape(Bs, Ts, -1) for t in proj_s]
    oa_s, k_sample, v_sample = _sample_attention(qa_s, ka_s, va_s,
                                                 cache_swa_k[l].reshape(Bs, R, WIDTH_A),
                                                 cache_swa_v[l].reshape(Bs, R, WIDTH_A))
    ob_s, st_s = _gla(qb_s, kb_s, gb_s, vb_s, zg_s, state_gla[l], g_gla, Ts, Ts)

    w_out_bf = w_out[l].astype(BF16)
    g_ffn = norm_ffn_g[l][None]
    w_router_pad = jnp.pad(w_router[l], ((0, 0), (0, LANES - N_EXPERTS)))
    b_router_c = b_router[l][:, None]
    Np, Ns = B * S, Bs * Ts
    h_p, xn_p, idx_p, gate_p = _merge(oa_p.reshape(Np, WIDTH_A), ob_p.reshape(Np, WIDTH_B),
                                      x_prompt.reshape(Np, D), w_out_bf, g_ffn, w_router_pad, b_router_c,
                                      MERGE_TILE)
    h_s, xn_s, idx_s, gate_s = _merge(oa_s.reshape(Ns, WIDTH_A), ob_s.reshape(Ns, WIDTH_B),
                                      x_sample.reshape(Ns, D), w_out_bf, g_ffn, w_router_pad, b_router_c,
                                      Ns)

    y_p, y_s = _moe([(xn_p, idx_p, gate_p, h_p), (xn_s, idx_s, gate_s, h_s)],
                    w_gate_up[l], b_gate_up[l], w_down[l], b_down[l], norm_final_g[None])
    return (y_p.reshape(B, S, D), y_s.reshape(Bs, Ts, D), k_prompt, v_prompt, st_p[None],
            k_sample.reshape(1, Bs, R, N_HEADS_A, HEAD_DIM_A),
            v_sample.reshape(1, Bs, R, N_HEADS_A, HEAD_DIM_A), st_s[None])


PAST_LEN = 16384
PROJ_TILE = 1024
MERGE_TILE = 1024
GLA_TILE = 1024
GLA_CHUNK = 64


def _moe(groups, w_gate_up, b_gate_up, w_down, b_down, g_final):
    sizes = [g[3].shape[0] for g in groups]
    N = sum(sizes)
    assert all(n % TOKEN_TILE == 0 for n in sizes)
    n_blocks = -(-(N * TOP_K + N_EXPERTS * (MOE_ROWS - 1)) // MOE_ROWS)
    idx = jnp.concatenate([g[1] for g in groups], axis=1)
    pos, meta, block_expert, misc = _route(idx, n_blocks)
    first_tile = [sum(sizes[:i]) // TOKEN_TILE for i in range(len(sizes))]
    xs = _dispatch(misc, meta, pos, [g[0] for g in groups], n_blocks)
    ys = _experts(block_expert[0], misc[0, :1], xs, w_gate_up, b_gate_up, w_down, b_down, n_blocks)
    return [_combine(meta, pos, gates, h, g_final, ys, t0)
            for (_, _, gates, h), t0 in zip(groups, first_tile)]
```

```python
import functools

import jax
import jax.numpy as jnp
from jax import lax
from jax.experimental import pallas as pl
from jax.experimental.pallas import tpu as pltpu

F32 = jnp.float32
BF16 = jnp.bfloat16

N_HEADS_A = 8
HEAD_DIM_A = 64
WIDTH_A = N_HEADS_A * HEAD_DIM_A
N_HEADS_B = 4
DK_B = 64
DV_B = 128
QK_B = N_HEADS_B * DK_B
WIDTH_B = N_HEADS_B * DV_B
GATE_RANK = 16
GATE_LOGIT_NORM = 16.0
DILATIONS = (1, 4, 16)
KEYS_PER_CONFIG = 128
ROPE_THETA = 10000.0
N_EXPERTS = 32
TOP_K = 4
SWIGLU_ALPHA = 1.702
SWIGLU_LIMIT = 7.0
EPS = 1e-6

LANES = 128
VMEM_LIMIT = 56 * 1024 * 1024


def _cparams(sem, vmem=VMEM_LIMIT):
    return pltpu.CompilerParams(dimension_semantics=sem, vmem_limit_bytes=vmem)


PROJ_MAIN = 3 * WIDTH_A + 2 * QK_B + 2 * WIDTH_B


def _rope_tables(pos):
    half = HEAD_DIM_A // 2
    inv = ROPE_THETA ** (-jnp.arange(half, dtype=F32) / half)
    ang = pos.astype(F32)[:, None] * inv[None, :]
    cos = jnp.cos(ang)
    sin = jnp.sin(ang)
    cos_t = jnp.concatenate([cos, cos, cos, cos], axis=-1)
    sin_t = jnp.concatenate([-sin, sin, -sin, sin], axis=-1)
    return cos_t, sin_t


def _rope_block(t, cos, sin, first_half):
    half = HEAD_DIM_A // 2
    partner = jnp.where(first_half, pltpu.roll(t, LANES - half, 1), pltpu.roll(t, half, 1))
    return t * cos + partner * sin


def _proj_kernel(x_ref, g_ref, w_ref, wlr_ref, wa_ref, ba_ref, cos_ref, sin_ref,
                 qa_ref, ka_ref, va_ref, qb_ref, kb_ref, vb_ref, zg_ref, gb_ref, *rest,
                 first_tail_tile, n_tail, grid):
    tail_refs, (xbuf, sems) = rest[:n_tail], rest[n_tail:]
    TM = xbuf.shape[1]
    n_steps = grid[0] * grid[1]
    step = pl.program_id(0) * grid[1] + pl.program_id(1)

    def x_copy(n, slot):
        rows = pl.ds(pl.multiple_of((n % grid[1]) * TM, TM), TM)
        return pltpu.make_async_copy(x_ref.at[n // grid[1], rows, :], xbuf.at[slot], sems.at[slot])

    @pl.when(step == 0)
    def _():
        for n in range(min(2, n_steps)):
            x_copy(n, n).start()

    if n_steps > 2:
        @pl.when(step + 2 < n_steps)
        def _():
            x_copy(step + 2, (step + 2) % 3).start()

    x_copy(step, step % 3).wait()
    x = xbuf[step % 3]
    ms = jnp.mean(x * x, axis=-1, keepdims=True)
    xn = ((x * lax.rsqrt(ms + EPS)) * g_ref[...]).astype(BF16)

    def cols(lo, hi):
        return jnp.dot(xn, w_ref[:, lo:hi], preferred_element_type=F32)

    cos = cos_ref[...]
    sin = sin_ref[...]
    lane = lax.broadcasted_iota(jnp.int32, cos.shape, 1)
    first_half = (lane % HEAD_DIM_A) < (HEAD_DIM_A // 2)
    q = cols(0, WIDTH_A)
    k = cols(WIDTH_A, 2 * WIDTH_A)
    for j in range(WIDTH_A // LANES):
        sl = slice(j * LANES, (j + 1) * LANES)
        qa_ref[:, sl] = _rope_block(q[:, sl], cos, sin, first_half) * (HEAD_DIM_A ** -0.5)
        ka_ref[:, sl] = _rope_block(k[:, sl], cos, sin, first_half)
    o = 2 * WIDTH_A
    va_ref[...] = cols(o, o + WIDTH_A)
    o += WIDTH_A
    qb_ref[...] = cols(o, o + QK_B) * (DK_B ** -0.5)
    o += QK_B
    kb_ref[...] = cols(o, o + QK_B)
    o += QK_B
    vb_ref[...] = cols(o, o + WIDTH_B).astype(vb_ref.dtype)
    o += WIDTH_B
    zg_ref[...] = cols(o, o + WIDTH_B)
    lr = jnp.dot(xn, wlr_ref[...], preferred_element_type=F32)
    z = jnp.dot(lr.astype(BF16), wa_ref[...], preferred_element_type=F32) + ba_ref[...]
    logsig = jnp.minimum(z, 0.0) - jnp.log(1.0 + jnp.exp(-jnp.abs(z)))
    gb_ref[...] = logsig / GATE_LOGIT_NORM
    if tail_refs:
        @pl.when(pl.program_id(1) >= first_tail_tile)
        def _():
            tail_refs[0][...] = ka_ref[...]
            tail_refs[1][...] = va_ref[...]


def _project(x, pos, norm_g, w_main, w_lr, w_alpha, b_alpha, tm, tail_rows=0):
    B, T, D = x.shape
    assert tail_rows % tm == 0
    cos_t, sin_t = _rope_tables(pos)
    grid = (B, T // tm)
    first_tail_tile = (T - tail_rows) // tm
    row = lambda w: pl.BlockSpec((None, tm, w), lambda b, i: (b, i, 0))
    tail = pl.BlockSpec((None, tm, WIDTH_A), lambda b, i: (b, jnp.maximum(i - first_tail_tile, 0), 0))
    full = lambda a: pl.BlockSpec(a.shape, lambda b, i: (0,) * a.ndim)
    tab = pl.BlockSpec((tm, LANES), lambda b, i: (i, 0))
    widths = (WIDTH_A, WIDTH_A, WIDTH_A, QK_B, QK_B, WIDTH_B, WIDTH_B, QK_B)
    dtypes = (F32, F32, F32, F32, F32, BF16, F32, F32)
    n_tail = 2 if tail_rows else 0
    return pl.pallas_call(
        functools.partial(_proj_kernel, first_tail_tile=first_tail_tile, n_tail=n_tail, grid=grid),
        grid=grid,
        in_specs=[pl.BlockSpec(memory_space=pl.ANY), full(norm_g), full(w_main), full(w_lr), full(w_alpha),
                  full(b_alpha), tab, tab],
        out_specs=[row(w) for w in widths] + [tail] * n_tail,
        out_shape=[jax.ShapeDtypeStruct((B, T, w), dt) for w, dt in zip(widths, dtypes)]
                  + [jax.ShapeDtypeStruct((B, tail_rows, WIDTH_A), F32)] * n_tail,
        scratch_shapes=[pltpu.VMEM((3, tm, D), F32), pltpu.SemaphoreType.DMA((3,))],
        compiler_params=_cparams(("arbitrary", "arbitrary")),
        name="proj",
    )(x, norm_g, w_main, w_lr, w_alpha, b_alpha, cos_t, sin_t)


Q_BLOCK = 128
NEG_INF = float("-inf")


def _attn_block(q, kb, v1, mask, head0, state):
    QB = q.shape[0]
    q2 = jnp.concatenate([jnp.where(head0, q, 0.0), jnp.where(head0, 0.0, q)], axis=0).astype(BF16)
    s = lax.dot_general(q2, kb, (((1,), (1,)), ((), ())), preferred_element_type=F32)
    s = jnp.where(mask, s, NEG_INF)
    if state is None:
        m2 = jnp.max(s, axis=1, keepdims=True)
    else:
        prev = jnp.concatenate([jnp.where(head0, state[0], NEG_INF),
                                jnp.where(head0, NEG_INF, state[0])], axis=0)
        m2 = jnp.max(jnp.concatenate([s, prev], axis=1), axis=1, keepdims=True)
    p = jnp.exp(s - m2)
    pv = jnp.dot(p.astype(BF16), v1, preferred_element_type=F32)
    m_full = jnp.where(head0, m2[:QB], m2[QB:])
    l_full = jnp.where(head0, pv[:QB, LANES:], pv[QB:, LANES:])
    pv_full = jnp.where(head0, pv[:QB, :LANES], pv[QB:, :LANES])
    if state is None:
        return m_full, l_full, pv_full
    a = jnp.exp(state[0] - m_full)
    return m_full, a * state[1] + l_full, a * state[2] + pv_full


def _attn_kernel(q_ref, k_ref, v_ref, o_ref, m_ref, l_ref, acc_ref):
    S = q_ref.shape[0]
    QB = Q_BLOCK
    lane = lax.broadcasted_iota(jnp.int32, (QB, LANES), 1)
    head0 = lane < HEAD_DIM_A
    qq = lax.broadcasted_iota(jnp.int32, (2 * QB, 2 * QB), 0) % QB
    kk = lax.broadcasted_iota(jnp.int32, (2 * QB, 2 * QB), 1)
    band = jnp.logical_and(kk >= qq, kk - qq <= KEYS_PER_CONFIG)
    causal = (lax.broadcasted_iota(jnp.int32, (2 * QB, QB), 1)
              <= lax.broadcasted_iota(jnp.int32, (2 * QB, QB), 0) % QB)

    GROUP = 8

    for ci, dil in enumerate(sorted(DILATIONS, reverse=True)):
        nblk = S // (dil * QB)
        assert nblk % GROUP == 0 or GROUP % nblk == 0

        def rows(start, n, dil=dil):
            return pl.ds(start, n) if dil == 1 else pl.ds(start, n, stride=dil)

        def do_group(blocks, ci=ci, dil=dil, rows=rows):
            chunks = {}

            def kv_chunk(r, rkey, jbase, c):
                if (rkey, c) not in chunks:
                    ks = rows(r + dil * QB * (jbase + c), QB)
                    v = v_ref[ks, :].astype(BF16)
                    chunks[(rkey, c)] = (k_ref[ks, :].astype(BF16),
                                         jnp.concatenate([v, jnp.ones(v.shape, BF16)], axis=1))
                return chunks[(rkey, c)]

            loaded = []
            for r, rkey, jbase, joff in blocks:
                first = isinstance(jbase, int) and jbase + joff == 0
                qs = rows(r + dil * QB * (jbase + joff), QB)
                parts = [kv_chunk(r, rkey, jbase, joff)]
                if not first:
                    parts.insert(0, kv_chunk(r, rkey, jbase, joff - 1))
                kb = jnp.concatenate([p[0] for p in parts], axis=0)
                v1 = jnp.concatenate([p[1] for p in parts], axis=0)
                state = None if ci == 0 else (m_ref[qs, :], l_ref[qs, :], acc_ref[qs, :])
                loaded.append((qs, q_ref[qs, :], kb, v1, causal if first else band, state))
            results = [_attn_block(q, kb, v1, mask, head0, state) for _, q, kb, v1, mask, state in loaded]
            for (qs, *_), (m, l, acc) in zip(loaded, results):
                m_ref[qs, :] = m
                l_ref[qs, :] = l
                acc_ref[qs, :] = acc

        if nblk >= GROUP:
            def residue(r, carry, nblk=nblk, do_group=do_group):
                do_group([(r, 0, 0, j) for j in range(GROUP)])

                def rest(g, c):
                    do_group([(r, 0, GROUP * g, u) for u in range(GROUP)])
                    return c

                return lax.fori_loop(1, nblk // GROUP, rest, carry)

            lax.fori_loop(0, dil, residue, 0)
        else:
            per = 2 * GROUP // nblk

            def residues(g, carry, nblk=nblk, per=per, do_group=do_group):
                do_group([(g * per + i, i, 0, j) for j in range(nblk) for i in range(per)])
                return carry

            lax.fori_loop(0, dil // per, residues, 0)

    def finish(i, c):
        rs = pl.ds(pl.multiple_of(i * QB, QB), QB)
        o_ref[rs, :] = (acc_ref[rs, :] / l_ref[rs, :]).astype(o_ref.dtype)
        return c

    lax.fori_loop(0, S // QB, finish, 0, unroll=4)


def _prompt_attention(qa, ka, va):
    B, S, W = qa.shape
    spec = pl.BlockSpec((None, S, LANES), lambda b, hp: (b, 0, hp))
    return pl.pallas_call(
        _attn_kernel,
        grid=(B, W // LANES),
        in_specs=[spec, spec, spec],
        out_specs=spec,
        out_shape=jax.ShapeDtypeStruct((B, S, W), BF16),
        scratch_shapes=[pltpu.VMEM((S, LANES), F32)] * 3,
        compiler_params=_cparams(("parallel", "parallel")),
        name="prompt_attn",
    )(qa, ka, va)


def _gla_kernel(q_ref, k_ref, g_ref, v_ref, z_ref, s0_ref, ng_ref, o_ref, sfin_ref, st_ref, *, chunk):
    C = chunk
    TS = q_ref.shape[0]
    n_pairs = N_HEADS_B // 2
    PW = 2 * DV_B
    t_idx = pl.program_id(1)

    lane_k = lax.broadcasted_iota(jnp.int32, (C, LANES), 1)
    head0 = lane_k < DK_B
    row_k = lax.broadcasted_iota(jnp.int32, (C, LANES), 0)
    tri2 = (lax.broadcasted_iota(jnp.int32, (2 * C, C), 1)
            <= lax.broadcasted_iota(jnp.int32, (2 * C, C), 0) % C)
    value_head0 = lax.broadcasted_iota(jnp.int32, (C, PW), 1) < DV_B
    bd_mask = ((lax.broadcasted_iota(jnp.int32, (PW, LANES), 0) // DV_B)
               == (lax.broadcasted_iota(jnp.int32, (PW, LANES), 1) // DK_B))

    def prefix_rows(x):
        shift = 1
        while shift < C:
            x = x + jnp.where(row_k >= shift, pltpu.roll(x, shift, 0), 0.0)
            shift *= 2
        return x

    @pl.when(t_idx == 0)
    def _():
        for p in range(n_pairs):
            for h in range(2):
                blk = jnp.transpose(s0_ref[2 * p + h])
                pad = jnp.zeros((DV_B, DK_B), F32)
                row = jnp.concatenate([blk, pad] if h == 0 else [pad, blk], axis=1)
                st_ref[p, h * DV_B:(h + 1) * DV_B, :] = row

    def chunk_body(c, carry):
        rs = pl.ds(pl.multiple_of(c * C, C), C)
        for p in range(n_pairs):
            kl = slice(p * LANES, (p + 1) * LANES)
            vl = slice(p * PW, (p + 1) * PW)
            q = q_ref[rs, kl]
            k = k_ref[rs, kl]
            g = g_ref[rs, kl]
            v = v_ref[rs, vl]
            b = prefix_rows(g)
            b_last = b[C - 1:C, :]
            b_mid = b[C // 2 - 1:C // 2, :] if C > 1 else b_last
            qe = q * jnp.exp(b - b_mid)
            ke = (k * jnp.exp(b_mid - b)).astype(BF16)
            st = st_ref[p]
            q_in = (q * jnp.exp(b)).astype(BF16)
            nt = (((1,), (1,)), ((), ()))
            o = lax.dot_general(q_in, st.astype(BF16), nt, preferred_element_type=F32)
            q2 = jnp.concatenate([jnp.where(head0, qe, 0.0), jnp.where(head0, 0.0, qe)], axis=0)
            a2 = lax.dot_general(q2.astype(BF16), ke, nt, preferred_element_type=F32)
            a2 = jnp.where(tri2, a2, 0.0).astype(BF16)
            av = jnp.dot(a2, v, preferred_element_type=F32)
            o = o + jnp.where(value_head0, av[:C], av[C:])
            k_dec = (k * jnp.exp(b_last - b)).astype(BF16)
            upd = lax.dot_general(v, k_dec, (((0,), (0,)), ((), ())), preferred_element_type=F32)
            st_ref[p] = jnp.exp(b_last) * st + jnp.where(bd_mask, upd, 0.0)
            for h in range(2):
                oh = o[:, h * DV_B:(h + 1) * DV_B]
                hl = slice((2 * p + h) * DV_B, (2 * p + h + 1) * DV_B)
                ms = jnp.mean(oh * oh, axis=-1, keepdims=True)
                z = z_ref[rs, hl]
                gated = (oh * lax.rsqrt(ms + EPS)) * ng_ref[:, hl] * (z / (1.0 + jnp.exp(-z)))
                o_ref[rs, hl] = gated.astype(o_ref.dtype)
        return carry

    lax.fori_loop(0, TS // C, chunk_body, 0, unroll=4 if (TS // C) % 4 == 0 else 1)

    @pl.when(t_idx == pl.num_programs(1) - 1)
    def _():
        for p in range(n_pairs):
            for h in range(2):
                blk = st_ref[p, h * DV_B:(h + 1) * DV_B, h * DK_B:(h + 1) * DK_B]
                sfin_ref[2 * p + h] = jnp.transpose(blk)


def _gla(qb, kb, gb, vb, zg, state0, norm_g, ts, chunk):
    B, T, _ = qb.shape
    row = lambda w: pl.BlockSpec((None, ts, w), lambda b, i: (b, i, 0))
    st_spec = pl.BlockSpec((None, N_HEADS_B, DK_B, DV_B), lambda b, i: (b, 0, 0, 0))
    return pl.pallas_call(
        functools.partial(_gla_kernel, chunk=chunk),
        grid=(B, T // ts),
        in_specs=[row(QK_B), row(QK_B), row(QK_B), row(WIDTH_B), row(WIDTH_B), st_spec,
                  pl.BlockSpec((1, WIDTH_B), lambda b, i: (0, 0))],
        out_specs=[row(WIDTH_B), st_spec],
        out_shape=[jax.ShapeDtypeStruct((B, T, WIDTH_B), BF16),
                   jax.ShapeDtypeStruct((B, N_HEADS_B, DK_B, DV_B), F32)],
        scratch_shapes=[pltpu.VMEM((N_HEADS_B // 2, 2 * DV_B, LANES), F32)],
        compiler_params=_cparams(("parallel", "arbitrary")),
        name="gla",
    )(qb, kb, gb, vb, zg, state0, norm_g)


def _sample_attn_kernel(q_ref, kn_ref, vn_ref, kc_ref, vc_ref, o_ref, ko_ref, vo_ref):
    T = q_ref.shape[0]
    R = kc_ref.shape[0]
    W = q_ref.shape[1]
    HT = N_HEADS_A * T
    q = q_ref[...]
    qx = jnp.concatenate([q] * N_HEADS_A, axis=0)
    own = ((lax.broadcasted_iota(jnp.int32, (HT, W), 0) // T)
           == (lax.broadcasted_iota(jnp.int32, (HT, W), 1) // HEAD_DIM_A))
    qx = jnp.where(own, qx, 0.0).astype(BF16)
    pad = jnp.zeros((LANES - T, W), F32)
    kn = jnp.concatenate([kn_ref[...], pad], axis=0).astype(BF16)
    vn = jnp.concatenate([vn_ref[...], pad], axis=0).astype(BF16)
    nt = (((1,), (1,)), ((), ()))
    s_c = lax.dot_general(qx, kc_ref[...].astype(BF16), nt, preferred_element_type=F32)
    s_n = lax.dot_general(qx, kn, nt, preferred_element_type=F32)

    def multiplicity(n_cols, first_row):
        t = lax.broadcasted_iota(jnp.int32, (HT, n_cols), 0) % T
        j = lax.broadcasted_iota(jnp.int32, (HT, n_cols), 1) + first_row
        delta = R + t - j
        cnt = jnp.zeros((HT, n_cols), F32)
        for dil in DILATIONS:
            hit = (delta >= 0) & (delta <= dil * KEYS_PER_CONFIG) & (delta % dil == 0)
            cnt = cnt + jnp.where(hit, 1.0, 0.0)
        return cnt

    cnt_c = multiplicity(R, 0)
    cnt_n = multiplicity(LANES, R)
    s_c = jnp.where(cnt_c > 0.0, s_c, NEG_INF)
    s_n = jnp.where(cnt_n > 0.0, s_n, NEG_INF)
    m = jnp.maximum(jnp.max(s_c, axis=1, keepdims=True), jnp.max(s_n, axis=1, keepdims=True))
    p_c = cnt_c * jnp.exp(s_c - m)
    p_n = cnt_n * jnp.exp(s_n - m)
    den = jnp.sum(p_c, axis=1, keepdims=True) + jnp.sum(p_n, axis=1, keepdims=True)
    full = (jnp.dot(p_c.astype(BF16), vc_ref[...].astype(BF16), preferred_element_type=F32)
            + jnp.dot(p_n.astype(BF16), vn, preferred_element_type=F32)) / den
    full = jnp.where(own, full, 0.0)
    out = full[0:T, :]
    for h in range(1, N_HEADS_A):
        out = out + full[h * T:(h + 1) * T, :]
    o_ref[...] = out.astype(o_ref.dtype)
    ko_ref[0:R - T, :] = kc_ref[T:R, :]
    ko_ref[R - T:R, :] = kn_ref[...]
    vo_ref[0:R - T, :] = vc_ref[T:R, :]
    vo_ref[R - T:R, :] = vn_ref[...]


def _sample_attention(qa, ka, va, cache_k, cache_v):
    B, T, W = qa.shape
    R = cache_k.shape[1]
    assert R >= DILATIONS[-1] * KEYS_PER_CONFIG and T % 8 == 0 and T <= LANES
    new = pl.BlockSpec((None, T, W), lambda b: (b, 0, 0))
    cache = pl.BlockSpec((None, R, W), lambda b: (b, 0, 0))
    return pl.pallas_call(
        _sample_attn_kernel,
        grid=(B,),
        in_specs=[new, new, new, cache, cache],
        out_specs=[new, cache, cache],
        out_shape=[jax.ShapeDtypeStruct((B, T, W), BF16),
                   jax.ShapeDtypeStruct((B, R, W), F32),
                   jax.ShapeDtypeStruct((B, R, W), F32)],
        compiler_params=_cparams(("parallel",)),
        name="sample_attn",
    )(qa, ka, va, cache_k, cache_v)


CHUNKS = 8


def _store_chunked(ref, val):
    n = val.shape[0]
    for s in range(CHUNKS):
        ref[pl.ds(s, n, stride=CHUNKS), :] = val[:, s * LANES:(s + 1) * LANES]


def _load_chunked(ref, n):
    return jnp.concatenate([ref[pl.ds(s, n, stride=CHUNKS), :] for s in range(CHUNKS)], axis=1)


def _split2(x):
    hi = x.astype(BF16)
    return hi, (x - hi.astype(F32)).astype(BF16)


def _merge_kernel(oa_ref, ob_ref, x_ref, wo_ref, g_ref, wr_ref, br_ref,
                  h_ref, xn_ref, idx_ref, gate_ref):
    TM = x_ref.shape[0]
    mixed = (jnp.dot(oa_ref[...], wo_ref[0:WIDTH_A, :], preferred_element_type=F32)
             + jnp.dot(ob_ref[...], wo_ref[WIDTH_A:, :], preferred_element_type=F32))
    h = x_ref[...] + mixed
    h_ref[...] = h
    ms = jnp.mean(h * h, axis=-1, keepdims=True)
    xn = (h * lax.rsqrt(ms + EPS)) * g_ref[...]
    xn_ref[...] = xn.astype(xn_ref.dtype)
    xh, xl = _split2(xn)
    wh, wl = _split2(wr_ref[...])
    both = jnp.dot(jnp.concatenate([xh, xl], axis=0), wh, preferred_element_type=F32)
    tok_major = both[:TM] + both[TM:] + jnp.dot(xh, wl, preferred_element_type=F32)
    logits = jnp.transpose(tok_major)[:N_EXPERTS] + br_ref[...]
    e_iota = lax.broadcasted_iota(jnp.int32, (N_EXPERTS, TM), 0)
    vals, idxs = [], []
    for _ in range(TOP_K):
        m = jnp.max(logits, axis=0, keepdims=True)
        sel = jnp.min(jnp.where(logits == m, e_iota, N_EXPERTS), axis=0, keepdims=True)
        vals.append(m)
        idxs.append(sel)
        logits = jnp.where(e_iota == sel, NEG_INF, logits)
    ex = [jnp.exp(v - vals[0]) for v in vals]
    den = ex[0] + ex[1] + ex[2] + ex[3]
    idx_ref[...] = jnp.concatenate(idxs, axis=0)
    gate_ref[...] = jnp.concatenate([e / den for e in ex], axis=0)


def _merge(oa, ob, x, w_out, norm_g, w_router_pad, b_router, tm):
    N, D = x.shape
    full = lambda a: pl.BlockSpec(a.shape, lambda i: (0,) * a.ndim)
    row = lambda w: pl.BlockSpec((tm, w), lambda i: (i, 0))
    col = pl.BlockSpec((TOP_K, tm), lambda i: (0, i))
    return pl.pallas_call(
        _merge_kernel,
        grid=(N // tm,),
        in_specs=[row(WIDTH_A), row(WIDTH_B), row(D), full(w_out), full(norm_g), full(w_router_pad),
                  full(b_router)],
        out_specs=[row(D), row(D), col, col],
        out_shape=[jax.ShapeDtypeStruct((N, D), F32),
                   jax.ShapeDtypeStruct((N, D), BF16),
                   jax.ShapeDtypeStruct((TOP_K, N), jnp.int32),
                   jax.ShapeDtypeStruct((TOP_K, N), F32)],
        compiler_params=_cparams(("parallel",)),
        name="merge_router",
    )(oa, ob, x, w_out, norm_g, w_router_pad, b_router)


MOE_ROWS = 512
TOKEN_TILE = 256
DMA_ROWS = 32
STAGE_ROWS = TOKEN_TILE * TOP_K


def _expert_row(col):
    r = lax.broadcasted_iota(jnp.int32, (N_EXPERTS, LANES), 0)
    c = lax.broadcasted_iota(jnp.int32, (N_EXPERTS, LANES), 1)
    return jnp.sum(jnp.where(r == c, col, 0.0), axis=0, keepdims=True)


def _expert_prefix(col):
    r = lax.broadcasted_iota(jnp.int32, (N_EXPERTS, LANES), 0)
    c = lax.broadcasted_iota(jnp.int32, (N_EXPERTS, LANES), 1)
    return jnp.sum(jnp.where(c < r, _expert_row(col), 0.0), axis=1, keepdims=True)


PIECE_SIZES = (DMA_ROWS, 16, 8, 4, 2, 1)
assert all(a == 2 * b for a, b in zip(PIECE_SIZES, PIECE_SIZES[1:])) and PIECE_SIZES[-1] == 1
META_ROWS = 16
assert 2 * len(PIECE_SIZES) + 1 <= META_ROWS and TOKEN_TILE * TOP_K // DMA_ROWS <= LANES


def _piece_list(count, src, dst, stride):
    first = _expert_prefix(count)
    f = lax.broadcasted_iota(jnp.int32, (N_EXPERTS, LANES), 1).astype(F32)
    owner = jnp.sum(jnp.where(first + count <= f, 1.0, 0.0), axis=0, keepdims=True)
    hit = lax.broadcasted_iota(jnp.int32, (N_EXPERTS, LANES), 0).astype(F32) == owner
    pick = lambda col: jnp.sum(jnp.where(hit, col, 0.0), axis=0, keepdims=True)
    j = f[0:1, :] - pick(first)
    return pick(src) + stride * j, pick(dst) + stride * j


def _route_kernel(idx_all_ref, idx_ref, pos_ref, meta_ref, be_ref, misc_ref, carry_ref, start_ref):
    i = pl.program_id(0)
    TT = idx_ref.shape[1]
    NBP = be_ref.shape[1]

    @pl.when(i == 0)
    def _():
        idx_all = idx_all_ref[...]
        e_all = lax.broadcasted_iota(jnp.int32, (N_EXPERTS, idx_all.shape[1]), 0)
        tot = jnp.zeros((N_EXPERTS, 1), F32)
        for k in range(TOP_K):
            tot = tot + jnp.sum(jnp.where(idx_all[k:k + 1, :] == e_all, 1.0, 0.0), axis=1, keepdims=True)
        padded = jnp.floor((tot + (MOE_ROWS - 1)) / MOE_ROWS) * MOE_ROWS
        start = _expert_prefix(padded)
        start_ref[...] = start
        carry_ref[...] = jnp.zeros_like(carry_ref)
        end = start + padded
        block_start = lax.broadcasted_iota(jnp.int32, (N_EXPERTS, NBP), 1).astype(F32) * MOE_ROWS
        be = jnp.sum(jnp.where(end <= block_start, 1.0, 0.0), axis=0, keepdims=True)
        be_ref[...] = jnp.minimum(be, N_EXPERTS - 1).astype(jnp.int32)
        n_used = jnp.broadcast_to(jnp.sum(padded, axis=0, keepdims=True) / MOE_ROWS, (1, LANES))
        zero = jnp.zeros((1, LANES), F32)
        misc_ref[...] = jnp.concatenate([n_used, _expert_row(start + tot)] + [zero] * 6,
                                        axis=0).astype(jnp.int32)

    idx = idx_ref[...]
    e_iota = lax.broadcasted_iota(jnp.int32, (N_EXPERTS, TT), 0)
    onehot = [idx[k:k + 1, :] == e_iota for k in range(TOP_K)]
    cnt = jnp.zeros((N_EXPERTS, TT), F32)
    for oh in onehot:
        cnt = cnt + jnp.where(oh, 1.0, 0.0)
    tile_tot = jnp.sum(cnt, axis=1, keepdims=True)
    earlier = (lax.broadcasted_iota(jnp.int32, (TT, TT), 0)
               < lax.broadcasted_iota(jnp.int32, (TT, TT), 1))
    before = jnp.dot(cnt.astype(BF16), jnp.where(earlier, 1.0, 0.0).astype(BF16),
                     preferred_element_type=F32)
    seg = _expert_prefix(tile_tot)
    where_staged = seg + before
    rows = [jnp.sum(jnp.where(oh, where_staged, 0.0), axis=0, keepdims=True) for oh in onehot]
    pos_ref[...] = jnp.concatenate(rows, axis=0).astype(jnp.int32)
    slot = start_ref[...] + carry_ref[...]
    whole = jnp.floor(tile_tot / DMA_ROWS)
    covered = whole * DMA_ROWS
    lists = list(_piece_list(whole, seg, slot, float(DMA_ROWS)))
    counts = [jnp.sum(whole, axis=0, keepdims=True)]
    for size in PIECE_SIZES[1:]:
        has = jnp.floor((tile_tot - covered) / size)
        lists += _piece_list(has, seg + covered, slot + covered, 0.0)
        counts.append(jnp.sum(has, axis=0, keepdims=True))
        covered = covered + has * size
    lane = lax.broadcasted_iota(jnp.int32, (1, LANES), 1)
    count_row = jnp.zeros((1, LANES), F32)
    for k, c in enumerate(counts):
        count_row = jnp.where(lane == k, c, count_row)
    zero = jnp.zeros((1, LANES), F32)
    meta_ref[...] = jnp.concatenate(lists + [count_row] + [zero] * (META_ROWS - len(lists) - 1),
                                    axis=0).astype(jnp.int32)
    carry_ref[...] += tile_tot


def _route(idx, n_blocks):
    _, N = idx.shape
    nbp = -(-n_blocks // LANES) * LANES
    tile = pl.BlockSpec((TOP_K, TOKEN_TILE), lambda i: (0, i))
    return pl.pallas_call(
        _route_kernel,
        grid=(N // TOKEN_TILE,),
        in_specs=[pl.BlockSpec((TOP_K, N), lambda i: (0, 0)), tile],
        out_specs=[tile,
                   pl.BlockSpec((META_ROWS, LANES), lambda i: (i, 0)),
                   pl.BlockSpec((1, nbp), lambda i: (0, 0)),
                   pl.BlockSpec((8, LANES), lambda i: (0, 0))],
        out_shape=[jax.ShapeDtypeStruct((TOP_K, N), jnp.int32),
                   jax.ShapeDtypeStruct((N // TOKEN_TILE * META_ROWS, LANES), jnp.int32),
                   jax.ShapeDtypeStruct((1, nbp), jnp.int32),
                   jax.ShapeDtypeStruct((8, LANES), jnp.int32)],
        scratch_shapes=[pltpu.VMEM((N_EXPERTS, 1), F32)] * 2,
        compiler_params=_cparams(("arbitrary",)),
        name="route",
    )(idx, idx)


def _rows(ref, first_row, n_rows):
    return ref.at[pl.ds(pl.multiple_of(first_row * CHUNKS, CHUNKS), n_rows * CHUNKS), :]


def _for_each_run_piece(meta_ref, fn):
    for k, size in enumerate(PIECE_SIZES):
        def body(j, c, k=k, size=size):
            fn(meta_ref[2 * k + 1, j], meta_ref[2 * k, j], size)
            return c

        lax.fori_loop(0, meta_ref[2 * len(PIECE_SIZES), k], body, 0)


def _wait_rows(n_rows, src_ref, dst_ref, sem):
    @pl.when(n_rows > 0)
    def _():
        n = n_rows * CHUNKS
        pltpu.make_async_copy(src_ref.at[pl.ds(0, n), :], dst_ref.at[pl.ds(0, n), :], sem).wait()


PAD_ROWS = MOE_ROWS


def _zero_padding(misc_ref, xs_ref, zeros_ref, sem, n_blocks):
    zeros_ref[...] = jnp.zeros_like(zeros_ref)

    def pad_copy(e):
        first = pl.multiple_of(misc_ref[1, e] * CHUNKS, CHUNKS)
        return pltpu.make_async_copy(zeros_ref, xs_ref.at[pl.ds(first, PAD_ROWS * CHUNKS), :], sem)

    def tail_copy(b):
        first = pl.multiple_of(b * (MOE_ROWS * CHUNKS), MOE_ROWS * CHUNKS)
        return pltpu.make_async_copy(zeros_ref.at[pl.ds(0, MOE_ROWS * CHUNKS), :],
                                     xs_ref.at[pl.ds(first, MOE_ROWS * CHUNKS), :], sem)

    def pad(e, c):
        pad_copy(e).start()
        pad_copy(e).wait()
        return c

    def tail(start_not_wait):
        def body(b, c):
            tail_copy(b).start() if start_not_wait else tail_copy(b).wait()
            return c

        lax.fori_loop(misc_ref[0, 0], n_blocks + 1, body, 0)

    lax.fori_loop(0, N_EXPERTS, pad, 0)
    tail(True)
    tail(False)


def _selection(pos_ref, fill_ref=None):
    TT = pos_ref.shape[1]
    p_iota = lax.broadcasted_iota(jnp.int32, (STAGE_ROWS, TT), 0)
    sel = jnp.zeros((STAGE_ROWS, TT), F32)
    for k in range(TOP_K):
        val = 1.0 if fill_ref is None else fill_ref[k:k + 1, :]
        sel = jnp.where(pos_ref[k:k + 1, :] == p_iota, val, sel)
    return sel.astype(BF16)


def _dispatch_kernel(misc_ref, meta_ref, pos_ref, *refs, first_tiles, n_tiles, n_blocks):
    x_refs = refs[:len(first_tiles)]
    xs_ref, stage_ref, zeros_ref, sems, pad_sem = refs[len(first_tiles):]
    i = pl.program_id(0)
    TT = pos_ref.shape[1]
    slot = i % 2

    @pl.when(i == 0)
    def _():
        _zero_padding(misc_ref, xs_ref, zeros_ref, pad_sem, n_blocks)

    x = x_refs[0][...]
    for t0, ref in zip(first_tiles[1:], x_refs[1:]):
        x = jnp.where(i >= t0, ref[...], x)
    staged = jnp.dot(_selection(pos_ref), x, preferred_element_type=F32)
    stage = stage_ref.at[slot]
    _store_chunked(stage, staged)
    _for_each_run_piece(meta_ref, lambda dst, src, n: pltpu.make_async_copy(
        _rows(stage, src, n), _rows(xs_ref, dst, n), sems.at[slot]).start())

    @pl.when(i > 0)
    def _():
        _wait_rows(TT * TOP_K, stage_ref.at[1 - slot], xs_ref, sems.at[1 - slot])

    @pl.when(i == n_tiles - 1)
    def _():
        _wait_rows(TT * TOP_K, stage, xs_ref, sems.at[slot])


def _dispatch(misc, meta, pos, xns, n_blocks):
    D = xns[0].shape[1]
    tiles = [x.shape[0] // TOKEN_TILE for x in xns]
    first_tiles = tuple(sum(tiles[:g]) for g in range(len(tiles)))

    def x_spec(t0, nt):
        return pl.BlockSpec((TOKEN_TILE, D), lambda i: (jnp.clip(i - t0, 0, nt - 1), 0))

    return pl.pallas_call(
        functools.partial(_dispatch_kernel, first_tiles=first_tiles, n_tiles=sum(tiles), n_blocks=n_blocks),
        grid=(sum(tiles),),
        in_specs=[pl.BlockSpec(memory_space=pltpu.SMEM),
                  pl.BlockSpec((META_ROWS, LANES), lambda i: (i, 0), memory_space=pltpu.SMEM),
                  pl.BlockSpec((TOP_K, TOKEN_TILE), lambda i: (0, i))]
                 + [x_spec(t0, nt) for t0, nt in zip(first_tiles, tiles)],
        out_specs=pl.BlockSpec(memory_space=pl.ANY),
        out_shape=jax.ShapeDtypeStruct(((n_blocks + 1) * MOE_ROWS * CHUNKS, LANES), F32),
        scratch_shapes=[pltpu.VMEM((2, STAGE_ROWS * CHUNKS, LANES), F32),
                        pltpu.VMEM((PAD_ROWS * CHUNKS, LANES), F32),
                        pltpu.SemaphoreType.DMA((2,)),
                        pltpu.SemaphoreType.DMA(())],
        compiler_params=_cparams(("arbitrary",)),
        name="dispatch",
    )(misc, meta, pos, *xns)


def _expert_kernel(be_ref, nused_ref, xs_ref, wgu_ref, bgu_ref, wd_ref, bd_ref, ys_ref,
                   wgu_bf, wd_bf):
    i = pl.program_id(0)
    D_FF = wd_ref.shape[0]
    new_expert = jnp.logical_or(i == 0, be_ref[i] != be_ref[jnp.maximum(i - 1, 0)])

    @pl.when(jnp.logical_and(i < nused_ref[0], new_expert))
    def _():
        wgu_bf[...] = wgu_ref[...].astype(BF16)
        wd_bf[...] = wd_ref[...].astype(BF16)

    @pl.when(i < nused_ref[0])
    def _():
        x = _load_chunked(xs_ref, MOE_ROWS).astype(BF16)
        hdn = jnp.dot(x, wgu_bf[...], preferred_element_type=F32) + bgu_ref[...]
        glu = jnp.minimum(hdn[:, :D_FF], SWIGLU_LIMIT)
        lin = jnp.clip(hdn[:, D_FF:], -SWIGLU_LIMIT, SWIGLU_LIMIT)
        act = glu * (1.0 / (1.0 + jnp.exp(-SWIGLU_ALPHA * glu))) * (lin + 1.0)
        y = jnp.dot(act.astype(BF16), wd_bf[...], preferred_element_type=F32) + bd_ref[...]
        _store_chunked(ys_ref, y)

    @pl.when(i >= nused_ref[0])
    def _():
        ys_ref[...] = jnp.zeros_like(ys_ref)


def _experts(block_expert, n_used, xs, w_gate_up, b_gate_up, w_down, b_down, n_blocks):
    E, D, F2 = w_gate_up.shape
    D_FF = w_down.shape[1]
    rows = pl.BlockSpec((MOE_ROWS * CHUNKS, LANES), lambda i, be, nu: (i, 0))
    rows_in = pl.BlockSpec((MOE_ROWS * CHUNKS, LANES), lambda i, be, nu: (jnp.minimum(i, nu[0] - 1), 0))
    grid_spec = pltpu.PrefetchScalarGridSpec(
        num_scalar_prefetch=2,
        grid=(n_blocks,),
        in_specs=[rows_in,
                  pl.BlockSpec((None, D, F2), lambda i, be, nu: (be[i], 0, 0)),
                  pl.BlockSpec((None, 1, F2), lambda i, be, nu: (be[i], 0, 0)),
                  pl.BlockSpec((None, D_FF, D), lambda i, be, nu: (be[i], 0, 0)),
                  pl.BlockSpec((None, 1, D), lambda i, be, nu: (be[i], 0, 0))],
        out_specs=rows,
        scratch_shapes=[pltpu.VMEM((D, F2), BF16), pltpu.VMEM((D_FF, D), BF16)],
    )
    return pl.pallas_call(
        _expert_kernel,
        grid_spec=grid_spec,
        out_shape=jax.ShapeDtypeStruct((n_blocks * MOE_ROWS * CHUNKS, LANES), F32),
        compiler_params=_cparams(("arbitrary",)),
        name="experts",
    )(block_expert, n_used, xs, w_gate_up, b_gate_up.reshape(E, 1, F2), w_down, b_down.reshape(E, 1, D))


def _combine_kernel(meta_ref, next_meta_ref, pos_ref, gate_ref, h_ref, g_ref, ys_ref, y_ref, stage_ref, sems,
                    *, n_tiles):
    i = pl.program_id(0)
    slot = i % 2

    def fetch(meta, s):
        _for_each_run_piece(meta, lambda src, dst, n: pltpu.make_async_copy(
            _rows(ys_ref, src, n), _rows(stage_ref.at[s], dst, n), sems.at[s]).start())

    @pl.when(i == 0)
    def _():
        fetch(meta_ref, 0)

    if n_tiles > 1:
        @pl.when(i + 1 < n_tiles)
        def _():
            fetch(next_meta_ref, 1 - slot)

    weights = _selection(pos_ref, gate_ref)
    _wait_rows(STAGE_ROWS, ys_ref, stage_ref.at[slot], sems.at[slot])
    staged = _load_chunked(stage_ref.at[slot], STAGE_ROWS).astype(BF16)
    moe = lax.dot_general(weights, staged, (((0,), (0,)), ((), ())), preferred_element_type=F32)
    hf = h_ref[...] + moe
    ms = jnp.mean(hf * hf, axis=-1, keepdims=True)
    y_ref[...] = (hf * lax.rsqrt(ms + EPS)) * g_ref[...]


def _combine(meta, pos, gates, h, norm_g, ys, tile0):
    n, D = h.shape
    n_tiles = n // TOKEN_TILE
    return pl.pallas_call(
        functools.partial(_combine_kernel, n_tiles=n_tiles),
        grid=(n_tiles,),
        in_specs=[pl.BlockSpec((META_ROWS, LANES), lambda i: (tile0 + i, 0), memory_space=pltpu.SMEM),
                  pl.BlockSpec((META_ROWS, LANES), lambda i: (tile0 + jnp.minimum(i + 1, n_tiles - 1), 0),
                               memory_space=pltpu.SMEM),
                  pl.BlockSpec((TOP_K, TOKEN_TILE), lambda i: (0, tile0 + i)),
                  pl.BlockSpec((TOP_K, TOKEN_TILE), lambda i: (0, i)),
                  pl.BlockSpec((TOKEN_TILE, D), lambda i: (i, 0)),
                  pl.BlockSpec((1, D), lambda i: (0, 0)),
                  pl.BlockSpec(memory_space=pl.ANY)],
        out_specs=pl.BlockSpec((TOKEN_TILE, D), lambda i: (i, 0)),
        out_shape=jax.ShapeDtypeStruct((n, D), F32),
        scratch_shapes=[pltpu.VMEM((2, STAGE_ROWS * CHUNKS, LANES), F32), pltpu.SemaphoreType.DMA((2,))],
        compiler_params=_cparams(("arbitrary",)),
        name="combine",
    )(meta, meta, pos, gates, h, norm_g, ys)


def _prep_weights(w_in, w_alpha):
    w_main = w_in[:, :PROJ_MAIN].astype(BF16)
    w_lr = jnp.pad(w_in[:, PROJ_MAIN:], ((0, 0), (0, LANES - GATE_RANK))).astype(BF16)
    w_al = jnp.pad(w_alpha, ((0, LANES - GATE_RANK), (0, 0))).astype(BF16)
    return w_main, w_lr, w_al


def kernel(x_prompt, x_sample, cache_swa_k, cache_swa_v, state_gla, norm_mix_g, w_in, w_alpha, b_alpha, gla_norm_g, w_out, norm_ffn_g, w_router, b_router, w_gate_up, b_gate_up, w_down, b_down, norm_final_g):
    B, S, D = x_prompt.shape
    Bs, Ts, _ = x_sample.shape
    assert w_in.shape[0] == 1, "single-layer trunk"
    l = 0
    R = cache_swa_k.shape[2]
    rows_p = min(DILATIONS[-1] * KEYS_PER_CONFIG, S)
    w_main, w_lr, w_al = _prep_weights(w_in[l], w_alpha[l])
    g_mix = norm_mix_g[l][None]
    b_al = b_alpha[l][None]
    g_gla = gla_norm_g[l][None]

    pos_p = jnp.arange(S, dtype=jnp.int32)
    qa, ka, va, qb, kb, vb, zg, gb, k_tail, v_tail = _project(x_prompt, pos_p, g_mix, w_main, w_lr, w_al, b_al,
                                                              PROJ_TILE, tail_rows=rows_p)
    oa_p = _prompt_attention(qa, ka, va)
    ob_p, st_p = _gla(qb, kb, gb, vb, zg, jnp.zeros((B, N_HEADS_B, DK_B, DV_B), F32), g_gla,
                      GLA_TILE, GLA_CHUNK)
    k_prompt = k_tail.reshape(1, B, rows_p, N_HEADS_A, HEAD_DIM_A)
    v_prompt = v_tail.reshape(1, B, rows_p, N_HEADS_A, HEAD_DIM_A)

    pos_s = PAST_LEN + (jnp.arange(Bs * Ts, dtype=jnp.int32) % Ts)
    proj_s = _project(x_sample.reshape(1, Bs * Ts, D), pos_s, g_mix, w_main, w_lr, w_al, b_al, Bs * Ts)
    qa_s, ka_s, va_s, qb_s, kb_s, vb_s, zg_s, gb_s = [t.reshape(Bs, Ts, -1) for t in proj_s]
    oa_s, k_sample, v_sample = _sample_attention(qa_s, ka_s, va_s,
                                                 cache_swa_k[l].reshape(Bs, R, WIDTH_A),
                                                 cache_swa_v[l].reshape(Bs, R, WIDTH_A))
    ob_s, st_s = _gla(qb_s, kb_s, gb_s, vb_s, zg_s, state_gla[l], g_gla, Ts, Ts)

    w_out_bf = w_out[l].astype(BF16)
    g_ffn = norm_ffn_g[l][None]
    w_router_pad = jnp.pad(w_router[l], ((0, 0), (0, LANES - N_EXPERTS)))
    b_router_c = b_router[l][:, None]
    Np, Ns = B * S, Bs * Ts
    h_p, xn_p, idx_p, gate_p = _merge(oa_p.reshape(Np, WIDTH_A), ob_p.reshape(Np, WIDTH_B),
                                      x_prompt.reshape(Np, D), w_out_bf, g_ffn, w_router_pad, b_router_c,
                                      MERGE_TILE)
    h_s, xn_s, idx_s, gate_s = _merge(oa_s.reshape(Ns, WIDTH_A), ob_s.reshape(Ns, WIDTH_B),
                                      x_sample.reshape(Ns, D), w_out_bf, g_ffn, w_router_pad, b_router_c,
                                      Ns)

    y_p, y_s = _moe([(xn_p, idx_p, gate_p, h_p), (xn_s, idx_s, gate_s, h_s)],
                    w_gate_up[l], b_gate_up[l], w_down[l], b_down[l], norm_final_g[None])
    return (y_p.reshape(B, S, D), y_s.reshape(Bs, Ts, D), k_prompt, v_prompt, st_p[None],
            k_sample.reshape(1, Bs, R, N_HEADS_A, HEAD_DIM_A),
            v_sample.reshape(1, Bs, R, N_HEADS_A, HEAD_DIM_A), st_s[None])


PAST_LEN = 16384
PROJ_TILE = 1024
MERGE_TILE = 1024
GLA_TILE = 1024
GLA_CHUNK = 64


def _moe(groups, w_gate_up, b_gate_up, w_down, b_down, g_final):
    sizes = [g[3].shape[0] for g in groups]
    N = sum(sizes)
    assert all(n % TOKEN_TILE == 0 for n in sizes)
    n_blocks = -(-(N * TOP_K + N_EXPERTS * (MOE_ROWS - 1)) // MOE_ROWS)
    idx = jnp.concatenate([g[1] for g in groups], axis=1)
    pos, meta, block_expert, misc = _route(idx, n_blocks)
    first_tile = [sum(sizes[:i]) // TOKEN_TILE for i in range(len(sizes))]
    xs = _dispatch(misc, meta, pos, [g[0] for g in groups], n_blocks)
    ys = _experts(block_expert[0], misc[0, :1], xs, w_gate_up, b_gate_up, w_down, b_down, n_blocks)
    return [_combine(meta, pos, gates, h, g_final, ys, t0)
            for (_, _, gates, h), t0 in zip(groups, first_tile)]
```
